```python
import math
import jax
import jax.numpy as jnp
from jax import lax
import numpy as np

D_MODEL = 4096
BATCH = 1
SEQ = 8192
DEPTH = 4

GRID_W = 64
CTX_LEN = 256
N_MIXERS = 4
DEEPNORM_ALPHA = (2 * DEPTH) ** 0.25
DEEPNORM_BETA = (8 * DEPTH) ** -0.25
LN_EPS = 1e-5
RMS_EPS = 1e-6
ADA_RANK = 256
ROPE_BASE = 10000.0

ML_HEADS = 8
ML_DQK = D_MODEL // (2 * ML_HEADS)
ML_DV = D_MODEL // ML_HEADS
ML_CHUNK = 64
ML_F_BIAS_LO = 3.0
ML_F_BIAS_HI = 6.0

DA_HEAD_DIM = 128
DA_HEADS = D_MODEL // (2 * DA_HEAD_DIM)
Q_BLOCK = 128

HY_ORDER = 2
HY_EMB = 33
HY_BANDS = (HY_EMB - 1) // 2
HY_FFN = 64
HY_FAST_DECAY = 0.3
HY_SLOW_DECAY = 1.5
HY_DECAY_TARGET = 1e-2

NA_HEAD_DIM = 128
NA_HEADS = D_MODEL // NA_HEAD_DIM
NA_ROWS = 8
NA_COLS = 16

N_EXPERTS = 64
EXPERT_FF = 192
TOP_K = 8
N_GROUPS = 8
TOPK_GROUPS = 4
ROUTED_SCALE = 2.5

kernel_name = 'hybrid_flow_backbone'


def _layer_norm(x, g, b):
    xf = x.astype(jnp.float32)
    mu = jnp.mean(xf, -1, keepdims=True)
    var = jnp.mean(jnp.square(xf - mu), -1, keepdims=True)
    return ((xf - mu) * lax.rsqrt(var + LN_EPS)).astype(x.dtype) * g + b


def _rms_norm(x, g):
    xf = x.astype(jnp.float32)
    return (xf * lax.rsqrt(jnp.mean(jnp.square(xf), -1, keepdims=True) + RMS_EPS)).astype(x.dtype) * g


def _axial_rope(x):
    L, dh = x.shape[1], x.shape[-1]
    a = dh // 2
    t = jnp.arange(L)
    pos = jnp.stack([t // GRID_W, t % GRID_W], -1).astype(jnp.float32)
    inv = ROPE_BASE ** (-jnp.arange(0, a, 2, dtype=jnp.float32) / a)
    ang = pos[:, :, None] * inv
    ang = jnp.concatenate([ang, ang], -1).reshape(L, dh)
    bshape = (1, L) + (1,) * (x.ndim - 3) + (dh,)
    cos = jnp.cos(ang).reshape(bshape).astype(x.dtype)
    sin = jnp.sin(ang).reshape(bshape).astype(x.dtype)
    xs = x.reshape(x.shape[:-1] + (2, 2, a // 2))
    rot = jnp.concatenate([-xs[..., 1:, :], xs[..., :1, :]], -2).reshape(x.shape)
    return x * cos + rot * sin


def _mlstm_scan(q, k, v, li, lf, state):
    B, H, L, _ = q.shape
    dv = v.shape[-1]
    T = min(ML_CHUNK, L)
    nc = L // T
    chunk = lambda t: jnp.moveaxis(t.reshape(t.shape[:2] + (nc, T) + t.shape[3:]), 2, 0)
    lower = jnp.tril(jnp.ones((T, T), dtype=bool))

    def step(carry, inp):
        C, n, m = carry
        qc, kc, vc, ic, fc = inp
        b = jnp.cumsum(fc, -1)
        dmat = jnp.where(lower, b[..., :, None] - b[..., None, :] + ic[..., None, :], -jnp.inf)
        inter = b + m[..., None]
        m_t = jnp.maximum(inter, jnp.max(dmat, -1))
        w_intra = jnp.exp(dmat - m_t[..., None])
        w_inter = jnp.exp(inter - m_t)
        s = jnp.einsum('bhtd,bhsd->bhts', qc, kc) * w_intra
        num = w_inter[..., None] * jnp.einsum('bhtd,bhvd->bhtv', qc, C) + jnp.einsum('bhts,bhsv->bhtv', s, vc)
        den = w_inter * jnp.einsum('bhtd,bhd->bht', qc, n) + jnp.sum(s, -1)
        h = num / jnp.maximum(jnp.abs(den), jnp.exp(-m_t))[..., None]
        b_last = b[..., -1]
        g = b_last[..., None] - b + ic
        m_new = jnp.maximum(b_last + m, jnp.max(g, -1))
        decay = jnp.exp(b_last + m - m_new)
        wk = jnp.exp(g - m_new[..., None])
        C_new = decay[..., None, None] * C + jnp.einsum('bhs,bhsv,bhsd->bhvd', wk, vc, kc)
        n_new = decay[..., None] * n + jnp.einsum('bhs,bhsd->bhd', wk, kc)
        return (C_new, n_new, m_new), h

    state, hs = lax.scan(step, state, (chunk(q), chunk(k), chunk(v), chunk(li), chunk(lf)))
    return jnp.moveaxis(hs, 0, 2).reshape(B, H, L, dv), state


def _mlstm_project(h, w_in, b_if):
    B, L, _ = h.shape
    nqk = ML_HEADS * ML_DQK
    q, k, v, o, g = jnp.split(h @ w_in, [nqk, 2 * nqk, 2 * nqk + D_MODEL, 2 * nqk + 2 * D_MODEL], axis=-1)
    heads = lambda t, d: t.reshape(B, L, ML_HEADS, d).transpose(0, 2, 1, 3).astype(jnp.float32)
    q = heads(q, ML_DQK) * ML_DQK ** -0.5
    k = heads(k, ML_DQK)
    v = heads(v, ML_DV)
    g = (g.astype(jnp.float32) + b_if.astype(jnp.float32)).reshape(B, L, 4, ML_HEADS).transpose(2, 0, 3, 1)
    gates_fwd = (g[0], jax.nn.log_sigmoid(g[1]))
    gates_bwd = (g[2], jax.nn.log_sigmoid(g[3]))
    return q, k, v, o, gates_fwd, gates_bwd


def _mlstm_bidir(q, k, v, gf, gb, st_f, st_b):
    flip = lambda t: jnp.flip(t, axis=2)
    h_f, st_f = _mlstm_scan(q, k, v, gf[0], gf[1], st_f)
    h_b, st_b = _mlstm_scan(flip(q), flip(k), flip(v), flip(gb[0]), flip(gb[1]), st_b)
    return h_f + flip(h_b), st_f, st_b


def _mlstm_out(hsum, o, norm_g, w_out):
    B, H, L, dv = hsum.shape
    hn = _rms_norm(hsum.transpose(0, 2, 1, 3), norm_g.reshape(H, dv))
    return (hn.reshape(B, L, H * dv).astype(o.dtype) * jax.nn.sigmoid(o)) @ w_out


def _mlstm_mixer(h_lat, h_ctx, w_in, b_if, norm_g, w_out, need_ctx):
    B = h_ctx.shape[0]
    zero = lambda: (jnp.zeros((B, ML_HEADS, ML_DV, ML_DQK), jnp.float32),
                    jnp.zeros((B, ML_HEADS, ML_DQK), jnp.float32),
                    jnp.zeros((B, ML_HEADS), jnp.float32))
    qc, kc, vc, oc, gfc, gbc = _mlstm_project(h_ctx, w_in, b_if)
    hc, st_f, st_b = _mlstm_bidir(qc, kc, vc, gfc, gbc, zero(), zero())
    ql, kl, vl, ol, gfl, gbl = _mlstm_project(h_lat, w_in, b_if)
    hl, _, _ = _mlstm_bidir(ql, kl, vl, gfl, gbl, st_f, st_b)
    y_lat = _mlstm_out(hl, ol, norm_g, w_out)
    y_ctx = _mlstm_out(hc, oc, norm_g, w_out) if need_ctx else None
    return y_lat, y_ctx


def _diff_attend(qb, keys, vals, lam):
    s = jnp.einsum('bqhid,bkhid->bhiqk', qb, keys).astype(jnp.float32) * DA_HEAD_DIM ** -0.5
    p = jax.nn.softmax(s, axis=-1)
    a = (p[:, :, 0] - lam * p[:, :, 1]).astype(vals.dtype)
    return jnp.einsum('bhqk,bkhe->bqhe', a, vals)


def _diff_attn_mixer(h_lat, h_ctx, w_qkv, lam_q1, lam_k1, lam_q2, lam_k2, subln_g, w_out, lambda_init, need_ctx):
    B, L, _ = h_lat.shape
    C = h_ctx.shape[1]

    def proj(h):
        q, k, v = jnp.split(h @ w_qkv, 3, axis=-1)
        sh = h.shape[:2] + (DA_HEADS, 2, DA_HEAD_DIM)
        return q.reshape(sh), k.reshape(sh), v.reshape(h.shape[:2] + (DA_HEADS, 2 * DA_HEAD_DIM))

    lam = (jnp.exp(jnp.sum(lam_q1 * lam_k1).astype(jnp.float32))
           - jnp.exp(jnp.sum(lam_q2 * lam_k2).astype(jnp.float32)) + lambda_init)
    qc, kc, vc = proj(h_ctx)
    ql, kl, vl = proj(h_lat)
    ql, kl = _axial_rope(ql), _axial_rope(kl)
    k_all = jnp.concatenate([kc, kl], axis=1)
    v_all = jnp.concatenate([vc, vl], axis=1)
    nb = L // Q_BLOCK
    q_blocks = jnp.moveaxis(ql.reshape(B, nb, Q_BLOCK, DA_HEADS, 2, DA_HEAD_DIM), 1, 0)
    o = lax.map(lambda qb: _diff_attend(qb, k_all, v_all, lam), q_blocks)
    o = jnp.moveaxis(o, 0, 1).reshape(B, L, DA_HEADS, 2 * DA_HEAD_DIM)
    out = lambda o_, n: (_rms_norm(o_, subln_g) * (1.0 - lambda_init)).reshape(B, n, D_MODEL) @ w_out
    y_lat = out(o, L)
    y_ctx = out(_diff_attend(qc, kc, vc, lam), C) if need_ctx else None
    return y_lat, y_ctx


def _short_conv(u, w, b):
    up = jnp.pad(u, ((0, 0), (1, 1), (0, 0)))
    return up[:, :-2] * w[0] + up[:, 1:-1] * w[1] + up[:, 2:] * w[2] + b


def _hyena_filters(L, f_w1, f_b1, f_w2, f_b2, f_w3, f_b3, f_w4):
    t = jnp.linspace(0.0, 1.0, L, dtype=jnp.float32)[:, None]
    w = 2.0 * math.pi * jnp.arange(L, dtype=jnp.float32)[:, None] / L
    f = jnp.linspace(1e-4, HY_BANDS - 1, HY_BANDS, dtype=jnp.float32)[None]
    z = jnp.concatenate([t, jnp.cos(w * f), -jnp.sin(w * f)], -1).astype(f_w1.dtype)
    hdn = jnp.sin(z @ f_w1 + f_b1)
    hdn = jnp.sin(hdn @ f_w2 + f_b2)
    hdn = jnp.sin(hdn @ f_w3 + f_b3)
    hf = (hdn @ f_w4).astype(jnp.float32).reshape(L, HY_ORDER, 2, D_MODEL)
    deltas = jnp.linspace(math.log(HY_DECAY_TARGET) / HY_SLOW_DECAY, math.log(HY_DECAY_TARGET) / HY_FAST_DECAY,
                          D_MODEL, dtype=jnp.float32)
    hf = hf * jnp.exp(-t * jnp.abs(deltas))[:, None, None, :]
    fwd, bwd = hf[:, :, 0], hf[:, :, 1]
    kern = jnp.concatenate([fwd, jnp.zeros_like(fwd[:1]), jnp.flip(bwd[1:], 0)], 0)
    kern = kern / jnp.sum(jnp.abs(kern), 0, keepdims=True)
    return jnp.fft.rfft(kern, axis=0)


def _fftconv(z, kf):
    L = z.shape[1]
    zf = jnp.fft.rfft(z.astype(jnp.float32), n=2 * L, axis=1)
    return jnp.fft.irfft(zf * kf[None], n=2 * L, axis=1)[:, :L].astype(z.dtype)


def _hyena_mixer(h_lat, h_ctx, w_in, conv_w, conv_b, f_w1, f_b1, f_w2, f_b2, f_w3, f_b3, f_w4, hy_bias, w_out,
                 need_ctx):
    def run(h):
        L = h.shape[1]
        u = _short_conv(h @ w_in, conv_w, conv_b)
        v, x1, x2 = jnp.split(u, 3, axis=-1)
        kf = _hyena_filters(L, f_w1, f_b1, f_w2, f_b2, f_w3, f_b3, f_w4)
        z = v
        for o, gate in enumerate((x1, x2)):
            z = gate * (_fftconv(z, kf[:, o]) + z * hy_bias[o])
        return z @ w_out

    y_lat = run(h_lat)
    y_ctx = run(h_ctx) if need_ctx else None
    return y_lat, y_ctx


def _na_mixer(h_lat, h_ctx, w_qkv, rpb, w_out, need_ctx):
    B, L, _ = h_lat.shape
    C = h_ctx.shape[1]
    scale = NA_HEAD_DIM ** -0.5

    def proj(h):
        q, k, v = jnp.split(h @ w_qkv, 3, axis=-1)
        sh = h.shape[:2] + (NA_HEADS, NA_HEAD_DIM)
        return q.reshape(sh), k.reshape(sh), v.reshape(sh)

    qc, kc, vc = proj(h_ctx)
    ql, kl, vl = proj(h_lat)
    rows = L // GRID_W
    kr = min(NA_ROWS, rows)
    k_grid = kl.reshape(B, rows, GRID_W, NA_HEADS, NA_HEAD_DIM)
    v_grid = vl.reshape(B, rows, GRID_W, NA_HEADS, NA_HEAD_DIM)
    cols = jnp.arange(GRID_W)
    col_idx = jnp.clip(cols - NA_COLS // 2, 0, GRID_W - NA_COLS)[:, None] + jnp.arange(NA_COLS)
    col_rel = col_idx - cols[:, None] + NA_COLS - 1
    bias_cols = rpb[:, :, col_rel]

    def row_block(args):
        r, q_row = args
        r0 = jnp.clip(r - kr // 2, 0, rows - kr)
        k_nb = lax.dynamic_slice_in_dim(k_grid, r0, kr, axis=1)[:, :, col_idx]
        v_nb = lax.dynamic_slice_in_dim(v_grid, r0, kr, axis=1)[:, :, col_idx]
        row_rel = r0 + jnp.arange(kr) - r + NA_ROWS - 1
        bias = jnp.transpose(bias_cols[:, row_rel], (0, 2, 1, 3))
        s_lat = jnp.einsum('bqhd,brqkhd->bhqrk', q_row, k_nb).astype(jnp.float32) * scale + bias[None]
        s_ctx = jnp.einsum('bqhd,bchd->bhqc', q_row, kc).astype(jnp.float32) * scale
        p = jax.nn.softmax(jnp.concatenate([s_ctx, s_lat.reshape(B, NA_HEADS, GRID_W, kr * NA_COLS)], -1),
                           axis=-1).astype(v_nb.dtype)
        p_ctx = p[..., :C]
        p_lat = p[..., C:].reshape(B, NA_HEADS, GRID_W, kr, NA_COLS)
        return jnp.einsum('bhqc,bchd->bqhd', p_ctx, vc) + jnp.einsum('bhqrk,brqkhd->bqhd', p_lat, v_nb)

    q_rows = jnp.moveaxis(ql.reshape(B, rows, GRID_W, NA_HEADS, NA_HEAD_DIM), 1, 0)
    o = lax.map(row_block, (jnp.arange(rows), q_rows))
    y_lat = jnp.moveaxis(o, 0, 1).reshape(B, L, D_MODEL) @ w_out
    y_ctx = None
    if need_ctx:
        s = jnp.einsum('bqhd,bkhd->bhqk', qc, kc).astype(jnp.float32) * scale
        p = jax.nn.softmax(s, axis=-1).astype(vc.dtype)
        y_ctx = jnp.einsum('bhqk,bkhd->bqhd', p, vc).reshape(B, C, D_MODEL) @ w_out
    return y_lat, y_ctx


def _moe(h, router_w, router_b, w_gate, w_up, w_down, s_gate, s_up, s_down):
    B, N, D = h.shape
    x = h.reshape(B * N, D)
    scores = jax.nn.sigmoid((x @ router_w).astype(jnp.float32))
    sel = scores + router_b.astype(jnp.float32)
    grp = sel.reshape(-1, N_GROUPS, N_EXPERTS // N_GROUPS)
    grp_score = jnp.sum(lax.top_k(grp, 2)[0], -1)
    g_idx = lax.top_k(grp_score, TOPK_GROUPS)[1]
    g_mask = jnp.sum(jax.nn.one_hot(g_idx, N_GROUPS, dtype=jnp.float32), -2)
    e_mask = jnp.repeat(g_mask, N_EXPERTS // N_GROUPS, axis=-1) > 0
    e_idx = lax.top_k(jnp.where(e_mask, sel, -jnp.inf), TOP_K)[1]
    w = jnp.take_along_axis(scores, e_idx, -1)
    w = w / jnp.sum(w, -1, keepdims=True) * ROUTED_SCALE
    gates = jnp.einsum('tk,tke->te', w, jax.nn.one_hot(e_idx, N_EXPERTS, dtype=jnp.float32)).astype(h.dtype)
    hid = jax.nn.silu(jnp.einsum('td,edf->tef', x, w_gate)) * jnp.einsum('td,edf->tef', x, w_up)
    routed = jnp.einsum('tef,efd->td', hid * gates[:, :, None], w_down)
    shared = (jax.nn.silu(x @ s_gate) * (x @ s_up)) @ s_down
    return (routed + shared).reshape(B, N, D)


def setup_inputs(seed: int = 0) -> dict:
    key = jax.random.key(seed)
    ks = iter(jax.random.split(key, 64))
    nrm = lambda shape, std: jax.random.normal(next(ks), shape, jnp.float32) * std
    D = D_MODEL
    nA, nB, nC, nD = (len(range(m, DEPTH, N_MIXERS)) for m in range(N_MIXERS))
    ml_cols = 2 * ML_HEADS * ML_DQK + 2 * D + 4 * ML_HEADS
    f_offset = jnp.array([0.0, 1.0, 0.0, 1.0], jnp.float32)[None, :, None] * \
        jnp.linspace(ML_F_BIAS_LO, ML_F_BIAS_HI, ML_HEADS, dtype=jnp.float32)[None, None, :]
    return {
        'x': nrm((BATCH, SEQ, D), 1.0),
        'c': nrm((BATCH, D), 1.0),
        'ctx': nrm((BATCH, CTX_LEN, D), 1.0),
        'c_ctx': nrm((D,), 1.0),
        'ada_a': nrm((DEPTH, D, ADA_RANK), D ** -0.5),
        'ada_b': nrm((DEPTH, ADA_RANK, 6 * D), 0.5 * ADA_RANK ** -0.5),
        'ada_bias': nrm((DEPTH, 6 * D), 0.02),
        'ln1_g': 1.0 + nrm((DEPTH, D), 0.02),
        'ln1_b': nrm((DEPTH, D), 0.02),
        'ln2_g': 1.0 + nrm((DEPTH, D), 0.02),
        'ln2_b': nrm((DEPTH, D), 0.02),
        'router_w': nrm((DEPTH, D, N_EXPERTS), D ** -0.5),
        'router_b': nrm((DEPTH, N_EXPERTS), 0.01),
        'moe_w_gate': nrm((DEPTH, N_EXPERTS, D, EXPERT_FF), D ** -0.5),
        'moe_w_up': nrm((DEPTH, N_EXPERTS, D, EXPERT_FF), D ** -0.5),
        'moe_w_down': nrm((DEPTH, N_EXPERTS, EXPERT_FF, D), DEEPNORM_BETA * EXPERT_FF ** -0.5),
        'sh_w_gate': nrm((DEPTH, D, EXPERT_FF), D ** -0.5),
        'sh_w_up': nrm((DEPTH, D, EXPERT_FF), D ** -0.5),
        'sh_w_down': nrm((DEPTH, EXPERT_FF, D), DEEPNORM_BETA * EXPERT_FF ** -0.5),
        'ml_w_in': nrm((nA, D, ml_cols), D ** -0.5),
        'ml_b_if': (nrm((nA, 4, ML_HEADS), 0.1) + f_offset).reshape(nA, 4 * ML_HEADS),
        'ml_norm_g': 1.0 + nrm((nA, D), 0.02),
        'ml_w_out': nrm((nA, D, D), DEEPNORM_BETA * D ** -0.5),
        'da_w_qkv': nrm((nB, D, 3 * D), D ** -0.5),
        'da_lam_q1': nrm((nB, DA_HEAD_DIM), 0.1),
        'da_lam_k1': nrm((nB, DA_HEAD_DIM), 0.1),
        'da_lam_q2': nrm((nB, DA_HEAD_DIM), 0.1),
        'da_lam_k2': nrm((nB, DA_HEAD_DIM), 0.1),
        'da_subln_g': 1.0 + nrm((nB, 2 * DA_HEAD_DIM), 0.02),
        'da_w_out': nrm((nB, D, D), DEEPNORM_BETA * D ** -0.5),
        'hy_w_in': nrm((nC, D, 3 * D), D ** -0.5),
        'hy_conv_w': nrm((nC, 3, 3 * D), 3 ** -0.5),
        'hy_conv_b': nrm((nC, 3 * D), 0.02),
        'hy_f_w1': nrm((nC, HY_EMB, HY_FFN), HY_EMB ** -0.5),
        'hy_f_b1': nrm((nC, HY_FFN), 0.1),
        'hy_f_w2': nrm((nC, HY_FFN, HY_FFN), HY_FFN ** -0.5),
        'hy_f_b2': nrm((nC, HY_FFN), 0.1),
        'hy_f_w3': nrm((nC, HY_FFN, HY_FFN), HY_FFN ** -0.5),
        'hy_f_b3': nrm((nC, HY_FFN), 0.1),
        'hy_f_w4': nrm((nC, HY_FFN, HY_ORDER * 2 * D), HY_FFN ** -0.5),
        'hy_bias': nrm((nC, HY_ORDER, D), 0.1),
        'hy_w_out': nrm((nC, D, D), DEEPNORM_BETA * D ** -0.5),
        'na_w_qkv': nrm((nD, D, 3 * D), D ** -0.5),
        'na_rpb': nrm((nD, NA_HEADS, 2 * NA_ROWS - 1, 2 * NA_COLS - 1), 0.02),
        'na_w_out': nrm((nD, D, D), DEEPNORM_BETA * D ** -0.5),
    }


def reference(x, c, ctx, c_ctx, ada_a, ada_b, ada_bias, ln1_g, ln1_b, ln2_g, ln2_b, router_w, router_b,
              moe_w_gate, moe_w_up, moe_w_down, sh_w_gate, sh_w_up, sh_w_down,
              ml_w_in, ml_b_if, ml_norm_g, ml_w_out,
              da_w_qkv, da_lam_q1, da_lam_k1, da_lam_q2, da_lam_k2, da_subln_g, da_w_out,
              hy_w_in, hy_conv_w, hy_conv_b, hy_f_w1, hy_f_b1, hy_f_w2, hy_f_b2, hy_f_w3, hy_f_b3, hy_f_w4,
              hy_bias, hy_w_out, na_w_qkv, na_rpb, na_w_out):
    C = ctx.shape[1]
    for i in range(DEPTH):
        m, j = i % N_MIXERS, i // N_MIXERS
        need_ctx = i < DEPTH - 1
        mod = (jax.nn.silu(c) @ ada_a[i]) @ ada_b[i] + ada_bias[i]
        mod_c = (jax.nn.silu(c_ctx) @ ada_a[i]) @ ada_b[i] + ada_bias[i]
        sh1, sc1, g1, sh2, sc2, g2 = jnp.split(mod[:, None, :], 6, axis=-1)
        csh1, csc1, cg1, csh2, csc2, cg2 = jnp.split(mod_c, 6, axis=-1)

        h_lat = x * (1.0 + sc1) + sh1
        h_ctx = ctx * (1.0 + csc1) + csh1
        if m == 0:
            y_lat, y_ctx = _mlstm_mixer(h_lat, h_ctx, ml_w_in[j], ml_b_if[j], ml_norm_g[j], ml_w_out[j], need_ctx)
        elif m == 1:
            lambda_init = 0.8 - 0.6 * math.exp(-0.3 * i)
            y_lat, y_ctx = _diff_attn_mixer(h_lat, h_ctx, da_w_qkv[j], da_lam_q1[j], da_lam_k1[j], da_lam_q2[j],
                                            da_lam_k2[j], da_subln_g[j], da_w_out[j], lambda_init, need_ctx)
        elif m == 2:
            y_lat, y_ctx = _hyena_mixer(h_lat, h_ctx, hy_w_in[j], hy_conv_w[j], hy_conv_b[j], hy_f_w1[j], hy_f_b1[j],
                                        hy_f_w2[j], hy_f_b2[j], hy_f_w3[j], hy_f_b3[j], hy_f_w4[j], hy_bias[j],
                                        hy_w_out[j], need_ctx)
        else:
            y_lat, y_ctx = _na_mixer(h_lat, h_ctx, na_w_qkv[j], na_rpb[j], na_w_out[j], need_ctx)
        x = _layer_norm(DEEPNORM_ALPHA * x + g1 * y_lat, ln1_g[i], ln1_b[i])
        if need_ctx:
            ctx = _layer_norm(DEEPNORM_ALPHA * ctx + cg1 * y_ctx, ln1_g[i], ln1_b[i])

        h_lat = x * (1.0 + sc2) + sh2
        moe_args = (router_w[i], router_b[i], moe_w_gate[i], moe_w_up[i], moe_w_down[i],
                    sh_w_gate[i], sh_w_up[i], sh_w_down[i])
        if need_ctx:
            h_ctx = ctx * (1.0 + csc2) + csh2
            y = _moe(jnp.concatenate([h_ctx, h_lat], axis=1), *moe_args)
            ctx = _layer_norm(DEEPNORM_ALPHA * ctx + cg2 * y[:, :C], ln2_g[i], ln2_b[i])
            y_lat = y[:, C:]
        else:
            y_lat = _moe(h_lat, *moe_args)
        x = _layer_norm(DEEPNORM_ALPHA * x + g2 * y_lat, ln2_g[i], ln2_b[i])
    return x
```

```python
import functools
import math

import numpy as np
import jax
import jax.numpy as jnp
from jax import lax
from jax.experimental import pallas as pl
from jax.experimental.pallas import tpu as pltpu

F32 = jnp.float32
BF16 = jnp.bfloat16
HIGHEST = lax.Precision.HIGHEST

V7X_VMEM_LIMIT_BYTES = 56 * 1024 * 1024
LANES = 128

GRID_W = 64
DEPTH = 4
DEEPNORM_ALPHA = (2 * DEPTH) ** 0.25
LN_EPS = 1e-5
RMS_EPS = 1e-6
ROPE_BASE = 10000.0
ML_HEADS = 8
ML_CHUNK = 64
DA_HEAD_DIM = 128
HY_ORDER = 2
HY_EMB = 33
HY_BANDS = (HY_EMB - 1) // 2
HY_FAST_DECAY = 0.3
HY_SLOW_DECAY = 1.5
HY_DECAY_TARGET = 1e-2
NA_HEAD_DIM = 128
NA_ROWS = 8
NA_COLS = 16
N_EXPERTS = 64
TOP_K = 8
N_GROUPS = 8
TOPK_GROUPS = 4
ROUTED_SCALE = 2.5


def _params(*sem):
    return pltpu.CompilerParams(dimension_semantics=sem, vmem_limit_bytes=V7X_VMEM_LIMIT_BYTES)


def _dot(a, b):
    return jnp.dot(a, b, preferred_element_type=F32)


def _dot_nt(a, b):
    return lax.dot_general(a, b, (((1,), (1,)), ((), ())), preferred_element_type=F32)


def _dot_tn(a, b):
    return lax.dot_general(a, b, (((0,), (0,)), ((), ())), preferred_element_type=F32)


def _pick(n, prefs):
    for p in prefs:
        if n % p == 0:
            return p
    return n


def _mm_kernel(*refs, nk, has_extra):
    if has_extra:
        a_ref, w_ref, a2_ref, w2_ref, o_ref, acc_ref = refs
    else:
        a_ref, w_ref, o_ref, acc_ref = refs
    k = pl.program_id(2)

    @pl.when(k == 0)
    def _():
        acc_ref[...] = jnp.zeros_like(acc_ref)

    acc_ref[...] += _dot(a_ref[...].astype(BF16), w_ref[...].astype(BF16))

    @pl.when(k == nk - 1)
    def _():
        acc = acc_ref[...]
        if has_extra:
            acc = acc + _dot(a2_ref[...].astype(BF16), w2_ref[...].astype(BF16))
        o_ref[...] = acc.astype(o_ref.dtype)


def matmul(a, w, *, out_dtype, layer=None, col0=0, ncols=None, extra=None, row0=0, nrows=None):
    M = a.shape[0] if nrows is None else nrows
    K = a.shape[1]
    N = w.shape[-1] if ncols is None else ncols
    tm = _pick(M, (1056, 1024, 512, 256, 128, 64, 32, 16, 8))
    tn = _pick(N, (1024, 512, 256, 128))
    tk = _pick(K, (1024, 512, 256, 128))
    assert col0 % tn == 0 and row0 % tm == 0
    cb, rb = col0 // tn, row0 // tm
    nk = K // tk
    if w.ndim == 3:
        w_spec = pl.BlockSpec((None, tk, tn), lambda i, j, k: (layer, k, j + cb))
    else:
        w_spec = pl.BlockSpec((tk, tn), lambda i, j, k: (k, j + cb))
    in_specs = [pl.BlockSpec((tm, tk), lambda i, j, k: (i + rb, k)), w_spec]
    args = [a, w]
    if extra is not None:
        a2, w2, layer2 = extra
        k2 = a2.shape[1]
        in_specs.append(pl.BlockSpec((tm, k2), lambda i, j, k: (i + rb, 0)))
        in_specs.append(pl.BlockSpec((None, k2, tn), lambda i, j, k: (layer2, 0, j)))
        args += [a2, w2]
    return pl.pallas_call(
        functools.partial(_mm_kernel, nk=nk, has_extra=extra is not None),
        grid=(M // tm, N // tn, nk),
        in_specs=in_specs,
        out_specs=pl.BlockSpec((tm, tn), lambda i, j, k: (i, j)),
        out_shape=jax.ShapeDtypeStruct((M, N), out_dtype),
        scratch_shapes=[pltpu.VMEM((tm, tn), F32)],
        compiler_params=_params("parallel", "parallel", "arbitrary"),
    )(*args)


def _ada_kernel(cc_ref, a_ref, b_ref, bias_ref, o_ref):
    cc = cc_ref[...]
    act = cc * jax.nn.sigmoid(cc)
    t = _dot(act.astype(BF16), a_ref[...].astype(BF16))
    o_ref[...] = _dot(t.astype(BF16), b_ref[...].astype(BF16)) + bias_ref[...]


def ada_modulation(cc, ada_a, ada_b, ada_bias):
    depth, D, R = ada_a.shape
    n6 = ada_b.shape[-1]
    tn = D
    return pl.pallas_call(
        _ada_kernel,
        grid=(depth, n6 // tn),
        in_specs=[
            pl.BlockSpec((8, D), lambda l, n: (0, 0)),
            pl.BlockSpec((None, D, R), lambda l, n: (l, 0, 0)),
            pl.BlockSpec((None, R, tn), lambda l, n: (l, 0, n)),
            pl.BlockSpec((None, 1, tn), lambda l, n: (l, 0, n)),
        ],
        out_specs=pl.BlockSpec((None, 8, tn), lambda l, n: (l, 0, n)),
        out_shape=jax.ShapeDtypeStruct((depth, 8, n6), F32),
        compiler_params=_params("parallel", "parallel"),
    )(cc, ada_a, ada_b, ada_bias.reshape(depth, 1, n6))


def _lnmod_kernel(*refs, has_ln, has_h, n_y):
    it = iter(refs)
    s_ref = next(it)
    if has_ln:
        y_refs = [next(it) for _ in range(n_y)]
        gate_ref, lng_ref, lnb_ref = next(it), next(it), next(it)
    if has_h:
        sc_ref, sh_ref = next(it), next(it)
    if has_ln:
        so_ref = next(it)
    if has_h:
        h_ref = next(it)
    x = s_ref[...]
    if has_ln:
        y = y_refs[0][...].astype(F32)
        for r in y_refs[1:]:
            y = y + r[...].astype(F32)
        v = DEEPNORM_ALPHA * x + gate_ref[...] * y
        mu = jnp.mean(v, axis=-1, keepdims=True)
        var = jnp.mean(jnp.square(v - mu), axis=-1, keepdims=True)
        x = (v - mu) * lax.rsqrt(var + LN_EPS) * lng_ref[...] + lnb_ref[...]
        so_ref[...] = x
    if has_h:
        h_ref[...] = (x * (1.0 + sc_ref[...]) + sh_ref[...]).astype(h_ref.dtype)


def ln_modulate(s, mods, *, n_ctx, ys=(), ln_layer=None, gate_idx=None, ln_g=None, ln_b=None,
                h_layer=None, sc_idx=None, sh_idx=None):
    R, D = s.shape
    tr = 128
    has_ln, has_h = len(ys) > 0, h_layer is not None
    n_ctx_blocks = n_ctx // tr
    row_spec = pl.BlockSpec((tr, D), lambda i: (i, 0))

    def mod_spec(layer, idx):
        return pl.BlockSpec((None, None, None, 1, D),
                            lambda i: (layer, jnp.where(i < n_ctx_blocks, 0, 1), idx, 0, 0))

    vec_spec = pl.BlockSpec((None, 1, D), lambda i: (ln_layer, 0, 0))
    in_specs, args = [row_spec], [s]
    if has_ln:
        in_specs += [row_spec] * len(ys) + [mod_spec(ln_layer, gate_idx), vec_spec, vec_spec]
        args += list(ys) + [mods, ln_g.reshape(DEPTH, 1, D), ln_b.reshape(DEPTH, 1, D)]
    if has_h:
        in_specs += [mod_spec(h_layer, sc_idx), mod_spec(h_layer, sh_idx)]
        args += [mods, mods]
    out_specs, out_shape = [], []
    if has_ln:
        out_specs.append(row_spec)
        out_shape.append(jax.ShapeDtypeStruct((R, D), F32))
    if has_h:
        out_specs.append(row_spec)
        out_shape.append(jax.ShapeDtypeStruct((R, D), BF16))
    outs = pl.pallas_call(
        functools.partial(_lnmod_kernel, has_ln=has_ln, has_h=has_h, n_y=len(ys)),
        grid=(R // tr,),
        in_specs=in_specs,
        out_specs=out_specs,
        out_shape=out_shape,
        compiler_params=_params("parallel"),
    )(*args)
    return outs


def _log_sigmoid(x):
    return jnp.minimum(x, 0.0) - jnp.log(1.0 + jnp.exp(-jnp.abs(x)))


def _mlstm_kernel(bias_ref, q_ref, k_ref, v_ref, ic_ref, fc_ref, ir_ref, fr_ref, o_ref, ct_ref, n_ref, m_ref,
                  *, T, dqk):
    d, h, t = pl.program_id(0), pl.program_id(1), pl.program_id(2)

    @pl.when(t == 0)
    def _():
        ct_ref[...] = jnp.zeros_like(ct_ref)
        n_ref[...] = jnp.zeros_like(n_ref)
        m_ref[...] = jnp.zeros_like(m_ref)

    bi = bias_ref[d * 2 * ML_HEADS + h]
    bf = bias_ref[(d * 2 + 1) * ML_HEADS + h]
    i_c = ic_ref[...] + bi
    i_r = ir_ref[...] + bi
    f_c = _log_sigmoid(fc_ref[...] + bf)
    f_r = _log_sigmoid(fr_ref[...] + bf)
    row = lax.broadcasted_iota(jnp.int32, (T, T), 0)
    col = lax.broadcasted_iota(jnp.int32, (T, T), 1)
    sgn = 1 - 2 * d
    incl = (col - row) * sgn <= 0
    incl_t = (row - col) * sgn <= 0
    b_c = jnp.sum(jnp.where(incl, f_r, 0.0), axis=1, keepdims=True)
    b_r = jnp.sum(jnp.where(incl_t, f_c, 0.0), axis=0, keepdims=True)
    f_tot = jnp.sum(f_r, axis=1, keepdims=True)
    m_prev = m_ref[...]
    dmat = jnp.where(incl, b_c - b_r + i_r, -jnp.inf)
    inter = b_c + m_prev
    m_t = jnp.maximum(inter, jnp.max(dmat, axis=1, keepdims=True))
    w_intra = jnp.exp(dmat - m_t)
    w_inter = jnp.exp(inter - m_t)
    q = q_ref[...] * (dqk ** -0.5)
    k = k_ref[...]
    v = v_ref[...]
    qb, kb, vb = q.astype(BF16), k.astype(BF16), v.astype(BF16)
    s = _dot_nt(qb, kb) * w_intra
    ct = ct_ref[...]
    n = n_ref[...]
    num = w_inter * _dot(qb, ct.astype(BF16)) + _dot(s.astype(BF16), vb)
    den = w_inter * jnp.sum(q * n, axis=1, keepdims=True) + jnp.sum(s, axis=1, keepdims=True)
    o_ref[...] = num / jnp.maximum(jnp.abs(den), jnp.exp(-m_t))
    g_r = f_tot - b_r + i_r
    g_c = f_tot - b_c + i_c
    m_new = jnp.maximum(f_tot + m_prev, jnp.max(g_r, axis=1, keepdims=True))
    decay = jnp.exp(f_tot + m_prev - m_new)
    wk = jnp.exp(g_c - m_new)
    ct_ref[...] = decay * ct + _dot_tn(kb, (wk * v).astype(BF16))
    n_ref[...] = decay * n + jnp.sum(wk * k, axis=0, keepdims=True)
    m_ref[...] = m_new


def mlstm_scan(u, gates, b_if, *, n_ctx):
    NT = u.shape[0]
    H, T = ML_HEADS, ML_CHUNK
    D = u.shape[1] // 3
    dqk, dv = D // (2 * H), D // H
    nc, ncc = NT // T, n_ctx // T
    g4 = gates[:, :4 * H].T
    g_col = g4.reshape(4 * H, NT, 1)
    g_row = g4.reshape(4 * H, nc, 1, T)

    def chunk(d, t):
        bwd = jnp.where(t < ncc, ncc - 1 - t, nc - 1 + ncc - t)
        return jnp.where(d == 0, t, bwd)

    kq, kk, kv = 0, (H * dqk) // dqk, (2 * H * dqk) // dv
    col_spec = lambda off: pl.BlockSpec((None, T, 1), lambda d, h, t: ((2 * d + off) * H + h, chunk(d, t), 0))
    row_spec = lambda off: pl.BlockSpec((None, None, 1, T), lambda d, h, t: ((2 * d + off) * H + h, chunk(d, t), 0, 0))
    return pl.pallas_call(
        functools.partial(_mlstm_kernel, T=T, dqk=dqk),
        grid=(2, H, nc),
        in_specs=[
            pl.BlockSpec(memory_space=pltpu.SMEM),
            pl.BlockSpec((T, dqk), lambda d, h, t: (chunk(d, t), kq + h)),
            pl.BlockSpec((T, dqk), lambda d, h, t: (chunk(d, t), kk + h)),
            pl.BlockSpec((T, dv), lambda d, h, t: (chunk(d, t), kv + h)),
            col_spec(0), col_spec(1), row_spec(0), row_spec(1),
        ],
        out_specs=pl.BlockSpec((None, T, dv), lambda d, h, t: (d, chunk(d, t), h)),
        out_shape=jax.ShapeDtypeStruct((2, NT, H * dv), F32),
        scratch_shapes=[pltpu.VMEM((dqk, dv), F32), pltpu.VMEM((1, dqk), F32), pltpu.VMEM((1, 1), F32)],
        compiler_params=_params("parallel", "parallel", "arbitrary"),
    )(b_if, u, u, u, g_col, g_col, g_row, g_row)


def _mlgate_kernel(hs_ref, o_ref, g_ref, out_ref, *, dv):
    D = out_ref.shape[1]
    for h in range(D // dv):
        sl = slice(h * dv, (h + 1) * dv)
        x = hs_ref[0, :, sl] + hs_ref[1, :, sl]
        r = lax.rsqrt(jnp.mean(jnp.square(x), axis=-1, keepdims=True) + RMS_EPS)
        out_ref[:, sl] = (x * r * g_ref[:, sl] * jax.nn.sigmoid(o_ref[:, sl])).astype(out_ref.dtype)


def mlstm_gate(hs, u, norm_g, *, o_col_block):
    _, NT, D = hs.shape
    tr = 128
    return pl.pallas_call(
        functools.partial(_mlgate_kernel, dv=D // ML_HEADS),
        grid=(NT // tr,),
        in_specs=[
            pl.BlockSpec((2, tr, D), lambda i: (0, i, 0)),
            pl.BlockSpec((tr, D), lambda i: (i, o_col_block)),
            pl.BlockSpec((1, D), lambda i: (0, 0)),
        ],
        out_specs=pl.BlockSpec((tr, D), lambda i: (i, 0)),
        out_shape=jax.ShapeDtypeStruct((NT, D), BF16),
        compiler_params=_params("parallel"),
    )(hs, u, norm_g.reshape(1, D))


def _rope_kernel(x_ref, cos_ref, sin_ref, o_ref):
    cos, sin = cos_ref[...], sin_ref[...]
    lane = lax.broadcasted_iota(jnp.int32, cos.shape, 1)
    first_half = (lane % 64) < 32
    for g in range(x_ref.shape[1] // LANES):
        sl = slice(g * LANES, (g + 1) * LANES)
        x = x_ref[:, sl]
        rot = jnp.where(first_half, -pltpu.roll(x, LANES - 32, 1), pltpu.roll(x, 32, 1))
        o_ref[:, sl] = (x * cos + rot * sin).astype(o_ref.dtype)


def rope_cast(x, cos, sin, *, row0, nrows, col0, ncols):
    tr = _pick(nrows, (256, 128))
    tc = _pick(ncols, (1024, 512, 256, 128))
    rb, cb = row0 // tr, col0 // tc
    assert row0 % tr == 0 and col0 % tc == 0
    return pl.pallas_call(
        _rope_kernel,
        grid=(nrows // tr, ncols // tc),
        in_specs=[
            pl.BlockSpec((tr, tc), lambda i, j: (i + rb, j + cb)),
            pl.BlockSpec((tr, LANES), lambda i, j: (i, 0)),
            pl.BlockSpec((tr, LANES), lambda i, j: (i, 0)),
        ],
        out_specs=pl.BlockSpec((tr, tc), lambda i, j: (i, j)),
        out_shape=jax.ShapeDtypeStruct((nrows, ncols), BF16),
        compiler_params=_params("parallel", "parallel"),
    )(x, cos, sin)


def _dattn_kernel(lam_ref, q_ref, k_ref, v_ref, g_ref, o_ref, m_ref, l_ref, acc_ref, *, tk, n_chunks, out_scale,
                  lambda_init):
    dh = DA_HEAD_DIM
    m_ref[...] = jnp.full_like(m_ref, -jnp.inf)
    l_ref[...] = jnp.zeros_like(l_ref)
    acc_ref[...] = jnp.zeros_like(acc_ref)

    def body(c, carry):
        start = pl.multiple_of(c * tk, tk)
        kc = k_ref[pl.ds(start, tk), :]
        vc = v_ref[pl.ds(start, tk), :]
        for i in range(2):
            s = _dot_nt(q_ref[:, i * dh:(i + 1) * dh], kc[:, i * dh:(i + 1) * dh])
            m_old = m_ref[i]
            m_new = jnp.maximum(m_old, jnp.max(s, axis=1, keepdims=True))
            p = jnp.exp(s - m_new)
            alpha = jnp.exp(m_old - m_new)
            l_ref[i] = alpha * l_ref[i] + jnp.sum(p, axis=1, keepdims=True)
            acc_ref[i] = alpha * acc_ref[i] + _dot(p.astype(BF16), vc)
            m_ref[i] = m_new
        return carry

    lax.fori_loop(0, n_chunks, body, 0)
    lam = (jnp.exp(jnp.sum(lam_ref[0:1, :] * lam_ref[1:2, :], axis=1, keepdims=True))
           - jnp.exp(jnp.sum(lam_ref[2:3, :] * lam_ref[3:4, :], axis=1, keepdims=True)) + lambda_init)
    o = acc_ref[0] / l_ref[0] - lam * (acc_ref[1] / l_ref[1])
    r = lax.rsqrt(jnp.mean(jnp.square(o), axis=-1, keepdims=True) + RMS_EPS)
    o_ref[...] = (o * r * g_ref[...] * out_scale).astype(o_ref.dtype)


def diff_attention(q, k, v, lam4, subln_g, *, kv_len, lambda_init):
    Nq, D = q.shape
    hd = 2 * DA_HEAD_DIM
    H = D // hd
    tq = _pick(Nq, (512, 256))
    tk = _pick(kv_len, (768, 512, 256))
    return pl.pallas_call(
        functools.partial(_dattn_kernel, tk=tk, n_chunks=kv_len // tk, out_scale=1.0 - lambda_init,
                          lambda_init=lambda_init),
        grid=(H, Nq // tq),
        in_specs=[
            pl.BlockSpec((4, DA_HEAD_DIM), lambda h, i: (0, 0)),
            pl.BlockSpec((tq, hd), lambda h, i: (i, h)),
            pl.BlockSpec((kv_len, hd), lambda h, i: (0, h)),
            pl.BlockSpec((kv_len, hd), lambda h, i: (0, h)),
            pl.BlockSpec((1, hd), lambda h, i: (0, 0)),
        ],
        out_specs=pl.BlockSpec((tq, hd), lambda h, i: (i, h)),
        out_shape=jax.ShapeDtypeStruct((Nq, D), BF16),
        scratch_shapes=[pltpu.VMEM((2, tq, 1), F32), pltpu.VMEM((2, tq, 1), F32), pltpu.VMEM((2, tq, hd), F32)],
        compiler_params=_params("parallel", "parallel"),
    )(lam4, q, k, v, subln_g.reshape(1, hd))


def _rope_tables(L, scale):
    a = DA_HEAD_DIM // 2
    t = jnp.arange(L)
    pos = jnp.stack([t // GRID_W, t % GRID_W], -1).astype(F32)
    inv = ROPE_BASE ** (-jnp.arange(0, a, 2, dtype=F32) / a)
    ang = pos[:, :, None] * inv
    ang = jnp.concatenate([ang, ang], -1).reshape(L, DA_HEAD_DIM)
    return jnp.cos(ang) * scale, jnp.sin(ang) * scale


def _shortconv_kernel(p_ref, prev_ref, next_ref, w_ref, b_ref, o_ref, *, n_seq_first, n_seq_last):
    i = pl.program_id(0)
    x = p_ref[...]
    tr = x.shape[0]
    rid = lax.broadcasted_iota(jnp.int32, x.shape, 0)
    is_first = functools.reduce(jnp.logical_or, [i == b for b in n_seq_first])
    is_last = functools.reduce(jnp.logical_or, [i == b for b in n_seq_last])
    prev_row = jnp.where(is_first, 0.0, prev_ref[7:8, :])
    next_row = jnp.where(is_last, 0.0, next_ref[0:1, :])
    xm = jnp.where(rid == 0, prev_row, pltpu.roll(x, 1, 0))
    xp = jnp.where(rid == tr - 1, next_row, pltpu.roll(x, tr - 1, 0))
    o_ref[...] = xm * w_ref[0:1, :] + x * w_ref[1:2, :] + xp * w_ref[2:3, :] + b_ref[...]


def short_conv(p, w, b, *, n_ctx):
    NT, N = p.shape
    tr = 256
    tc = _pick(N, (1024, 512, 256, 128))
    nb, n8 = NT // tr, NT // 8
    firsts = (0, n_ctx // tr)
    lasts = (n_ctx // tr - 1, nb - 1)
    return pl.pallas_call(
        functools.partial(_shortconv_kernel, n_seq_first=firsts, n_seq_last=lasts),
        grid=(nb, N // tc),
        in_specs=[
            pl.BlockSpec((tr, tc), lambda i, j: (i, j)),
            pl.BlockSpec((8, tc), lambda i, j: (jnp.maximum(i * (tr // 8) - 1, 0), j)),
            pl.BlockSpec((8, tc), lambda i, j: (jnp.minimum((i + 1) * (tr // 8), n8 - 1), j)),
            pl.BlockSpec((3, tc), lambda i, j: (0, j)),
            pl.BlockSpec((1, tc), lambda i, j: (0, j)),
        ],
        out_specs=pl.BlockSpec((tr, tc), lambda i, j: (i, j)),
        out_shape=jax.ShapeDtypeStruct((NT, N), F32),
        compiler_params=_params("parallel", "parallel"),
    )(p, p, p, w, b.reshape(1, N))


def _filter_kernel(zf_ref, t_ref, w1_ref, b1_ref, w2_ref, b2_ref, w3_ref, b3_ref, w4f_ref, w4b_ref, dl_ref,
                   o_ref, asum_ref, *, L):
    i = pl.program_id(1)
    hdn = jnp.sin(_dot(zf_ref[...].astype(BF16), w1_ref[...].astype(BF16)) + b1_ref[...])
    hdn = jnp.sin(_dot(hdn.astype(BF16), w2_ref[...].astype(BF16)) + b2_ref[...])
    hdn = jnp.sin(_dot(hdn.astype(BF16), w3_ref[...].astype(BF16)) + b3_ref[...]).astype(BF16)
    hf = _dot(hdn, w4f_ref[...].astype(BF16))
    hb = _dot(hdn, w4b_ref[...].astype(BF16))
    tr = hf.shape[0]
    ridx = i * tr + lax.broadcasted_iota(jnp.int32, hf.shape, 0)
    kern = jnp.where(ridx < L, hf, jnp.where(ridx > L, hb, 0.0)) * jnp.exp(-t_ref[...] * dl_ref[...])
    o_ref[...] = kern

    @pl.when(i == 0)
    def _():
        asum_ref[...] = jnp.zeros_like(asum_ref)

    asum_ref[...] += jnp.sum(jnp.abs(kern), axis=0, keepdims=True)


def hyena_filters(L, D, f_w1, f_b1, f_w2, f_b2, f_w3, f_b3, f_w4):
    t = jnp.linspace(0.0, 1.0, L, dtype=F32)[:, None]
    w = 2.0 * math.pi * jnp.arange(L, dtype=F32)[:, None] / L
    f = jnp.linspace(1e-4, HY_BANDS - 1, HY_BANDS, dtype=F32)[None]
    z = jnp.concatenate([t, jnp.cos(w * f), -jnp.sin(w * f)], -1)
    src = jnp.concatenate([jnp.arange(L), jnp.zeros((1,), jnp.int32), L - 1 - jnp.arange(L - 1)])
    zf = jnp.pad(z[src], ((0, 0), (0, LANES - HY_EMB)))
    tcol = t[src]
    deltas = jnp.abs(jnp.linspace(math.log(HY_DECAY_TARGET) / HY_SLOW_DECAY, math.log(HY_DECAY_TARGET) / HY_FAST_DECAY,
                                  D, dtype=F32)).reshape(1, D)
    w1 = jnp.pad(f_w1, ((0, LANES - HY_EMB), (0, 0)))
    nf = f_w1.shape[1]
    R2 = 2 * L
    tr = _pick(R2, (512,))
    tc = _pick(D, (1024, 512, 256, 128))
    cpd = D // tc
    full = lambda shape: pl.BlockSpec(shape, lambda j, i: (0, 0))
    return pl.pallas_call(
        functools.partial(_filter_kernel, L=L),
        grid=(HY_ORDER * cpd, R2 // tr),
        in_specs=[
            pl.BlockSpec((tr, LANES), lambda j, i: (i, 0)),
            pl.BlockSpec((tr, 1), lambda j, i: (i, 0)),
            full((LANES, nf)), full((1, nf)), full((nf, nf)), full((1, nf)), full((nf, nf)), full((1, nf)),
            pl.BlockSpec((nf, tc), lambda j, i: (0, (j // cpd) * 2 * cpd + j % cpd)),
            pl.BlockSpec((nf, tc), lambda j, i: (0, ((j // cpd) * 2 + 1) * cpd + j % cpd)),
            pl.BlockSpec((1, tc), lambda j, i: (0, j % cpd)),
        ],
        out_specs=[pl.BlockSpec((tr, tc), lambda j, i: (i, j)), pl.BlockSpec((1, tc), lambda j, i: (0, j))],
        out_shape=[jax.ShapeDtypeStruct((R2, HY_ORDER * D), F32), jax.ShapeDtypeStruct((1, HY_ORDER * D), F32)],
        compiler_params=_params("parallel", "arbitrary"),
    )(zf, tcol, w1, f_b1.reshape(1, nf), f_w2, f_b2.reshape(1, nf), f_w3, f_b3.reshape(1, nf), f_w4, f_w4, deltas)


def _hpmm_kernel(*refs, gated):
    if gated:
        a_ref, b_ref, gate_ref, z_ref, inv_ref, bias_ref, o_ref = refs
    else:
        a_ref, b_ref, o_ref = refs
    y = jnp.dot(a_ref[...], b_ref[...], precision=HIGHEST, preferred_element_type=F32)
    if gated:
        y = gate_ref[...] * (y / inv_ref[...] + z_ref[...] * bias_ref[...])
    o_ref[...] = y


def hp_matmul(a, b, *, gate=None, z=None, asum=None, bias=None, chan_period=None, chan_block0=0):
    M, K = a.shape
    N = b.shape[1]
    tn = _pick(N, (2048, 1024, 512, 256, 128))
    in_specs = [pl.BlockSpec((M, K), lambda j: (0, 0)), pl.BlockSpec((K, tn), lambda j: (0, j))]
    args = [a, b]
    gated = gate is not None
    if gated:
        cpd = chan_period // tn
        blk = pl.BlockSpec((M, tn), lambda j: (0, j))
        vec = pl.BlockSpec((1, tn), lambda j: (0, chan_block0 * cpd + j % cpd))
        in_specs += [blk, blk, vec, vec]
        args += [gate, z, asum, bias]
    return pl.pallas_call(
        functools.partial(_hpmm_kernel, gated=gated),
        grid=(N // tn,),
        in_specs=in_specs,
        out_specs=pl.BlockSpec((M, tn), lambda j: (0, j)),
        out_shape=jax.ShapeDtypeStruct((M, N), F32),
        compiler_params=_params("parallel"),
    )(*args)


def _fft_mid_kernel(*refs, g, conv):
    if conv:
        a_ref, kf_ref, twc_ref, tws_ref, w2_ref, w2i_ref, o_ref = refs
    else:
        a_ref, twc_ref, tws_ref, w2_ref, o_ref = refs
    n2 = a_ref.shape[2]
    for kk in range(g):
        ar, ai = a_ref[0, kk], a_ref[1, kk]
        c, s = twc_ref[kk], tws_ref[kk]
        x = jnp.concatenate([ar * c + ai * s, ai * c - ar * s], axis=0)
        x = jnp.dot(w2_ref[...], x, precision=HIGHEST, preferred_element_type=F32)
        if conv:
            xr, xi = x[:n2], x[n2:]
            kr, ki = kf_ref[0, kk], kf_ref[1, kk]
            y = jnp.concatenate([xr * kr - xi * ki, xr * ki + xi * kr], axis=0)
            y = jnp.dot(w2i_ref[...], y, precision=HIGHEST, preferred_element_type=F32)
            yr, yi = y[:n2], y[n2:]
            o_ref[0, kk] = yr * c - yi * s
            o_ref[1, kk] = yr * s + yi * c
        else:
            o_ref[0, kk] = x[:n2]
            o_ref[1, kk] = x[n2:]


def fft_mid(a, tw_c, tw_s, w2, w2i=None, kf=None, *, kf_col0=0):
    _, n1, n2, C = a.shape
    g = 4
    tc = _pick(C, (512, 256, 128))
    conv = kf is not None
    blk = pl.BlockSpec((2, g, n2, tc), lambda j, k: (0, k, 0, j))
    tw = pl.BlockSpec((g, n2, 1), lambda j, k: (k, 0, 0))
    mat = pl.BlockSpec((2 * n2, 2 * n2), lambda j, k: (0, 0))
    if conv:
        kb = kf_col0 // tc
        in_specs = [blk, pl.BlockSpec((2, g, n2, tc), lambda j, k: (0, k, 0, j + kb)), tw, tw, mat, mat]
        args = [a, kf, tw_c, tw_s, w2, w2i]
    else:
        in_specs = [blk, tw, tw, mat]
        args = [a, tw_c, tw_s, w2]
    return pl.pallas_call(
        functools.partial(_fft_mid_kernel, g=g, conv=conv),
        grid=(C // tc, n1 // g),
        in_specs=in_specs,
        out_specs=blk,
        out_shape=jax.ShapeDtypeStruct(a.shape, F32),
        compiler_params=_params("parallel", "parallel"),
    )(*args)


def _cmul_kernel(x_ref, k_ref, o_ref):
    xr, xi, kr, ki = x_ref[0], x_ref[1], k_ref[0], k_ref[1]
    o_ref[0] = xr * kr - xi * ki
    o_ref[1] = xr * ki + xi * kr


def complex_mul(x, kf, *, kf_col0):
    _, R, C = x.shape
    tc = _pick(C, (512, 256, 128))
    kb = kf_col0 // tc
    return pl.pallas_call(
        _cmul_kernel,
        grid=(C // tc,),
        in_specs=[pl.BlockSpec((2, R, tc), lambda j: (0, 0, j)), pl.BlockSpec((2, R, tc), lambda j: (0, 0, j + kb))],
        out_specs=pl.BlockSpec((2, R, tc), lambda j: (0, 0, j)),
        out_shape=jax.ShapeDtypeStruct(x.shape, F32),
        compiler_params=_params("parallel"),
    )(x, kf)


def _dft_constants(n1, n2, l1):
    N = n1 * n2
    k1 = np.arange(n1)[:, None]
    ang1 = 2 * np.pi * k1 * np.arange(n1)[None, :] / n1
    w1 = np.concatenate([np.cos(ang1), -np.sin(ang1)], 0)
    w1_inv = np.concatenate([np.cos(ang1[:l1]), -np.sin(ang1[:l1])], 1) / N
    ang_t = 2 * np.pi * k1 * np.arange(n2)[None, :] / N
    k2 = np.arange(n2)[:, None]
    ang2 = 2 * np.pi * k2 * np.arange(n2)[None, :] / n2
    c2, s2 = np.cos(ang2), np.sin(ang2)
    w2 = np.block([[c2, s2], [-s2, c2]])
    w2_inv = np.block([[c2, -s2], [s2, c2]])
    f = lambda x: jnp.asarray(x, F32)
    return dict(w1_data=f(w1[:, :l1]), w1_full=f(w1), w1_inv=f(w1_inv), tw_c=f(np.cos(ang_t))[:, :, None],
                tw_s=f(np.sin(ang_t))[:, :, None], w2=f(w2), w2_inv=f(w2_inv))


def _dense_dft_constants(L):
    N = 2 * L
    ang = 2 * np.pi * np.arange(N)[:, None] * np.arange(N)[None, :] / N
    wf = np.concatenate([np.cos(ang), -np.sin(ang)], 0)
    wi = np.concatenate([np.cos(ang[:L]), -np.sin(ang[:L])], 1) / N
    return jnp.asarray(wf, F32), jnp.asarray(wi, F32)


def hyena_long_conv_pair(v, x1, x2, kern, asum, hy_bias, *, two_stage):
    L, D = v.shape
    gates = (x1, x2)
    bias = hy_bias.reshape(1, HY_ORDER * D)
    if two_stage:
        n2 = 128
        n1, l1 = 2 * L // n2, L // n2
        cst = _dft_constants(n1, n2, l1)
        ka = hp_matmul(cst["w1_full"], kern.reshape(n1, n2 * HY_ORDER * D))
        kf = fft_mid(ka.reshape(2, n1, n2, HY_ORDER * D), cst["tw_c"], cst["tw_s"], cst["w2"])
        z = v
        for o in range(HY_ORDER):
            a = hp_matmul(cst["w1_data"], z.reshape(l1, n2 * D))
            bmid = fft_mid(a.reshape(2, n1, n2, D), cst["tw_c"], cst["tw_s"], cst["w2"], cst["w2_inv"], kf,
                           kf_col0=o * D)
            z = hp_matmul(cst["w1_inv"], bmid.reshape(2 * n1, n2 * D), gate=gates[o].reshape(l1, n2 * D),
                          z=z.reshape(l1, n2 * D), asum=asum, bias=bias, chan_period=D,
                          chan_block0=o).reshape(L, D)
        return z
    wf, wi = _dense_dft_constants(L)
    kf = hp_matmul(wf, kern).reshape(2, 2 * L, HY_ORDER * D)
    z = v
    for o in range(HY_ORDER):
        xf = hp_matmul(wf[:, :L], z).reshape(2, 2 * L, D)
        y = complex_mul(xf, kf, kf_col0=o * D).reshape(4 * L, D)
        z = hp_matmul(wi, y, gate=gates[o], z=z, asum=asum, bias=bias, chan_period=D, chan_block0=o)
    return z


def _na_kernel(q_ref, k_ref, v_ref, bias_ref, o_ref, *, n_ctx, rows, heads_per_step):
    r = pl.program_id(1)
    kr = NA_ROWS
    r0 = jnp.clip(r - kr // 2, 0, rows - kr)
    start = pl.multiple_of(n_ctx + r0 * GRID_W, GRID_W)
    dh = NA_HEAD_DIM
    scale = dh ** -0.5
    for hh in range(heads_per_step):
        sl = slice(hh * dh, (hh + 1) * dh)
        q = q_ref[:, sl]
        s_ctx = _dot_nt(q, k_ref[0:n_ctx, sl]) * scale
        s_lat = _dot_nt(q, k_ref[pl.ds(start, kr * GRID_W), sl]) * scale + bias_ref[hh]
        m = jnp.maximum(jnp.max(s_ctx, axis=1, keepdims=True), jnp.max(s_lat, axis=1, keepdims=True))
        p_ctx = jnp.exp(s_ctx - m)
        p_lat = jnp.exp(s_lat - m)
        l = jnp.sum(p_ctx, axis=1, keepdims=True) + jnp.sum(p_lat, axis=1, keepdims=True)
        o = _dot(p_ctx.astype(BF16), v_ref[0:n_ctx, sl]) + _dot(p_lat.astype(BF16), v_ref[pl.ds(start, kr * GRID_W), sl])
        o_ref[:, sl] = (o / l).astype(o_ref.dtype)


def _na_bias_table(rpb, rows):
    H = rpb.shape[0]
    cols = jnp.arange(GRID_W)
    c0 = jnp.clip(cols - NA_COLS // 2, 0, GRID_W - NA_COLS)
    kc = jnp.arange(GRID_W)[None, :]
    inside = (kc >= c0[:, None]) & (kc < c0[:, None] + NA_COLS)
    rel = jnp.clip(kc - cols[:, None] + NA_COLS - 1, 0, 2 * NA_COLS - 2)
    rr = jnp.arange(NA_ROWS)[:, None] + jnp.arange(NA_ROWS)[None, :]
    t = rpb[:, rr]
    t = t[:, :, :, rel]
    t = jnp.where(inside[None, None, None], t, -1e30)
    t = jnp.transpose(t, (1, 0, 3, 2, 4))
    return t.reshape(NA_ROWS, H, GRID_W, NA_ROWS * GRID_W)


def na_attention(qkv, rpb, *, n_ctx):
    NT, D3 = qkv.shape
    D = D3 // 3
    L = NT - n_ctx
    rows = L // GRID_W
    hps = 2
    hw = hps * NA_HEAD_DIM
    nhb = D // hw
    table = _na_bias_table(rpb, rows)

    def pattern(r):
        r0 = jnp.clip(r - NA_ROWS // 2, 0, rows - NA_ROWS)
        return r0 - r + NA_ROWS - 1

    return pl.pallas_call(
        functools.partial(_na_kernel, n_ctx=n_ctx, rows=rows, heads_per_step=hps),
        grid=(nhb, rows),
        in_specs=[
            pl.BlockSpec((GRID_W, hw), lambda h, r: (r + n_ctx // GRID_W, h)),
            pl.BlockSpec((NT, hw), lambda h, r: (0, nhb + h)),
            pl.BlockSpec((NT, hw), lambda h, r: (0, 2 * nhb + h)),
            pl.BlockSpec((None, hps, GRID_W, NA_ROWS * GRID_W), lambda h, r: (pattern(r), h, 0, 0)),
        ],
        out_specs=pl.BlockSpec((GRID_W, hw), lambda h, r: (r, h)),
        out_shape=jax.ShapeDtypeStruct((L, D), BF16),
        compiler_params=_params("parallel", "parallel"),
    )(qkv, qkv, qkv, table)


def _router_kernel(x_ref, w_ref, b_ref, o_ref):
    E, G = N_EXPERTS, N_GROUPS
    per = E // G
    logits = _dot_nt(w_ref[...].astype(BF16), x_ref[...])
    scores = jax.nn.sigmoid(logits)
    sel = scores + b_ref[...]
    tm = sel.shape[1]
    grp = sel.reshape(G, per, tm)
    eidx = lax.broadcasted_iota(jnp.int32, grp.shape, 1)
    m1 = jnp.max(grp, axis=1, keepdims=True)
    first = jnp.min(jnp.where(grp == m1, eidx, per), axis=1, keepdims=True)
    m2 = jnp.max(jnp.where(eidx == first, -jnp.inf, grp), axis=1, keepdims=True)
    gs = (m1 + m2).reshape(G, tm)
    gidx = lax.broadcasted_iota(jnp.int32, gs.shape, 0)
    gmask = jnp.zeros(gs.shape, jnp.bool_)
    for _ in range(TOPK_GROUPS):
        mx = jnp.max(gs, axis=0, keepdims=True)
        pick = gidx == jnp.min(jnp.where(gs == mx, gidx, G), axis=0, keepdims=True)
        gmask = jnp.logical_or(gmask, pick)
        gs = jnp.where(pick, -jnp.inf, gs)
    emask = jnp.broadcast_to(gmask.reshape(G, 1, tm), (G, per, tm)).reshape(E, tm)
    cand = jnp.where(emask, sel, -jnp.inf)
    xidx = lax.broadcasted_iota(jnp.int32, cand.shape, 0)
    chosen = jnp.zeros(cand.shape, jnp.bool_)
    for _ in range(TOP_K):
        mx = jnp.max(cand, axis=0, keepdims=True)
        pick = xidx == jnp.min(jnp.where(cand == mx, xidx, E), axis=0, keepdims=True)
        chosen = jnp.logical_or(chosen, pick)
        cand = jnp.where(pick, -jnp.inf, cand)
    wsel = jnp.where(chosen, scores, 0.0)
    gates = wsel / jnp.sum(wsel, axis=0, keepdims=True) * ROUTED_SCALE
    o_ref[...] = gates.T


def moe_router(h, router_w_t, router_b, layer):
    T, D = h.shape
    E = router_w_t.shape[1]
    tm = _pick(T, (256, 128))
    return pl.pallas_call(
        _router_kernel,
        grid=(T // tm,),
        in_specs=[
            pl.BlockSpec((tm, D), lambda i: (i, 0)),
            pl.BlockSpec((None, E, D), lambda i: (layer, 0, 0)),
            pl.BlockSpec((None, E, 1), lambda i: (layer, 0, 0)),
        ],
        out_specs=pl.BlockSpec((tm, E), lambda i: (i, 0)),
        out_shape=jax.ShapeDtypeStruct((T, E), F32),
        compiler_params=_params("parallel"),
    )(h, router_w_t, router_b.reshape(-1, E, 1))


def _moe_up_kernel(*refs, nk, ne, gated):
    if gated:
        x_ref, wg_ref, wu_ref, g_ref, o_ref, accg_ref, accu_ref = refs
    else:
        x_ref, wg_ref, wu_ref, o_ref, accg_ref, accu_ref = refs
    e0 = pl.program_id(1) * ne
    k = pl.program_id(2)

    @pl.when(k == 0)
    def _():
        accg_ref[...] = jnp.zeros_like(accg_ref)
        accu_ref[...] = jnp.zeros_like(accu_ref)

    x = x_ref[...]
    for ee in range(ne):
        accg_ref[ee] += _dot(x, wg_ref[ee].astype(BF16))
        accu_ref[ee] += _dot(x, wu_ref[ee].astype(BF16))

    @pl.when(k == nk - 1)
    def _():
        outs = []
        for ee in range(ne):
            a = accg_ref[ee]
            hid = a * jax.nn.sigmoid(a) * accu_ref[ee]
            if gated:
                g = g_ref[...]
                lane = lax.broadcasted_iota(jnp.int32, g.shape, 1)
                hid = hid * jnp.sum(jnp.where(lane == e0 + ee, g, 0.0), axis=1, keepdims=True)
            outs.append(hid)
        o_ref[...] = (outs[0] if ne == 1 else jnp.concatenate(outs, axis=1)).astype(o_ref.dtype)


def moe_up(x, w_gate, w_up, layer, gates=None):
    T, D = x.shape
    if w_gate.ndim == 3:
        w_gate, w_up = w_gate[:, None], w_up[:, None]
    E, F = w_gate.shape[1], w_gate.shape[3]
    ne = 2 if E % 2 == 0 else 1
    tm = _pick(T, (1056, 1024, 512, 256, 128))
    tk = _pick(D, (1024, 512, 256, 128))
    nk = D // tk
    gated = gates is not None
    w_spec = pl.BlockSpec((None, ne, tk, F), lambda i, e, k: (layer, e, k, 0))
    in_specs = [pl.BlockSpec((tm, tk), lambda i, e, k: (i, k)), w_spec, w_spec]
    args = [x, w_gate, w_up]
    if gated:
        in_specs.append(pl.BlockSpec((tm, E), lambda i, e, k: (i, 0)))
        args.append(gates)
    return pl.pallas_call(
        functools.partial(_moe_up_kernel, nk=nk, ne=ne, gated=gated),
        grid=(T // tm, E // ne, nk),
        in_specs=in_specs,
        out_specs=pl.BlockSpec((tm, ne * F), lambda i, e, k: (i, e)),
        out_shape=jax.ShapeDtypeStruct((T, E * F), BF16),
        scratch_shapes=[pltpu.VMEM((ne, tm, F), F32), pltpu.VMEM((ne, tm, F), F32)],
        compiler_params=_params("parallel", "parallel", "arbitrary"),
    )(*args)


def moe(h, layer, router_w_t, router_b, w_gate, w_up, w_down, s_gate, s_up, s_down):
    gates = moe_router(h, router_w_t, router_b, layer)
    hid = moe_up(h, w_gate, w_up, layer, gates)
    hid_sh = moe_up(h, s_gate, s_up, layer)
    E, F, D = w_down.shape[1:]
    return matmul(hid, w_down.reshape(DEPTH, E * F, D), out_dtype=F32, layer=layer, extra=(hid_sh, s_down, layer))


def kernel(x, c, ctx, c_ctx, ada_a, ada_b, ada_bias, ln1_g, ln1_b, ln2_g, ln2_b, router_w, router_b, moe_w_gate, moe_w_up, moe_w_down, sh_w_gate, sh_w_up, sh_w_down, ml_w_in, ml_b_if, ml_norm_g, ml_w_out, da_w_qkv, da_lam_q1, da_lam_k1, da_lam_q2, da_lam_k2, da_subln_g, da_w_out, hy_w_in, hy_conv_w, hy_conv_b, hy_f_w1, hy_f_b1, hy_f_w2, hy_f_b2, hy_f_w3, hy_f_b3, hy_f_w4, hy_bias, hy_w_out, na_w_qkv, na_rpb, na_w_out):
    _, L, D = x.shape
    C = ctx.shape[1]
    NT = C + L
    s = jnp.concatenate([ctx[0], x[0]], axis=0)

    cc = jnp.zeros((8, D), F32).at[0].set(c_ctx).at[1].set(c[0])
    mods = ada_modulation(cc, ada_a, ada_b, ada_bias)[:, :2].reshape(DEPTH, 2, 6, 1, D)
    SH1, SC1, G1, SH2, SC2, G2 = range(6)
    router_w_t = jnp.swapaxes(router_w, 1, 2)
    moe_args = (router_w_t, router_b, moe_w_gate, moe_w_up, moe_w_down, sh_w_gate, sh_w_up, sh_w_down)

    (h,) = ln_modulate(s, mods, n_ctx=C, h_layer=0, sc_idx=SC1, sh_idx=SH1)
    for i in range(DEPTH):
        last = i == DEPTH - 1
        if i == 0:
            y = mlstm_mixer(h, ml_w_in, ml_b_if[0], ml_norm_g[0], ml_w_out, n_ctx=C)
        elif i == 1:
            lam4 = jnp.stack([da_lam_q1[0], da_lam_k1[0], da_lam_q2[0], da_lam_k2[0]])
            y = diff_attn_mixer(h, da_w_qkv, lam4, da_subln_g[0], da_w_out, n_ctx=C,
                                lambda_init=0.8 - 0.6 * math.exp(-0.3 * i))
        elif i == 2:
            y = hyena_mixer(h, hy_w_in, hy_conv_w[0], hy_conv_b[0], hy_f_w1[0], hy_f_b1[0], hy_f_w2[0], hy_f_b2[0],
                            hy_f_w3[0], hy_f_b3[0], hy_f_w4[0], hy_bias[0], hy_w_out, n_ctx=C)
        else:
            y = na_mixer(h, na_w_qkv, na_rpb[0], na_w_out, n_ctx=C)
        n_ctx = C
        if last:
            s, n_ctx = s[C:], 0
        s, h = ln_modulate(s, mods, n_ctx=n_ctx, ys=(y,), ln_layer=i, gate_idx=G1, ln_g=ln1_g, ln_b=ln1_b,
                           h_layer=i, sc_idx=SC2, sh_idx=SH2)
        y = moe(h, i, *moe_args)
        if last:
            (s,) = ln_modulate(s, mods, n_ctx=n_ctx, ys=(y,), ln_layer=i, gate_idx=G2, ln_g=ln2_g, ln_b=ln2_b)
        else:
            s, h = ln_modulate(s, mods, n_ctx=n_ctx, ys=(y,), ln_layer=i, gate_idx=G2, ln_g=ln2_g, ln_b=ln2_b,
                               h_layer=i + 1, sc_idx=SC1, sh_idx=SH1)
    return s[None]


def mlstm_mixer(h, w_in, b_if, norm_g, w_out, *, n_ctx):
    D = h.shape[1]
    u = matmul(h, w_in, out_dtype=F32, layer=0, ncols=3 * D)
    w_if = jnp.pad(w_in[0, :, 3 * D:], ((0, 0), (0, LANES - 4 * ML_HEADS)))
    g_if = matmul(h, w_if, out_dtype=F32)
    hs = mlstm_scan(u, g_if, b_if, n_ctx=n_ctx)
    return matmul(mlstm_gate(hs, u, norm_g, o_col_block=2), w_out, out_dtype=F32, layer=0)


def diff_attn_mixer(h, w_qkv, lam4, subln_g, w_out, *, n_ctx, lambda_init):
    NT, D = h.shape
    L = NT - n_ctx
    scale = DA_HEAD_DIM ** -0.5
    qk = matmul(h, w_qkv, out_dtype=F32, layer=0, ncols=2 * D)
    v = matmul(h, w_qkv, out_dtype=BF16, layer=0, col0=2 * D, ncols=D)
    cos, sin = _rope_tables(L, 1.0)
    ones, zeros = jnp.ones((n_ctx, LANES), F32), jnp.zeros((n_ctx, LANES), F32)
    q_lat = rope_cast(qk, cos * scale, sin * scale, row0=n_ctx, nrows=L, col0=0, ncols=D)
    q_ctx = rope_cast(qk, ones * scale, zeros, row0=0, nrows=n_ctx, col0=0, ncols=D)
    k = rope_cast(qk, jnp.concatenate([ones, cos]), jnp.concatenate([zeros, sin]), row0=0, nrows=NT, col0=D, ncols=D)
    o_lat = diff_attention(q_lat, k, v, lam4, subln_g, kv_len=NT, lambda_init=lambda_init)
    o_ctx = diff_attention(q_ctx, k, v, lam4, subln_g, kv_len=n_ctx, lambda_init=lambda_init)
    return matmul(jnp.concatenate([o_ctx, o_lat]), w_out, out_dtype=F32, layer=0)


def hyena_mixer(h, w_in, conv_w, conv_b, f_w1, f_b1, f_w2, f_b2, f_w3, f_b3, f_w4, hy_bias, w_out, *, n_ctx):
    NT, D = h.shape
    L = NT - n_ctx
    u = short_conv(matmul(h, w_in, out_dtype=F32, layer=0), conv_w, conv_b, n_ctx=n_ctx)
    fargs = (f_w1, f_b1, f_w2, f_b2, f_w3, f_b3, f_w4)
    zs = []
    for r0, n, two_stage in ((0, n_ctx, False), (n_ctx, L, True)):
        v, x1, x2 = (u[r0:r0 + n, j * D:(j + 1) * D] for j in range(3))
        kern, asum = hyena_filters(n, D, *fargs)
        zs.append(hyena_long_conv_pair(v, x1, x2, kern, asum, hy_bias, two_stage=two_stage))
    return matmul(jnp.concatenate(zs), w_out, out_dtype=F32, layer=0)


def na_mixer(h, w_qkv, rpb, w_out, *, n_ctx):
    qkv = matmul(h, w_qkv, out_dtype=BF16, layer=0)
    return matmul(na_attention(qkv, rpb, n_ctx=n_ctx), w_out, out_dtype=F32, layer=0)
```

```python
import functools
import math

import numpy as np
import jax
import jax.numpy as jnp
from jax import lax
from jax.experimental import pallas as pl
from jax.experimental.pallas import tpu as pltpu

F32 = jnp.float32
BF16 = jnp.bfloat16
HIGHEST = lax.Precision.HIGHEST

V7X_VMEM_LIMIT_BYTES = 56 * 1024 * 1024
LANES = 128

GRID_W = 64
DEPTH = 4
DEEPNORM_ALPHA = (2 * DEPTH) ** 0.25
LN_EPS = 1e-5
RMS_EPS = 1e-6
ROPE_BASE = 10000.0
ML_HEADS = 8
ML_CHUNK = 256
DA_HEAD_DIM = 128
HY_ORDER = 2
HY_EMB = 33
HY_BANDS = (HY_EMB - 1) // 2
HY_FAST_DECAY = 0.3
HY_SLOW_DECAY = 1.5
HY_DECAY_TARGET = 1e-2
NA_HEAD_DIM = 128
NA_ROWS = 8
NA_COLS = 16
N_EXPERTS = 64
TOP_K = 8
N_GROUPS = 8
TOPK_GROUPS = 4
ROUTED_SCALE = 2.5


def _params(*sem):
    return pltpu.CompilerParams(dimension_semantics=sem, vmem_limit_bytes=V7X_VMEM_LIMIT_BYTES)


def _dot(a, b):
    return jnp.dot(a, b, preferred_element_type=F32)


def _dot_nt(a, b):
    return lax.dot_general(a, b, (((1,), (1,)), ((), ())), preferred_element_type=F32)


def _dot_tn(a, b):
    return lax.dot_general(a, b, (((0,), (0,)), ((), ())), preferred_element_type=F32)


def _pick(n, prefs):
    for p in prefs:
        if n % p == 0:
            return p
    return n


def _mm_kernel(*refs, nk, has_extra):
    if has_extra:
        a_ref, w_ref, a2_ref, w2_ref, o_ref, acc_ref = refs
    else:
        a_ref, w_ref, o_ref, acc_ref = refs
    k = pl.program_id(2)

    @pl.when(k == 0)
    def _():
        acc_ref[...] = jnp.zeros_like(acc_ref)

    acc_ref[...] += _dot(a_ref[...].astype(BF16), w_ref[...].astype(BF16))

    @pl.when(k == nk - 1)
    def _():
        acc = acc_ref[...]
        if has_extra:
            acc = acc + _dot(a2_ref[...].astype(BF16), w2_ref[...].astype(BF16))
        o_ref[...] = acc.astype(o_ref.dtype)


def _mm_resident_kernel(a_ref, w_ref, o_ref):
    o_ref[...] = _dot(a_ref[...], w_ref[...].astype(BF16)).astype(o_ref.dtype)


def _matmul_resident(a, w, *, out_dtype, layer, col0, N):
    M, K = a.shape
    tm = _pick(M, (2112, 2048, 1024, 512, 256, 128))
    tn = _pick(N, (256, 128))
    cb = col0 // tn
    if w.ndim == 3:
        w_spec = pl.BlockSpec((None, K, tn), lambda i, j: (layer, 0, j + cb))
    else:
        w_spec = pl.BlockSpec((K, tn), lambda i, j: (0, j + cb))
    return pl.pallas_call(
        _mm_resident_kernel,
        grid=(M // tm, N // tn),
        in_specs=[pl.BlockSpec((tm, K), lambda i, j: (i, 0), pipeline_mode=pl.Buffered(1)), w_spec],
        out_specs=pl.BlockSpec((tm, tn), lambda i, j: (i, j)),
        out_shape=jax.ShapeDtypeStruct((M, N), out_dtype),
        compiler_params=_params("parallel", "arbitrary"),
    )(a, w)


def matmul(a, w, *, out_dtype, layer=None, col0=0, ncols=None, extra=None, row0=0, nrows=None):
    M = a.shape[0] if nrows is None else nrows
    K = a.shape[1]
    N = w.shape[-1] if ncols is None else ncols
    if a.dtype == BF16 and K <= 4096 and extra is None and nrows is None and M >= 1024:
        return _matmul_resident(a, w, out_dtype=out_dtype, layer=layer, col0=col0, N=N)
    tm = _pick(M, (1056, 1024, 512, 256, 128, 64, 32, 16, 8))
    tn = _pick(N, (1024, 512, 256, 128))
    tk = _pick(K, (2048, 1024, 512, 256, 128))
    assert col0 % tn == 0 and row0 % tm == 0
    cb, rb = col0 // tn, row0 // tm
    nk = K // tk
    if w.ndim == 3:
        w_spec = pl.BlockSpec((None, tk, tn), lambda i, j, k: (layer, k, j + cb))
    else:
        w_spec = pl.BlockSpec((tk, tn), lambda i, j, k: (k, j + cb))
    in_specs = [pl.BlockSpec((tm, tk), lambda i, j, k: (i + rb, k)), w_spec]
    args = [a, w]
    if extra is not None:
        a2, w2, layer2 = extra
        k2 = a2.shape[1]
        in_specs.append(pl.BlockSpec((tm, k2), lambda i, j, k: (i + rb, 0)))
        in_specs.append(pl.BlockSpec((None, k2, tn), lambda i, j, k: (layer2, 0, j)))
        args += [a2, w2]
    return pl.pallas_call(
        functools.partial(_mm_kernel, nk=nk, has_extra=extra is not None),
        grid=(M // tm, N // tn, nk),
        in_specs=in_specs,
        out_specs=pl.BlockSpec((tm, tn), lambda i, j, k: (i, j)),
        out_shape=jax.ShapeDtypeStruct((M, N), out_dtype),
        scratch_shapes=[pltpu.VMEM((tm, tn), F32)],
        compiler_params=_params("parallel", "parallel", "arbitrary"),
    )(*args)


def _ada_kernel(cc_ref, a_ref, b_ref, bias_ref, o_ref):
    cc = cc_ref[...]
    act = cc * jax.nn.sigmoid(cc)
    t = _dot(act.astype(BF16), a_ref[...].astype(BF16))
    o_ref[...] = _dot(t.astype(BF16), b_ref[...].astype(BF16)) + bias_ref[...]


def ada_modulation(cc, ada_a, ada_b, ada_bias):
    depth, D, R = ada_a.shape
    n6 = ada_b.shape[-1]
    tn = D
    return pl.pallas_call(
        _ada_kernel,
        grid=(depth, n6 // tn),
        in_specs=[
            pl.BlockSpec((8, D), lambda l, n: (0, 0)),
            pl.BlockSpec((None, D, R), lambda l, n: (l, 0, 0)),
            pl.BlockSpec((None, R, tn), lambda l, n: (l, 0, n)),
            pl.BlockSpec((None, 1, tn), lambda l, n: (l, 0, n)),
        ],
        out_specs=pl.BlockSpec((None, 8, tn), lambda l, n: (l, 0, n)),
        out_shape=jax.ShapeDtypeStruct((depth, 8, n6), F32),
        compiler_params=_params("parallel", "parallel"),
    )(cc, ada_a, ada_b, ada_bias.reshape(depth, 1, n6))


def _lnmod_kernel(*refs, has_ln, has_h, n_y):
    it = iter(refs)
    s_ref = next(it)
    if has_ln:
        y_refs = [next(it) for _ in range(n_y)]
        gate_ref, lng_ref, lnb_ref = next(it), next(it), next(it)
    if has_h:
        sc_ref, sh_ref = next(it), next(it)
    if has_ln:
        so_ref = next(it)
    if has_h:
        h_ref = next(it)
    x = s_ref[...]
    if has_ln:
        y = y_refs[0][...].astype(F32)
        for r in y_refs[1:]:
            y = y + r[...].astype(F32)
        v = DEEPNORM_ALPHA * x + gate_ref[...] * y
        mu = jnp.mean(v, axis=-1, keepdims=True)
        var = jnp.mean(jnp.square(v - mu), axis=-1, keepdims=True)
        x = (v - mu) * lax.rsqrt(var + LN_EPS) * lng_ref[...] + lnb_ref[...]
        so_ref[...] = x
    if has_h:
        h_ref[...] = (x * (1.0 + sc_ref[...]) + sh_ref[...]).astype(h_ref.dtype)


def ln_modulate(s, mods, *, n_ctx, ys=(), ln_layer=None, gate_idx=None, ln_g=None, ln_b=None,
                h_layer=None, sc_idx=None, sh_idx=None):
    R, D = s.shape
    tr = 128
    has_ln, has_h = len(ys) > 0, h_layer is not None
    n_lat_blocks = (R - n_ctx) // tr
    row_spec = pl.BlockSpec((tr, D), lambda i: (i, 0))

    def mod_spec(layer, idx):
        return pl.BlockSpec((None, None, None, 1, D),
                            lambda i: (layer, jnp.where(i >= n_lat_blocks, 0, 1), idx, 0, 0))

    vec_spec = pl.BlockSpec((None, 1, D), lambda i: (ln_layer, 0, 0))
    in_specs, args = [row_spec], [s]
    if has_ln:
        in_specs += [row_spec] * len(ys) + [mod_spec(ln_layer, gate_idx), vec_spec, vec_spec]
        args += list(ys) + [mods, ln_g.reshape(DEPTH, 1, D), ln_b.reshape(DEPTH, 1, D)]
    if has_h:
        in_specs += [mod_spec(h_layer, sc_idx), mod_spec(h_layer, sh_idx)]
        args += [mods, mods]
    out_specs, out_shape = [], []
    if has_ln:
        out_specs.append(row_spec)
        out_shape.append(jax.ShapeDtypeStruct((R, D), F32))
    if has_h:
        out_specs.append(row_spec)
        out_shape.append(jax.ShapeDtypeStruct((R, D), BF16))
    outs = pl.pallas_call(
        functools.partial(_lnmod_kernel, has_ln=has_ln, has_h=has_h, n_y=len(ys)),
        grid=(R // tr,),
        in_specs=in_specs,
        out_specs=out_specs,
        out_shape=out_shape,
        compiler_params=_params("parallel"),
    )(*args)
    return outs


def _log_sigmoid(x):
    return jnp.minimum(x, 0.0) - jnp.log(1.0 + jnp.exp(-jnp.abs(x)))


def _mlstm_kernel(bias_ref, q_ref, k_ref, v_ref, ic_ref, fc_ref, ir_ref, fr_ref, o_ref, ct_ref, n_ref, m_ref,
                  *, T, dqk):
    d, h, t = pl.program_id(0), pl.program_id(1), pl.program_id(2)

    @pl.when(t == 0)
    def _():
        ct_ref[...] = jnp.zeros_like(ct_ref)
        n_ref[...] = jnp.zeros_like(n_ref)
        m_ref[...] = jnp.zeros_like(m_ref)

    bi = bias_ref[d * 2 * ML_HEADS + h]
    bf = bias_ref[(d * 2 + 1) * ML_HEADS + h]
    i_c = ic_ref[...] + bi
    i_r = ir_ref[...] + bi
    f_c = _log_sigmoid(fc_ref[...] + bf)
    f_r = _log_sigmoid(fr_ref[...] + bf)
    row = lax.broadcasted_iota(jnp.int32, (T, T), 0)
    col = lax.broadcasted_iota(jnp.int32, (T, T), 1)
    sgn = 1 - 2 * d
    incl = (col - row) * sgn <= 0
    incl_t = (row - col) * sgn <= 0
    b_c = jnp.sum(jnp.where(incl, f_r, 0.0), axis=1, keepdims=True)
    b_r = jnp.sum(jnp.where(incl_t, f_c, 0.0), axis=0, keepdims=True)
    f_tot = jnp.sum(f_r, axis=1, keepdims=True)
    m_prev = m_ref[...]
    dmat = jnp.where(incl, b_c - b_r + i_r, -jnp.inf)
    inter = b_c + m_prev
    m_t = jnp.maximum(inter, jnp.max(dmat, axis=1, keepdims=True))
    w_intra = jnp.exp(dmat - m_t)
    w_inter = jnp.exp(inter - m_t)
    q = q_ref[...] * (dqk ** -0.5)
    k = k_ref[...]
    v = v_ref[...]
    qb, kb, vb = q.astype(BF16), k.astype(BF16), v.astype(BF16)
    s = _dot_nt(qb, kb) * w_intra
    ct = ct_ref[...]
    n = n_ref[...]
    num = w_inter * _dot(qb, ct.astype(BF16)) + _dot(s.astype(BF16), vb)
    den = w_inter * jnp.sum(q * n, axis=1, keepdims=True) + jnp.sum(s, axis=1, keepdims=True)
    o_ref[...] = num / jnp.maximum(jnp.abs(den), jnp.exp(-m_t))
    g_r = f_tot - b_r + i_r
    g_c = f_tot - b_c + i_c
    m_new = jnp.maximum(f_tot + m_prev, jnp.max(g_r, axis=1, keepdims=True))
    decay = jnp.exp(f_tot + m_prev - m_new)
    wk = jnp.exp(g_c - m_new)
    ct_ref[...] = decay * ct + _dot_tn(kb, (wk * v).astype(BF16))
    n_ref[...] = decay * n + jnp.sum(wk * k, axis=0, keepdims=True)
    m_ref[...] = m_new


def mlstm_scan(u, gates, b_if, *, n_ctx):
    NT = u.shape[0]
    H, T = ML_HEADS, ML_CHUNK
    D = u.shape[1] // 3
    dqk, dv = D // (2 * H), D // H
    nc, ncc = NT // T, n_ctx // T
    ncl = nc - ncc
    g4 = gates[:, :4 * H].T
    g_col = g4.reshape(4 * H, NT, 1)
    g_row = g4.reshape(4 * H, nc, 1, T)

    def chunk(d, t):
        fwd = jnp.where(t < ncc, ncl + t, t - ncc)
        return jnp.where(d == 0, fwd, nc - 1 - t)

    kq, kk, kv = 0, (H * dqk) // dqk, (2 * H * dqk) // dv
    col_spec = lambda off: pl.BlockSpec((None, T, 1), lambda d, h, t: ((2 * d + off) * H + h, chunk(d, t), 0))
    row_spec = lambda off: pl.BlockSpec((None, None, 1, T), lambda d, h, t: ((2 * d + off) * H + h, chunk(d, t), 0, 0))
    return pl.pallas_call(
        functools.partial(_mlstm_kernel, T=T, dqk=dqk),
        grid=(2, H, nc),
        in_specs=[
            pl.BlockSpec(memory_space=pltpu.SMEM),
            pl.BlockSpec((T, dqk), lambda d, h, t: (chunk(d, t), kq + h)),
            pl.BlockSpec((T, dqk), lambda d, h, t: (chunk(d, t), kk + h)),
            pl.BlockSpec((T, dv), lambda d, h, t: (chunk(d, t), kv + h)),
            col_spec(0), col_spec(1), row_spec(0), row_spec(1),
        ],
        out_specs=pl.BlockSpec((None, T, dv), lambda d, h, t: (d, chunk(d, t), h)),
        out_shape=jax.ShapeDtypeStruct((2, NT, H * dv), F32),
        scratch_shapes=[pltpu.VMEM((dqk, dv), F32), pltpu.VMEM((1, dqk), F32), pltpu.VMEM((1, 1), F32)],
        compiler_params=_params("parallel", "parallel", "arbitrary"),
    )(b_if, u, u, u, g_col, g_col, g_row, g_row)


def _mlgate_kernel(hs_ref, o_ref, g_ref, out_ref, *, dv):
    D = out_ref.shape[1]
    for h in range(D // dv):
        sl = slice(h * dv, (h + 1) * dv)
        x = hs_ref[0, :, sl] + hs_ref[1, :, sl]
        r = lax.rsqrt(jnp.mean(jnp.square(x), axis=-1, keepdims=True) + RMS_EPS)
        out_ref[:, sl] = (x * r * g_ref[:, sl] * jax.nn.sigmoid(o_ref[:, sl])).astype(out_ref.dtype)


def mlstm_gate(hs, u, norm_g, *, o_col_block):
    _, NT, D = hs.shape
    tr = 128
    return pl.pallas_call(
        functools.partial(_mlgate_kernel, dv=D // ML_HEADS),
        grid=(NT // tr,),
        in_specs=[
            pl.BlockSpec((2, tr, D), lambda i: (0, i, 0)),
            pl.BlockSpec((tr, D), lambda i: (i, o_col_block)),
            pl.BlockSpec((1, D), lambda i: (0, 0)),
        ],
        out_specs=pl.BlockSpec((tr, D), lambda i: (i, 0)),
        out_shape=jax.ShapeDtypeStruct((NT, D), BF16),
        compiler_params=_params("parallel"),
    )(hs, u, norm_g.reshape(1, D))


def _rope_kernel(x_ref, cos_ref, sin_ref, o_ref):
    cos, sin = cos_ref[...], sin_ref[...]
    lane = lax.broadcasted_iota(jnp.int32, cos.shape, 1)
    first_half = (lane % 64) < 32
    for g in range(x_ref.shape[1] // LANES):
        sl = slice(g * LANES, (g + 1) * LANES)
        x = x_ref[:, sl]
        rot = jnp.where(first_half, -pltpu.roll(x, LANES - 32, 1), pltpu.roll(x, 32, 1))
        o_ref[:, sl] = (x * cos + rot * sin).astype(o_ref.dtype)


def rope_cast(x, cos, sin, *, row0, nrows, col0, ncols):
    tr = _pick(nrows, (256, 128))
    tc = _pick(ncols, (1024, 512, 256, 128))
    rb, cb = row0 // tr, col0 // tc
    assert row0 % tr == 0 and col0 % tc == 0
    return pl.pallas_call(
        _rope_kernel,
        grid=(nrows // tr, ncols // tc),
        in_specs=[
            pl.BlockSpec((tr, tc), lambda i, j: (i + rb, j + cb)),
            pl.BlockSpec((tr, LANES), lambda i, j: (i, 0)),
            pl.BlockSpec((tr, LANES), lambda i, j: (i, 0)),
        ],
        out_specs=pl.BlockSpec((tr, tc), lambda i, j: (i, j)),
        out_shape=jax.ShapeDtypeStruct((nrows, ncols), BF16),
        compiler_params=_params("parallel", "parallel"),
    )(x, cos, sin)


def _dattn_kernel(lam_ref, q_ref, k_ref, v_ref, g_ref, o_ref, m_ref, l_ref, acc_ref, s_ref, p_ref, a_ref,
                  *, tk, n_chunks, out_scale, lambda_init, rb):
    dh = DA_HEAD_DIM
    tq = q_ref.shape[0]
    m_ref[...] = jnp.full_like(m_ref, -jnp.inf)
    l_ref[...] = jnp.zeros_like(l_ref)
    acc_ref[...] = jnp.zeros_like(acc_ref)

    def body(c, carry):
        start = pl.multiple_of(c * tk, tk)
        vc = v_ref[pl.ds(start, tk), :]
        for i in range(2):
            s_ref[...] = _dot_nt(q_ref[:, i * dh:(i + 1) * dh], k_ref[pl.ds(start, tk), i * dh:(i + 1) * dh])

            def rows(r, cr):
                rs = pl.ds(pl.multiple_of(r * rb, rb), rb)
                s = s_ref[rs, :]
                m_old = m_ref[i, rs, :]
                m_new = jnp.maximum(m_old, jnp.max(s, axis=1, keepdims=True))
                p = jnp.exp2(s - m_new)
                alpha = jnp.exp2(m_old - m_new)
                l_ref[i, rs, :] = alpha * l_ref[i, rs, :] + jnp.sum(p, axis=1, keepdims=True)
                m_ref[i, rs, :] = m_new
                a_ref[rs, :] = alpha
                p_ref[rs, :] = p.astype(p_ref.dtype)
                return cr

            lax.fori_loop(0, tq // rb, rows, 0)
            acc_ref[i] = a_ref[...] * acc_ref[i] + _dot(p_ref[...], vc)
        return carry

    lax.fori_loop(0, n_chunks, body, 0)
    lam = (jnp.exp(jnp.sum(lam_ref[0:1, :] * lam_ref[1:2, :], axis=1, keepdims=True))
           - jnp.exp(jnp.sum(lam_ref[2:3, :] * lam_ref[3:4, :], axis=1, keepdims=True)) + lambda_init)
    o = acc_ref[0] / l_ref[0] - lam * (acc_ref[1] / l_ref[1])
    r = lax.rsqrt(jnp.mean(jnp.square(o), axis=-1, keepdims=True) + RMS_EPS)
    o_ref[...] = (o * r * g_ref[...] * out_scale).astype(o_ref.dtype)


def diff_attention(q, k, v, lam4, subln_g, *, kv_row0, kv_len, lambda_init):
    Nq, D = q.shape
    hd = 2 * DA_HEAD_DIM
    H = D // hd
    tq = _pick(Nq, (512, 256))
    tk = _pick(kv_len, (768, 512, 256))
    kvb = kv_row0 // kv_len
    assert kv_row0 % kv_len == 0
    return pl.pallas_call(
        functools.partial(_dattn_kernel, tk=tk, n_chunks=kv_len // tk, out_scale=1.0 - lambda_init,
                          lambda_init=lambda_init, rb=32),
        grid=(H, Nq // tq),
        in_specs=[
            pl.BlockSpec((4, DA_HEAD_DIM), lambda h, i: (0, 0)),
            pl.BlockSpec((tq, hd), lambda h, i: (i, h)),
            pl.BlockSpec((kv_len, hd), lambda h, i: (kvb, h)),
            pl.BlockSpec((kv_len, hd), lambda h, i: (kvb, h)),
            pl.BlockSpec((1, hd), lambda h, i: (0, 0)),
        ],
        out_specs=pl.BlockSpec((tq, hd), lambda h, i: (i, h)),
        out_shape=jax.ShapeDtypeStruct((Nq, D), BF16),
        scratch_shapes=[pltpu.VMEM((2, tq, 1), F32), pltpu.VMEM((2, tq, 1), F32), pltpu.VMEM((2, tq, hd), F32),
                        pltpu.VMEM((tq, tk), F32), pltpu.VMEM((tq, tk), BF16), pltpu.VMEM((tq, 1), F32)],
        compiler_params=_params("parallel", "parallel"),
    )(lam4, q, k, v, subln_g.reshape(1, hd))


def _rope_tables(L, scale):
    a = DA_HEAD_DIM // 2
    t = jnp.arange(L)
    pos = jnp.stack([t // GRID_W, t % GRID_W], -1).astype(F32)
    inv = ROPE_BASE ** (-jnp.arange(0, a, 2, dtype=F32) / a)
    ang = pos[:, :, None] * inv
    ang = jnp.concatenate([ang, ang], -1).reshape(L, DA_HEAD_DIM)
    return jnp.cos(ang) * scale, jnp.sin(ang) * scale


def _shortconv_kernel(p_ref, prev_ref, next_ref, w_ref, b_ref, o_ref, *, n_seq_first, n_seq_last):
    i = pl.program_id(0)
    x = p_ref[...]
    tr = x.shape[0]
    rid = lax.broadcasted_iota(jnp.int32, x.shape, 0)
    is_first = functools.reduce(jnp.logical_or, [i == b for b in n_seq_first])
    is_last = functools.reduce(jnp.logical_or, [i == b for b in n_seq_last])
    prev_row = jnp.where(is_first, 0.0, prev_ref[7:8, :])
    next_row = jnp.where(is_last, 0.0, next_ref[0:1, :])
    xm = jnp.where(rid == 0, prev_row, pltpu.roll(x, 1, 0))
    xp = jnp.where(rid == tr - 1, next_row, pltpu.roll(x, tr - 1, 0))
    o_ref[...] = xm * w_ref[0:1, :] + x * w_ref[1:2, :] + xp * w_ref[2:3, :] + b_ref[...]


def short_conv(p, w, b, *, n_ctx):
    NT, N = p.shape
    tr = 256
    tc = _pick(N, (1024, 512, 256, 128))
    nb, n8 = NT // tr, NT // 8
    n_lat_blocks = (NT - n_ctx) // tr
    firsts = (0, n_lat_blocks)
    lasts = (n_lat_blocks - 1, nb - 1)
    return pl.pallas_call(
        functools.partial(_shortconv_kernel, n_seq_first=firsts, n_seq_last=lasts),
        grid=(nb, N // tc),
        in_specs=[
            pl.BlockSpec((tr, tc), lambda i, j: (i, j)),
            pl.BlockSpec((8, tc), lambda i, j: (jnp.maximum(i * (tr // 8) - 1, 0), j)),
            pl.BlockSpec((8, tc), lambda i, j: (jnp.minimum((i + 1) * (tr // 8), n8 - 1), j)),
            pl.BlockSpec((3, tc), lambda i, j: (0, j)),
            pl.BlockSpec((1, tc), lambda i, j: (0, j)),
        ],
        out_specs=pl.BlockSpec((tr, tc), lambda i, j: (i, j)),
        out_shape=jax.ShapeDtypeStruct((NT, N), F32),
        compiler_params=_params("parallel", "parallel"),
    )(p, p, p, w, b.reshape(1, N))


def _filter_kernel(zf_ref, t_ref, w1_ref, b1_ref, w2_ref, b2_ref, w3_ref, b3_ref, w4f_ref, w4b_ref, dl_ref,
                   o_ref, asum_ref, *, L):
    i = pl.program_id(1)
    hdn = jnp.sin(_dot(zf_ref[...].astype(BF16), w1_ref[...].astype(BF16)) + b1_ref[...])
    hdn = jnp.sin(_dot(hdn.astype(BF16), w2_ref[...].astype(BF16)) + b2_ref[...])
    hdn = jnp.sin(_dot(hdn.astype(BF16), w3_ref[...].astype(BF16)) + b3_ref[...]).astype(BF16)
    hf = _dot(hdn, w4f_ref[...].astype(BF16))
    hb = _dot(hdn, w4b_ref[...].astype(BF16))
    tr = hf.shape[0]
    ridx = i * tr + lax.broadcasted_iota(jnp.int32, hf.shape, 0)
    kern = jnp.where(ridx < L, hf, jnp.where(ridx > L, hb, 0.0)) * jnp.exp(-t_ref[...] * dl_ref[...])
    o_ref[...] = kern

    @pl.when(i == 0)
    def _():
        asum_ref[...] = jnp.zeros_like(asum_ref)

    asum_ref[...] += jnp.sum(jnp.abs(kern), axis=0, keepdims=True)


def hyena_filters(L, D, f_w1, f_b1, f_w2, f_b2, f_w3, f_b3, f_w4):
    t = jnp.linspace(0.0, 1.0, L, dtype=F32)[:, None]
    w = 2.0 * math.pi * jnp.arange(L, dtype=F32)[:, None] / L
    f = jnp.linspace(1e-4, HY_BANDS - 1, HY_BANDS, dtype=F32)[None]
    z = jnp.concatenate([t, jnp.cos(w * f), -jnp.sin(w * f)], -1)
    src = jnp.concatenate([jnp.arange(L), jnp.zeros((1,), jnp.int32), L - 1 - jnp.arange(L - 1)])
    zf = jnp.pad(z[src], ((0, 0), (0, LANES - HY_EMB)))
    tcol = t[src]
    deltas = jnp.abs(jnp.linspace(math.log(HY_DECAY_TARGET) / HY_SLOW_DECAY, math.log(HY_DECAY_TARGET) / HY_FAST_DECAY,
                                  D, dtype=F32)).reshape(1, D)
    w1 = jnp.pad(f_w1, ((0, LANES - HY_EMB), (0, 0)))
    nf = f_w1.shape[1]
    R2 = 2 * L
    tr = _pick(R2, (512,))
    tc = _pick(D, (1024, 512, 256, 128))
    cpd = D // tc
    full = lambda shape: pl.BlockSpec(shape, lambda j, i: (0, 0))
    return pl.pallas_call(
        functools.partial(_filter_kernel, L=L),
        grid=(HY_ORDER * cpd, R2 // tr),
        in_specs=[
            pl.BlockSpec((tr, LANES), lambda j, i: (i, 0)),
            pl.BlockSpec((tr, 1), lambda j, i: (i, 0)),
            full((LANES, nf)), full((1, nf)), full((nf, nf)), full((1, nf)), full((nf, nf)), full((1, nf)),
            pl.BlockSpec((nf, tc), lambda j, i: (0, (j // cpd) * 2 * cpd + j % cpd)),
            pl.BlockSpec((nf, tc), lambda j, i: (0, ((j // cpd) * 2 + 1) * cpd + j % cpd)),
            pl.BlockSpec((1, tc), lambda j, i: (0, j % cpd)),
        ],
        out_specs=[pl.BlockSpec((tr, tc), lambda j, i: (i, j)), pl.BlockSpec((1, tc), lambda j, i: (0, j))],
        out_shape=[jax.ShapeDtypeStruct((R2, HY_ORDER * D), F32), jax.ShapeDtypeStruct((1, HY_ORDER * D), F32)],
        compiler_params=_params("parallel", "arbitrary"),
    )(zf, tcol, w1, f_b1.reshape(1, nf), f_w2, f_b2.reshape(1, nf), f_w3, f_b3.reshape(1, nf), f_w4, f_w4, deltas)


def _hpmm_kernel(*refs, gated):
    if gated:
        a_ref, b_ref, gate_ref, z_ref, inv_ref, bias_ref, o_ref = refs
    else:
        a_ref, b_ref, o_ref = refs
    y = jnp.dot(a_ref[...], b_ref[...], precision=HIGHEST, preferred_element_type=F32)
    if gated:
        y = gate_ref[...] * (y / inv_ref[...] + z_ref[...] * bias_ref[...])
    o_ref[...] = y


def hp_matmul(a, b, *, gate=None, z=None, asum=None, bias=None, chan_period=None, chan_block0=0):
    M, K = a.shape
    N = b.shape[1]
    tn = _pick(N, (2048, 1024, 512, 256, 128))
    in_specs = [pl.BlockSpec((M, K), lambda j: (0, 0)), pl.BlockSpec((K, tn), lambda j: (0, j))]
    args = [a, b]
    gated = gate is not None
    if gated:
        cpd = chan_period // tn
        blk = pl.BlockSpec((M, tn), lambda j: (0, j))
        vec = pl.BlockSpec((1, tn), lambda j: (0, chan_block0 * cpd + j % cpd))
        in_specs += [blk, blk, vec, vec]
        args += [gate, z, asum, bias]
    return pl.pallas_call(
        functools.partial(_hpmm_kernel, gated=gated),
        grid=(N // tn,),
        in_specs=in_specs,
        out_specs=pl.BlockSpec((M, tn), lambda j: (0, j)),
        out_shape=jax.ShapeDtypeStruct((M, N), F32),
        compiler_params=_params("parallel"),
    )(*args)


def _fft_mid_kernel(*refs, g, conv):
    if conv:
        a_ref, kf_ref, twc_ref, tws_ref, w2_ref, w2i_ref, o_ref = refs
    else:
        a_ref, twc_ref, tws_ref, w2_ref, o_ref = refs
    n2 = a_ref.shape[2]
    for kk in range(g):
        ar, ai = a_ref[0, kk], a_ref[1, kk]
        c, s = twc_ref[kk], tws_ref[kk]
        x = jnp.concatenate([ar * c + ai * s, ai * c - ar * s], axis=0)
        x = jnp.dot(w2_ref[...], x, precision=HIGHEST, preferred_element_type=F32)
        if conv:
            xr, xi = x[:n2], x[n2:]
            kr, ki = kf_ref[0, kk], kf_ref[1, kk]
            y = jnp.concatenate([xr * kr - xi * ki, xr * ki + xi * kr], axis=0)
            y = jnp.dot(w2i_ref[...], y, precision=HIGHEST, preferred_element_type=F32)
            yr, yi = y[:n2], y[n2:]
            o_ref[0, kk] = yr * c - yi * s
            o_ref[1, kk] = yr * s + yi * c
        else:
            o_ref[0, kk] = x[:n2]
            o_ref[1, kk] = x[n2:]


def fft_mid(a, tw_c, tw_s, w2, w2i=None, kf=None, *, kf_col0=0):
    _, n1, n2, C = a.shape
    g = 4
    tc = _pick(C, (512, 256, 128))
    conv = kf is not None
    blk = pl.BlockSpec((2, g, n2, tc), lambda j, k: (0, k, 0, j))
    tw = pl.BlockSpec((g, n2, 1), lambda j, k: (k, 0, 0))
    mat = pl.BlockSpec((2 * n2, 2 * n2), lambda j, k: (0, 0))
    if conv:
        kb = kf_col0 // tc
        in_specs = [blk, pl.BlockSpec((2, g, n2, tc), lambda j, k: (0, k, 0, j + kb)), tw, tw, mat, mat]
        args = [a, kf, tw_c, tw_s, w2, w2i]
    else:
        in_specs = [blk, tw, tw, mat]
        args = [a, tw_c, tw_s, w2]
    return pl.pallas_call(
        functools.partial(_fft_mid_kernel, g=g, conv=conv),
        grid=(C // tc, n1 // g),
        in_specs=in_specs,
        out_specs=blk,
        out_shape=jax.ShapeDtypeStruct(a.shape, F32),
        compiler_params=_params("parallel", "parallel"),
    )(*args)


def _cmul_kernel(x_ref, k_ref, o_ref):
    xr, xi, kr, ki = x_ref[0], x_ref[1], k_ref[0], k_ref[1]
    o_ref[0] = xr * kr - xi * ki
    o_ref[1] = xr * ki + xi * kr


def complex_mul(x, kf, *, kf_col0):
    _, R, C = x.shape
    tc = _pick(C, (512, 256, 128))
    kb = kf_col0 // tc
    return pl.pallas_call(
        _cmul_kernel,
        grid=(C // tc,),
        in_specs=[pl.BlockSpec((2, R, tc), lambda j: (0, 0, j)), pl.BlockSpec((2, R, tc), lambda j: (0, 0, j + kb))],
        out_specs=pl.BlockSpec((2, R, tc), lambda j: (0, 0, j)),
        out_shape=jax.ShapeDtypeStruct(x.shape, F32),
        compiler_params=_params("parallel"),
    )(x, kf)


def _dft_constants(n1, n2, l1):
    N = n1 * n2
    k1 = np.arange(n1)[:, None]
    ang1 = 2 * np.pi * k1 * np.arange(n1)[None, :] / n1
    w1 = np.concatenate([np.cos(ang1), -np.sin(ang1)], 0)
    w1_inv = np.concatenate([np.cos(ang1[:l1]), -np.sin(ang1[:l1])], 1) / N
    ang_t = 2 * np.pi * k1 * np.arange(n2)[None, :] / N
    k2 = np.arange(n2)[:, None]
    ang2 = 2 * np.pi * k2 * np.arange(n2)[None, :] / n2
    c2, s2 = np.cos(ang2), np.sin(ang2)
    w2 = np.block([[c2, s2], [-s2, c2]])
    w2_inv = np.block([[c2, -s2], [s2, c2]])
    f = lambda x: jnp.asarray(x, F32)
    return dict(w1_data=f(w1[:, :l1]), w1_full=f(w1), w1_inv=f(w1_inv), tw_c=f(np.cos(ang_t))[:, :, None],
                tw_s=f(np.sin(ang_t))[:, :, None], w2=f(w2), w2_inv=f(w2_inv))


def _dense_dft_constants(L):
    N = 2 * L
    ang = 2 * np.pi * np.arange(N)[:, None] * np.arange(N)[None, :] / N
    wf = np.concatenate([np.cos(ang), -np.sin(ang)], 0)
    wi = np.concatenate([np.cos(ang[:L]), -np.sin(ang[:L])], 1) / N
    return jnp.asarray(wf, F32), jnp.asarray(wi, F32)


def hyena_long_conv_pair(v, x1, x2, kern, asum, hy_bias, *, two_stage):
    L, D = v.shape
    gates = (x1, x2)
    bias = hy_bias.reshape(1, HY_ORDER * D)
    if two_stage:
        n2 = 128
        n1, l1 = 2 * L // n2, L // n2
        cst = _dft_constants(n1, n2, l1)
        ka = hp_matmul(cst["w1_full"], kern.reshape(n1, n2 * HY_ORDER * D))
        kf = fft_mid(ka.reshape(2, n1, n2, HY_ORDER * D), cst["tw_c"], cst["tw_s"], cst["w2"])
        z = v
        for o in range(HY_ORDER):
            a = hp_matmul(cst["w1_data"], z.reshape(l1, n2 * D))
            bmid = fft_mid(a.reshape(2, n1, n2, D), cst["tw_c"], cst["tw_s"], cst["w2"], cst["w2_inv"], kf,
                           kf_col0=o * D)
            z = hp_matmul(cst["w1_inv"], bmid.reshape(2 * n1, n2 * D), gate=gates[o].reshape(l1, n2 * D),
                          z=z.reshape(l1, n2 * D), asum=asum, bias=bias, chan_period=D,
                          chan_block0=o).reshape(L, D)
        return z
    wf, wi = _dense_dft_constants(L)
    kf = hp_matmul(wf, kern).reshape(2, 2 * L, HY_ORDER * D)
    z = v
    for o in range(HY_ORDER):
        xf = hp_matmul(wf[:, :L], z).reshape(2, 2 * L, D)
        y = complex_mul(xf, kf, kf_col0=o * D).reshape(4 * L, D)
        z = hp_matmul(wi, y, gate=gates[o], z=z, asum=asum, bias=bias, chan_period=D, chan_block0=o)
    return z


def _na_kernel(q_ref, k_ref, v_ref, bias_ref, o_ref, *, n_ctx, rows, heads_per_step):
    r = pl.program_id(1)
    kr = NA_ROWS
    r0 = jnp.clip(r - kr // 2, 0, rows - kr)
    start = pl.multiple_of(r0 * GRID_W, GRID_W)
    n_lat = rows * GRID_W
    dh = NA_HEAD_DIM
    scale = dh ** -0.5
    for hh in range(heads_per_step):
        sl = slice(hh * dh, (hh + 1) * dh)
        q = q_ref[:, sl]
        s_ctx = _dot_nt(q, k_ref[n_lat:n_lat + n_ctx, sl]) * scale
        s_lat = _dot_nt(q, k_ref[pl.ds(start, kr * GRID_W), sl]) * scale + bias_ref[hh]
        m = jnp.maximum(jnp.max(s_ctx, axis=1, keepdims=True), jnp.max(s_lat, axis=1, keepdims=True))
        p_ctx = jnp.exp(s_ctx - m)
        p_lat = jnp.exp(s_lat - m)
        l = jnp.sum(p_ctx, axis=1, keepdims=True) + jnp.sum(p_lat, axis=1, keepdims=True)
        o = (_dot(p_ctx.astype(BF16), v_ref[n_lat:n_lat + n_ctx, sl])
             + _dot(p_lat.astype(BF16), v_ref[pl.ds(start, kr * GRID_W), sl]))
        o_ref[:, sl] = (o / l).astype(o_ref.dtype)


def _na_bias_table(rpb, rows):
    H = rpb.shape[0]
    cols = jnp.arange(GRID_W)
    c0 = jnp.clip(cols - NA_COLS // 2, 0, GRID_W - NA_COLS)
    kc = jnp.arange(GRID_W)[None, :]
    inside = (kc >= c0[:, None]) & (kc < c0[:, None] + NA_COLS)
    rel = jnp.clip(kc - cols[:, None] + NA_COLS - 1, 0, 2 * NA_COLS - 2)
    rr = jnp.arange(NA_ROWS)[:, None] + jnp.arange(NA_ROWS)[None, :]
    t = rpb[:, rr]
    t = t[:, :, :, rel]
    t = jnp.where(inside[None, None, None], t, -1e30)
    t = jnp.transpose(t, (1, 0, 3, 2, 4))
    return t.reshape(NA_ROWS, H, GRID_W, NA_ROWS * GRID_W)


def na_attention(qkv, rpb, *, n_ctx):
    NT, D3 = qkv.shape
    D = D3 // 3
    L = NT - n_ctx
    rows = L // GRID_W
    hps = 4
    hw = hps * NA_HEAD_DIM
    nhb = D // hw
    table = _na_bias_table(rpb, rows)

    def pattern(r):
        r0 = jnp.clip(r - NA_ROWS // 2, 0, rows - NA_ROWS)
        return r0 - r + NA_ROWS - 1

    return pl.pallas_call(
        functools.partial(_na_kernel, n_ctx=n_ctx, rows=rows, heads_per_step=hps),
        grid=(nhb, rows),
        in_specs=[
            pl.BlockSpec((GRID_W, hw), lambda h, r: (r, h)),
            pl.BlockSpec((NT, hw), lambda h, r: (0, nhb + h)),
            pl.BlockSpec((NT, hw), lambda h, r: (0, 2 * nhb + h)),
            pl.BlockSpec((None, hps, GRID_W, NA_ROWS * GRID_W), lambda h, r: (pattern(r), h, 0, 0)),
        ],
        out_specs=pl.BlockSpec((GRID_W, hw), lambda h, r: (r, h)),
        out_shape=jax.ShapeDtypeStruct((L, D), BF16),
        compiler_params=_params("parallel", "parallel"),
    )(qkv, qkv, qkv, table)


def _router_kernel(x_ref, w_ref, b_ref, o_ref):
    E, G = N_EXPERTS, N_GROUPS
    per = E // G
    logits = _dot_nt(w_ref[...].astype(BF16), x_ref[...])
    scores = jax.nn.sigmoid(logits)
    sel = scores + b_ref[...]
    tm = sel.shape[1]
    grp = sel.reshape(G, per, tm)
    eidx = lax.broadcasted_iota(jnp.int32, grp.shape, 1)
    m1 = jnp.max(grp, axis=1, keepdims=True)
    first = jnp.min(jnp.where(grp == m1, eidx, per), axis=1, keepdims=True)
    m2 = jnp.max(jnp.where(eidx == first, -jnp.inf, grp), axis=1, keepdims=True)
    gs = (m1 + m2).reshape(G, tm)
    gidx = lax.broadcasted_iota(jnp.int32, gs.shape, 0)
    gmask = jnp.zeros(gs.shape, jnp.bool_)
    for _ in range(TOPK_GROUPS):
        mx = jnp.max(gs, axis=0, keepdims=True)
        pick = gidx == jnp.min(jnp.where(gs == mx, gidx, G), axis=0, keepdims=True)
        gmask = jnp.logical_or(gmask, pick)
        gs = jnp.where(pick, -jnp.inf, gs)
    emask = jnp.broadcast_to(gmask.reshape(G, 1, tm), (G, per, tm)).reshape(E, tm)
    cand = jnp.where(emask, sel, -jnp.inf)
    xidx = lax.broadcasted_iota(jnp.int32, cand.shape, 0)
    chosen = jnp.zeros(cand.shape, jnp.bool_)
    for _ in range(TOP_K):
        mx = jnp.max(cand, axis=0, keepdims=True)
        pick = xidx == jnp.min(jnp.where(cand == mx, xidx, E), axis=0, keepdims=True)
        chosen = jnp.logical_or(chosen, pick)
        cand = jnp.where(pick, -jnp.inf, cand)
    wsel = jnp.where(chosen, scores, 0.0)
    gates = wsel / jnp.sum(wsel, axis=0, keepdims=True) * ROUTED_SCALE
    o_ref[...] = gates.T


def moe_router(h, router_w_t, router_b, layer):
    T, D = h.shape
    E = router_w_t.shape[1]
    tm = _pick(T, (256, 128))
    return pl.pallas_call(
        _router_kernel,
        grid=(T // tm,),
        in_specs=[
            pl.BlockSpec((tm, D), lambda i: (i, 0)),
            pl.BlockSpec((None, E, D), lambda i: (layer, 0, 0)),
            pl.BlockSpec((None, E, 1), lambda i: (layer, 0, 0)),
        ],
        out_specs=pl.BlockSpec((tm, E), lambda i: (i, 0)),
        out_shape=jax.ShapeDtypeStruct((T, E), F32),
        compiler_params=_params("parallel"),
    )(h, router_w_t, router_b.reshape(-1, E, 1))


def _moe_up_kernel(*refs, nk, ne, gated, F):
    if gated:
        x_ref, wg_ref, wu_ref, g_ref, o_ref, accg_ref, accu_ref = refs
    else:
        x_ref, wg_ref, wu_ref, o_ref, accg_ref, accu_ref = refs
    e0 = pl.program_id(1) * ne
    k = pl.program_id(2)

    @pl.when(k == 0)
    def _():
        accg_ref[...] = jnp.zeros_like(accg_ref)
        accu_ref[...] = jnp.zeros_like(accu_ref)

    def experts_side_by_side(ref):
        w = ref[0] if ne == 1 else jnp.concatenate([ref[ee] for ee in range(ne)], axis=1)
        return w.astype(BF16)

    x = x_ref[...]
    accg_ref[...] += _dot(x, experts_side_by_side(wg_ref))
    accu_ref[...] += _dot(x, experts_side_by_side(wu_ref))

    @pl.when(k == nk - 1)
    def _():
        a = accg_ref[...]
        hid = a * jax.nn.sigmoid(a) * accu_ref[...]
        if gated:
            g = g_ref[...]
            lane = lax.broadcasted_iota(jnp.int32, g.shape, 1)
            col = lax.broadcasted_iota(jnp.int32, (1, ne * F), 1) // F
            gexp = jnp.zeros_like(hid)
            for ee in range(ne):
                ge = jnp.sum(jnp.where(lane == e0 + ee, g, 0.0), axis=1, keepdims=True)
                gexp = jnp.where(col == ee, ge, gexp)
            hid = hid * gexp
        o_ref[...] = hid.astype(o_ref.dtype)


def moe_up(x, w_gate, w_up, layer, gates=None):
    T, D = x.shape
    if w_gate.ndim == 3:
        w_gate, w_up = w_gate[:, None], w_up[:, None]
    E, F = w_gate.shape[1], w_gate.shape[3]
    ne = 4 if E % 4 == 0 else 1
    tm = _pick(T, (1056, 1024, 512, 256, 128))
    tk = _pick(D, (1024, 512, 256, 128))
    nk = D // tk
    gated = gates is not None
    w_spec = pl.BlockSpec((None, ne, tk, F), lambda i, e, k: (layer, e, k, 0))
    in_specs = [pl.BlockSpec((tm, tk), lambda i, e, k: (i, k)), w_spec, w_spec]
    args = [x, w_gate, w_up]
    if gated:
        in_specs.append(pl.BlockSpec((tm, E), lambda i, e, k: (i, 0)))
        args.append(gates)
    return pl.pallas_call(
        functools.partial(_moe_up_kernel, nk=nk, ne=ne, gated=gated, F=F),
        grid=(T // tm, E // ne, nk),
        in_specs=in_specs,
        out_specs=pl.BlockSpec((tm, ne * F), lambda i, e, k: (i, e)),
        out_shape=jax.ShapeDtypeStruct((T, E * F), BF16),
        scratch_shapes=[pltpu.VMEM((tm, ne * F), F32), pltpu.VMEM((tm, ne * F), F32)],
        compiler_params=_params("parallel", "parallel", "arbitrary"),
    )(*args)


def moe(h, layer, router_w_t, router_b, w_gate, w_up, w_down, s_gate, s_up, s_down):
    gates = moe_router(h, router_w_t, router_b, layer)
    hid = moe_up(h, w_gate, w_up, layer, gates)
    hid_sh = moe_up(h, s_gate, s_up, layer)
    E, F, D = w_down.shape[1:]
    return matmul(hid, w_down.reshape(DEPTH, E * F, D), out_dtype=F32, layer=layer, extra=(hid_sh, s_down, layer))


def kernel(x, c, ctx, c_ctx, ada_a, ada_b, ada_bias, ln1_g, ln1_b, ln2_g, ln2_b, router_w, router_b, moe_w_gate, moe_w_up, moe_w_down, sh_w_gate, sh_w_up, sh_w_down, ml_w_in, ml_b_if, ml_norm_g, ml_w_out, da_w_qkv, da_lam_q1, da_lam_k1, da_lam_q2, da_lam_k2, da_subln_g, da_w_out, hy_w_in, hy_conv_w, hy_conv_b, hy_f_w1, hy_f_b1, hy_f_w2, hy_f_b2, hy_f_w3, hy_f_b3, hy_f_w4, hy_bias, hy_w_out, na_w_qkv, na_rpb, na_w_out):
    _, L, D = x.shape
    C = ctx.shape[1]
    NT = C + L
    s = jnp.concatenate([x[0], ctx[0]], axis=0)

    cc = jnp.zeros((8, D), F32).at[0].set(c_ctx).at[1].set(c[0])
    mods = ada_modulation(cc, ada_a, ada_b, ada_bias)[:, :2].reshape(DEPTH, 2, 6, 1, D)
    SH1, SC1, G1, SH2, SC2, G2 = range(6)
    router_w_t = jnp.swapaxes(router_w, 1, 2)
    moe_args = (router_w_t, router_b, moe_w_gate, moe_w_up, moe_w_down, sh_w_gate, sh_w_up, sh_w_down)

    (h,) = ln_modulate(s, mods, n_ctx=C, h_layer=0, sc_idx=SC1, sh_idx=SH1)
    for i in range(DEPTH):
        last = i == DEPTH - 1
        if i == 0:
            y = mlstm_mixer(h, ml_w_in, ml_b_if[0], ml_norm_g[0], ml_w_out, n_ctx=C)
        elif i == 1:
            lam4 = jnp.stack([da_lam_q1[0], da_lam_k1[0], da_lam_q2[0], da_lam_k2[0]])
            y = diff_attn_mixer(h, da_w_qkv, lam4, da_subln_g[0], da_w_out, n_ctx=C,
                                lambda_init=0.8 - 0.6 * math.exp(-0.3 * i))
        elif i == 2:
            y = hyena_mixer(h, hy_w_in, hy_conv_w[0], hy_conv_b[0], hy_f_w1[0], hy_f_b1[0], hy_f_w2[0], hy_f_b2[0],
                            hy_f_w3[0], hy_f_b3[0], hy_f_w4[0], hy_bias[0], hy_w_out, n_ctx=C)
        else:
            y = na_mixer(h, na_w_qkv, na_rpb[0], na_w_out, n_ctx=C)
        n_ctx = C
        if last:
            s, n_ctx = s[:L], 0
        s, h = ln_modulate(s, mods, n_ctx=n_ctx, ys=(y,), ln_layer=i, gate_idx=G1, ln_g=ln1_g, ln_b=ln1_b,
                           h_layer=i, sc_idx=SC2, sh_idx=SH2)
        y = moe(h, i, *moe_args)
        if last:
            (s,) = ln_modulate(s, mods, n_ctx=n_ctx, ys=(y,), ln_layer=i, gate_idx=G2, ln_g=ln2_g, ln_b=ln2_b)
        else:
            s, h = ln_modulate(s, mods, n_ctx=n_ctx, ys=(y,), ln_layer=i, gate_idx=G2, ln_g=ln2_g, ln_b=ln2_b,
                               h_layer=i + 1, sc_idx=SC1, sh_idx=SH1)
    return s[None]


def mlstm_mixer(h, w_in, b_if, norm_g, w_out, *, n_ctx):
    D = h.shape[1]
    u = matmul(h, w_in, out_dtype=F32, layer=0, ncols=3 * D)
    w_if = jnp.pad(w_in[0, :, 3 * D:], ((0, 0), (0, LANES - 4 * ML_HEADS)))
    g_if = matmul(h, w_if, out_dtype=F32)
    hs = mlstm_scan(u, g_if, b_if, n_ctx=n_ctx)
    return matmul(mlstm_gate(hs, u, norm_g, o_col_block=2), w_out, out_dtype=F32, layer=0)


def diff_attn_mixer(h, w_qkv, lam4, subln_g, w_out, *, n_ctx, lambda_init):
    NT, D = h.shape
    L = NT - n_ctx
    scale = DA_HEAD_DIM ** -0.5 * math.log2(math.e)
    qk = matmul(h, w_qkv, out_dtype=F32, layer=0, ncols=2 * D)
    v = matmul(h, w_qkv, out_dtype=BF16, layer=0, col0=2 * D, ncols=D)
    cos, sin = _rope_tables(L, 1.0)
    ones, zeros = jnp.ones((n_ctx, LANES), F32), jnp.zeros((n_ctx, LANES), F32)
    q_lat = rope_cast(qk, cos * scale, sin * scale, row0=0, nrows=L, col0=0, ncols=D)
    q_ctx = rope_cast(qk, ones * scale, zeros, row0=L, nrows=n_ctx, col0=0, ncols=D)
    k = rope_cast(qk, jnp.concatenate([cos, ones]), jnp.concatenate([sin, zeros]), row0=0, nrows=NT, col0=D, ncols=D)
    o_lat = diff_attention(q_lat, k, v, lam4, subln_g, kv_row0=0, kv_len=NT, lambda_init=lambda_init)
    o_ctx = diff_attention(q_ctx, k, v, lam4, subln_g, kv_row0=L, kv_len=n_ctx, lambda_init=lambda_init)
    return matmul(jnp.concatenate([o_lat, o_ctx]), w_out, out_dtype=F32, layer=0)


def hyena_mixer(h, w_in, conv_w, conv_b, f_w1, f_b1, f_w2, f_b2, f_w3, f_b3, f_w4, hy_bias, w_out, *, n_ctx):
    NT, D = h.shape
    L = NT - n_ctx
    u = short_conv(matmul(h, w_in, out_dtype=F32, layer=0), conv_w, conv_b, n_ctx=n_ctx)
    fargs = (f_w1, f_b1, f_w2, f_b2, f_w3, f_b3, f_w4)
    zs = []
    for r0, n, two_stage in ((0, L, True), (L, n_ctx, False)):
        v, x1, x2 = (u[r0:r0 + n, j * D:(j + 1) * D] for j in range(3))
        kern, asum = hyena_filters(n, D, *fargs)
        zs.append(hyena_long_conv_pair(v, x1, x2, kern, asum, hy_bias, two_stage=two_stage))
    return matmul(jnp.concatenate(zs), w_out, out_dtype=F32, layer=0)


def na_mixer(h, w_qkv, rpb, w_out, *, n_ctx):
    qkv = matmul(h, w_qkv, out_dtype=BF16, layer=0)
    return matmul(na_attention(qkv, rpb, n_ctx=n_ctx), w_out, out_dtype=F32, layer=0)
```

```python
import functools
import math

import numpy as np
import jax
import jax.numpy as jnp
from jax import lax
from jax.experimental import pallas as pl
from jax.experimental.pallas import tpu as pltpu

F32 = jnp.float32
BF16 = jnp.bfloat16

V7X_VMEM_LIMIT_BYTES = 56 * 1024 * 1024
LANES = 128

GRID_W = 64
DEPTH = 4
DEEPNORM_ALPHA = (2 * DEPTH) ** 0.25
LN_EPS = 1e-5
RMS_EPS = 1e-6
ROPE_BASE = 10000.0
ML_HEADS = 8
ML_CHUNK = 256
DA_HEAD_DIM = 128
HY_ORDER = 2
HY_EMB = 33
HY_BANDS = (HY_EMB - 1) // 2
HY_FAST_DECAY = 0.3
HY_SLOW_DECAY = 1.5
HY_DECAY_TARGET = 1e-2
NA_HEAD_DIM = 128
NA_ROWS = 8
NA_COLS = 16
N_EXPERTS = 64
TOP_K = 8
N_GROUPS = 8
TOPK_GROUPS = 4
ROUTED_SCALE = 2.5


def _params(*sem):
    return pltpu.CompilerParams(dimension_semantics=sem, vmem_limit_bytes=V7X_VMEM_LIMIT_BYTES)


def _dot(a, b):
    return jnp.dot(a, b, preferred_element_type=F32)


def _dot_nt(a, b):
    return lax.dot_general(a, b, (((1,), (1,)), ((), ())), preferred_element_type=F32)


def _dot_tn(a, b):
    return lax.dot_general(a, b, (((0,), (0,)), ((), ())), preferred_element_type=F32)


def _pick(n, prefs):
    for p in prefs:
        if n % p == 0:
            return p
    return n


def _mm_kernel(*refs, nk, has_extra):
    if has_extra:
        a_ref, w_ref, a2_ref, w2_ref, o_ref, acc_ref = refs
    else:
        a_ref, w_ref, o_ref, acc_ref = refs
    k = pl.program_id(2)

    @pl.when(k == 0)
    def _():
        acc_ref[...] = jnp.zeros_like(acc_ref)

    acc_ref[...] += _dot(a_ref[...].astype(BF16), w_ref[...].astype(BF16))

    @pl.when(k == nk - 1)
    def _():
        acc = acc_ref[...]
        if has_extra:
            acc = acc + _dot(a2_ref[...].astype(BF16), w2_ref[...].astype(BF16))
        o_ref[...] = acc.astype(o_ref.dtype)


def _mm_resident_kernel(a_ref, w_ref, o_ref):
    o_ref[...] = _dot(a_ref[...].astype(BF16), w_ref[...].astype(BF16)).astype(o_ref.dtype)


def _matmul_resident(a, w, *, out_dtype, layer, col0, N):
    M, K = a.shape
    tm = _pick(M, (2112, 2048, 1024, 512, 256, 128) if a.dtype == BF16 else (1056, 1024, 512, 256, 128))
    tn = _pick(N, (256, 128))
    cb = col0 // tn
    if w.ndim == 3:
        w_spec = pl.BlockSpec((None, K, tn), lambda i, j: (layer, 0, j + cb))
    else:
        w_spec = pl.BlockSpec((K, tn), lambda i, j: (0, j + cb))
    return pl.pallas_call(
        _mm_resident_kernel,
        grid=(M // tm, N // tn),
        in_specs=[pl.BlockSpec((tm, K), lambda i, j: (i, 0), pipeline_mode=pl.Buffered(1)), w_spec],
        out_specs=pl.BlockSpec((tm, tn), lambda i, j: (i, j)),
        out_shape=jax.ShapeDtypeStruct((M, N), out_dtype),
        compiler_params=_params("parallel", "arbitrary"),
    )(a, w)


def matmul(a, w, *, out_dtype, layer=None, col0=0, ncols=None, extra=None, row0=0, nrows=None):
    M = a.shape[0] if nrows is None else nrows
    K = a.shape[1]
    N = w.shape[-1] if ncols is None else ncols
    if K <= 4096 and extra is None and nrows is None and M >= 1024:
        return _matmul_resident(a, w, out_dtype=out_dtype, layer=layer, col0=col0, N=N)
    tm = _pick(M, (1056, 1024, 512, 256, 128, 64, 32, 16, 8))
    tn = _pick(N, (1024, 512, 256, 128))
    tk = _pick(K, (2048, 1024, 512, 256, 128))
    assert col0 % tn == 0 and row0 % tm == 0
    cb, rb = col0 // tn, row0 // tm
    nk = K // tk
    if w.ndim == 3:
        w_spec = pl.BlockSpec((None, tk, tn), lambda i, j, k: (layer, k, j + cb))
    else:
        w_spec = pl.BlockSpec((tk, tn), lambda i, j, k: (k, j + cb))
    in_specs = [pl.BlockSpec((tm, tk), lambda i, j, k: (i + rb, k)), w_spec]
    args = [a, w]
    if extra is not None:
        a2, w2, layer2 = extra
        k2 = a2.shape[1]
        in_specs.append(pl.BlockSpec((tm, k2), lambda i, j, k: (i + rb, 0)))
        in_specs.append(pl.BlockSpec((None, k2, tn), lambda i, j, k: (layer2, 0, j)))
        args += [a2, w2]
    return pl.pallas_call(
        functools.partial(_mm_kernel, nk=nk, has_extra=extra is not None),
        grid=(M // tm, N // tn, nk),
        in_specs=in_specs,
        out_specs=pl.BlockSpec((tm, tn), lambda i, j, k: (i, j)),
        out_shape=jax.ShapeDtypeStruct((M, N), out_dtype),
        scratch_shapes=[pltpu.VMEM((tm, tn), F32)],
        compiler_params=_params("parallel", "parallel", "arbitrary"),
    )(*args)


def _ada_kernel(cc_ref, a_ref, b_ref, bias_ref, o_ref):
    cc = cc_ref[...]
    act = cc * jax.nn.sigmoid(cc)
    t = _dot(act.astype(BF16), a_ref[...].astype(BF16))
    o_ref[...] = _dot(t.astype(BF16), b_ref[...].astype(BF16)) + bias_ref[...]


def ada_modulation(cc, ada_a, ada_b, ada_bias):
    depth, D, R = ada_a.shape
    n6 = ada_b.shape[-1]
    tn = D
    return pl.pallas_call(
        _ada_kernel,
        grid=(depth, n6 // tn),
        in_specs=[
            pl.BlockSpec((8, D), lambda l, n: (0, 0)),
            pl.BlockSpec((None, D, R), lambda l, n: (l, 0, 0)),
            pl.BlockSpec((None, R, tn), lambda l, n: (l, 0, n)),
            pl.BlockSpec((None, 1, tn), lambda l, n: (l, 0, n)),
        ],
        out_specs=pl.BlockSpec((None, 8, tn), lambda l, n: (l, 0, n)),
        out_shape=jax.ShapeDtypeStruct((depth, 8, n6), F32),
        compiler_params=_params("parallel", "parallel"),
    )(cc, ada_a, ada_b, ada_bias.reshape(depth, 1, n6))


def _lnmod_kernel(*refs, has_ln, has_h, n_y):
    it = iter(refs)
    s_ref = next(it)
    if has_ln:
        y_refs = [next(it) for _ in range(n_y)]
        gate_ref, lng_ref, lnb_ref = next(it), next(it), next(it)
    if has_h:
        sc_ref, sh_ref = next(it), next(it)
    if has_ln:
        so_ref = next(it)
    if has_h:
        h_ref = next(it)
    x = s_ref[...]
    if has_ln:
        y = y_refs[0][...].astype(F32)
        for r in y_refs[1:]:
            y = y + r[...].astype(F32)
        v = DEEPNORM_ALPHA * x + gate_ref[...] * y
        mu = jnp.mean(v, axis=-1, keepdims=True)
        var = jnp.mean(jnp.square(v - mu), axis=-1, keepdims=True)
        x = (v - mu) * lax.rsqrt(var + LN_EPS) * lng_ref[...] + lnb_ref[...]
        so_ref[...] = x
    if has_h:
        h_ref[...] = (x * (1.0 + sc_ref[...]) + sh_ref[...]).astype(h_ref.dtype)


def ln_modulate(s, mods, *, n_ctx, ys=(), ln_layer=None, gate_idx=None, ln_g=None, ln_b=None,
                h_layer=None, sc_idx=None, sh_idx=None):
    R, D = s.shape
    tr = 128
    has_ln, has_h = len(ys) > 0, h_layer is not None
    n_lat_blocks = (R - n_ctx) // tr
    row_spec = pl.BlockSpec((tr, D), lambda i: (i, 0))

    def mod_spec(layer, idx):
        return pl.BlockSpec((None, None, None, 1, D),
                            lambda i: (layer, jnp.where(i >= n_lat_blocks, 0, 1), idx, 0, 0))

    vec_spec = pl.BlockSpec((None, 1, D), lambda i: (ln_layer, 0, 0))
    in_specs, args = [row_spec], [s]
    if has_ln:
        in_specs += [row_spec] * len(ys) + [mod_spec(ln_layer, gate_idx), vec_spec, vec_spec]
        args += list(ys) + [mods, ln_g.reshape(DEPTH, 1, D), ln_b.reshape(DEPTH, 1, D)]
    if has_h:
        in_specs += [mod_spec(h_layer, sc_idx), mod_spec(h_layer, sh_idx)]
        args += [mods, mods]
    out_specs, out_shape = [], []
    if has_ln:
        out_specs.append(row_spec)
        out_shape.append(jax.ShapeDtypeStruct((R, D), F32))
    if has_h:
        out_specs.append(row_spec)
        out_shape.append(jax.ShapeDtypeStruct((R, D), BF16))
    outs = pl.pallas_call(
        functools.partial(_lnmod_kernel, has_ln=has_ln, has_h=has_h, n_y=len(ys)),
        grid=(R // tr,),
        in_specs=in_specs,
        out_specs=out_specs,
        out_shape=out_shape,
        compiler_params=_params("parallel"),
    )(*args)
    return outs


def _log_sigmoid(x):
    return jnp.minimum(x, 0.0) - jnp.log(1.0 + jnp.exp(-jnp.abs(x)))


def _mlstm_kernel(bias_ref, q_ref, k_ref, v_ref, ic_ref, fc_ref, ir_ref, fr_ref, o_ref, ct_ref, n_ref, m_ref,
                  *, T, dqk):
    d, h, t = pl.program_id(0), pl.program_id(1), pl.program_id(2)

    @pl.when(t == 0)
    def _():
        ct_ref[...] = jnp.zeros_like(ct_ref)
        n_ref[...] = jnp.zeros_like(n_ref)
        m_ref[...] = jnp.zeros_like(m_ref)

    bi = bias_ref[d * 2 * ML_HEADS + h]
    bf = bias_ref[(d * 2 + 1) * ML_HEADS + h]
    i_c = ic_ref[...] + bi
    i_r = ir_ref[...] + bi
    f_c = _log_sigmoid(fc_ref[...] + bf)
    f_r = _log_sigmoid(fr_ref[...] + bf)
    row = lax.broadcasted_iota(jnp.int32, (T, T), 0)
    col = lax.broadcasted_iota(jnp.int32, (T, T), 1)
    sgn = 1 - 2 * d
    incl = (col - row) * sgn <= 0
    incl_t = (row - col) * sgn <= 0
    b_c = jnp.sum(jnp.where(incl, f_r, 0.0), axis=1, keepdims=True)
    b_r = jnp.sum(jnp.where(incl_t, f_c, 0.0), axis=0, keepdims=True)
    f_tot = jnp.sum(f_r, axis=1, keepdims=True)
    m_prev = m_ref[...]
    dmat = jnp.where(incl, b_c - b_r + i_r, -jnp.inf)
    inter = b_c + m_prev
    m_t = jnp.maximum(inter, jnp.max(dmat, axis=1, keepdims=True))
    w_intra = jnp.exp(dmat - m_t)
    w_inter = jnp.exp(inter - m_t)
    q = q_ref[...] * (dqk ** -0.5)
    k = k_ref[...]
    v = v_ref[...]
    qb, kb, vb = q.astype(BF16), k.astype(BF16), v.astype(BF16)
    s = _dot_nt(qb, kb) * w_intra
    ct = ct_ref[...]
    n = n_ref[...]
    num = w_inter * _dot(qb, ct.astype(BF16)) + _dot(s.astype(BF16), vb)
    den = w_inter * jnp.sum(q * n, axis=1, keepdims=True) + jnp.sum(s, axis=1, keepdims=True)
    o_ref[...] = num / jnp.maximum(jnp.abs(den), jnp.exp(-m_t))
    g_r = f_tot - b_r + i_r
    g_c = f_tot - b_c + i_c
    m_new = jnp.maximum(f_tot + m_prev, jnp.max(g_r, axis=1, keepdims=True))
    decay = jnp.exp(f_tot + m_prev - m_new)
    wk = jnp.exp(g_c - m_new)
    ct_ref[...] = decay * ct + _dot_tn(kb, (wk * v).astype(BF16))
    n_ref[...] = decay * n + jnp.sum(wk * k, axis=0, keepdims=True)
    m_ref[...] = m_new


def mlstm_scan(u, gates, b_if, *, n_ctx):
    NT = u.shape[0]
    H, T = ML_HEADS, ML_CHUNK
    D = u.shape[1] // 3
    dqk, dv = D // (2 * H), D // H
    nc, ncc = NT // T, n_ctx // T
    ncl = nc - ncc
    g4 = gates[:, :4 * H].T
    g_col = g4.reshape(4 * H, NT, 1)
    g_row = g4.reshape(4 * H, nc, 1, T)

    def chunk(d, t):
        fwd = jnp.where(t < ncc, ncl + t, t - ncc)
        return jnp.where(d == 0, fwd, nc - 1 - t)

    kq, kk, kv = 0, (H * dqk) // dqk, (2 * H * dqk) // dv
    col_spec = lambda off: pl.BlockSpec((None, T, 1), lambda d, h, t: ((2 * d + off) * H + h, chunk(d, t), 0))
    row_spec = lambda off: pl.BlockSpec((None, None, 1, T), lambda d, h, t: ((2 * d + off) * H + h, chunk(d, t), 0, 0))
    return pl.pallas_call(
        functools.partial(_mlstm_kernel, T=T, dqk=dqk),
        grid=(2, H, nc),
        in_specs=[
            pl.BlockSpec(memory_space=pltpu.SMEM),
            pl.BlockSpec((T, dqk), lambda d, h, t: (chunk(d, t), kq + h)),
            pl.BlockSpec((T, dqk), lambda d, h, t: (chunk(d, t), kk + h)),
            pl.BlockSpec((T, dv), lambda d, h, t: (chunk(d, t), kv + h)),
            col_spec(0), col_spec(1), row_spec(0), row_spec(1),
        ],
        out_specs=pl.BlockSpec((None, T, dv), lambda d, h, t: (d, chunk(d, t), h)),
        out_shape=jax.ShapeDtypeStruct((2, NT, H * dv), F32),
        scratch_shapes=[pltpu.VMEM((dqk, dv), F32), pltpu.VMEM((1, dqk), F32), pltpu.VMEM((1, 1), F32)],
        compiler_params=_params("parallel", "parallel", "arbitrary"),
    )(b_if, u, u, u, g_col, g_col, g_row, g_row)


def _mlgate_kernel(hs_ref, o_ref, g_ref, out_ref, *, dv):
    D = out_ref.shape[1]
    for h in range(D // dv):
        sl = slice(h * dv, (h + 1) * dv)
        x = hs_ref[0, :, sl] + hs_ref[1, :, sl]
        r = lax.rsqrt(jnp.mean(jnp.square(x), axis=-1, keepdims=True) + RMS_EPS)
        out_ref[:, sl] = (x * r * g_ref[:, sl] * jax.nn.sigmoid(o_ref[:, sl])).astype(out_ref.dtype)


def mlstm_gate(hs, u, norm_g, *, o_col_block):
    _, NT, D = hs.shape
    tr = 128
    return pl.pallas_call(
        functools.partial(_mlgate_kernel, dv=D // ML_HEADS),
        grid=(NT // tr,),
        in_specs=[
            pl.BlockSpec((2, tr, D), lambda i: (0, i, 0)),
            pl.BlockSpec((tr, D), lambda i: (i, o_col_block)),
            pl.BlockSpec((1, D), lambda i: (0, 0)),
        ],
        out_specs=pl.BlockSpec((tr, D), lambda i: (i, 0)),
        out_shape=jax.ShapeDtypeStruct((NT, D), BF16),
        compiler_params=_params("parallel"),
    )(hs, u, norm_g.reshape(1, D))


def _rope_kernel(x_ref, cos_ref, sin_ref, o_ref):
    cos, sin = cos_ref[...], sin_ref[...]
    lane = lax.broadcasted_iota(jnp.int32, cos.shape, 1)
    first_half = (lane % 64) < 32
    for g in range(x_ref.shape[1] // LANES):
        sl = slice(g * LANES, (g + 1) * LANES)
        x = x_ref[:, sl]
        rot = jnp.where(first_half, -pltpu.roll(x, LANES - 32, 1), pltpu.roll(x, 32, 1))
        o_ref[:, sl] = (x * cos + rot * sin).astype(o_ref.dtype)


def rope_cast(x, cos, sin, *, row0, nrows, col0, ncols):
    tr = _pick(nrows, (256, 128))
    tc = _pick(ncols, (1024, 512, 256, 128))
    rb, cb = row0 // tr, col0 // tc
    assert row0 % tr == 0 and col0 % tc == 0
    return pl.pallas_call(
        _rope_kernel,
        grid=(nrows // tr, ncols // tc),
        in_specs=[
            pl.BlockSpec((tr, tc), lambda i, j: (i + rb, j + cb)),
            pl.BlockSpec((tr, LANES), lambda i, j: (i, 0)),
            pl.BlockSpec((tr, LANES), lambda i, j: (i, 0)),
        ],
        out_specs=pl.BlockSpec((tr, tc), lambda i, j: (i, j)),
        out_shape=jax.ShapeDtypeStruct((nrows, ncols), BF16),
        compiler_params=_params("parallel", "parallel"),
    )(x, cos, sin)


def _dattn_kernel(lam_ref, q_ref, k_ref, v_ref, g_ref, o_ref, m_ref, l_ref, acc_ref,
                  *, tk, n_chunks, out_scale, lambda_init):
    dh = DA_HEAD_DIM
    m_ref[...] = jnp.full_like(m_ref, -jnp.inf)
    l_ref[...] = jnp.zeros_like(l_ref)
    acc_ref[...] = jnp.zeros_like(acc_ref)

    def body(c, carry):
        start = pl.multiple_of(c * tk, tk)
        kc = k_ref[pl.ds(start, tk), :]
        vc = v_ref[pl.ds(start, tk), :]
        for i in range(2):
            s = _dot_nt(q_ref[:, i * dh:(i + 1) * dh], kc[:, i * dh:(i + 1) * dh])
            m_old = m_ref[i]
            m_new = jnp.maximum(m_old, jnp.max(s, axis=1, keepdims=True))
            p = jnp.exp2(s - m_new)
            alpha = jnp.exp2(m_old - m_new)
            l_ref[i] = alpha * l_ref[i] + jnp.sum(p, axis=1, keepdims=True)
            acc_ref[i] = alpha * acc_ref[i] + _dot(p.astype(BF16), vc)
            m_ref[i] = m_new
        return carry

    lax.fori_loop(0, n_chunks, body, 0)
    lam = (jnp.exp(jnp.sum(lam_ref[0:1, :] * lam_ref[1:2, :], axis=1, keepdims=True))
           - jnp.exp(jnp.sum(lam_ref[2:3, :] * lam_ref[3:4, :], axis=1, keepdims=True)) + lambda_init)
    o = acc_ref[0] / l_ref[0] - lam * (acc_ref[1] / l_ref[1])
    r = lax.rsqrt(jnp.mean(jnp.square(o), axis=-1, keepdims=True) + RMS_EPS)
    o_ref[...] = (o * r * g_ref[...] * out_scale).astype(o_ref.dtype)


def diff_attention(q, k, v, lam4, subln_g, *, kv_row0, kv_len, lambda_init):
    Nq, D = q.shape
    hd = 2 * DA_HEAD_DIM
    H = D // hd
    tq = _pick(Nq, (512, 256))
    tk = _pick(kv_len, (768, 512, 256))
    kvb = kv_row0 // kv_len
    assert kv_row0 % kv_len == 0
    return pl.pallas_call(
        functools.partial(_dattn_kernel, tk=tk, n_chunks=kv_len // tk, out_scale=1.0 - lambda_init,
                          lambda_init=lambda_init),
        grid=(H, Nq // tq),
        in_specs=[
            pl.BlockSpec((4, DA_HEAD_DIM), lambda h, i: (0, 0)),
            pl.BlockSpec((tq, hd), lambda h, i: (i, h)),
            pl.BlockSpec((kv_len, hd), lambda h, i: (kvb, h)),
            pl.BlockSpec((kv_len, hd), lambda h, i: (kvb, h)),
            pl.BlockSpec((1, hd), lambda h, i: (0, 0)),
        ],
        out_specs=pl.BlockSpec((tq, hd), lambda h, i: (i, h)),
        out_shape=jax.ShapeDtypeStruct((Nq, D), BF16),
        scratch_shapes=[pltpu.VMEM((2, tq, 1), F32), pltpu.VMEM((2, tq, 1), F32), pltpu.VMEM((2, tq, hd), F32)],
        compiler_params=_params("parallel", "parallel"),
    )(lam4, q, k, v, subln_g.reshape(1, hd))


def _rope_tables(L, scale):
    a = DA_HEAD_DIM // 2
    t = jnp.arange(L)
    pos = jnp.stack([t // GRID_W, t % GRID_W], -1).astype(F32)
    inv = ROPE_BASE ** (-jnp.arange(0, a, 2, dtype=F32) / a)
    ang = pos[:, :, None] * inv
    ang = jnp.concatenate([ang, ang], -1).reshape(L, DA_HEAD_DIM)
    return jnp.cos(ang) * scale, jnp.sin(ang) * scale


def _shortconv_kernel(p_ref, prev_ref, next_ref, w_ref, b_ref, o_ref, *, n_seq_first, n_seq_last):
    i = pl.program_id(0)
    x = p_ref[...]
    tr = x.shape[0]
    rid = lax.broadcasted_iota(jnp.int32, x.shape, 0)
    is_first = functools.reduce(jnp.logical_or, [i == b for b in n_seq_first])
    is_last = functools.reduce(jnp.logical_or, [i == b for b in n_seq_last])
    prev_row = jnp.where(is_first, 0.0, prev_ref[7:8, :])
    next_row = jnp.where(is_last, 0.0, next_ref[0:1, :])
    xm = jnp.where(rid == 0, prev_row, pltpu.roll(x, 1, 0))
    xp = jnp.where(rid == tr - 1, next_row, pltpu.roll(x, tr - 1, 0))
    o_ref[...] = xm * w_ref[0:1, :] + x * w_ref[1:2, :] + xp * w_ref[2:3, :] + b_ref[...]


def short_conv(p, w, b, *, n_ctx):
    NT, N = p.shape
    tr = 256
    tc = _pick(N, (1024, 512, 256, 128))
    nb, n8 = NT // tr, NT // 8
    n_lat_blocks = (NT - n_ctx) // tr
    firsts = (0, n_lat_blocks)
    lasts = (n_lat_blocks - 1, nb - 1)
    return pl.pallas_call(
        functools.partial(_shortconv_kernel, n_seq_first=firsts, n_seq_last=lasts),
        grid=(nb, N // tc),
        in_specs=[
            pl.BlockSpec((tr, tc), lambda i, j: (i, j)),
            pl.BlockSpec((8, tc), lambda i, j: (jnp.maximum(i * (tr // 8) - 1, 0), j)),
            pl.BlockSpec((8, tc), lambda i, j: (jnp.minimum((i + 1) * (tr // 8), n8 - 1), j)),
            pl.BlockSpec((3, tc), lambda i, j: (0, j)),
            pl.BlockSpec((1, tc), lambda i, j: (0, j)),
        ],
        out_specs=pl.BlockSpec((tr, tc), lambda i, j: (i, j)),
        out_shape=jax.ShapeDtypeStruct((NT, N), F32),
        compiler_params=_params("parallel", "parallel"),
    )(p, p, p, w, b.reshape(1, N))


def _filter_kernel(zf_ref, t_ref, w1_ref, b1_ref, w2_ref, b2_ref, w3_ref, b3_ref, w4f_ref, w4b_ref, dl_ref,
                   o_ref, asum_ref, *, L):
    i = pl.program_id(1)
    hdn = jnp.sin(_dot(zf_ref[...].astype(BF16), w1_ref[...].astype(BF16)) + b1_ref[...])
    hdn = jnp.sin(_dot(hdn.astype(BF16), w2_ref[...].astype(BF16)) + b2_ref[...])
    hdn = jnp.sin(_dot(hdn.astype(BF16), w3_ref[...].astype(BF16)) + b3_ref[...]).astype(BF16)
    hf = _dot(hdn, w4f_ref[...].astype(BF16))
    hb = _dot(hdn, w4b_ref[...].astype(BF16))
    tr = hf.shape[0]
    ridx = i * tr + lax.broadcasted_iota(jnp.int32, hf.shape, 0)
    kern = jnp.where(ridx < L, hf, jnp.where(ridx > L, hb, 0.0)) * jnp.exp(-t_ref[...] * dl_ref[...])
    o_ref[...] = kern

    @pl.when(i == 0)
    def _():
        asum_ref[...] = jnp.zeros_like(asum_ref)

    asum_ref[...] += jnp.sum(jnp.abs(kern), axis=0, keepdims=True)


def hyena_filters(L, D, f_w1, f_b1, f_w2, f_b2, f_w3, f_b3, f_w4):
    t = jnp.linspace(0.0, 1.0, L, dtype=F32)[:, None]
    w = 2.0 * math.pi * jnp.arange(L, dtype=F32)[:, None] / L
    f = jnp.linspace(1e-4, HY_BANDS - 1, HY_BANDS, dtype=F32)[None]
    z = jnp.concatenate([t, jnp.cos(w * f), -jnp.sin(w * f)], -1)
    src = jnp.concatenate([jnp.arange(L), jnp.zeros((1,), jnp.int32), L - 1 - jnp.arange(L - 1)])
    zf = jnp.pad(z[src], ((0, 0), (0, LANES - HY_EMB)))
    tcol = t[src]
    deltas = jnp.abs(jnp.linspace(math.log(HY_DECAY_TARGET) / HY_SLOW_DECAY, math.log(HY_DECAY_TARGET) / HY_FAST_DECAY,
                                  D, dtype=F32)).reshape(1, D)
    w1 = jnp.pad(f_w1, ((0, LANES - HY_EMB), (0, 0)))
    nf = f_w1.shape[1]
    R2 = 2 * L
    tr = _pick(R2, (512,))
    tc = _pick(D, (1024, 512, 256, 128))
    cpd = D // tc
    full = lambda shape: pl.BlockSpec(shape, lambda j, i: (0, 0))
    return pl.pallas_call(
        functools.partial(_filter_kernel, L=L),
        grid=(HY_ORDER * cpd, R2 // tr),
        in_specs=[
            pl.BlockSpec((tr, LANES), lambda j, i: (i, 0)),
            pl.BlockSpec((tr, 1), lambda j, i: (i, 0)),
            full((LANES, nf)), full((1, nf)), full((nf, nf)), full((1, nf)), full((nf, nf)), full((1, nf)),
            pl.BlockSpec((nf, tc), lambda j, i: (0, (j // cpd) * 2 * cpd + j % cpd)),
            pl.BlockSpec((nf, tc), lambda j, i: (0, ((j // cpd) * 2 + 1) * cpd + j % cpd)),
            pl.BlockSpec((1, tc), lambda j, i: (0, j % cpd)),
        ],
        out_specs=[pl.BlockSpec((tr, tc), lambda j, i: (i, j)), pl.BlockSpec((1, tc), lambda j, i: (0, j))],
        out_shape=[jax.ShapeDtypeStruct((R2, HY_ORDER * D), F32), jax.ShapeDtypeStruct((1, HY_ORDER * D), F32)],
        compiler_params=_params("parallel", "arbitrary"),
    )(zf, tcol, w1, f_b1.reshape(1, nf), f_w2, f_b2.reshape(1, nf), f_w3, f_b3.reshape(1, nf), f_w4, f_w4, deltas)


FFT_N2 = 128
FFT_TT2 = 8


def _split(x):
    hi = x.astype(BF16)
    return hi, (x - hi.astype(F32)).astype(BF16)


def _dot3(w_hi, w_lo, x):
    x_hi, x_lo = _split(x)
    return _dot(w_hi, x_hi) + (_dot(w_lo, x_hi) + _dot(w_hi, x_lo))


def _hilo(a):
    return _split(jnp.asarray(a, F32))


def _fft_first_kernel(whi_ref, wlo_ref, z_ref, o_ref):
    n1 = o_ref.shape[1]
    for j in range(z_ref.shape[1]):
        a = _dot3(whi_ref[...], wlo_ref[...], z_ref[:, j, :])
        o_ref[0, :, j, :] = a[:n1]
        o_ref[1, :, j, :] = a[n1:]


def fft_first(w_hl, z3, *, col0, C):
    w_hi, w_lo = w_hl
    n1x2, l1 = w_hi.shape
    n1, n2 = n1x2 // 2, z3.shape[1]
    tt2 = FFT_TT2
    tc = _pick(C, (512, 256, 128))
    cb = col0 // tc
    full = pl.BlockSpec((n1x2, l1), lambda t, j: (0, 0))
    return pl.pallas_call(
        _fft_first_kernel,
        grid=(n2 // tt2, C // tc),
        in_specs=[full, full, pl.BlockSpec((l1, tt2, tc), lambda t, j: (0, t, j + cb))],
        out_specs=pl.BlockSpec((2, n1, tt2, tc), lambda t, j: (0, 0, t, j)),
        out_shape=jax.ShapeDtypeStruct((2, n1, n2, C), F32),
        compiler_params=_params("parallel", "parallel"),
    )(w_hi, w_lo, z3)


def _fft_mid_kernel(*refs, g, conv):
    if conv:
        a_ref, kf_ref, twc_ref, tws_ref, w2h_ref, w2l_ref, w2ih_ref, w2il_ref, o_ref = refs
    else:
        a_ref, twc_ref, tws_ref, w2h_ref, w2l_ref, o_ref = refs
    n2 = a_ref.shape[2]
    for kk in range(g):
        ar, ai = a_ref[0, kk], a_ref[1, kk]
        c, s = twc_ref[kk], tws_ref[kk]
        x = _dot3(w2h_ref[...], w2l_ref[...], jnp.concatenate([ar * c + ai * s, ai * c - ar * s], axis=0))
        if conv:
            xr, xi = x[:n2], x[n2:]
            kr, ki = kf_ref[0, kk], kf_ref[1, kk]
            y = _dot3(w2ih_ref[...], w2il_ref[...], jnp.concatenate([xr * kr - xi * ki, xr * ki + xi * kr], axis=0))
            yr, yi = y[:n2], y[n2:]
            o_ref[0, kk] = yr * c - yi * s
            o_ref[1, kk] = yr * s + yi * c
        else:
            o_ref[0, kk] = x[:n2]
            o_ref[1, kk] = x[n2:]


def fft_mid(a, tw_c, tw_s, w2_hl, w2i_hl=None, kf=None, *, kf_col0=0):
    _, n1, n2, C = a.shape
    g = 4
    tc = _pick(C, (512, 256, 128))
    conv = kf is not None
    blk = pl.BlockSpec((2, g, n2, tc), lambda j, k: (0, k, 0, j))
    tw = pl.BlockSpec((g, n2, 1), lambda j, k: (k, 0, 0))
    mat = pl.BlockSpec((2 * n2, 2 * n2), lambda j, k: (0, 0))
    if conv:
        kb = kf_col0 // tc
        in_specs = [blk, pl.BlockSpec((2, g, n2, tc), lambda j, k: (0, k, 0, j + kb)), tw, tw, mat, mat, mat, mat]
        args = [a, kf, tw_c, tw_s, *w2_hl, *w2i_hl]
    else:
        in_specs = [blk, tw, tw, mat, mat]
        args = [a, tw_c, tw_s, *w2_hl]
    return pl.pallas_call(
        functools.partial(_fft_mid_kernel, g=g, conv=conv),
        grid=(C // tc, n1 // g),
        in_specs=in_specs,
        out_specs=blk,
        out_shape=jax.ShapeDtypeStruct(a.shape, F32),
        compiler_params=_params("parallel", "parallel"),
    )(*args)


def _fft_last_kernel(whi_ref, wlo_ref, b_ref, gate_ref, zp_ref, asum_ref, bias_ref, o_ref):
    inv_asum = 1.0 / asum_ref[...]
    for j in range(b_ref.shape[2]):
        bj = jnp.concatenate([b_ref[0, :, j, :], b_ref[1, :, j, :]], axis=0)
        x = _dot3(whi_ref[...], wlo_ref[...], bj)
        o_ref[:, j, :] = gate_ref[:, j, :] * (x * inv_asum + zp_ref[:, j, :] * bias_ref[...])


def fft_last(w_hl, b, gate3, gate_col0, zp3, zp_col0, asum, bias, *, vec_col0):
    w_hi, w_lo = w_hl
    l1 = w_hi.shape[0]
    _, n1, n2, C = b.shape
    tt2 = FFT_TT2
    tc = _pick(C, (512, 256, 128))
    gcb, zcb, vcb = gate_col0 // tc, zp_col0 // tc, vec_col0 // tc
    full = pl.BlockSpec((l1, 2 * n1), lambda t, j: (0, 0))
    return pl.pallas_call(
        _fft_last_kernel,
        grid=(n2 // tt2, C // tc),
        in_specs=[
            full, full,
            pl.BlockSpec((2, n1, tt2, tc), lambda t, j: (0, 0, t, j)),
            pl.BlockSpec((l1, tt2, tc), lambda t, j: (0, t, j + gcb)),
            pl.BlockSpec((l1, tt2, tc), lambda t, j: (0, t, j + zcb)),
            pl.BlockSpec((1, tc), lambda t, j: (0, j + vcb)),
            pl.BlockSpec((1, tc), lambda t, j: (0, j + vcb)),
        ],
        out_specs=pl.BlockSpec((l1, tt2, tc), lambda t, j: (0, t, j)),
        out_shape=jax.ShapeDtypeStruct((l1, n2, C), F32),
        compiler_params=_params("parallel", "parallel"),
    )(w_hi, w_lo, b, gate3, zp3, asum, bias)


def _hpmm_kernel(*refs, gated):
    if gated:
        ah_ref, al_ref, b_ref, gate_ref, z_ref, asum_ref, bias_ref, o_ref = refs
    else:
        ah_ref, al_ref, b_ref, o_ref = refs
    y = _dot3(ah_ref[...], al_ref[...], b_ref[...])
    if gated:
        y = gate_ref[...] * (y / asum_ref[...] + z_ref[...] * bias_ref[...])
    o_ref[...] = y


def hp_matmul(a_hl, b, *, gate=None, z=None, asum=None, bias=None, vec_col0=0):
    a_hi, a_lo = a_hl
    M, Kd = a_hi.shape
    N = b.shape[1]
    tn = _pick(N, (1024, 512, 256, 128))
    full = pl.BlockSpec((M, Kd), lambda j: (0, 0))
    in_specs = [full, full, pl.BlockSpec((Kd, tn), lambda j: (0, j))]
    args = [a_hi, a_lo, b]
    gated = gate is not None
    if gated:
        vb = vec_col0 // tn
        blk = pl.BlockSpec((M, tn), lambda j: (0, j))
        vec = pl.BlockSpec((1, tn), lambda j: (0, j + vb))
        in_specs += [blk, blk, vec, vec]
        args += [gate, z, asum, bias]
    return pl.pallas_call(
        functools.partial(_hpmm_kernel, gated=gated),
        grid=(N // tn,),
        in_specs=in_specs,
        out_specs=pl.BlockSpec((M, tn), lambda j: (0, j)),
        out_shape=jax.ShapeDtypeStruct((M, N), F32),
        compiler_params=_params("parallel"),
    )(*args)


def _cmul_kernel(x_ref, k_ref, o_ref):
    xr, xi, kr, ki = x_ref[0], x_ref[1], k_ref[0], k_ref[1]
    o_ref[0] = xr * kr - xi * ki
    o_ref[1] = xr * ki + xi * kr


def complex_mul(x, kf, *, kf_col0):
    _, R, C = x.shape
    tc = _pick(C, (512, 256, 128))
    kb = kf_col0 // tc
    return pl.pallas_call(
        _cmul_kernel,
        grid=(C // tc,),
        in_specs=[pl.BlockSpec((2, R, tc), lambda j: (0, 0, j)), pl.BlockSpec((2, R, tc), lambda j: (0, 0, j + kb))],
        out_specs=pl.BlockSpec((2, R, tc), lambda j: (0, 0, j)),
        out_shape=jax.ShapeDtypeStruct(x.shape, F32),
        compiler_params=_params("parallel"),
    )(x, kf)


def _dft_constants(n1, n2, l1):
    N = n1 * n2
    k1 = np.arange(n1)[:, None]
    ang1 = 2 * np.pi * k1 * np.arange(n1)[None, :] / n1
    w1 = np.concatenate([np.cos(ang1), -np.sin(ang1)], 0)
    w1_inv = np.concatenate([np.cos(ang1[:l1]), -np.sin(ang1[:l1])], 1) / N
    ang_t = 2 * np.pi * k1 * np.arange(n2)[None, :] / N
    k2 = np.arange(n2)[:, None]
    ang2 = 2 * np.pi * k2 * np.arange(n2)[None, :] / n2
    c2, s2 = np.cos(ang2), np.sin(ang2)
    f = lambda x: jnp.asarray(x, F32)
    return dict(w1_data=_hilo(w1[:, :l1]), w1_full=_hilo(w1), w1_inv=_hilo(w1_inv),
                tw_c=f(np.cos(ang_t))[:, :, None], tw_s=f(np.sin(ang_t))[:, :, None],
                w2=_hilo(np.block([[c2, s2], [-s2, c2]])), w2_inv=_hilo(np.block([[c2, -s2], [s2, c2]])))


def hyena_conv_lat(u3, kern, asum, bias, *, L, D):
    n2 = FFT_N2
    n1, l1 = 2 * L // n2, L // n2
    cst = _dft_constants(n1, n2, l1)
    ka = fft_first(cst["w1_full"], kern.reshape(n1, n2, HY_ORDER * D), col0=0, C=HY_ORDER * D)
    kf = fft_mid(ka, cst["tw_c"], cst["tw_s"], cst["w2"])
    z3, zc0 = u3, 0
    for o in range(HY_ORDER):
        a = fft_first(cst["w1_data"], z3, col0=zc0, C=D)
        bmid = fft_mid(a, cst["tw_c"], cst["tw_s"], cst["w2"], cst["w2_inv"], kf, kf_col0=o * D)
        z3 = fft_last(cst["w1_inv"], bmid, u3, (o + 1) * D, z3, zc0, asum, bias, vec_col0=o * D)
        zc0 = 0
    return z3.reshape(L, D)


def _dense_dft_constants(L):
    N = 2 * L
    ang = 2 * np.pi * np.arange(N)[:, None] * np.arange(N)[None, :] / N
    wf = np.concatenate([np.cos(ang), -np.sin(ang)], 0)
    wi = np.concatenate([np.cos(ang[:L]), -np.sin(ang[:L])], 1) / N
    return _hilo(wf), _hilo(wf[:, :L]), _hilo(wi)


def hyena_conv_ctx(v, x1, x2, kern, asum, bias):
    L, D = v.shape
    wf, wf_data, wi = _dense_dft_constants(L)
    kf = hp_matmul(wf, kern).reshape(2, 2 * L, HY_ORDER * D)
    z = v
    for o, gate in enumerate((x1, x2)):
        xf = hp_matmul(wf_data, z).reshape(2, 2 * L, D)
        y = complex_mul(xf, kf, kf_col0=o * D).reshape(4 * L, D)
        z = hp_matmul(wi, y, gate=gate, z=z, asum=asum, bias=bias, vec_col0=o * D)
    return z


def _na_kernel(q_ref, k_ref, v_ref, bias_ref, o_ref, *, n_ctx, rows, heads_per_step):
    r = pl.program_id(1)
    kr = NA_ROWS
    r0 = jnp.clip(r - kr // 2, 0, rows - kr)
    start = pl.multiple_of(r0 * GRID_W, GRID_W)
    n_lat = rows * GRID_W
    dh = NA_HEAD_DIM
    scale = dh ** -0.5
    for hh in range(heads_per_step):
        sl = slice(hh * dh, (hh + 1) * dh)
        q = q_ref[:, sl]
        s_ctx = _dot_nt(q, k_ref[n_lat:n_lat + n_ctx, sl]) * scale
        s_lat = _dot_nt(q, k_ref[pl.ds(start, kr * GRID_W), sl]) * scale + bias_ref[hh]
        m = jnp.maximum(jnp.max(s_ctx, axis=1, keepdims=True), jnp.max(s_lat, axis=1, keepdims=True))
        p_ctx = jnp.exp(s_ctx - m)
        p_lat = jnp.exp(s_lat - m)
        l = jnp.sum(p_ctx, axis=1, keepdims=True) + jnp.sum(p_lat, axis=1, keepdims=True)
        o = (_dot(p_ctx.astype(BF16), v_ref[n_lat:n_lat + n_ctx, sl])
             + _dot(p_lat.astype(BF16), v_ref[pl.ds(start, kr * GRID_W), sl]))
        o_ref[:, sl] = (o / l).astype(o_ref.dtype)


def _na_bias_table(rpb, rows):
    H = rpb.shape[0]
    cols = jnp.arange(GRID_W)
    c0 = jnp.clip(cols - NA_COLS // 2, 0, GRID_W - NA_COLS)
    kc = jnp.arange(GRID_W)[None, :]
    inside = (kc >= c0[:, None]) & (kc < c0[:, None] + NA_COLS)
    rel = jnp.clip(kc - cols[:, None] + NA_COLS - 1, 0, 2 * NA_COLS - 2)
    rr = jnp.arange(NA_ROWS)[:, None] + jnp.arange(NA_ROWS)[None, :]
    t = rpb[:, rr]
    t = t[:, :, :, rel]
    t = jnp.where(inside[None, None, None], t, -1e30)
    t = jnp.transpose(t, (1, 0, 3, 2, 4))
    return t.reshape(NA_ROWS, H, GRID_W, NA_ROWS * GRID_W)


def na_attention(qkv, rpb, *, n_ctx):
    NT, D3 = qkv.shape
    D = D3 // 3
    L = NT - n_ctx
    rows = L // GRID_W
    hps = 4
    hw = hps * NA_HEAD_DIM
    nhb = D // hw
    table = _na_bias_table(rpb, rows)

    def pattern(r):
        r0 = jnp.clip(r - NA_ROWS // 2, 0, rows - NA_ROWS)
        return r0 - r + NA_ROWS - 1

    return pl.pallas_call(
        functools.partial(_na_kernel, n_ctx=n_ctx, rows=rows, heads_per_step=hps),
        grid=(nhb, rows),
        in_specs=[
            pl.BlockSpec((GRID_W, hw), lambda h, r: (r, h)),
            pl.BlockSpec((NT, hw), lambda h, r: (0, nhb + h)),
            pl.BlockSpec((NT, hw), lambda h, r: (0, 2 * nhb + h)),
            pl.BlockSpec((None, hps, GRID_W, NA_ROWS * GRID_W), lambda h, r: (pattern(r), h, 0, 0)),
        ],
        out_specs=pl.BlockSpec((GRID_W, hw), lambda h, r: (r, h)),
        out_shape=jax.ShapeDtypeStruct((L, D), BF16),
        compiler_params=_params("parallel", "parallel"),
    )(qkv, qkv, qkv, table)


def _router_kernel(x_ref, w_ref, b_ref, o_ref):
    E, G = N_EXPERTS, N_GROUPS
    per = E // G
    logits = _dot_nt(w_ref[...].astype(BF16), x_ref[...])
    scores = jax.nn.sigmoid(logits)
    sel = scores + b_ref[...]
    tm = sel.shape[1]
    grp = sel.reshape(G, per, tm)
    eidx = lax.broadcasted_iota(jnp.int32, grp.shape, 1)
    m1 = jnp.max(grp, axis=1, keepdims=True)
    first = jnp.min(jnp.where(grp == m1, eidx, per), axis=1, keepdims=True)
    m2 = jnp.max(jnp.where(eidx == first, -jnp.inf, grp), axis=1, keepdims=True)
    gs = (m1 + m2).reshape(G, tm)
    gidx = lax.broadcasted_iota(jnp.int32, gs.shape, 0)
    gmask = jnp.zeros(gs.shape, jnp.bool_)
    for _ in range(TOPK_GROUPS):
        mx = jnp.max(gs, axis=0, keepdims=True)
        pick = gidx == jnp.min(jnp.where(gs == mx, gidx, G), axis=0, keepdims=True)
        gmask = jnp.logical_or(gmask, pick)
        gs = jnp.where(pick, -jnp.inf, gs)
    emask = jnp.broadcast_to(gmask.reshape(G, 1, tm), (G, per, tm)).reshape(E, tm)
    cand = jnp.where(emask, sel, -jnp.inf)
    xidx = lax.broadcasted_iota(jnp.int32, cand.shape, 0)
    chosen = jnp.zeros(cand.shape, jnp.bool_)
    for _ in range(TOP_K):
        mx = jnp.max(cand, axis=0, keepdims=True)
        pick = xidx == jnp.min(jnp.where(cand == mx, xidx, E), axis=0, keepdims=True)
        chosen = jnp.logical_or(chosen, pick)
        cand = jnp.where(pick, -jnp.inf, cand)
    wsel = jnp.where(chosen, scores, 0.0)
    gates = wsel / jnp.sum(wsel, axis=0, keepdims=True) * ROUTED_SCALE
    o_ref[...] = gates.T


def moe_router(h, router_w_t, router_b, layer):
    T, D = h.shape
    E = router_w_t.shape[1]
    tm = _pick(T, (256, 128))
    return pl.pallas_call(
        _router_kernel,
        grid=(T // tm,),
        in_specs=[
            pl.BlockSpec((tm, D), lambda i: (i, 0)),
            pl.BlockSpec((None, E, D), lambda i: (layer, 0, 0)),
            pl.BlockSpec((None, E, 1), lambda i: (layer, 0, 0)),
        ],
        out_specs=pl.BlockSpec((tm, E), lambda i: (i, 0)),
        out_shape=jax.ShapeDtypeStruct((T, E), F32),
        compiler_params=_params("parallel"),
    )(h, router_w_t, router_b.reshape(-1, E, 1))


def _moe_up_kernel(*refs, nk, ne, gated, F):
    if gated:
        x_ref, wg_ref, wu_ref, g_ref, o_ref, accg_ref, accu_ref = refs
    else:
        x_ref, wg_ref, wu_ref, o_ref, accg_ref, accu_ref = refs
    e0 = pl.program_id(1) * ne
    k = pl.program_id(2)

    @pl.when(k == 0)
    def _():
        accg_ref[...] = jnp.zeros_like(accg_ref)
        accu_ref[...] = jnp.zeros_like(accu_ref)

    def experts_side_by_side(ref):
        w = ref[0] if ne == 1 else jnp.concatenate([ref[ee] for ee in range(ne)], axis=1)
        return w.astype(BF16)

    x = x_ref[...]
    accg_ref[...] += _dot(x, experts_side_by_side(wg_ref))
    accu_ref[...] += _dot(x, experts_side_by_side(wu_ref))

    @pl.when(k == nk - 1)
    def _():
        a = accg_ref[...]
        hid = a * jax.nn.sigmoid(a) * accu_ref[...]
        if gated:
            g = g_ref[...]
            lane = lax.broadcasted_iota(jnp.int32, g.shape, 1)
            col = lax.broadcasted_iota(jnp.int32, (1, ne * F), 1) // F
            gexp = jnp.zeros_like(hid)
            for ee in range(ne):
                ge = jnp.sum(jnp.where(lane == e0 + ee, g, 0.0), axis=1, keepdims=True)
                gexp = jnp.where(col == ee, ge, gexp)
            hid = hid * gexp
        o_ref[...] = hid.astype(o_ref.dtype)


def moe_up(x, w_gate, w_up, layer, gates=None):
    T, D = x.shape
    if w_gate.ndim == 3:
        w_gate, w_up = w_gate[:, None], w_up[:, None]
    E, F = w_gate.shape[1], w_gate.shape[3]
    ne = 4 if E % 4 == 0 else 1
    tm = _pick(T, (1056, 1024, 512, 256, 128))
    tk = _pick(D, (1024, 512, 256, 128))
    nk = D // tk
    gated = gates is not None
    w_spec = pl.BlockSpec((None, ne, tk, F), lambda i, e, k: (layer, e, k, 0))
    in_specs = [pl.BlockSpec((tm, tk), lambda i, e, k: (i, k)), w_spec, w_spec]
    args = [x, w_gate, w_up]
    if gated:
        in_specs.append(pl.BlockSpec((tm, E), lambda i, e, k: (i, 0)))
        args.append(gates)
    return pl.pallas_call(
        functools.partial(_moe_up_kernel, nk=nk, ne=ne, gated=gated, F=F),
        grid=(T // tm, E // ne, nk),
        in_specs=in_specs,
        out_specs=pl.BlockSpec((tm, ne * F), lambda i, e, k: (i, e)),
        out_shape=jax.ShapeDtypeStruct((T, E * F), BF16),
        scratch_shapes=[pltpu.VMEM((tm, ne * F), F32), pltpu.VMEM((tm, ne * F), F32)],
        compiler_params=_params("parallel", "parallel", "arbitrary"),
    )(*args)


def moe(h, layer, router_w_t, router_b, w_gate, w_up, w_down, s_gate, s_up, s_down):
    gates = moe_router(h, router_w_t, router_b, layer)
    hid = moe_up(h, w_gate, w_up, layer, gates)
    hid_sh = moe_up(h, s_gate, s_up, layer)
    E, F, D = w_down.shape[1:]
    return matmul(hid, w_down.reshape(DEPTH, E * F, D), out_dtype=F32, layer=layer, extra=(hid_sh, s_down, layer))


def kernel(x, c, ctx, c_ctx, ada_a, ada_b, ada_bias, ln1_g, ln1_b, ln2_g, ln2_b, router_w, router_b, moe_w_gate, moe_w_up, moe_w_down, sh_w_gate, sh_w_up, sh_w_down, ml_w_in, ml_b_if, ml_norm_g, ml_w_out, da_w_qkv, da_lam_q1, da_lam_k1, da_lam_q2, da_lam_k2, da_subln_g, da_w_out, hy_w_in, hy_conv_w, hy_conv_b, hy_f_w1, hy_f_b1, hy_f_w2, hy_f_b2, hy_f_w3, hy_f_b3, hy_f_w4, hy_bias, hy_w_out, na_w_qkv, na_rpb, na_w_out):
    _, L, D = x.shape
    C = ctx.shape[1]
    NT = C + L
    s = jnp.concatenate([x[0], ctx[0]], axis=0)

    cc = jnp.zeros((8, D), F32).at[0].set(c_ctx).at[1].set(c[0])
    mods = ada_modulation(cc, ada_a, ada_b, ada_bias)[:, :2].reshape(DEPTH, 2, 6, 1, D)
    SH1, SC1, G1, SH2, SC2, G2 = range(6)
    router_w_t = jnp.swapaxes(router_w, 1, 2)
    moe_args = (router_w_t, router_b, moe_w_gate, moe_w_up, moe_w_down, sh_w_gate, sh_w_up, sh_w_down)

    (h,) = ln_modulate(s, mods, n_ctx=C, h_layer=0, sc_idx=SC1, sh_idx=SH1)
    for i in range(DEPTH):
        last = i == DEPTH - 1
        if i == 0:
            y = mlstm_mixer(h, ml_w_in, ml_b_if[0], ml_norm_g[0], ml_w_out, n_ctx=C)
        elif i == 1:
            lam4 = jnp.stack([da_lam_q1[0], da_lam_k1[0], da_lam_q2[0], da_lam_k2[0]])
            y = diff_attn_mixer(h, da_w_qkv, lam4, da_subln_g[0], da_w_out, n_ctx=C,
                                lambda_init=0.8 - 0.6 * math.exp(-0.3 * i))
        elif i == 2:
            y = hyena_mixer(h, hy_w_in, hy_conv_w[0], hy_conv_b[0], hy_f_w1[0], hy_f_b1[0], hy_f_w2[0], hy_f_b2[0],
                            hy_f_w3[0], hy_f_b3[0], hy_f_w4[0], hy_bias[0], hy_w_out, n_ctx=C)
        else:
            y = na_mixer(h, na_w_qkv, na_rpb[0], na_w_out, n_ctx=C)
        n_ctx = C
        if last:
            s, n_ctx = s[:L], 0
        s, h = ln_modulate(s, mods, n_ctx=n_ctx, ys=(y,), ln_layer=i, gate_idx=G1, ln_g=ln1_g, ln_b=ln1_b,
                           h_layer=i, sc_idx=SC2, sh_idx=SH2)
        y = moe(h, i, *moe_args)
        if last:
            (s,) = ln_modulate(s, mods, n_ctx=n_ctx, ys=(y,), ln_layer=i, gate_idx=G2, ln_g=ln2_g, ln_b=ln2_b)
        else:
            s, h = ln_modulate(s, mods, n_ctx=n_ctx, ys=(y,), ln_layer=i, gate_idx=G2, ln_g=ln2_g, ln_b=ln2_b,
                               h_layer=i + 1, sc_idx=SC1, sh_idx=SH1)
    return s[None]


def mlstm_mixer(h, w_in, b_if, norm_g, w_out, *, n_ctx):
    D = h.shape[1]
    u = matmul(h, w_in, out_dtype=F32, layer=0, ncols=3 * D)
    w_if = jnp.pad(w_in[0, :, 3 * D:], ((0, 0), (0, LANES - 4 * ML_HEADS)))
    g_if = matmul(h, w_if, out_dtype=F32)
    hs = mlstm_scan(u, g_if, b_if, n_ctx=n_ctx)
    return matmul(mlstm_gate(hs, u, norm_g, o_col_block=2), w_out, out_dtype=F32, layer=0)


def diff_attn_mixer(h, w_qkv, lam4, subln_g, w_out, *, n_ctx, lambda_init):
    NT, D = h.shape
    L = NT - n_ctx
    scale = DA_HEAD_DIM ** -0.5 * math.log2(math.e)
    qk = matmul(h, w_qkv, out_dtype=F32, layer=0, ncols=2 * D)
    v = matmul(h, w_qkv, out_dtype=BF16, layer=0, col0=2 * D, ncols=D)
    cos, sin = _rope_tables(L, 1.0)
    ones, zeros = jnp.ones((n_ctx, LANES), F32), jnp.zeros((n_ctx, LANES), F32)
    q_lat = rope_cast(qk, cos * scale, sin * scale, row0=0, nrows=L, col0=0, ncols=D)
    q_ctx = rope_cast(qk, ones * scale, zeros, row0=L, nrows=n_ctx, col0=0, ncols=D)
    k = rope_cast(qk, jnp.concatenate([cos, ones]), jnp.concatenate([sin, zeros]), row0=0, nrows=NT, col0=D, ncols=D)
    o_lat = diff_attention(q_lat, k, v, lam4, subln_g, kv_row0=0, kv_len=NT, lambda_init=lambda_init)
    o_ctx = diff_attention(q_ctx, k, v, lam4, subln_g, kv_row0=L, kv_len=n_ctx, lambda_init=lambda_init)
    return matmul(jnp.concatenate([o_lat, o_ctx]), w_out, out_dtype=F32, layer=0)


def hyena_mixer(h, w_in, conv_w, conv_b, f_w1, f_b1, f_w2, f_b2, f_w3, f_b3, f_w4, hy_bias, w_out, *, n_ctx):
    NT, D = h.shape
    L = NT - n_ctx
    u = short_conv(matmul(h, w_in, out_dtype=F32, layer=0), conv_w, conv_b, n_ctx=n_ctx)
    fargs = (f_w1, f_b1, f_w2, f_b2, f_w3, f_b3, f_w4)
    bias = hy_bias.reshape(1, HY_ORDER * D)
    kern_l, asum_l = hyena_filters(L, D, *fargs)
    z_lat = hyena_conv_lat(u.reshape(NT // FFT_N2, FFT_N2, 3 * D), kern_l, asum_l, bias, L=L, D=D)
    kern_c, asum_c = hyena_filters(n_ctx, D, *fargs)
    v, x1, x2 = (u[L:, j * D:(j + 1) * D] for j in range(3))
    z_ctx = hyena_conv_ctx(v, x1, x2, kern_c, asum_c, bias)
    return matmul(jnp.concatenate([z_lat, z_ctx]), w_out, out_dtype=F32, layer=0)


def na_mixer(h, w_qkv, rpb, w_out, *, n_ctx):
    qkv = matmul(h, w_qkv, out_dtype=BF16, layer=0)
    return matmul(na_attention(qkv, rpb, n_ctx=n_ctx), w_out, out_dtype=F32, layer=0)
```

```python
import functools
import math

import numpy as np
import jax
import jax.numpy as jnp
from jax import lax
from jax.experimental import pallas as pl
from jax.experimental.pallas import tpu as pltpu

F32 = jnp.float32
BF16 = jnp.bfloat16

V7X_VMEM_LIMIT_BYTES = 56 * 1024 * 1024
LANES = 128

GRID_W = 64
DEPTH = 4
DEEPNORM_ALPHA = (2 * DEPTH) ** 0.25
LN_EPS = 1e-5
RMS_EPS = 1e-6
ROPE_BASE = 10000.0
ML_HEADS = 8
ML_CHUNK = 256
DA_HEAD_DIM = 128
HY_ORDER = 2
HY_EMB = 33
HY_BANDS = (HY_EMB - 1) // 2
HY_FAST_DECAY = 0.3
HY_SLOW_DECAY = 1.5
HY_DECAY_TARGET = 1e-2
NA_HEAD_DIM = 128
NA_ROWS = 8
NA_COLS = 16
N_EXPERTS = 64
TOP_K = 8
N_GROUPS = 8
TOPK_GROUPS = 4
ROUTED_SCALE = 2.5


def _params(*sem):
    return pltpu.CompilerParams(dimension_semantics=sem, vmem_limit_bytes=V7X_VMEM_LIMIT_BYTES)


def _dot(a, b):
    return jnp.dot(a, b, preferred_element_type=F32)


def _dot_nt(a, b):
    return lax.dot_general(a, b, (((1,), (1,)), ((), ())), preferred_element_type=F32)


def _dot_tn(a, b):
    return lax.dot_general(a, b, (((0,), (0,)), ((), ())), preferred_element_type=F32)


def _pick(n, prefs):
    for p in prefs:
        if n % p == 0:
            return p
    return n


def _mm_kernel(*refs, nk, has_extra):
    if has_extra:
        a_ref, w_ref, a2_ref, w2_ref, o_ref, acc_ref = refs
    else:
        a_ref, w_ref, o_ref, acc_ref = refs
    k = pl.program_id(2)

    @pl.when(k == 0)
    def _():
        acc_ref[...] = jnp.zeros_like(acc_ref)

    acc_ref[...] += _dot(a_ref[...].astype(BF16), w_ref[...].astype(BF16))

    @pl.when(k == nk - 1)
    def _():
        acc = acc_ref[...]
        if has_extra:
            acc = acc + _dot(a2_ref[...].astype(BF16), w2_ref[...].astype(BF16))
        o_ref[...] = acc.astype(o_ref.dtype)


def _mm_resident_kernel(a_ref, w_ref, o_ref, *, w_transposed):
    dot = _dot_nt if w_transposed else _dot
    o_ref[...] = dot(a_ref[...].astype(BF16), w_ref[...].astype(BF16)).astype(o_ref.dtype)


def _matmul_resident(a, w, *, out_dtype, layer, col0, N, w_transposed):
    M, K = a.shape
    tm = _pick(M, (2112, 2048, 1024, 512, 256, 128) if a.dtype == BF16 else (1056, 1024, 512, 256, 128))
    tn = _pick(N, (256, 128))
    cb = col0 // tn
    lead = (None,) if w.ndim == 3 else ()
    lidx = (layer,) if w.ndim == 3 else ()
    if w_transposed:
        w_spec = pl.BlockSpec(lead + (tn, K), lambda i, j: lidx + (j + cb, 0))
    else:
        w_spec = pl.BlockSpec(lead + (K, tn), lambda i, j: lidx + (0, j + cb))
    return pl.pallas_call(
        functools.partial(_mm_resident_kernel, w_transposed=w_transposed),
        grid=(M // tm, N // tn),
        in_specs=[pl.BlockSpec((tm, K), lambda i, j: (i, 0), pipeline_mode=pl.Buffered(1)), w_spec],
        out_specs=pl.BlockSpec((tm, tn), lambda i, j: (i, j)),
        out_shape=jax.ShapeDtypeStruct((M, N), out_dtype),
        compiler_params=_params("parallel", "arbitrary"),
    )(a, w)


def matmul(a, w, *, out_dtype, layer=None, col0=0, ncols=None, extra=None, row0=0, nrows=None, w_transposed=False):
    M = a.shape[0] if nrows is None else nrows
    K = a.shape[1]
    N = (w.shape[-2] if w_transposed else w.shape[-1]) if ncols is None else ncols
    if K <= 4096 and extra is None and nrows is None and M >= 1024:
        return _matmul_resident(a, w, out_dtype=out_dtype, layer=layer, col0=col0, N=N, w_transposed=w_transposed)
    assert not w_transposed
    tm = _pick(M, (1056, 1024, 512, 256, 128, 64, 32, 16, 8))
    tn = _pick(N, (1024, 512, 256, 128))
    tk = _pick(K, (2048, 1024, 512, 256, 128))
    assert col0 % tn == 0 and row0 % tm == 0
    cb, rb = col0 // tn, row0 // tm
    nk = K // tk
    if w.ndim == 3:
        w_spec = pl.BlockSpec((None, tk, tn), lambda i, j, k: (layer, k, j + cb))
    else:
        w_spec = pl.BlockSpec((tk, tn), lambda i, j, k: (k, j + cb))
    in_specs = [pl.BlockSpec((tm, tk), lambda i, j, k: (i + rb, k)), w_spec]
    args = [a, w]
    if extra is not None:
        a2, w2, layer2 = extra
        k2 = a2.shape[1]
        in_specs.append(pl.BlockSpec((tm, k2), lambda i, j, k: (i + rb, 0)))
        in_specs.append(pl.BlockSpec((None, k2, tn), lambda i, j, k: (layer2, 0, j)))
        args += [a2, w2]
    return pl.pallas_call(
        functools.partial(_mm_kernel, nk=nk, has_extra=extra is not None),
        grid=(M // tm, N // tn, nk),
        in_specs=in_specs,
        out_specs=pl.BlockSpec((tm, tn), lambda i, j, k: (i, j)),
        out_shape=jax.ShapeDtypeStruct((M, N), out_dtype),
        scratch_shapes=[pltpu.VMEM((tm, tn), F32)],
        compiler_params=_params("parallel", "parallel", "arbitrary"),
    )(*args)


def _ada_kernel(cc_ref, a_ref, b_ref, bias_ref, o_ref):
    cc = cc_ref[...]
    act = cc * jax.nn.sigmoid(cc)
    t = _dot(act.astype(BF16), a_ref[...].astype(BF16))
    o_ref[...] = _dot(t.astype(BF16), b_ref[...].astype(BF16)) + bias_ref[...]


def ada_modulation(cc, ada_a, ada_b, ada_bias):
    depth, D, R = ada_a.shape
    n6 = ada_b.shape[-1]
    tn = D
    return pl.pallas_call(
        _ada_kernel,
        grid=(depth, n6 // tn),
        in_specs=[
            pl.BlockSpec((8, D), lambda l, n: (0, 0)),
            pl.BlockSpec((None, D, R), lambda l, n: (l, 0, 0)),
            pl.BlockSpec((None, R, tn), lambda l, n: (l, 0, n)),
            pl.BlockSpec((None, 1, tn), lambda l, n: (l, 0, n)),
        ],
        out_specs=pl.BlockSpec((None, 8, tn), lambda l, n: (l, 0, n)),
        out_shape=jax.ShapeDtypeStruct((depth, 8, n6), F32),
        compiler_params=_params("parallel", "parallel"),
    )(cc, ada_a, ada_b, ada_bias.reshape(depth, 1, n6))


def _lnmod_kernel(*refs, has_ln, has_h, n_y):
    it = iter(refs)
    s_ref = next(it)
    if has_ln:
        y_refs = [next(it) for _ in range(n_y)]
        gate_ref, lng_ref, lnb_ref = next(it), next(it), next(it)
    if has_h:
        sc_ref, sh_ref = next(it), next(it)
    if has_ln:
        so_ref = next(it)
    if has_h:
        h_ref = next(it)
    x = s_ref[...]
    if has_ln:
        y = y_refs[0][...].astype(F32)
        for r in y_refs[1:]:
            y = y + r[...].astype(F32)
        v = DEEPNORM_ALPHA * x + gate_ref[...] * y
        mu = jnp.mean(v, axis=-1, keepdims=True)
        var = jnp.mean(jnp.square(v - mu), axis=-1, keepdims=True)
        x = (v - mu) * lax.rsqrt(var + LN_EPS) * lng_ref[...] + lnb_ref[...]
        so_ref[...] = x
    if has_h:
        h_ref[...] = (x * (1.0 + sc_ref[...]) + sh_ref[...]).astype(h_ref.dtype)


def ln_modulate(s, mods, *, n_ctx, ys=(), ln_layer=None, gate_idx=None, ln_g=None, ln_b=None,
                h_layer=None, sc_idx=None, sh_idx=None):
    R, D = s.shape
    tr = 128
    has_ln, has_h = len(ys) > 0, h_layer is not None
    n_lat_blocks = (R - n_ctx) // tr
    row_spec = pl.BlockSpec((tr, D), lambda i: (i, 0))

    def mod_spec(layer, idx):
        return pl.BlockSpec((None, None, None, 1, D),
                            lambda i: (layer, jnp.where(i >= n_lat_blocks, 0, 1), idx, 0, 0))

    vec_spec = pl.BlockSpec((None, 1, D), lambda i: (ln_layer, 0, 0))
    in_specs, args = [row_spec], [s]
    if has_ln:
        in_specs += [row_spec] * len(ys) + [mod_spec(ln_layer, gate_idx), vec_spec, vec_spec]
        args += list(ys) + [mods, ln_g.reshape(DEPTH, 1, D), ln_b.reshape(DEPTH, 1, D)]
    if has_h:
        in_specs += [mod_spec(h_layer, sc_idx), mod_spec(h_layer, sh_idx)]
        args += [mods, mods]
    out_specs, out_shape = [], []
    if has_ln:
        out_specs.append(row_spec)
        out_shape.append(jax.ShapeDtypeStruct((R, D), F32))
    if has_h:
        out_specs.append(row_spec)
        out_shape.append(jax.ShapeDtypeStruct((R, D), BF16))
    outs = pl.pallas_call(
        functools.partial(_lnmod_kernel, has_ln=has_ln, has_h=has_h, n_y=len(ys)),
        grid=(R // tr,),
        in_specs=in_specs,
        out_specs=out_specs,
        out_shape=out_shape,
        compiler_params=_params("parallel"),
    )(*args)
    return outs


def _log_sigmoid(x):
    return jnp.minimum(x, 0.0) - jnp.log(1.0 + jnp.exp(-jnp.abs(x)))


def _mlstm_kernel(bias_ref, q_ref, k_ref, v_ref, ic_ref, fc_ref, ir_ref, fr_ref, o_ref, ct_ref, n_ref, m_ref,
                  *, T, dqk):
    d, h, t = pl.program_id(0), pl.program_id(1), pl.program_id(2)

    @pl.when(t == 0)
    def _():
        ct_ref[...] = jnp.zeros_like(ct_ref)
        n_ref[...] = jnp.zeros_like(n_ref)
        m_ref[...] = jnp.zeros_like(m_ref)

    bi = bias_ref[d * 2 * ML_HEADS + h]
    bf = bias_ref[(d * 2 + 1) * ML_HEADS + h]
    i_c = ic_ref[...] + bi
    i_r = ir_ref[...] + bi
    f_c = _log_sigmoid(fc_ref[...] + bf)
    f_r = _log_sigmoid(fr_ref[...] + bf)
    row = lax.broadcasted_iota(jnp.int32, (T, T), 0)
    col = lax.broadcasted_iota(jnp.int32, (T, T), 1)
    sgn = 1 - 2 * d
    incl = (col - row) * sgn <= 0
    incl_t = (row - col) * sgn <= 0
    b_c = jnp.sum(jnp.where(incl, f_r, 0.0), axis=1, keepdims=True)
    b_r = jnp.sum(jnp.where(incl_t, f_c, 0.0), axis=0, keepdims=True)
    f_tot = jnp.sum(f_r, axis=1, keepdims=True)
    m_prev = m_ref[...]
    dmat = jnp.where(incl, b_c - b_r + i_r, -jnp.inf)
    inter = b_c + m_prev
    m_t = jnp.maximum(inter, jnp.max(dmat, axis=1, keepdims=True))
    w_intra = jnp.exp(dmat - m_t)
    w_inter = jnp.exp(inter - m_t)
    q = q_ref[...] * (dqk ** -0.5)
    k = k_ref[...]
    v = v_ref[...]
    qb, kb, vb = q.astype(BF16), k.astype(BF16), v.astype(BF16)
    s = _dot_nt(qb, kb) * w_intra
    ct = ct_ref[...]
    n = n_ref[...]
    num = w_inter * _dot(qb, ct.astype(BF16)) + _dot(s.astype(BF16), vb)
    den = w_inter * jnp.sum(q * n, axis=1, keepdims=True) + jnp.sum(s, axis=1, keepdims=True)
    o_ref[...] = num / jnp.maximum(jnp.abs(den), jnp.exp(-m_t))
    g_r = f_tot - b_r + i_r
    g_c = f_tot - b_c + i_c
    m_new = jnp.maximum(f_tot + m_prev, jnp.max(g_r, axis=1, keepdims=True))
    decay = jnp.exp(f_tot + m_prev - m_new)
    wk = jnp.exp(g_c - m_new)
    ct_ref[...] = decay * ct + _dot_tn(kb, (wk * v).astype(BF16))
    n_ref[...] = decay * n + jnp.sum(wk * k, axis=0, keepdims=True)
    m_ref[...] = m_new


def mlstm_scan(u, gates, b_if, *, n_ctx):
    NT = u.shape[0]
    H, T = ML_HEADS, ML_CHUNK
    D = u.shape[1] // 3
    dqk, dv = D // (2 * H), D // H
    nc, ncc = NT // T, n_ctx // T
    ncl = nc - ncc
    g4 = gates[:, :4 * H].T
    g_col = g4.reshape(4 * H, NT, 1)
    g_row = g4.reshape(4 * H, nc, 1, T)

    def chunk(d, t):
        fwd = jnp.where(t < ncc, ncl + t, t - ncc)
        return jnp.where(d == 0, fwd, nc - 1 - t)

    kq, kk, kv = 0, (H * dqk) // dqk, (2 * H * dqk) // dv
    col_spec = lambda off: pl.BlockSpec((None, T, 1), lambda d, h, t: ((2 * d + off) * H + h, chunk(d, t), 0))
    row_spec = lambda off: pl.BlockSpec((None, None, 1, T), lambda d, h, t: ((2 * d + off) * H + h, chunk(d, t), 0, 0))
    return pl.pallas_call(
        functools.partial(_mlstm_kernel, T=T, dqk=dqk),
        grid=(2, H, nc),
        in_specs=[
            pl.BlockSpec(memory_space=pltpu.SMEM),
            pl.BlockSpec((T, dqk), lambda d, h, t: (chunk(d, t), kq + h)),
            pl.BlockSpec((T, dqk), lambda d, h, t: (chunk(d, t), kk + h)),
            pl.BlockSpec((T, dv), lambda d, h, t: (chunk(d, t), kv + h)),
            col_spec(0), col_spec(1), row_spec(0), row_spec(1),
        ],
        out_specs=pl.BlockSpec((None, T, dv), lambda d, h, t: (d, chunk(d, t), h)),
        out_shape=jax.ShapeDtypeStruct((2, NT, H * dv), F32),
        scratch_shapes=[pltpu.VMEM((dqk, dv), F32), pltpu.VMEM((1, dqk), F32), pltpu.VMEM((1, 1), F32)],
        compiler_params=_params("parallel", "parallel", "arbitrary"),
    )(b_if, u, u, u, g_col, g_col, g_row, g_row)


def _mlgate_kernel(hs_ref, o_ref, g_ref, out_ref, *, dv):
    D = out_ref.shape[1]
    for h in range(D // dv):
        sl = slice(h * dv, (h + 1) * dv)
        x = hs_ref[0, :, sl] + hs_ref[1, :, sl]
        r = lax.rsqrt(jnp.mean(jnp.square(x), axis=-1, keepdims=True) + RMS_EPS)
        out_ref[:, sl] = (x * r * g_ref[:, sl] * jax.nn.sigmoid(o_ref[:, sl])).astype(out_ref.dtype)


def mlstm_gate(hs, u, norm_g, *, o_col_block):
    _, NT, D = hs.shape
    tr = 128
    return pl.pallas_call(
        functools.partial(_mlgate_kernel, dv=D // ML_HEADS),
        grid=(NT // tr,),
        in_specs=[
            pl.BlockSpec((2, tr, D), lambda i: (0, i, 0)),
            pl.BlockSpec((tr, D), lambda i: (i, o_col_block)),
            pl.BlockSpec((1, D), lambda i: (0, 0)),
        ],
        out_specs=pl.BlockSpec((tr, D), lambda i: (i, 0)),
        out_shape=jax.ShapeDtypeStruct((NT, D), BF16),
        compiler_params=_params("parallel"),
    )(hs, u, norm_g.reshape(1, D))


def _rope_kernel(x_ref, cos_ref, sin_ref, o_ref):
    cos, sin = cos_ref[...], sin_ref[...]
    lane = lax.broadcasted_iota(jnp.int32, cos.shape, 1)
    first_half = (lane % 64) < 32
    for g in range(x_ref.shape[1] // LANES):
        sl = slice(g * LANES, (g + 1) * LANES)
        x = x_ref[:, sl]
        rot = jnp.where(first_half, -pltpu.roll(x, LANES - 32, 1), pltpu.roll(x, 32, 1))
        o_ref[:, sl] = (x * cos + rot * sin).astype(o_ref.dtype)


def rope_cast(x, cos, sin, *, row0, nrows, col0, ncols):
    tr = _pick(nrows, (256, 128))
    tc = _pick(ncols, (1024, 512, 256, 128))
    rb, cb = row0 // tr, col0 // tc
    assert row0 % tr == 0 and col0 % tc == 0
    return pl.pallas_call(
        _rope_kernel,
        grid=(nrows // tr, ncols // tc),
        in_specs=[
            pl.BlockSpec((tr, tc), lambda i, j: (i + rb, j + cb)),
            pl.BlockSpec((tr, LANES), lambda i, j: (i, 0)),
            pl.BlockSpec((tr, LANES), lambda i, j: (i, 0)),
        ],
        out_specs=pl.BlockSpec((tr, tc), lambda i, j: (i, j)),
        out_shape=jax.ShapeDtypeStruct((nrows, ncols), BF16),
        compiler_params=_params("parallel", "parallel"),
    )(x, cos, sin)


def _dattn_kernel(lam_ref, q_ref, k_ref, v_ref, g_ref, o_ref, m_ref, l_ref, acc_ref,
                  *, tk, n_chunks, out_scale, lambda_init):
    dh = DA_HEAD_DIM
    m_ref[...] = jnp.full_like(m_ref, -jnp.inf)
    l_ref[...] = jnp.zeros_like(l_ref)
    acc_ref[...] = jnp.zeros_like(acc_ref)

    def body(c, carry):
        start = pl.multiple_of(c * tk, tk)
        kc = k_ref[pl.ds(start, tk), :]
        vc = v_ref[pl.ds(start, tk), :]
        for i in range(2):
            s = _dot_nt(q_ref[:, i * dh:(i + 1) * dh], kc[:, i * dh:(i + 1) * dh])
            m_old = m_ref[i]
            m_new = jnp.maximum(m_old, jnp.max(s, axis=1, keepdims=True))
            p = jnp.exp2(s - m_new)
            alpha = jnp.exp2(m_old - m_new)
            l_ref[i] = alpha * l_ref[i] + jnp.sum(p, axis=1, keepdims=True)
            acc_ref[i] = alpha * acc_ref[i] + _dot(p.astype(BF16), vc)
            m_ref[i] = m_new
        return carry

    lax.fori_loop(0, n_chunks, body, 0)
    lam = (jnp.exp(jnp.sum(lam_ref[0:1, :] * lam_ref[1:2, :], axis=1, keepdims=True))
           - jnp.exp(jnp.sum(lam_ref[2:3, :] * lam_ref[3:4, :], axis=1, keepdims=True)) + lambda_init)
    o = acc_ref[0] / l_ref[0] - lam * (acc_ref[1] / l_ref[1])
    r = lax.rsqrt(jnp.mean(jnp.square(o), axis=-1, keepdims=True) + RMS_EPS)
    o_ref[...] = (o * r * g_ref[...] * out_scale).astype(o_ref.dtype)


def diff_attention(q, k, v, lam4, subln_g, *, kv_row0, kv_len, lambda_init):
    Nq, D = q.shape
    hd = 2 * DA_HEAD_DIM
    H = D // hd
    tq = _pick(Nq, (512, 256))
    tk = _pick(kv_len, (768, 512, 256))
    kvb = kv_row0 // kv_len
    assert kv_row0 % kv_len == 0
    return pl.pallas_call(
        functools.partial(_dattn_kernel, tk=tk, n_chunks=kv_len // tk, out_scale=1.0 - lambda_init,
                          lambda_init=lambda_init),
        grid=(H, Nq // tq),
        in_specs=[
            pl.BlockSpec((4, DA_HEAD_DIM), lambda h, i: (0, 0)),
            pl.BlockSpec((tq, hd), lambda h, i: (i, h)),
            pl.BlockSpec((kv_len, hd), lambda h, i: (kvb, h)),
            pl.BlockSpec((kv_len, hd), lambda h, i: (kvb, h)),
            pl.BlockSpec((1, hd), lambda h, i: (0, 0)),
        ],
        out_specs=pl.BlockSpec((tq, hd), lambda h, i: (i, h)),
        out_shape=jax.ShapeDtypeStruct((Nq, D), BF16),
        scratch_shapes=[pltpu.VMEM((2, tq, 1), F32), pltpu.VMEM((2, tq, 1), F32), pltpu.VMEM((2, tq, hd), F32)],
        compiler_params=_params("parallel", "parallel"),
    )(lam4, q, k, v, subln_g.reshape(1, hd))


def _rope_tables(L, scale):
    a = DA_HEAD_DIM // 2
    t = jnp.arange(L)
    pos = jnp.stack([t // GRID_W, t % GRID_W], -1).astype(F32)
    inv = ROPE_BASE ** (-jnp.arange(0, a, 2, dtype=F32) / a)
    ang = pos[:, :, None] * inv
    ang = jnp.concatenate([ang, ang], -1).reshape(L, DA_HEAD_DIM)
    return jnp.cos(ang) * scale, jnp.sin(ang) * scale


def _shortconv_kernel(p_ref, prev_ref, next_ref, w_ref, b_ref, o_ref, *, n_seq_first, n_seq_last):
    i = pl.program_id(0)
    x = p_ref[...]
    tr = x.shape[0]
    rid = lax.broadcasted_iota(jnp.int32, x.shape, 0)
    is_first = functools.reduce(jnp.logical_or, [i == b for b in n_seq_first])
    is_last = functools.reduce(jnp.logical_or, [i == b for b in n_seq_last])
    prev_row = jnp.where(is_first, 0.0, prev_ref[7:8, :])
    next_row = jnp.where(is_last, 0.0, next_ref[0:1, :])
    xm = jnp.where(rid == 0, prev_row, pltpu.roll(x, 1, 0))
    xp = jnp.where(rid == tr - 1, next_row, pltpu.roll(x, tr - 1, 0))
    o_ref[...] = xm * w_ref[0:1, :] + x * w_ref[1:2, :] + xp * w_ref[2:3, :] + b_ref[...]


def short_conv(p, w, b, *, n_ctx):
    NT, N = p.shape
    tr = 256
    tc = _pick(N, (1024, 512, 256, 128))
    nb, n8 = NT // tr, NT // 8
    n_lat_blocks = (NT - n_ctx) // tr
    firsts = (0, n_lat_blocks)
    lasts = (n_lat_blocks - 1, nb - 1)
    return pl.pallas_call(
        functools.partial(_shortconv_kernel, n_seq_first=firsts, n_seq_last=lasts),
        grid=(nb, N // tc),
        in_specs=[
            pl.BlockSpec((tr, tc), lambda i, j: (i, j)),
            pl.BlockSpec((8, tc), lambda i, j: (jnp.maximum(i * (tr // 8) - 1, 0), j)),
            pl.BlockSpec((8, tc), lambda i, j: (jnp.minimum((i + 1) * (tr // 8), n8 - 1), j)),
            pl.BlockSpec((3, tc), lambda i, j: (0, j)),
            pl.BlockSpec((1, tc), lambda i, j: (0, j)),
        ],
        out_specs=pl.BlockSpec((tr, tc), lambda i, j: (i, j)),
        out_shape=jax.ShapeDtypeStruct((NT, N), F32),
        compiler_params=_params("parallel", "parallel"),
    )(p, p, p, w, b.reshape(1, N))


def _filter_kernel(zf_ref, t_ref, w1_ref, b1_ref, w2_ref, b2_ref, w3_ref, b3_ref, w4f_ref, w4b_ref, dl_ref,
                   o_ref, asum_ref, *, L):
    i = pl.program_id(1)
    hdn = jnp.sin(_dot(zf_ref[...].astype(BF16), w1_ref[...].astype(BF16)) + b1_ref[...])
    hdn = jnp.sin(_dot(hdn.astype(BF16), w2_ref[...].astype(BF16)) + b2_ref[...])
    hdn = jnp.sin(_dot(hdn.astype(BF16), w3_ref[...].astype(BF16)) + b3_ref[...]).astype(BF16)
    hf = _dot(hdn, w4f_ref[...].astype(BF16))
    hb = _dot(hdn, w4b_ref[...].astype(BF16))
    tr = hf.shape[0]
    ridx = i * tr + lax.broadcasted_iota(jnp.int32, hf.shape, 0)
    kern = jnp.where(ridx < L, hf, jnp.where(ridx > L, hb, 0.0)) * jnp.exp(-t_ref[...] * dl_ref[...])
    o_ref[...] = kern

    @pl.when(i == 0)
    def _():
        asum_ref[...] = jnp.zeros_like(asum_ref)

    asum_ref[...] += jnp.sum(jnp.abs(kern), axis=0, keepdims=True)


def hyena_filters(L, D, f_w1, f_b1, f_w2, f_b2, f_w3, f_b3, f_w4):
    t = jnp.linspace(0.0, 1.0, L, dtype=F32)[:, None]
    w = 2.0 * math.pi * jnp.arange(L, dtype=F32)[:, None] / L
    f = jnp.linspace(1e-4, HY_BANDS - 1, HY_BANDS, dtype=F32)[None]
    z = jnp.concatenate([t, jnp.cos(w * f), -jnp.sin(w * f)], -1)
    src = jnp.concatenate([jnp.arange(L), jnp.zeros((1,), jnp.int32), L - 1 - jnp.arange(L - 1)])
    zf = jnp.pad(z[src], ((0, 0), (0, LANES - HY_EMB)))
    tcol = t[src]
    deltas = jnp.abs(jnp.linspace(math.log(HY_DECAY_TARGET) / HY_SLOW_DECAY, math.log(HY_DECAY_TARGET) / HY_FAST_DECAY,
                                  D, dtype=F32)).reshape(1, D)
    w1 = jnp.pad(f_w1, ((0, LANES - HY_EMB), (0, 0)))
    nf = f_w1.shape[1]
    R2 = 2 * L
    tr = _pick(R2, (512,))
    tc = _pick(D, (1024, 512, 256, 128))
    cpd = D // tc
    full = lambda shape: pl.BlockSpec(shape, lambda j, i: (0, 0))
    return pl.pallas_call(
        functools.partial(_filter_kernel, L=L),
        grid=(HY_ORDER * cpd, R2 // tr),
        in_specs=[
            pl.BlockSpec((tr, LANES), lambda j, i: (i, 0)),
            pl.BlockSpec((tr, 1), lambda j, i: (i, 0)),
            full((LANES, nf)), full((1, nf)), full((nf, nf)), full((1, nf)), full((nf, nf)), full((1, nf)),
            pl.BlockSpec((nf, tc), lambda j, i: (0, (j // cpd) * 2 * cpd + j % cpd)),
            pl.BlockSpec((nf, tc), lambda j, i: (0, ((j // cpd) * 2 + 1) * cpd + j % cpd)),
            pl.BlockSpec((1, tc), lambda j, i: (0, j % cpd)),
        ],
        out_specs=[pl.BlockSpec((tr, tc), lambda j, i: (i, j)), pl.BlockSpec((1, tc), lambda j, i: (0, j))],
        out_shape=[jax.ShapeDtypeStruct((R2, HY_ORDER * D), F32), jax.ShapeDtypeStruct((1, HY_ORDER * D), F32)],
        compiler_params=_params("parallel", "arbitrary"),
    )(zf, tcol, w1, f_b1.reshape(1, nf), f_w2, f_b2.reshape(1, nf), f_w3, f_b3.reshape(1, nf), f_w4, f_w4, deltas)


FFT_N2 = 128
FFT_TT2 = 8


def _split(x):
    hi = x.astype(BF16)
    return hi, (x - hi.astype(F32)).astype(BF16)


def _dot3(w_hi, w_lo, x):
    x_hi, x_lo = _split(x)
    return _dot(w_hi, x_hi) + (_dot(w_lo, x_hi) + _dot(w_hi, x_lo))


def _hilo(a):
    return _split(jnp.asarray(a, F32))


def _fft_first_kernel(whi_ref, wlo_ref, z_ref, o_ref):
    n1 = o_ref.shape[1]
    for j in range(z_ref.shape[1]):
        a = _dot3(whi_ref[...], wlo_ref[...], z_ref[:, j, :])
        o_ref[0, :, j, :] = a[:n1]
        o_ref[1, :, j, :] = a[n1:]


def fft_first(w_hl, z3, *, col0, C):
    w_hi, w_lo = w_hl
    n1x2, l1 = w_hi.shape
    n1, n2 = n1x2 // 2, z3.shape[1]
    tt2 = FFT_TT2
    tc = _pick(C, (512, 256, 128))
    cb = col0 // tc
    full = pl.BlockSpec((n1x2, l1), lambda t, j: (0, 0))
    return pl.pallas_call(
        _fft_first_kernel,
        grid=(n2 // tt2, C // tc),
        in_specs=[full, full, pl.BlockSpec((l1, tt2, tc), lambda t, j: (0, t, j + cb))],
        out_specs=pl.BlockSpec((2, n1, tt2, tc), lambda t, j: (0, 0, t, j)),
        out_shape=jax.ShapeDtypeStruct((2, n1, n2, C), F32),
        compiler_params=_params("parallel", "parallel"),
    )(w_hi, w_lo, z3)


def _fft_mid_kernel(*refs, g, conv):
    if conv:
        a_ref, kf_ref, twc_ref, tws_ref, w2h_ref, w2l_ref, w2ih_ref, w2il_ref, o_ref = refs
    else:
        a_ref, twc_ref, tws_ref, w2h_ref, w2l_ref, o_ref = refs
    n2 = a_ref.shape[2]
    for kk in range(g):
        ar, ai = a_ref[0, kk], a_ref[1, kk]
        c, s = twc_ref[kk], tws_ref[kk]
        x = _dot3(w2h_ref[...], w2l_ref[...], jnp.concatenate([ar * c + ai * s, ai * c - ar * s], axis=0))
        if conv:
            xr, xi = x[:n2], x[n2:]
            kr, ki = kf_ref[0, kk], kf_ref[1, kk]
            y = _dot3(w2ih_ref[...], w2il_ref[...], jnp.concatenate([xr * kr - xi * ki, xr * ki + xi * kr], axis=0))
            yr, yi = y[:n2], y[n2:]
            o_ref[0, kk] = yr * c - yi * s
            o_ref[1, kk] = yr * s + yi * c
        else:
            o_ref[0, kk] = x[:n2]
            o_ref[1, kk] = x[n2:]


def fft_mid(a, tw_c, tw_s, w2_hl, w2i_hl=None, kf=None, *, kf_col0=0):
    _, n1, n2, C = a.shape
    g = 4
    tc = _pick(C, (512, 256, 128))
    conv = kf is not None
    blk = pl.BlockSpec((2, g, n2, tc), lambda j, k: (0, k, 0, j))
    tw = pl.BlockSpec((g, n2, 1), lambda j, k: (k, 0, 0))
    mat = pl.BlockSpec((2 * n2, 2 * n2), lambda j, k: (0, 0))
    if conv:
        kb = kf_col0 // tc
        in_specs = [blk, pl.BlockSpec((2, g, n2, tc), lambda j, k: (0, k, 0, j + kb)), tw, tw, mat, mat, mat, mat]
        args = [a, kf, tw_c, tw_s, *w2_hl, *w2i_hl]
    else:
        in_specs = [blk, tw, tw, mat, mat]
        args = [a, tw_c, tw_s, *w2_hl]
    return pl.pallas_call(
        functools.partial(_fft_mid_kernel, g=g, conv=conv),
        grid=(C // tc, n1 // g),
        in_specs=in_specs,
        out_specs=blk,
        out_shape=jax.ShapeDtypeStruct(a.shape, F32),
        compiler_params=_params("parallel", "parallel"),
    )(*args)


def _fft_last_kernel(whi_ref, wlo_ref, b_ref, gate_ref, zp_ref, asum_ref, bias_ref, o_ref):
    inv_asum = 1.0 / asum_ref[...]
    for j in range(b_ref.shape[2]):
        bj = jnp.concatenate([b_ref[0, :, j, :], b_ref[1, :, j, :]], axis=0)
        x = _dot3(whi_ref[...], wlo_ref[...], bj)
        o_ref[:, j, :] = gate_ref[:, j, :] * (x * inv_asum + zp_ref[:, j, :] * bias_ref[...])


def fft_last(w_hl, b, gate3, gate_col0, zp3, zp_col0, asum, bias, *, vec_col0):
    w_hi, w_lo = w_hl
    l1 = w_hi.shape[0]
    _, n1, n2, C = b.shape
    tt2 = FFT_TT2
    tc = _pick(C, (512, 256, 128))
    gcb, zcb, vcb = gate_col0 // tc, zp_col0 // tc, vec_col0 // tc
    full = pl.BlockSpec((l1, 2 * n1), lambda t, j: (0, 0))
    return pl.pallas_call(
        _fft_last_kernel,
        grid=(n2 // tt2, C // tc),
        in_specs=[
            full, full,
            pl.BlockSpec((2, n1, tt2, tc), lambda t, j: (0, 0, t, j)),
            pl.BlockSpec((l1, tt2, tc), lambda t, j: (0, t, j + gcb)),
            pl.BlockSpec((l1, tt2, tc), lambda t, j: (0, t, j + zcb)),
            pl.BlockSpec((1, tc), lambda t, j: (0, j + vcb)),
            pl.BlockSpec((1, tc), lambda t, j: (0, j + vcb)),
        ],
        out_specs=pl.BlockSpec((l1, tt2, tc), lambda t, j: (0, t, j)),
        out_shape=jax.ShapeDtypeStruct((l1, n2, C), F32),
        compiler_params=_params("parallel", "parallel"),
    )(w_hi, w_lo, b, gate3, zp3, asum, bias)


def _hpmm_kernel(*refs, gated):
    if gated:
        ah_ref, al_ref, b_ref, gate_ref, z_ref, asum_ref, bias_ref, o_ref = refs
    else:
        ah_ref, al_ref, b_ref, o_ref = refs
    y = _dot3(ah_ref[...], al_ref[...], b_ref[...])
    if gated:
        y = gate_ref[...] * (y / asum_ref[...] + z_ref[...] * bias_ref[...])
    o_ref[...] = y


def hp_matmul(a_hl, b, *, gate=None, z=None, asum=None, bias=None, vec_col0=0):
    a_hi, a_lo = a_hl
    M, Kd = a_hi.shape
    N = b.shape[1]
    tn = _pick(N, (1024, 512, 256, 128))
    full = pl.BlockSpec((M, Kd), lambda j: (0, 0))
    in_specs = [full, full, pl.BlockSpec((Kd, tn), lambda j: (0, j))]
    args = [a_hi, a_lo, b]
    gated = gate is not None
    if gated:
        vb = vec_col0 // tn
        blk = pl.BlockSpec((M, tn), lambda j: (0, j))
        vec = pl.BlockSpec((1, tn), lambda j: (0, j + vb))
        in_specs += [blk, blk, vec, vec]
        args += [gate, z, asum, bias]
    return pl.pallas_call(
        functools.partial(_hpmm_kernel, gated=gated),
        grid=(N // tn,),
        in_specs=in_specs,
        out_specs=pl.BlockSpec((M, tn), lambda j: (0, j)),
        out_shape=jax.ShapeDtypeStruct((M, N), F32),
        compiler_params=_params("parallel"),
    )(*args)


def _cmul_kernel(x_ref, k_ref, o_ref):
    xr, xi, kr, ki = x_ref[0], x_ref[1], k_ref[0], k_ref[1]
    o_ref[0] = xr * kr - xi * ki
    o_ref[1] = xr * ki + xi * kr


def complex_mul(x, kf, *, kf_col0):
    _, R, C = x.shape
    tc = _pick(C, (512, 256, 128))
    kb = kf_col0 // tc
    return pl.pallas_call(
        _cmul_kernel,
        grid=(C // tc,),
        in_specs=[pl.BlockSpec((2, R, tc), lambda j: (0, 0, j)), pl.BlockSpec((2, R, tc), lambda j: (0, 0, j + kb))],
        out_specs=pl.BlockSpec((2, R, tc), lambda j: (0, 0, j)),
        out_shape=jax.ShapeDtypeStruct(x.shape, F32),
        compiler_params=_params("parallel"),
    )(x, kf)


def _dft_constants(n1, n2, l1):
    N = n1 * n2
    n1h = min(n1, -(-(n1 // 2 + 1) // 8) * 8)
    k1 = np.arange(n1h)[:, None]
    herm = np.where((k1 == 0) | (k1 == n1 // 2), 1.0, np.where(k1 < n1 // 2, 2.0, 0.0))
    ang1 = 2 * np.pi * k1 * np.arange(n1)[None, :] / n1
    w1 = np.concatenate([np.cos(ang1), -np.sin(ang1)], 0)
    w1_inv = np.concatenate([(herm * np.cos(ang1[:, :l1])).T, (-herm * np.sin(ang1[:, :l1])).T], 1) / N
    ang_t = 2 * np.pi * k1 * np.arange(n2)[None, :] / N
    k2 = np.arange(n2)[:, None]
    ang2 = 2 * np.pi * k2 * np.arange(n2)[None, :] / n2
    c2, s2 = np.cos(ang2), np.sin(ang2)
    f = lambda x: jnp.asarray(x, F32)
    return dict(w1_data=_hilo(w1[:, :l1]), w1_full=_hilo(w1), w1_inv=_hilo(w1_inv),
                tw_c=f(np.cos(ang_t))[:, :, None], tw_s=f(np.sin(ang_t))[:, :, None],
                w2=_hilo(np.block([[c2, s2], [-s2, c2]])), w2_inv=_hilo(np.block([[c2, -s2], [s2, c2]])))


def hyena_conv_lat(u3, kern, asum, bias, *, L, D):
    n2 = FFT_N2
    n1, l1 = 2 * L // n2, L // n2
    cst = _dft_constants(n1, n2, l1)
    ka = fft_first(cst["w1_full"], kern.reshape(n1, n2, HY_ORDER * D), col0=0, C=HY_ORDER * D)
    kf = fft_mid(ka, cst["tw_c"], cst["tw_s"], cst["w2"])
    z3, zc0 = u3, 0
    for o in range(HY_ORDER):
        a = fft_first(cst["w1_data"], z3, col0=zc0, C=D)
        bmid = fft_mid(a, cst["tw_c"], cst["tw_s"], cst["w2"], cst["w2_inv"], kf, kf_col0=o * D)
        z3 = fft_last(cst["w1_inv"], bmid, u3, (o + 1) * D, z3, zc0, asum, bias, vec_col0=o * D)
        zc0 = 0
    return z3.reshape(L, D)


def _dense_dft_constants(L):
    N = 2 * L
    ang = 2 * np.pi * np.arange(N)[:, None] * np.arange(N)[None, :] / N
    wf = np.concatenate([np.cos(ang), -np.sin(ang)], 0)
    wi = np.concatenate([np.cos(ang[:L]), -np.sin(ang[:L])], 1) / N
    return _hilo(wf), _hilo(wf[:, :L]), _hilo(wi)


def hyena_conv_ctx(v, x1, x2, kern, asum, bias):
    L, D = v.shape
    wf, wf_data, wi = _dense_dft_constants(L)
    kf = hp_matmul(wf, kern).reshape(2, 2 * L, HY_ORDER * D)
    z = v
    for o, gate in enumerate((x1, x2)):
        xf = hp_matmul(wf_data, z).reshape(2, 2 * L, D)
        y = complex_mul(xf, kf, kf_col0=o * D).reshape(4 * L, D)
        z = hp_matmul(wi, y, gate=gate, z=z, asum=asum, bias=bias, vec_col0=o * D)
    return z


def _na_kernel(q_ref, k_ref, v_ref, bias_ref, o_ref, *, n_ctx, rows, heads_per_step):
    r = pl.program_id(1)
    kr = NA_ROWS
    r0 = jnp.clip(r - kr // 2, 0, rows - kr)
    start = pl.multiple_of(r0 * GRID_W, GRID_W)
    n_lat = rows * GRID_W
    dh = NA_HEAD_DIM
    scale = dh ** -0.5
    for hh in range(heads_per_step):
        sl = slice(hh * dh, (hh + 1) * dh)
        q = q_ref[:, sl]
        s_ctx = _dot_nt(q, k_ref[n_lat:n_lat + n_ctx, sl]) * scale
        s_lat = _dot_nt(q, k_ref[pl.ds(start, kr * GRID_W), sl]) * scale + bias_ref[hh]
        m = jnp.maximum(jnp.max(s_ctx, axis=1, keepdims=True), jnp.max(s_lat, axis=1, keepdims=True))
        p_ctx = jnp.exp(s_ctx - m)
        p_lat = jnp.exp(s_lat - m)
        l = jnp.sum(p_ctx, axis=1, keepdims=True) + jnp.sum(p_lat, axis=1, keepdims=True)
        o = (_dot(p_ctx.astype(BF16), v_ref[n_lat:n_lat + n_ctx, sl])
             + _dot(p_lat.astype(BF16), v_ref[pl.ds(start, kr * GRID_W), sl]))
        o_ref[:, sl] = (o / l).astype(o_ref.dtype)


def _na_bias_table(rpb, rows):
    H = rpb.shape[0]
    cols = jnp.arange(GRID_W)
    c0 = jnp.clip(cols - NA_COLS // 2, 0, GRID_W - NA_COLS)
    kc = jnp.arange(GRID_W)[None, :]
    inside = (kc >= c0[:, None]) & (kc < c0[:, None] + NA_COLS)
    rel = jnp.clip(kc - cols[:, None] + NA_COLS - 1, 0, 2 * NA_COLS - 2)
    rr = jnp.arange(NA_ROWS)[:, None] + jnp.arange(NA_ROWS)[None, :]
    t = rpb[:, rr]
    t = t[:, :, :, rel]
    t = jnp.where(inside[None, None, None], t, -1e30)
    t = jnp.transpose(t, (1, 0, 3, 2, 4))
    return t.reshape(NA_ROWS, H, GRID_W, NA_ROWS * GRID_W)


def na_attention(qkv, rpb, *, n_ctx):
    NT, D3 = qkv.shape
    D = D3 // 3
    L = NT - n_ctx
    rows = L // GRID_W
    hps = 4
    hw = hps * NA_HEAD_DIM
    nhb = D // hw
    table = _na_bias_table(rpb, rows)

    def pattern(r):
        r0 = jnp.clip(r - NA_ROWS // 2, 0, rows - NA_ROWS)
        return r0 - r + NA_ROWS - 1

    return pl.pallas_call(
        functools.partial(_na_kernel, n_ctx=n_ctx, rows=rows, heads_per_step=hps),
        grid=(nhb, rows),
        in_specs=[
            pl.BlockSpec((GRID_W, hw), lambda h, r: (r, h)),
            pl.BlockSpec((NT, hw), lambda h, r: (0, nhb + h)),
            pl.BlockSpec((NT, hw), lambda h, r: (0, 2 * nhb + h)),
            pl.BlockSpec((None, hps, GRID_W, NA_ROWS * GRID_W), lambda h, r: (pattern(r), h, 0, 0)),
        ],
        out_specs=pl.BlockSpec((GRID_W, hw), lambda h, r: (r, h)),
        out_shape=jax.ShapeDtypeStruct((L, D), BF16),
        compiler_params=_params("parallel", "parallel"),
    )(qkv, qkv, qkv, table)


def _router_kernel(x_ref, w_ref, b_ref, o_ref):
    E, G = N_EXPERTS, N_GROUPS
    per = E // G
    logits = _dot_nt(w_ref[...].astype(BF16), x_ref[...])
    scores = jax.nn.sigmoid(logits)
    sel = scores + b_ref[...]
    tm = sel.shape[1]
    grp = sel.reshape(G, per, tm)
    eidx = lax.broadcasted_iota(jnp.int32, grp.shape, 1)
    m1 = jnp.max(grp, axis=1, keepdims=True)
    first = jnp.min(jnp.where(grp == m1, eidx, per), axis=1, keepdims=True)
    m2 = jnp.max(jnp.where(eidx == first, -jnp.inf, grp), axis=1, keepdims=True)
    gs = (m1 + m2).reshape(G, tm)
    gidx = lax.broadcasted_iota(jnp.int32, gs.shape, 0)
    gmask = jnp.zeros(gs.shape, jnp.bool_)
    for _ in range(TOPK_GROUPS):
        mx = jnp.max(gs, axis=0, keepdims=True)
        pick = gidx == jnp.min(jnp.where(gs == mx, gidx, G), axis=0, keepdims=True)
        gmask = jnp.logical_or(gmask, pick)
        gs = jnp.where(pick, -jnp.inf, gs)
    emask = jnp.broadcast_to(gmask.reshape(G, 1, tm), (G, per, tm)).reshape(E, tm)
    cand = jnp.where(emask, sel, -jnp.inf)
    xidx = lax.broadcasted_iota(jnp.int32, cand.shape, 0)
    chosen = jnp.zeros(cand.shape, jnp.bool_)
    for _ in range(TOP_K):
        mx = jnp.max(cand, axis=0, keepdims=True)
        pick = xidx == jnp.min(jnp.where(cand == mx, xidx, E), axis=0, keepdims=True)
        chosen = jnp.logical_or(chosen, pick)
        cand = jnp.where(pick, -jnp.inf, cand)
    wsel = jnp.where(chosen, scores, 0.0)
    gates = wsel / jnp.sum(wsel, axis=0, keepdims=True) * ROUTED_SCALE
    o_ref[...] = gates.T


def moe_router(h, router_w_t, router_b, layer):
    T, D = h.shape
    E = router_w_t.shape[1]
    tm = _pick(T, (256, 128))
    return pl.pallas_call(
        _router_kernel,
        grid=(T // tm,),
        in_specs=[
            pl.BlockSpec((tm, D), lambda i: (i, 0)),
            pl.BlockSpec((None, E, D), lambda i: (layer, 0, 0)),
            pl.BlockSpec((None, E, 1), lambda i: (layer, 0, 0)),
        ],
        out_specs=pl.BlockSpec((tm, E), lambda i: (i, 0)),
        out_shape=jax.ShapeDtypeStruct((T, E), F32),
        compiler_params=_params("parallel"),
    )(h, router_w_t, router_b.reshape(-1, E, 1))


def _moe_up_kernel(*refs, nk, ne, gated, F):
    if gated:
        x_ref, wg_ref, wu_ref, g_ref, o_ref, accg_ref, accu_ref = refs
    else:
        x_ref, wg_ref, wu_ref, o_ref, accg_ref, accu_ref = refs
    e0 = pl.program_id(1) * ne
    k = pl.program_id(2)

    @pl.when(k == 0)
    def _():
        accg_ref[...] = jnp.zeros_like(accg_ref)
        accu_ref[...] = jnp.zeros_like(accu_ref)

    def stacked_experts(ref):
        w = ref[0] if ne == 1 else jnp.concatenate([ref[ee] for ee in range(ne)], axis=0)
        return w.astype(BF16)

    x = x_ref[...]
    accg_ref[...] += _dot_nt(x, stacked_experts(wg_ref))
    accu_ref[...] += _dot_nt(x, stacked_experts(wu_ref))

    @pl.when(k == nk - 1)
    def _():
        a = accg_ref[...]
        hid = a * jax.nn.sigmoid(a) * accu_ref[...]
        if gated:
            g = g_ref[...]
            lane = lax.broadcasted_iota(jnp.int32, g.shape, 1)
            col = lax.broadcasted_iota(jnp.int32, (1, ne * F), 1) // F
            gexp = jnp.zeros_like(hid)
            for ee in range(ne):
                ge = jnp.sum(jnp.where(lane == e0 + ee, g, 0.0), axis=1, keepdims=True)
                gexp = jnp.where(col == ee, ge, gexp)
            hid = hid * gexp
        o_ref[...] = hid.astype(o_ref.dtype)


def moe_up(x, w_gate, w_up, layer, gates=None):
    T, D = x.shape
    if w_gate.ndim == 3:
        w_gate, w_up = w_gate[:, None], w_up[:, None]
    w_gate, w_up = jnp.swapaxes(w_gate, 2, 3), jnp.swapaxes(w_up, 2, 3)
    E, F = w_gate.shape[1], w_gate.shape[2]
    ne = 4 if E % 4 == 0 else 1
    tm = _pick(T, (1056, 1024, 512, 256, 128))
    tk = _pick(D, (1024, 512, 256, 128))
    nk = D // tk
    gated = gates is not None
    w_spec = pl.BlockSpec((None, ne, F, tk), lambda i, e, k: (layer, e, 0, k))
    in_specs = [pl.BlockSpec((tm, tk), lambda i, e, k: (i, k)), w_spec, w_spec]
    args = [x, w_gate, w_up]
    if gated:
        in_specs.append(pl.BlockSpec((tm, E), lambda i, e, k: (i, 0)))
        args.append(gates)
    return pl.pallas_call(
        functools.partial(_moe_up_kernel, nk=nk, ne=ne, gated=gated, F=F),
        grid=(T // tm, E // ne, nk),
        in_specs=in_specs,
        out_specs=pl.BlockSpec((tm, ne * F), lambda i, e, k: (i, e)),
        out_shape=jax.ShapeDtypeStruct((T, E * F), BF16),
        scratch_shapes=[pltpu.VMEM((tm, ne * F), F32), pltpu.VMEM((tm, ne * F), F32)],
        compiler_params=_params("parallel", "parallel", "arbitrary"),
    )(*args)


def moe(h, layer, router_w_t, router_b, w_gate, w_up, w_down, s_gate, s_up, s_down):
    gates = moe_router(h, router_w_t, router_b, layer)
    hid = moe_up(h, w_gate, w_up, layer, gates)
    hid_sh = moe_up(h, s_gate, s_up, layer)
    E, F, D = w_down.shape[1:]
    return matmul(hid, w_down.reshape(DEPTH, E * F, D), out_dtype=F32, layer=layer, extra=(hid_sh, s_down, layer))


def kernel(x, c, ctx, c_ctx, ada_a, ada_b, ada_bias, ln1_g, ln1_b, ln2_g, ln2_b, router_w, router_b, moe_w_gate, moe_w_up, moe_w_down, sh_w_gate, sh_w_up, sh_w_down, ml_w_in, ml_b_if, ml_norm_g, ml_w_out, da_w_qkv, da_lam_q1, da_lam_k1, da_lam_q2, da_lam_k2, da_subln_g, da_w_out, hy_w_in, hy_conv_w, hy_conv_b, hy_f_w1, hy_f_b1, hy_f_w2, hy_f_b2, hy_f_w3, hy_f_b3, hy_f_w4, hy_bias, hy_w_out, na_w_qkv, na_rpb, na_w_out):
    _, L, D = x.shape
    C = ctx.shape[1]
    NT = C + L
    s = jnp.concatenate([x[0], ctx[0]], axis=0)

    cc = jnp.zeros((8, D), F32).at[0].set(c_ctx).at[1].set(c[0])
    mods = ada_modulation(cc, ada_a, ada_b, ada_bias)[:, :2].reshape(DEPTH, 2, 6, 1, D)
    SH1, SC1, G1, SH2, SC2, G2 = range(6)
    router_w_t = jnp.swapaxes(router_w, 1, 2)
    moe_args = (router_w_t, router_b, moe_w_gate, moe_w_up, moe_w_down, sh_w_gate, sh_w_up, sh_w_down)

    (h,) = ln_modulate(s, mods, n_ctx=C, h_layer=0, sc_idx=SC1, sh_idx=SH1)
    for i in range(DEPTH):
        last = i == DEPTH - 1
        if i == 0:
            y = mlstm_mixer(h, ml_w_in, ml_b_if[0], ml_norm_g[0], ml_w_out, n_ctx=C)
        elif i == 1:
            lam4 = jnp.stack([da_lam_q1[0], da_lam_k1[0], da_lam_q2[0], da_lam_k2[0]])
            y = diff_attn_mixer(h, da_w_qkv, lam4, da_subln_g[0], da_w_out, n_ctx=C,
                                lambda_init=0.8 - 0.6 * math.exp(-0.3 * i))
        elif i == 2:
            y = hyena_mixer(h, hy_w_in, hy_conv_w[0], hy_conv_b[0], hy_f_w1[0], hy_f_b1[0], hy_f_w2[0], hy_f_b2[0],
                            hy_f_w3[0], hy_f_b3[0], hy_f_w4[0], hy_bias[0], hy_w_out, n_ctx=C)
        else:
            y = na_mixer(h, na_w_qkv, na_rpb[0], na_w_out, n_ctx=C)
        n_ctx = C
        if last:
            s, n_ctx = s[:L], 0
        s, h = ln_modulate(s, mods, n_ctx=n_ctx, ys=(y,), ln_layer=i, gate_idx=G1, ln_g=ln1_g, ln_b=ln1_b,
                           h_layer=i, sc_idx=SC2, sh_idx=SH2)
        y = moe(h, i, *moe_args)
        if last:
            (s,) = ln_modulate(s, mods, n_ctx=n_ctx, ys=(y,), ln_layer=i, gate_idx=G2, ln_g=ln2_g, ln_b=ln2_b)
        else:
            s, h = ln_modulate(s, mods, n_ctx=n_ctx, ys=(y,), ln_layer=i, gate_idx=G2, ln_g=ln2_g, ln_b=ln2_b,
                               h_layer=i + 1, sc_idx=SC1, sh_idx=SH1)
    return s[None]


def mlstm_mixer(h, w_in, b_if, norm_g, w_out, *, n_ctx):
    D = h.shape[1]
    w_in_t = jnp.swapaxes(w_in, 1, 2)
    u = matmul(h, w_in_t, out_dtype=F32, layer=0, ncols=3 * D, w_transposed=True)
    w_if_t = jnp.pad(w_in_t[0, 3 * D:], ((0, LANES - 4 * ML_HEADS), (0, 0)))
    g_if = matmul(h, w_if_t, out_dtype=F32, w_transposed=True)
    hs = mlstm_scan(u, g_if, b_if, n_ctx=n_ctx)
    return matmul(mlstm_gate(hs, u, norm_g, o_col_block=2), w_out, out_dtype=F32, layer=0)


def diff_attn_mixer(h, w_qkv, lam4, subln_g, w_out, *, n_ctx, lambda_init):
    NT, D = h.shape
    L = NT - n_ctx
    scale = DA_HEAD_DIM ** -0.5 * math.log2(math.e)
    qk = matmul(h, w_qkv, out_dtype=F32, layer=0, ncols=2 * D)
    v = matmul(h, w_qkv, out_dtype=BF16, layer=0, col0=2 * D, ncols=D)
    cos, sin = _rope_tables(L, 1.0)
    ones, zeros = jnp.ones((n_ctx, LANES), F32), jnp.zeros((n_ctx, LANES), F32)
    q_lat = rope_cast(qk, cos * scale, sin * scale, row0=0, nrows=L, col0=0, ncols=D)
    q_ctx = rope_cast(qk, ones * scale, zeros, row0=L, nrows=n_ctx, col0=0, ncols=D)
    k = rope_cast(qk, jnp.concatenate([cos, ones]), jnp.concatenate([sin, zeros]), row0=0, nrows=NT, col0=D, ncols=D)
    o_lat = diff_attention(q_lat, k, v, lam4, subln_g, kv_row0=0, kv_len=NT, lambda_init=lambda_init)
    o_ctx = diff_attention(q_ctx, k, v, lam4, subln_g, kv_row0=L, kv_len=n_ctx, lambda_init=lambda_init)
    return matmul(jnp.concatenate([o_lat, o_ctx]), w_out, out_dtype=F32, layer=0)


def hyena_mixer(h, w_in, conv_w, conv_b, f_w1, f_b1, f_w2, f_b2, f_w3, f_b3, f_w4, hy_bias, w_out, *, n_ctx):
    NT, D = h.shape
    L = NT - n_ctx
    u = short_conv(matmul(h, w_in, out_dtype=F32, layer=0), conv_w, conv_b, n_ctx=n_ctx)
    fargs = (f_w1, f_b1, f_w2, f_b2, f_w3, f_b3, f_w4)
    bias = hy_bias.reshape(1, HY_ORDER * D)
    kern_l, asum_l = hyena_filters(L, D, *fargs)
    z_lat = hyena_conv_lat(u.reshape(NT // FFT_N2, FFT_N2, 3 * D), kern_l, asum_l, bias, L=L, D=D)
    kern_c, asum_c = hyena_filters(n_ctx, D, *fargs)
    v, x1, x2 = (u[L:, j * D:(j + 1) * D] for j in range(3))
    z_ctx = hyena_conv_ctx(v, x1, x2, kern_c, asum_c, bias)
    return matmul(jnp.concatenate([z_lat, z_ctx]), w_out, out_dtype=F32, layer=0)


def na_mixer(h, w_qkv, rpb, w_out, *, n_ctx):
    qkv = matmul(h, w_qkv, out_dtype=BF16, layer=0)
    return matmul(na_attention(qkv, rpb, n_ctx=n_ctx), w_out, out_dtype=F32, layer=0)
```

```python
import functools
import math

import numpy as np
import jax
import jax.numpy as jnp
from jax import lax
from jax.experimental import pallas as pl
from jax.experimental.pallas import tpu as pltpu

F32 = jnp.float32
BF16 = jnp.bfloat16

V7X_VMEM_LIMIT_BYTES = 56 * 1024 * 1024
LANES = 128

GRID_W = 64
DEPTH = 4
DEEPNORM_ALPHA = (2 * DEPTH) ** 0.25
LN_EPS = 1e-5
RMS_EPS = 1e-6
ROPE_BASE = 10000.0
ML_HEADS = 8
ML_CHUNK = 256
DA_HEAD_DIM = 128
HY_ORDER = 2
HY_EMB = 33
HY_BANDS = (HY_EMB - 1) // 2
HY_FAST_DECAY = 0.3
HY_SLOW_DECAY = 1.5
HY_DECAY_TARGET = 1e-2
NA_HEAD_DIM = 128
NA_ROWS = 8
NA_COLS = 16
N_EXPERTS = 64
TOP_K = 8
N_GROUPS = 8
TOPK_GROUPS = 4
ROUTED_SCALE = 2.5


def _params(*sem):
    return pltpu.CompilerParams(dimension_semantics=sem, vmem_limit_bytes=V7X_VMEM_LIMIT_BYTES)


def _dot(a, b):
    return jnp.dot(a, b, preferred_element_type=F32)


def _dot_nt(a, b):
    return lax.dot_general(a, b, (((1,), (1,)), ((), ())), preferred_element_type=F32)


def _dot_tn(a, b):
    return lax.dot_general(a, b, (((0,), (0,)), ((), ())), preferred_element_type=F32)


def _pick(n, prefs):
    for p in prefs:
        if n % p == 0:
            return p
    return n


def _mm_kernel(*refs, nk, has_extra):
    if has_extra:
        a_ref, w_ref, a2_ref, w2_ref, o_ref, acc_ref = refs
    else:
        a_ref, w_ref, o_ref, acc_ref = refs
    k = pl.program_id(2)

    @pl.when(k == 0)
    def _():
        acc_ref[...] = jnp.zeros_like(acc_ref)

    acc_ref[...] += _dot(a_ref[...].astype(BF16), w_ref[...].astype(BF16))

    @pl.when(k == nk - 1)
    def _():
        acc = acc_ref[...]
        if has_extra:
            acc = acc + _dot(a2_ref[...].astype(BF16), w2_ref[...].astype(BF16))
        o_ref[...] = acc.astype(o_ref.dtype)


def _mm_resident_kernel(a_ref, w_ref, o_ref, *, w_transposed):
    dot = _dot_nt if w_transposed else _dot
    o_ref[...] = dot(a_ref[...].astype(BF16), w_ref[...].astype(BF16)).astype(o_ref.dtype)


def _matmul_resident(a, w, *, out_dtype, layer, col0, N, w_transposed):
    M, K = a.shape
    tm = _pick(M, (2112, 2048, 1024, 512, 256, 128) if a.dtype == BF16 else (1056, 1024, 512, 256, 128))
    tn = _pick(N, (256, 128))
    cb = col0 // tn
    lead = (None,) if w.ndim == 3 else ()
    lidx = (layer,) if w.ndim == 3 else ()
    if w_transposed:
        w_spec = pl.BlockSpec(lead + (tn, K), lambda i, j: lidx + (j + cb, 0))
    else:
        w_spec = pl.BlockSpec(lead + (K, tn), lambda i, j: lidx + (0, j + cb))
    return pl.pallas_call(
        functools.partial(_mm_resident_kernel, w_transposed=w_transposed),
        grid=(M // tm, N // tn),
        in_specs=[pl.BlockSpec((tm, K), lambda i, j: (i, 0), pipeline_mode=pl.Buffered(1)), w_spec],
        out_specs=pl.BlockSpec((tm, tn), lambda i, j: (i, j)),
        out_shape=jax.ShapeDtypeStruct((M, N), out_dtype),
        compiler_params=_params("parallel", "arbitrary"),
    )(a, w)


def matmul(a, w, *, out_dtype, layer=None, col0=0, ncols=None, extra=None, row0=0, nrows=None, w_transposed=False):
    M = a.shape[0] if nrows is None else nrows
    K = a.shape[1]
    N = (w.shape[-2] if w_transposed else w.shape[-1]) if ncols is None else ncols
    if K <= 4096 and extra is None and nrows is None and M >= 1024:
        return _matmul_resident(a, w, out_dtype=out_dtype, layer=layer, col0=col0, N=N, w_transposed=w_transposed)
    assert not w_transposed
    tm = _pick(M, (1056, 1024, 512, 256, 128, 64, 32, 16, 8))
    tn = _pick(N, (1024, 512, 256, 128))
    tk = _pick(K, (2048, 1024, 512, 256, 128))
    assert col0 % tn == 0 and row0 % tm == 0
    cb, rb = col0 // tn, row0 // tm
    nk = K // tk
    if w.ndim == 3:
        w_spec = pl.BlockSpec((None, tk, tn), lambda i, j, k: (layer, k, j + cb))
    else:
        w_spec = pl.BlockSpec((tk, tn), lambda i, j, k: (k, j + cb))
    in_specs = [pl.BlockSpec((tm, tk), lambda i, j, k: (i + rb, k)), w_spec]
    args = [a, w]
    if extra is not None:
        a2, w2, layer2 = extra
        k2 = a2.shape[1]
        in_specs.append(pl.BlockSpec((tm, k2), lambda i, j, k: (i + rb, 0)))
        in_specs.append(pl.BlockSpec((None, k2, tn), lambda i, j, k: (layer2, 0, j)))
        args += [a2, w2]
    return pl.pallas_call(
        functools.partial(_mm_kernel, nk=nk, has_extra=extra is not None),
        grid=(M // tm, N // tn, nk),
        in_specs=in_specs,
        out_specs=pl.BlockSpec((tm, tn), lambda i, j, k: (i, j)),
        out_shape=jax.ShapeDtypeStruct((M, N), out_dtype),
        scratch_shapes=[pltpu.VMEM((tm, tn), F32)],
        compiler_params=_params("parallel", "parallel", "arbitrary"),
    )(*args)


def _ada_kernel(cc_ref, a_ref, b_ref, bias_ref, o_ref):
    cc = cc_ref[...]
    act = cc * jax.nn.sigmoid(cc)
    t = _dot(act.astype(BF16), a_ref[...].astype(BF16))
    o_ref[...] = _dot(t.astype(BF16), b_ref[...].astype(BF16)) + bias_ref[...]


def ada_modulation(cc, ada_a, ada_b, ada_bias):
    depth, D, R = ada_a.shape
    n6 = ada_b.shape[-1]
    tn = D
    return pl.pallas_call(
        _ada_kernel,
        grid=(depth, n6 // tn),
        in_specs=[
            pl.BlockSpec((8, D), lambda l, n: (0, 0)),
            pl.BlockSpec((None, D, R), lambda l, n: (l, 0, 0)),
            pl.BlockSpec((None, R, tn), lambda l, n: (l, 0, n)),
            pl.BlockSpec((None, 1, tn), lambda l, n: (l, 0, n)),
        ],
        out_specs=pl.BlockSpec((None, 8, tn), lambda l, n: (l, 0, n)),
        out_shape=jax.ShapeDtypeStruct((depth, 8, n6), F32),
        compiler_params=_params("parallel", "parallel"),
    )(cc, ada_a, ada_b, ada_bias.reshape(depth, 1, n6))


def _lnmod_kernel(*refs, has_ln, has_h, n_y):
    it = iter(refs)
    s_ref = next(it)
    if has_ln:
        y_refs = [next(it) for _ in range(n_y)]
        gate_ref, lng_ref, lnb_ref = next(it), next(it), next(it)
    if has_h:
        sc_ref, sh_ref = next(it), next(it)
    if has_ln:
        so_ref = next(it)
    if has_h:
        h_ref = next(it)
    x = s_ref[...]
    if has_ln:
        y = y_refs[0][...].astype(F32)
        for r in y_refs[1:]:
            y = y + r[...].astype(F32)
        v = DEEPNORM_ALPHA * x + gate_ref[...] * y
        mu = jnp.mean(v, axis=-1, keepdims=True)
        var = jnp.mean(jnp.square(v - mu), axis=-1, keepdims=True)
        x = (v - mu) * lax.rsqrt(var + LN_EPS) * lng_ref[...] + lnb_ref[...]
        so_ref[...] = x
    if has_h:
        h_ref[...] = (x * (1.0 + sc_ref[...]) + sh_ref[...]).astype(h_ref.dtype)


def ln_modulate(s, mods, *, n_ctx, ys=(), ln_layer=None, gate_idx=None, ln_g=None, ln_b=None,
                h_layer=None, sc_idx=None, sh_idx=None):
    R, D = s.shape
    tr = 128
    has_ln, has_h = len(ys) > 0, h_layer is not None
    n_lat_blocks = (R - n_ctx) // tr
    row_spec = pl.BlockSpec((tr, D), lambda i: (i, 0))

    def mod_spec(layer, idx):
        return pl.BlockSpec((None, None, None, 1, D),
                            lambda i: (layer, jnp.where(i >= n_lat_blocks, 0, 1), idx, 0, 0))

    vec_spec = pl.BlockSpec((None, 1, D), lambda i: (ln_layer, 0, 0))
    in_specs, args = [row_spec], [s]
    if has_ln:
        in_specs += [row_spec] * len(ys) + [mod_spec(ln_layer, gate_idx), vec_spec, vec_spec]
        args += list(ys) + [mods, ln_g.reshape(DEPTH, 1, D), ln_b.reshape(DEPTH, 1, D)]
    if has_h:
        in_specs += [mod_spec(h_layer, sc_idx), mod_spec(h_layer, sh_idx)]
        args += [mods, mods]
    out_specs, out_shape = [], []
    if has_ln:
        out_specs.append(row_spec)
        out_shape.append(jax.ShapeDtypeStruct((R, D), F32))
    if has_h:
        out_specs.append(row_spec)
        out_shape.append(jax.ShapeDtypeStruct((R, D), BF16))
    outs = pl.pallas_call(
        functools.partial(_lnmod_kernel, has_ln=has_ln, has_h=has_h, n_y=len(ys)),
        grid=(R // tr,),
        in_specs=in_specs,
        out_specs=out_specs,
        out_shape=out_shape,
        compiler_params=_params("parallel"),
    )(*args)
    return outs


def _log_sigmoid(x):
    return jnp.minimum(x, 0.0) - jnp.log(1.0 + jnp.exp(-jnp.abs(x)))


def _mlstm_kernel(bias_ref, q_ref, k_ref, v_ref, ic_ref, fc_ref, ir_ref, fr_ref, o_ref, ct_ref, n_ref, m_ref,
                  *, T, dqk):
    d, h, t = pl.program_id(0), pl.program_id(1), pl.program_id(2)

    @pl.when(t == 0)
    def _():
        ct_ref[...] = jnp.zeros_like(ct_ref)
        n_ref[...] = jnp.zeros_like(n_ref)
        m_ref[...] = jnp.zeros_like(m_ref)

    bi = bias_ref[d * 2 * ML_HEADS + h]
    bf = bias_ref[(d * 2 + 1) * ML_HEADS + h]
    i_c = ic_ref[...] + bi
    i_r = ir_ref[...] + bi
    f_c = _log_sigmoid(fc_ref[...] + bf)
    f_r = _log_sigmoid(fr_ref[...] + bf)
    row = lax.broadcasted_iota(jnp.int32, (T, T), 0)
    col = lax.broadcasted_iota(jnp.int32, (T, T), 1)
    sgn = 1 - 2 * d
    incl = (col - row) * sgn <= 0
    incl_t = (row - col) * sgn <= 0
    b_c = jnp.sum(jnp.where(incl, f_r, 0.0), axis=1, keepdims=True)
    b_r = jnp.sum(jnp.where(incl_t, f_c, 0.0), axis=0, keepdims=True)
    f_tot = jnp.sum(f_r, axis=1, keepdims=True)
    m_prev = m_ref[...]
    dmat = jnp.where(incl, b_c - b_r + i_r, -jnp.inf)
    inter = b_c + m_prev
    m_t = jnp.maximum(inter, jnp.max(dmat, axis=1, keepdims=True))
    w_intra = jnp.exp(dmat - m_t)
    w_inter = jnp.exp(inter - m_t)
    q = q_ref[...] * (dqk ** -0.5)
    k = k_ref[...]
    v = v_ref[...]
    qb, kb, vb = q.astype(BF16), k.astype(BF16), v.astype(BF16)
    s = _dot_nt(qb, kb) * w_intra
    ct = ct_ref[...]
    n = n_ref[...]
    num = w_inter * _dot(qb, ct.astype(BF16)) + _dot(s.astype(BF16), vb)
    den = w_inter * jnp.sum(q * n, axis=1, keepdims=True) + jnp.sum(s, axis=1, keepdims=True)
    o_ref[...] = num / jnp.maximum(jnp.abs(den), jnp.exp(-m_t))
    g_r = f_tot - b_r + i_r
    g_c = f_tot - b_c + i_c
    m_new = jnp.maximum(f_tot + m_prev, jnp.max(g_r, axis=1, keepdims=True))
    decay = jnp.exp(f_tot + m_prev - m_new)
    wk = jnp.exp(g_c - m_new)
    ct_ref[...] = decay * ct + _dot_tn(kb, (wk * v).astype(BF16))
    n_ref[...] = decay * n + jnp.sum(wk * k, axis=0, keepdims=True)
    m_ref[...] = m_new


def mlstm_scan(u, gates, b_if, *, n_ctx):
    NT = u.shape[0]
    H, T = ML_HEADS, ML_CHUNK
    D = u.shape[1] // 3
    dqk, dv = D // (2 * H), D // H
    nc, ncc = NT // T, n_ctx // T
    ncl = nc - ncc
    g4 = gates[:, :4 * H].T
    g_col = g4.reshape(4 * H, NT, 1)
    g_row = g4.reshape(4 * H, nc, 1, T)

    def chunk(d, t):
        fwd = jnp.where(t < ncc, ncl + t, t - ncc)
        return jnp.where(d == 0, fwd, nc - 1 - t)

    kq, kk, kv = 0, (H * dqk) // dqk, (2 * H * dqk) // dv
    col_spec = lambda off: pl.BlockSpec((None, T, 1), lambda d, h, t: ((2 * d + off) * H + h, chunk(d, t), 0))
    row_spec = lambda off: pl.BlockSpec((None, None, 1, T), lambda d, h, t: ((2 * d + off) * H + h, chunk(d, t), 0, 0))
    return pl.pallas_call(
        functools.partial(_mlstm_kernel, T=T, dqk=dqk),
        grid=(2, H, nc),
        in_specs=[
            pl.BlockSpec(memory_space=pltpu.SMEM),
            pl.BlockSpec((T, dqk), lambda d, h, t: (chunk(d, t), kq + h)),
            pl.BlockSpec((T, dqk), lambda d, h, t: (chunk(d, t), kk + h)),
            pl.BlockSpec((T, dv), lambda d, h, t: (chunk(d, t), kv + h)),
            col_spec(0), col_spec(1), row_spec(0), row_spec(1),
        ],
        out_specs=pl.BlockSpec((None, T, dv), lambda d, h, t: (d, chunk(d, t), h)),
        out_shape=jax.ShapeDtypeStruct((2, NT, H * dv), F32),
        scratch_shapes=[pltpu.VMEM((dqk, dv), F32), pltpu.VMEM((1, dqk), F32), pltpu.VMEM((1, 1), F32)],
        compiler_params=_params("parallel", "parallel", "arbitrary"),
    )(b_if, u, u, u, g_col, g_col, g_row, g_row)


def _mlgate_kernel(hs_ref, o_ref, g_ref, out_ref, *, dv):
    D = out_ref.shape[1]
    for h in range(D // dv):
        sl = slice(h * dv, (h + 1) * dv)
        x = hs_ref[0, :, sl] + hs_ref[1, :, sl]
        r = lax.rsqrt(jnp.mean(jnp.square(x), axis=-1, keepdims=True) + RMS_EPS)
        out_ref[:, sl] = (x * r * g_ref[:, sl] * jax.nn.sigmoid(o_ref[:, sl])).astype(out_ref.dtype)


def mlstm_gate(hs, u, norm_g, *, o_col_block):
    _, NT, D = hs.shape
    tr = 128
    return pl.pallas_call(
        functools.partial(_mlgate_kernel, dv=D // ML_HEADS),
        grid=(NT // tr,),
        in_specs=[
            pl.BlockSpec((2, tr, D), lambda i: (0, i, 0)),
            pl.BlockSpec((tr, D), lambda i: (i, o_col_block)),
            pl.BlockSpec((1, D), lambda i: (0, 0)),
        ],
        out_specs=pl.BlockSpec((tr, D), lambda i: (i, 0)),
        out_shape=jax.ShapeDtypeStruct((NT, D), BF16),
        compiler_params=_params("parallel"),
    )(hs, u, norm_g.reshape(1, D))


def _rope_kernel(x_ref, cos_ref, sin_ref, o_ref):
    cos, sin = cos_ref[...], sin_ref[...]
    lane = lax.broadcasted_iota(jnp.int32, cos.shape, 1)
    first_half = (lane % 64) < 32
    for g in range(x_ref.shape[1] // LANES):
        sl = slice(g * LANES, (g + 1) * LANES)
        x = x_ref[:, sl]
        rot = jnp.where(first_half, -pltpu.roll(x, LANES - 32, 1), pltpu.roll(x, 32, 1))
        o_ref[:, sl] = (x * cos + rot * sin).astype(o_ref.dtype)


def rope_cast(x, cos, sin, *, row0, nrows, col0, ncols):
    tr = _pick(nrows, (256, 128))
    tc = _pick(ncols, (1024, 512, 256, 128))
    rb, cb = row0 // tr, col0 // tc
    assert row0 % tr == 0 and col0 % tc == 0
    return pl.pallas_call(
        _rope_kernel,
        grid=(nrows // tr, ncols // tc),
        in_specs=[
            pl.BlockSpec((tr, tc), lambda i, j: (i + rb, j + cb)),
            pl.BlockSpec((tr, LANES), lambda i, j: (i, 0)),
            pl.BlockSpec((tr, LANES), lambda i, j: (i, 0)),
        ],
        out_specs=pl.BlockSpec((tr, tc), lambda i, j: (i, j)),
        out_shape=jax.ShapeDtypeStruct((nrows, ncols), BF16),
        compiler_params=_params("parallel", "parallel"),
    )(x, cos, sin)


def _dattn_kernel(lam_ref, q_ref, k_ref, v_ref, g_ref, o_ref, m_ref, l_ref, acc_ref,
                  *, tk, n_chunks, out_scale, lambda_init):
    dh = DA_HEAD_DIM
    m_ref[...] = jnp.full_like(m_ref, -jnp.inf)
    l_ref[...] = jnp.zeros_like(l_ref)
    acc_ref[...] = jnp.zeros_like(acc_ref)

    def body(c, carry):
        start = pl.multiple_of(c * tk, tk)
        kc = k_ref[pl.ds(start, tk), :]
        vc = v_ref[pl.ds(start, tk), :]
        for i in range(2):
            s = _dot_nt(q_ref[:, i * dh:(i + 1) * dh], kc[:, i * dh:(i + 1) * dh])
            m_old = m_ref[i]
            m_new = jnp.maximum(m_old, jnp.max(s, axis=1, keepdims=True))
            p = jnp.exp2(s - m_new)
            alpha = jnp.exp2(m_old - m_new)
            l_ref[i] = alpha * l_ref[i] + jnp.sum(p, axis=1, keepdims=True)
            acc_ref[i] = alpha * acc_ref[i] + _dot(p.astype(BF16), vc)
            m_ref[i] = m_new
        return carry

    lax.fori_loop(0, n_chunks, body, 0)
    lam = (jnp.exp(jnp.sum(lam_ref[0:1, :] * lam_ref[1:2, :], axis=1, keepdims=True))
           - jnp.exp(jnp.sum(lam_ref[2:3, :] * lam_ref[3:4, :], axis=1, keepdims=True)) + lambda_init)
    o = acc_ref[0] / l_ref[0] - lam * (acc_ref[1] / l_ref[1])
    r = lax.rsqrt(jnp.mean(jnp.square(o), axis=-1, keepdims=True) + RMS_EPS)
    o_ref[...] = (o * r * g_ref[...] * out_scale).astype(o_ref.dtype)


def diff_attention(q, k, v, lam4, subln_g, *, kv_row0, kv_len, lambda_init):
    Nq, D = q.shape
    hd = 2 * DA_HEAD_DIM
    H = D // hd
    tq = _pick(Nq, (512, 256))
    tk = _pick(kv_len, (768, 512, 256))
    kvb = kv_row0 // kv_len
    assert kv_row0 % kv_len == 0
    return pl.pallas_call(
        functools.partial(_dattn_kernel, tk=tk, n_chunks=kv_len // tk, out_scale=1.0 - lambda_init,
                          lambda_init=lambda_init),
        grid=(H, Nq // tq),
        in_specs=[
            pl.BlockSpec((4, DA_HEAD_DIM), lambda h, i: (0, 0)),
            pl.BlockSpec((tq, hd), lambda h, i: (i, h)),
            pl.BlockSpec((kv_len, hd), lambda h, i: (kvb, h)),
            pl.BlockSpec((kv_len, hd), lambda h, i: (kvb, h)),
            pl.BlockSpec((1, hd), lambda h, i: (0, 0)),
        ],
        out_specs=pl.BlockSpec((tq, hd), lambda h, i: (i, h)),
        out_shape=jax.ShapeDtypeStruct((Nq, D), BF16),
        scratch_shapes=[pltpu.VMEM((2, tq, 1), F32), pltpu.VMEM((2, tq, 1), F32), pltpu.VMEM((2, tq, hd), F32)],
        compiler_params=_params("parallel", "parallel"),
    )(lam4, q, k, v, subln_g.reshape(1, hd))


def _rope_tables(L, scale):
    a = DA_HEAD_DIM // 2
    t = jnp.arange(L)
    pos = jnp.stack([t // GRID_W, t % GRID_W], -1).astype(F32)
    inv = ROPE_BASE ** (-jnp.arange(0, a, 2, dtype=F32) / a)
    ang = pos[:, :, None] * inv
    ang = jnp.concatenate([ang, ang], -1).reshape(L, DA_HEAD_DIM)
    return jnp.cos(ang) * scale, jnp.sin(ang) * scale


def _shortconv_kernel(p_ref, prev_ref, next_ref, w_ref, b_ref, o_ref, *, n_seq_first, n_seq_last):
    i = pl.program_id(0)
    x = p_ref[...]
    tr = x.shape[0]
    rid = lax.broadcasted_iota(jnp.int32, x.shape, 0)
    is_first = functools.reduce(jnp.logical_or, [i == b for b in n_seq_first])
    is_last = functools.reduce(jnp.logical_or, [i == b for b in n_seq_last])
    prev_row = jnp.where(is_first, 0.0, prev_ref[7:8, :])
    next_row = jnp.where(is_last, 0.0, next_ref[0:1, :])
    xm = jnp.where(rid == 0, prev_row, pltpu.roll(x, 1, 0))
    xp = jnp.where(rid == tr - 1, next_row, pltpu.roll(x, tr - 1, 0))
    o_ref[...] = xm * w_ref[0:1, :] + x * w_ref[1:2, :] + xp * w_ref[2:3, :] + b_ref[...]


def short_conv(p, w, b, *, n_ctx):
    NT, N = p.shape
    tr = 256
    tc = _pick(N, (1024, 512, 256, 128))
    nb, n8 = NT // tr, NT // 8
    n_lat_blocks = (NT - n_ctx) // tr
    firsts = (0, n_lat_blocks)
    lasts = (n_lat_blocks - 1, nb - 1)
    return pl.pallas_call(
        functools.partial(_shortconv_kernel, n_seq_first=firsts, n_seq_last=lasts),
        grid=(nb, N // tc),
        in_specs=[
            pl.BlockSpec((tr, tc), lambda i, j: (i, j)),
            pl.BlockSpec((8, tc), lambda i, j: (jnp.maximum(i * (tr // 8) - 1, 0), j)),
            pl.BlockSpec((8, tc), lambda i, j: (jnp.minimum((i + 1) * (tr // 8), n8 - 1), j)),
            pl.BlockSpec((3, tc), lambda i, j: (0, j)),
            pl.BlockSpec((1, tc), lambda i, j: (0, j)),
        ],
        out_specs=pl.BlockSpec((tr, tc), lambda i, j: (i, j)),
        out_shape=jax.ShapeDtypeStruct((NT, N), F32),
        compiler_params=_params("parallel", "parallel"),
    )(p, p, p, w, b.reshape(1, N))


def _filter_kernel(zf_ref, t_ref, w1_ref, b1_ref, w2_ref, b2_ref, w3_ref, b3_ref, w4f_ref, w4b_ref, dl_ref,
                   o_ref, asum_ref):
    i = pl.program_id(1)
    hdn = jnp.sin(_dot(zf_ref[...].astype(BF16), w1_ref[...].astype(BF16)) + b1_ref[...])
    hdn = jnp.sin(_dot(hdn.astype(BF16), w2_ref[...].astype(BF16)) + b2_ref[...])
    hdn = jnp.sin(_dot(hdn.astype(BF16), w3_ref[...].astype(BF16)) + b3_ref[...]).astype(BF16)
    hf = _dot(hdn, w4f_ref[...].astype(BF16))
    hb = _dot(hdn, w4b_ref[...].astype(BF16))
    side = t_ref[:, 1:2]
    kern = jnp.where(side > 0.0, hf, jnp.where(side < 0.0, hb, 0.0)) * jnp.exp(-t_ref[:, 0:1] * dl_ref[...])
    o_ref[...] = kern

    @pl.when(i == 0)
    def _():
        asum_ref[...] = jnp.zeros_like(asum_ref)

    asum_ref[...] += jnp.sum(jnp.abs(kern), axis=0, keepdims=True)


def hyena_filters(L, D, f_w1, f_b1, f_w2, f_b2, f_w3, f_b3, f_w4, *, t2_major_n2=None):
    t = jnp.linspace(0.0, 1.0, L, dtype=F32)[:, None]
    w = 2.0 * math.pi * jnp.arange(L, dtype=F32)[:, None] / L
    f = jnp.linspace(1e-4, HY_BANDS - 1, HY_BANDS, dtype=F32)[None]
    z = jnp.concatenate([t, jnp.cos(w * f), -jnp.sin(w * f)], -1)
    src = jnp.concatenate([jnp.arange(L), jnp.zeros((1,), jnp.int32), L - 1 - jnp.arange(L - 1)])
    r = jnp.arange(2 * L)
    side = jnp.where(r < L, 1.0, jnp.where(r > L, -1.0, 0.0)).astype(F32)[:, None]
    if t2_major_n2 is not None:
        perm = r.reshape(2 * L // t2_major_n2, t2_major_n2).T.reshape(-1)
        src, side = src[perm], side[perm]
    zf = jnp.pad(z[src], ((0, 0), (0, LANES - HY_EMB)))
    tcol = jnp.concatenate([t[src], side], axis=1)
    deltas = jnp.abs(jnp.linspace(math.log(HY_DECAY_TARGET) / HY_SLOW_DECAY, math.log(HY_DECAY_TARGET) / HY_FAST_DECAY,
                                  D, dtype=F32)).reshape(1, D)
    w1 = jnp.pad(f_w1, ((0, LANES - HY_EMB), (0, 0)))
    nf = f_w1.shape[1]
    R2 = 2 * L
    tr = _pick(R2, (512,))
    tc = _pick(D, (1024, 512, 256, 128))
    cpd = D // tc
    full = lambda shape: pl.BlockSpec(shape, lambda j, i: (0, 0))
    return pl.pallas_call(
        _filter_kernel,
        grid=(HY_ORDER * cpd, R2 // tr),
        in_specs=[
            pl.BlockSpec((tr, LANES), lambda j, i: (i, 0)),
            pl.BlockSpec((tr, 2), lambda j, i: (i, 0)),
            full((LANES, nf)), full((1, nf)), full((nf, nf)), full((1, nf)), full((nf, nf)), full((1, nf)),
            pl.BlockSpec((nf, tc), lambda j, i: (0, (j // cpd) * 2 * cpd + j % cpd)),
            pl.BlockSpec((nf, tc), lambda j, i: (0, ((j // cpd) * 2 + 1) * cpd + j % cpd)),
            pl.BlockSpec((1, tc), lambda j, i: (0, j % cpd)),
        ],
        out_specs=[pl.BlockSpec((tr, tc), lambda j, i: (i, j)), pl.BlockSpec((1, tc), lambda j, i: (0, j))],
        out_shape=[jax.ShapeDtypeStruct((R2, HY_ORDER * D), F32), jax.ShapeDtypeStruct((1, HY_ORDER * D), F32)],
        compiler_params=_params("parallel", "arbitrary"),
    )(zf, tcol, w1, f_b1.reshape(1, nf), f_w2, f_b2.reshape(1, nf), f_w3, f_b3.reshape(1, nf), f_w4, f_w4, deltas)


FFT_N2 = 128
FFT_TT2 = 8


def _split(x):
    hi = x.astype(BF16)
    return hi, (x - hi.astype(F32)).astype(BF16)


def _dot3(w_hi, w_lo, x):
    x_hi, x_lo = _split(x)
    return _dot(w_hi, x_hi) + (_dot(w_lo, x_hi) + _dot(w_hi, x_lo))


def _hilo(a):
    return _split(jnp.asarray(a, F32))


def _fft_first_kernel(whi_ref, wlo_ref, z_ref, o_ref, *, t2_major):
    n1 = o_ref.shape[1]
    for j in range(o_ref.shape[2]):
        a = _dot3(whi_ref[...], wlo_ref[...], z_ref[j] if t2_major else z_ref[:, j, :])
        o_ref[0, :, j, :] = a[:n1]
        o_ref[1, :, j, :] = a[n1:]


def fft_first(w_hl, z3, *, col0, C, t2_major=False):
    w_hi, w_lo = w_hl
    n1x2, l1 = w_hi.shape
    n1, n2 = n1x2 // 2, z3.shape[0 if t2_major else 1]
    tt2 = FFT_TT2
    tc = _pick(C, (512, 256, 128))
    cb = col0 // tc
    full = pl.BlockSpec((n1x2, l1), lambda t, j: (0, 0))
    if t2_major:
        z_spec = pl.BlockSpec((tt2, l1, tc), lambda t, j: (t, 0, j + cb))
    else:
        z_spec = pl.BlockSpec((l1, tt2, tc), lambda t, j: (0, t, j + cb))
    return pl.pallas_call(
        functools.partial(_fft_first_kernel, t2_major=t2_major),
        grid=(n2 // tt2, C // tc),
        in_specs=[full, full, z_spec],
        out_specs=pl.BlockSpec((2, n1, tt2, tc), lambda t, j: (0, 0, t, j)),
        out_shape=jax.ShapeDtypeStruct((2, n1, n2, C), F32),
        compiler_params=_params("parallel", "parallel"),
    )(w_hi, w_lo, z3)


def _fft_mid_kernel(*refs, g, conv):
    if conv:
        a_ref, kf_ref, twc_ref, tws_ref, w2h_ref, w2l_ref, w2ih_ref, w2il_ref, o_ref = refs
    else:
        a_ref, twc_ref, tws_ref, w2h_ref, w2l_ref, o_ref = refs
    n2 = a_ref.shape[2]
    for kk in range(g):
        ar, ai = a_ref[0, kk], a_ref[1, kk]
        c, s = twc_ref[kk], tws_ref[kk]
        x = _dot3(w2h_ref[...], w2l_ref[...], jnp.concatenate([ar * c + ai * s, ai * c - ar * s], axis=0))
        if conv:
            xr, xi = x[:n2], x[n2:]
            kr, ki = kf_ref[0, kk], kf_ref[1, kk]
            y = _dot3(w2ih_ref[...], w2il_ref[...], jnp.concatenate([xr * kr - xi * ki, xr * ki + xi * kr], axis=0))
            yr, yi = y[:n2], y[n2:]
            o_ref[0, kk] = yr * c - yi * s
            o_ref[1, kk] = yr * s + yi * c
        else:
            o_ref[0, kk] = x[:n2]
            o_ref[1, kk] = x[n2:]


def fft_mid(a, tw_c, tw_s, w2_hl, w2i_hl=None, kf=None, *, kf_col0=0):
    _, n1, n2, C = a.shape
    g = 4
    tc = _pick(C, (512, 256, 128))
    conv = kf is not None
    blk = pl.BlockSpec((2, g, n2, tc), lambda j, k: (0, k, 0, j))
    tw = pl.BlockSpec((g, n2, 1), lambda j, k: (k, 0, 0))
    mat = pl.BlockSpec((2 * n2, 2 * n2), lambda j, k: (0, 0))
    if conv:
        kb = kf_col0 // tc
        in_specs = [blk, pl.BlockSpec((2, g, n2, tc), lambda j, k: (0, k, 0, j + kb)), tw, tw, mat, mat, mat, mat]
        args = [a, kf, tw_c, tw_s, *w2_hl, *w2i_hl]
    else:
        in_specs = [blk, tw, tw, mat, mat]
        args = [a, tw_c, tw_s, *w2_hl]
    return pl.pallas_call(
        functools.partial(_fft_mid_kernel, g=g, conv=conv),
        grid=(C // tc, n1 // g),
        in_specs=in_specs,
        out_specs=blk,
        out_shape=jax.ShapeDtypeStruct(a.shape, F32),
        compiler_params=_params("parallel", "parallel"),
    )(*args)


def _fft_last_kernel(whi_ref, wlo_ref, b_ref, gate_ref, zp_ref, asum_ref, bias_ref, o_ref):
    inv_asum = 1.0 / asum_ref[...]
    for j in range(b_ref.shape[2]):
        bj = jnp.concatenate([b_ref[0, :, j, :], b_ref[1, :, j, :]], axis=0)
        x = _dot3(whi_ref[...], wlo_ref[...], bj)
        o_ref[:, j, :] = gate_ref[:, j, :] * (x * inv_asum + zp_ref[:, j, :] * bias_ref[...])


def fft_last(w_hl, b, gate3, gate_col0, zp3, zp_col0, asum, bias, *, vec_col0):
    w_hi, w_lo = w_hl
    l1 = w_hi.shape[0]
    _, n1, n2, C = b.shape
    tt2 = FFT_TT2
    tc = _pick(C, (512, 256, 128))
    gcb, zcb, vcb = gate_col0 // tc, zp_col0 // tc, vec_col0 // tc
    full = pl.BlockSpec((l1, 2 * n1), lambda t, j: (0, 0))
    return pl.pallas_call(
        _fft_last_kernel,
        grid=(n2 // tt2, C // tc),
        in_specs=[
            full, full,
            pl.BlockSpec((2, n1, tt2, tc), lambda t, j: (0, 0, t, j)),
            pl.BlockSpec((l1, tt2, tc), lambda t, j: (0, t, j + gcb)),
            pl.BlockSpec((l1, tt2, tc), lambda t, j: (0, t, j + zcb)),
            pl.BlockSpec((1, tc), lambda t, j: (0, j + vcb)),
            pl.BlockSpec((1, tc), lambda t, j: (0, j + vcb)),
        ],
        out_specs=pl.BlockSpec((l1, tt2, tc), lambda t, j: (0, t, j)),
        out_shape=jax.ShapeDtypeStruct((l1, n2, C), F32),
        compiler_params=_params("parallel", "parallel"),
    )(w_hi, w_lo, b, gate3, zp3, asum, bias)


def _hpmm_kernel(*refs, gated):
    if gated:
        ah_ref, al_ref, b_ref, gate_ref, z_ref, asum_ref, bias_ref, o_ref = refs
    else:
        ah_ref, al_ref, b_ref, o_ref = refs
    y = _dot3(ah_ref[...], al_ref[...], b_ref[...])
    if gated:
        y = gate_ref[...] * (y / asum_ref[...] + z_ref[...] * bias_ref[...])
    o_ref[...] = y


def hp_matmul(a_hl, b, *, gate=None, z=None, asum=None, bias=None, vec_col0=0):
    a_hi, a_lo = a_hl
    M, Kd = a_hi.shape
    N = b.shape[1]
    tn = _pick(N, (1024, 512, 256, 128))
    full = pl.BlockSpec((M, Kd), lambda j: (0, 0))
    in_specs = [full, full, pl.BlockSpec((Kd, tn), lambda j: (0, j))]
    args = [a_hi, a_lo, b]
    gated = gate is not None
    if gated:
        vb = vec_col0 // tn
        blk = pl.BlockSpec((M, tn), lambda j: (0, j))
        vec = pl.BlockSpec((1, tn), lambda j: (0, j + vb))
        in_specs += [blk, blk, vec, vec]
        args += [gate, z, asum, bias]
    return pl.pallas_call(
        functools.partial(_hpmm_kernel, gated=gated),
        grid=(N // tn,),
        in_specs=in_specs,
        out_specs=pl.BlockSpec((M, tn), lambda j: (0, j)),
        out_shape=jax.ShapeDtypeStruct((M, N), F32),
        compiler_params=_params("parallel"),
    )(*args)


def _cmul_kernel(x_ref, k_ref, o_ref):
    xr, xi, kr, ki = x_ref[0], x_ref[1], k_ref[0], k_ref[1]
    o_ref[0] = xr * kr - xi * ki
    o_ref[1] = xr * ki + xi * kr


def complex_mul(x, kf, *, kf_col0):
    _, R, C = x.shape
    tc = _pick(C, (512, 256, 128))
    kb = kf_col0 // tc
    return pl.pallas_call(
        _cmul_kernel,
        grid=(C // tc,),
        in_specs=[pl.BlockSpec((2, R, tc), lambda j: (0, 0, j)), pl.BlockSpec((2, R, tc), lambda j: (0, 0, j + kb))],
        out_specs=pl.BlockSpec((2, R, tc), lambda j: (0, 0, j)),
        out_shape=jax.ShapeDtypeStruct(x.shape, F32),
        compiler_params=_params("parallel"),
    )(x, kf)


def _dft_constants(n1, n2, l1):
    N = n1 * n2
    n1h = min(n1, -(-(n1 // 2 + 1) // 8) * 8)
    k1 = np.arange(n1h)[:, None]
    herm = np.where((k1 == 0) | (k1 == n1 // 2), 1.0, np.where(k1 < n1 // 2, 2.0, 0.0))
    ang1 = 2 * np.pi * k1 * np.arange(n1)[None, :] / n1
    w1 = np.concatenate([np.cos(ang1), -np.sin(ang1)], 0)
    w1_inv = np.concatenate([(herm * np.cos(ang1[:, :l1])).T, (-herm * np.sin(ang1[:, :l1])).T], 1) / N
    ang_t = 2 * np.pi * k1 * np.arange(n2)[None, :] / N
    k2 = np.arange(n2)[:, None]
    ang2 = 2 * np.pi * k2 * np.arange(n2)[None, :] / n2
    c2, s2 = np.cos(ang2), np.sin(ang2)
    f = lambda x: jnp.asarray(x, F32)
    return dict(w1_data=_hilo(w1[:, :l1]), w1_full=_hilo(w1), w1_inv=_hilo(w1_inv),
                tw_c=f(np.cos(ang_t))[:, :, None], tw_s=f(np.sin(ang_t))[:, :, None],
                w2=_hilo(np.block([[c2, s2], [-s2, c2]])), w2_inv=_hilo(np.block([[c2, -s2], [s2, c2]])))


def hyena_conv_lat(u3, kern, asum, bias, *, L, D):
    n2 = FFT_N2
    n1, l1 = 2 * L // n2, L // n2
    cst = _dft_constants(n1, n2, l1)
    ka = fft_first(cst["w1_full"], kern.reshape(n2, n1, HY_ORDER * D), col0=0, C=HY_ORDER * D, t2_major=True)
    kf = fft_mid(ka, cst["tw_c"], cst["tw_s"], cst["w2"])
    z3, zc0 = u3, 0
    for o in range(HY_ORDER):
        a = fft_first(cst["w1_data"], z3, col0=zc0, C=D)
        bmid = fft_mid(a, cst["tw_c"], cst["tw_s"], cst["w2"], cst["w2_inv"], kf, kf_col0=o * D)
        z3 = fft_last(cst["w1_inv"], bmid, u3, (o + 1) * D, z3, zc0, asum, bias, vec_col0=o * D)
        zc0 = 0
    return z3.reshape(L, D)


def _dense_dft_constants(L):
    N = 2 * L
    ang = 2 * np.pi * np.arange(N)[:, None] * np.arange(N)[None, :] / N
    wf = np.concatenate([np.cos(ang), -np.sin(ang)], 0)
    wi = np.concatenate([np.cos(ang[:L]), -np.sin(ang[:L])], 1) / N
    return _hilo(wf), _hilo(wf[:, :L]), _hilo(wi)


def hyena_conv_ctx(v, x1, x2, kern, asum, bias):
    L, D = v.shape
    wf, wf_data, wi = _dense_dft_constants(L)
    kf = hp_matmul(wf, kern).reshape(2, 2 * L, HY_ORDER * D)
    z = v
    for o, gate in enumerate((x1, x2)):
        xf = hp_matmul(wf_data, z).reshape(2, 2 * L, D)
        y = complex_mul(xf, kf, kf_col0=o * D).reshape(4 * L, D)
        z = hp_matmul(wi, y, gate=gate, z=z, asum=asum, bias=bias, vec_col0=o * D)
    return z


def _na_kernel(q_ref, k_ref, v_ref, *rest, n_ctx, rows, heads_per_step, rows_per_step):
    bias_refs, o_ref = rest[:rows_per_step], rest[rows_per_step]
    kr = NA_ROWS
    n_lat = rows * GRID_W
    dh = NA_HEAD_DIM
    scale = dh ** -0.5
    for rr in range(rows_per_step):
        r = pl.program_id(1) * rows_per_step + rr
        r0 = jnp.clip(r - kr // 2, 0, rows - kr)
        start = pl.multiple_of(r0 * GRID_W, GRID_W)
        qs = slice(rr * GRID_W, (rr + 1) * GRID_W)
        for hh in range(heads_per_step):
            sl = slice(hh * dh, (hh + 1) * dh)
            q = q_ref[qs, sl]
            s_ctx = _dot_nt(q, k_ref[n_lat:n_lat + n_ctx, sl]) * scale
            s_lat = _dot_nt(q, k_ref[pl.ds(start, kr * GRID_W), sl]) * scale + bias_refs[rr][hh]
            m = jnp.maximum(jnp.max(s_ctx, axis=1, keepdims=True), jnp.max(s_lat, axis=1, keepdims=True))
            p_ctx = jnp.exp(s_ctx - m)
            p_lat = jnp.exp(s_lat - m)
            l = jnp.sum(p_ctx, axis=1, keepdims=True) + jnp.sum(p_lat, axis=1, keepdims=True)
            o = (_dot(p_ctx.astype(BF16), v_ref[n_lat:n_lat + n_ctx, sl])
                 + _dot(p_lat.astype(BF16), v_ref[pl.ds(start, kr * GRID_W), sl]))
            o_ref[qs, sl] = (o / l).astype(o_ref.dtype)


def _na_bias_table(rpb, rows):
    H = rpb.shape[0]
    cols = jnp.arange(GRID_W)
    c0 = jnp.clip(cols - NA_COLS // 2, 0, GRID_W - NA_COLS)
    kc = jnp.arange(GRID_W)[None, :]
    inside = (kc >= c0[:, None]) & (kc < c0[:, None] + NA_COLS)
    rel = jnp.clip(kc - cols[:, None] + NA_COLS - 1, 0, 2 * NA_COLS - 2)
    rr = jnp.arange(NA_ROWS)[:, None] + jnp.arange(NA_ROWS)[None, :]
    t = rpb[:, rr]
    t = t[:, :, :, rel]
    t = jnp.where(inside[None, None, None], t, -1e30)
    t = jnp.transpose(t, (1, 0, 3, 2, 4))
    return t.reshape(NA_ROWS, H, GRID_W, NA_ROWS * GRID_W)


def na_attention(qkv, rpb, *, n_ctx):
    NT, D3 = qkv.shape
    D = D3 // 3
    L = NT - n_ctx
    rows = L // GRID_W
    hps = 4
    hw = hps * NA_HEAD_DIM
    nhb = D // hw
    table = _na_bias_table(rpb, rows)

    def pattern(r):
        r0 = jnp.clip(r - NA_ROWS // 2, 0, rows - NA_ROWS)
        return r0 - r + NA_ROWS - 1

    rps = 2
    bias_specs = [pl.BlockSpec((None, hps, GRID_W, NA_ROWS * GRID_W),
                               lambda h, r, rr=rr: (pattern(r * rps + rr), h, 0, 0)) for rr in range(rps)]
    return pl.pallas_call(
        functools.partial(_na_kernel, n_ctx=n_ctx, rows=rows, heads_per_step=hps, rows_per_step=rps),
        grid=(nhb, rows // rps),
        in_specs=[
            pl.BlockSpec((rps * GRID_W, hw), lambda h, r: (r, h)),
            pl.BlockSpec((NT, hw), lambda h, r: (0, nhb + h)),
            pl.BlockSpec((NT, hw), lambda h, r: (0, 2 * nhb + h)),
            *bias_specs,
        ],
        out_specs=pl.BlockSpec((rps * GRID_W, hw), lambda h, r: (r, h)),
        out_shape=jax.ShapeDtypeStruct((L, D), BF16),
        compiler_params=_params("parallel", "parallel"),
    )(qkv, qkv, qkv, *([table] * rps))


def _router_kernel(x_ref, w_ref, b_ref, o_ref):
    E, G = N_EXPERTS, N_GROUPS
    per = E // G
    logits = _dot_nt(w_ref[...].astype(BF16), x_ref[...])
    scores = jax.nn.sigmoid(logits)
    sel = scores + b_ref[...]
    tm = sel.shape[1]
    grp = sel.reshape(G, per, tm)
    eidx = lax.broadcasted_iota(jnp.int32, grp.shape, 1)
    m1 = jnp.max(grp, axis=1, keepdims=True)
    first = jnp.min(jnp.where(grp == m1, eidx, per), axis=1, keepdims=True)
    m2 = jnp.max(jnp.where(eidx == first, -jnp.inf, grp), axis=1, keepdims=True)
    gs = (m1 + m2).reshape(G, tm)
    gidx = lax.broadcasted_iota(jnp.int32, gs.shape, 0)
    gmask = jnp.zeros(gs.shape, jnp.bool_)
    for _ in range(TOPK_GROUPS):
        mx = jnp.max(gs, axis=0, keepdims=True)
        pick = gidx == jnp.min(jnp.where(gs == mx, gidx, G), axis=0, keepdims=True)
        gmask = jnp.logical_or(gmask, pick)
        gs = jnp.where(pick, -jnp.inf, gs)
    emask = jnp.broadcast_to(gmask.reshape(G, 1, tm), (G, per, tm)).reshape(E, tm)
    cand = jnp.where(emask, sel, -jnp.inf)
    xidx = lax.broadcasted_iota(jnp.int32, cand.shape, 0)
    chosen = jnp.zeros(cand.shape, jnp.bool_)
    for _ in range(TOP_K):
        mx = jnp.max(cand, axis=0, keepdims=True)
        pick = xidx == jnp.min(jnp.where(cand == mx, xidx, E), axis=0, keepdims=True)
        chosen = jnp.logical_or(chosen, pick)
        cand = jnp.where(pick, -jnp.inf, cand)
    wsel = jnp.where(chosen, scores, 0.0)
    gates = wsel / jnp.sum(wsel, axis=0, keepdims=True) * ROUTED_SCALE
    o_ref[...] = gates.T


def moe_router(h, router_w_t, router_b, layer):
    T, D = h.shape
    E = router_w_t.shape[1]
    tm = _pick(T, (256, 128))
    return pl.pallas_call(
        _router_kernel,
        grid=(T // tm,),
        in_specs=[
            pl.BlockSpec((tm, D), lambda i: (i, 0)),
            pl.BlockSpec((None, E, D), lambda i: (layer, 0, 0)),
            pl.BlockSpec((None, E, 1), lambda i: (layer, 0, 0)),
        ],
        out_specs=pl.BlockSpec((tm, E), lambda i: (i, 0)),
        out_shape=jax.ShapeDtypeStruct((T, E), F32),
        compiler_params=_params("parallel"),
    )(h, router_w_t, router_b.reshape(-1, E, 1))


def _moe_up_kernel(*refs, nk, ne, gated, F):
    if gated:
        x_ref, wg_ref, wu_ref, g_ref, o_ref, accg_ref, accu_ref = refs
    else:
        x_ref, wg_ref, wu_ref, o_ref, accg_ref, accu_ref = refs
    e0 = pl.program_id(1) * ne
    k = pl.program_id(2)

    @pl.when(k == 0)
    def _():
        accg_ref[...] = jnp.zeros_like(accg_ref)
        accu_ref[...] = jnp.zeros_like(accu_ref)

    def stacked_experts(ref):
        w = ref[0] if ne == 1 else jnp.concatenate([ref[ee] for ee in range(ne)], axis=0)
        return w.astype(BF16)

    x = x_ref[...]
    accg_ref[...] += _dot_nt(x, stacked_experts(wg_ref))
    accu_ref[...] += _dot_nt(x, stacked_experts(wu_ref))

    @pl.when(k == nk - 1)
    def _():
        a = accg_ref[...]
        hid = a * jax.nn.sigmoid(a) * accu_ref[...]
        if gated:
            g = g_ref[...]
            lane = lax.broadcasted_iota(jnp.int32, g.shape, 1)
            col = lax.broadcasted_iota(jnp.int32, (1, ne * F), 1) // F
            gexp = jnp.zeros_like(hid)
            for ee in range(ne):
                ge = jnp.sum(jnp.where(lane == e0 + ee, g, 0.0), axis=1, keepdims=True)
                gexp = jnp.where(col == ee, ge, gexp)
            hid = hid * gexp
        o_ref[...] = hid.astype(o_ref.dtype)


def moe_up(x, w_gate, w_up, layer, gates=None):
    T, D = x.shape
    if w_gate.ndim == 3:
        w_gate, w_up = w_gate[:, None], w_up[:, None]
    w_gate, w_up = jnp.swapaxes(w_gate, 2, 3), jnp.swapaxes(w_up, 2, 3)
    E, F = w_gate.shape[1], w_gate.shape[2]
    ne = 4 if E % 4 == 0 else 1
    tm = _pick(T, (1056, 1024, 512, 256, 128))
    tk = _pick(D, (2048, 1024, 512, 256, 128))
    nk = D // tk
    gated = gates is not None
    w_spec = pl.BlockSpec((None, ne, F, tk), lambda i, e, k: (layer, e, 0, k))
    in_specs = [pl.BlockSpec((tm, tk), lambda i, e, k: (i, k)), w_spec, w_spec]
    args = [x, w_gate, w_up]
    if gated:
        in_specs.append(pl.BlockSpec((tm, E), lambda i, e, k: (i, 0)))
        args.append(gates)
    return pl.pallas_call(
        functools.partial(_moe_up_kernel, nk=nk, ne=ne, gated=gated, F=F),
        grid=(T // tm, E // ne, nk),
        in_specs=in_specs,
        out_specs=pl.BlockSpec((tm, ne * F), lambda i, e, k: (i, e)),
        out_shape=jax.ShapeDtypeStruct((T, E * F), BF16),
        scratch_shapes=[pltpu.VMEM((tm, ne * F), F32), pltpu.VMEM((tm, ne * F), F32)],
        compiler_params=_params("parallel", "parallel", "arbitrary"),
    )(*args)


def moe(h, layer, router_w_t, router_b, w_gate, w_up, w_down, s_gate, s_up, s_down):
    gates = moe_router(h, router_w_t, router_b, layer)
    hid = moe_up(h, w_gate, w_up, layer, gates)
    hid_sh = moe_up(h, s_gate, s_up, layer)
    E, F, D = w_down.shape[1:]
    return matmul(hid, w_down.reshape(DEPTH, E * F, D), out_dtype=F32, layer=layer, extra=(hid_sh, s_down, layer))


def kernel(x, c, ctx, c_ctx, ada_a, ada_b, ada_bias, ln1_g, ln1_b, ln2_g, ln2_b, router_w, router_b, moe_w_gate, moe_w_up, moe_w_down, sh_w_gate, sh_w_up, sh_w_down, ml_w_in, ml_b_if, ml_norm_g, ml_w_out, da_w_qkv, da_lam_q1, da_lam_k1, da_lam_q2, da_lam_k2, da_subln_g, da_w_out, hy_w_in, hy_conv_w, hy_conv_b, hy_f_w1, hy_f_b1, hy_f_w2, hy_f_b2, hy_f_w3, hy_f_b3, hy_f_w4, hy_bias, hy_w_out, na_w_qkv, na_rpb, na_w_out):
    _, L, D = x.shape
    C = ctx.shape[1]
    NT = C + L
    s = jnp.concatenate([x[0], ctx[0]], axis=0)

    cc = jnp.zeros((8, D), F32).at[0].set(c_ctx).at[1].set(c[0])
    mods = ada_modulation(cc, ada_a, ada_b, ada_bias)[:, :2].reshape(DEPTH, 2, 6, 1, D)
    SH1, SC1, G1, SH2, SC2, G2 = range(6)
    router_w_t = jnp.swapaxes(router_w, 1, 2)
    moe_args = (router_w_t, router_b, moe_w_gate, moe_w_up, moe_w_down, sh_w_gate, sh_w_up, sh_w_down)

    (h,) = ln_modulate(s, mods, n_ctx=C, h_layer=0, sc_idx=SC1, sh_idx=SH1)
    for i in range(DEPTH):
        last = i == DEPTH - 1
        if i == 0:
            y = mlstm_mixer(h, ml_w_in, ml_b_if[0], ml_norm_g[0], ml_w_out, n_ctx=C)
        elif i == 1:
            lam4 = jnp.stack([da_lam_q1[0], da_lam_k1[0], da_lam_q2[0], da_lam_k2[0]])
            y = diff_attn_mixer(h, da_w_qkv, lam4, da_subln_g[0], da_w_out, n_ctx=C,
                                lambda_init=0.8 - 0.6 * math.exp(-0.3 * i))
        elif i == 2:
            y = hyena_mixer(h, hy_w_in, hy_conv_w[0], hy_conv_b[0], hy_f_w1[0], hy_f_b1[0], hy_f_w2[0], hy_f_b2[0],
                            hy_f_w3[0], hy_f_b3[0], hy_f_w4[0], hy_bias[0], hy_w_out, n_ctx=C)
        else:
            y = na_mixer(h, na_w_qkv, na_rpb[0], na_w_out, n_ctx=C)
        n_ctx = C
        if last:
            s, n_ctx = s[:L], 0
        s, h = ln_modulate(s, mods, n_ctx=n_ctx, ys=(y,), ln_layer=i, gate_idx=G1, ln_g=ln1_g, ln_b=ln1_b,
                           h_layer=i, sc_idx=SC2, sh_idx=SH2)
        y = moe(h, i, *moe_args)
        if last:
            (s,) = ln_modulate(s, mods, n_ctx=n_ctx, ys=(y,), ln_layer=i, gate_idx=G2, ln_g=ln2_g, ln_b=ln2_b)
        else:
            s, h = ln_modulate(s, mods, n_ctx=n_ctx, ys=(y,), ln_layer=i, gate_idx=G2, ln_g=ln2_g, ln_b=ln2_b,
                               h_layer=i + 1, sc_idx=SC1, sh_idx=SH1)
    return s[None]


def mlstm_mixer(h, w_in, b_if, norm_g, w_out, *, n_ctx):
    D = h.shape[1]
    w_in_t = jnp.swapaxes(w_in, 1, 2)
    u = matmul(h, w_in_t, out_dtype=F32, layer=0, ncols=3 * D, w_transposed=True)
    w_if_t = jnp.pad(w_in_t[0, 3 * D:], ((0, LANES - 4 * ML_HEADS), (0, 0)))
    g_if = matmul(h, w_if_t, out_dtype=F32, w_transposed=True)
    hs = mlstm_scan(u, g_if, b_if, n_ctx=n_ctx)
    return matmul(mlstm_gate(hs, u, norm_g, o_col_block=2), w_out, out_dtype=F32, layer=0)


def diff_attn_mixer(h, w_qkv, lam4, subln_g, w_out, *, n_ctx, lambda_init):
    NT, D = h.shape
    L = NT - n_ctx
    scale = DA_HEAD_DIM ** -0.5 * math.log2(math.e)
    qk = matmul(h, w_qkv, out_dtype=F32, layer=0, ncols=2 * D)
    v = matmul(h, w_qkv, out_dtype=BF16, layer=0, col0=2 * D, ncols=D)
    cos, sin = _rope_tables(L, 1.0)
    ones, zeros = jnp.ones((n_ctx, LANES), F32), jnp.zeros((n_ctx, LANES), F32)
    q_lat = rope_cast(qk, cos * scale, sin * scale, row0=0, nrows=L, col0=0, ncols=D)
    q_ctx = rope_cast(qk, ones * scale, zeros, row0=L, nrows=n_ctx, col0=0, ncols=D)
    k = rope_cast(qk, jnp.concatenate([cos, ones]), jnp.concatenate([sin, zeros]), row0=0, nrows=NT, col0=D, ncols=D)
    o_lat = diff_attention(q_lat, k, v, lam4, subln_g, kv_row0=0, kv_len=NT, lambda_init=lambda_init)
    o_ctx = diff_attention(q_ctx, k, v, lam4, subln_g, kv_row0=L, kv_len=n_ctx, lambda_init=lambda_init)
    return matmul(jnp.concatenate([o_lat, o_ctx]), w_out, out_dtype=F32, layer=0)


def hyena_mixer(h, w_in, conv_w, conv_b, f_w1, f_b1, f_w2, f_b2, f_w3, f_b3, f_w4, hy_bias, w_out, *, n_ctx):
    NT, D = h.shape
    L = NT - n_ctx
    u = short_conv(matmul(h, w_in, out_dtype=F32, layer=0), conv_w, conv_b, n_ctx=n_ctx)
    fargs = (f_w1, f_b1, f_w2, f_b2, f_w3, f_b3, f_w4)
    bias = hy_bias.reshape(1, HY_ORDER * D)
    kern_l, asum_l = hyena_filters(L, D, *fargs, t2_major_n2=FFT_N2)
    z_lat = hyena_conv_lat(u.reshape(NT // FFT_N2, FFT_N2, 3 * D), kern_l, asum_l, bias, L=L, D=D)
    kern_c, asum_c = hyena_filters(n_ctx, D, *fargs)
    v, x1, x2 = (u[L:, j * D:(j + 1) * D] for j in range(3))
    z_ctx = hyena_conv_ctx(v, x1, x2, kern_c, asum_c, bias)
    return matmul(jnp.concatenate([z_lat, z_ctx]), w_out, out_dtype=F32, layer=0)


def na_mixer(h, w_qkv, rpb, w_out, *, n_ctx):
    qkv = matmul(h, w_qkv, out_dtype=BF16, layer=0)
    return matmul(na_attention(qkv, rpb, n_ctx=n_ctx), w_out, out_dtype=F32, layer=0)
```

```python
import functools
import math

import numpy as np
import jax
import jax.numpy as jnp
from jax import lax
from jax.experimental import pallas as pl
from jax.experimental.pallas import tpu as pltpu

F32 = jnp.float32
BF16 = jnp.bfloat16

V7X_VMEM_LIMIT_BYTES = 56 * 1024 * 1024
LANES = 128

GRID_W = 64
DEPTH = 4
DEEPNORM_ALPHA = (2 * DEPTH) ** 0.25
LN_EPS = 1e-5
RMS_EPS = 1e-6
ROPE_BASE = 10000.0
ML_HEADS = 8
ML_CHUNK = 256
DA_HEAD_DIM = 128
HY_ORDER = 2
HY_EMB = 33
HY_BANDS = (HY_EMB - 1) // 2
HY_FAST_DECAY = 0.3
HY_SLOW_DECAY = 1.5
HY_DECAY_TARGET = 1e-2
NA_HEAD_DIM = 128
NA_ROWS = 8
NA_COLS = 16
N_EXPERTS = 64
TOP_K = 8
N_GROUPS = 8
TOPK_GROUPS = 4
ROUTED_SCALE = 2.5


def _params(*sem):
    return pltpu.CompilerParams(dimension_semantics=sem, vmem_limit_bytes=V7X_VMEM_LIMIT_BYTES)


def _dot(a, b):
    return jnp.dot(a, b, preferred_element_type=F32)


def _dot_nt(a, b):
    return lax.dot_general(a, b, (((1,), (1,)), ((), ())), preferred_element_type=F32)


def _dot_tn(a, b):
    return lax.dot_general(a, b, (((0,), (0,)), ((), ())), preferred_element_type=F32)


def _pick(n, prefs):
    for p in prefs:
        if n % p == 0:
            return p
    return n


def _mm_kernel(*refs, nk, has_extra):
    if has_extra:
        a_ref, w_ref, a2_ref, w2_ref, o_ref, acc_ref = refs
    else:
        a_ref, w_ref, o_ref, acc_ref = refs
    k = pl.program_id(2)

    @pl.when(k == 0)
    def _():
        acc_ref[...] = jnp.zeros_like(acc_ref)

    acc_ref[...] += _dot(a_ref[...].astype(BF16), w_ref[...].astype(BF16))

    @pl.when(k == nk - 1)
    def _():
        acc = acc_ref[...]
        if has_extra:
            acc = acc + _dot(a2_ref[...].astype(BF16), w2_ref[...].astype(BF16))
        o_ref[...] = acc.astype(o_ref.dtype)


def _mm_resident_kernel(a_ref, w_ref, o_ref, *, w_transposed):
    dot = _dot_nt if w_transposed else _dot
    o_ref[...] = dot(a_ref[...].astype(BF16), w_ref[...].astype(BF16)).astype(o_ref.dtype)


def _matmul_resident(a, w, *, out_dtype, layer, col0, N, w_transposed):
    M, K = a.shape
    tm = _pick(M, (2112, 2048, 1024, 512, 256, 128) if a.dtype == BF16 else (1056, 1024, 512, 256, 128))
    tn = _pick(N, (256, 128))
    cb = col0 // tn
    lead = (None,) if w.ndim == 3 else ()
    lidx = (layer,) if w.ndim == 3 else ()
    if w_transposed:
        w_spec = pl.BlockSpec(lead + (tn, K), lambda i, j: lidx + (j + cb, 0))
    else:
        w_spec = pl.BlockSpec(lead + (K, tn), lambda i, j: lidx + (0, j + cb))
    return pl.pallas_call(
        functools.partial(_mm_resident_kernel, w_transposed=w_transposed),
        grid=(M // tm, N // tn),
        in_specs=[pl.BlockSpec((tm, K), lambda i, j: (i, 0), pipeline_mode=pl.Buffered(1)), w_spec],
        out_specs=pl.BlockSpec((tm, tn), lambda i, j: (i, j)),
        out_shape=jax.ShapeDtypeStruct((M, N), out_dtype),
        compiler_params=_params("parallel", "arbitrary"),
    )(a, w)


def matmul(a, w, *, out_dtype, layer=None, col0=0, ncols=None, extra=None, row0=0, nrows=None, w_transposed=False):
    M = a.shape[0] if nrows is None else nrows
    K = a.shape[1]
    N = (w.shape[-2] if w_transposed else w.shape[-1]) if ncols is None else ncols
    if K <= 4096 and extra is None and nrows is None and M >= 1024:
        return _matmul_resident(a, w, out_dtype=out_dtype, layer=layer, col0=col0, N=N, w_transposed=w_transposed)
    assert not w_transposed
    tm = _pick(M, (1056, 1024, 512, 256, 128, 64, 32, 16, 8))
    tn = _pick(N, (1024, 512, 256, 128))
    tk = _pick(K, (2048, 1024, 512, 256, 128))
    assert col0 % tn == 0 and row0 % tm == 0
    cb, rb = col0 // tn, row0 // tm
    nk = K // tk
    if w.ndim == 3:
        w_spec = pl.BlockSpec((None, tk, tn), lambda i, j, k: (layer, k, j + cb))
    else:
        w_spec = pl.BlockSpec((tk, tn), lambda i, j, k: (k, j + cb))
    in_specs = [pl.BlockSpec((tm, tk), lambda i, j, k: (i + rb, k)), w_spec]
    args = [a, w]
    if extra is not None:
        a2, w2, layer2 = extra
        k2 = a2.shape[1]
        in_specs.append(pl.BlockSpec((tm, k2), lambda i, j, k: (i + rb, 0)))
        in_specs.append(pl.BlockSpec((None, k2, tn), lambda i, j, k: (layer2, 0, j)))
        args += [a2, w2]
    return pl.pallas_call(
        functools.partial(_mm_kernel, nk=nk, has_extra=extra is not None),
        grid=(M // tm, N // tn, nk),
        in_specs=in_specs,
        out_specs=pl.BlockSpec((tm, tn), lambda i, j, k: (i, j)),
        out_shape=jax.ShapeDtypeStruct((M, N), out_dtype),
        scratch_shapes=[pltpu.VMEM((tm, tn), F32)],
        compiler_params=_params("parallel", "parallel", "arbitrary"),
    )(*args)


def _ada_kernel(cc_ref, a_ref, b_ref, bias_ref, o_ref):
    cc = cc_ref[...]
    act = cc * jax.nn.sigmoid(cc)
    t = _dot(act.astype(BF16), a_ref[...].astype(BF16))
    o_ref[...] = _dot(t.astype(BF16), b_ref[...].astype(BF16)) + bias_ref[...]


def ada_modulation(cc, ada_a, ada_b, ada_bias):
    depth, D, R = ada_a.shape
    n6 = ada_b.shape[-1]
    tn = D
    return pl.pallas_call(
        _ada_kernel,
        grid=(depth, n6 // tn),
        in_specs=[
            pl.BlockSpec((8, D), lambda l, n: (0, 0)),
            pl.BlockSpec((None, D, R), lambda l, n: (l, 0, 0)),
            pl.BlockSpec((None, R, tn), lambda l, n: (l, 0, n)),
            pl.BlockSpec((None, 1, tn), lambda l, n: (l, 0, n)),
        ],
        out_specs=pl.BlockSpec((None, 8, tn), lambda l, n: (l, 0, n)),
        out_shape=jax.ShapeDtypeStruct((depth, 8, n6), F32),
        compiler_params=_params("parallel", "parallel"),
    )(cc, ada_a, ada_b, ada_bias.reshape(depth, 1, n6))


def _lnmod_kernel(*refs, has_ln, has_h, n_y):
    it = iter(refs)
    s_ref = next(it)
    if has_ln:
        y_refs = [next(it) for _ in range(n_y)]
        gate_ref, lng_ref, lnb_ref = next(it), next(it), next(it)
    if has_h:
        sc_ref, sh_ref = next(it), next(it)
    if has_ln:
        so_ref = next(it)
    if has_h:
        h_ref = next(it)
    x = s_ref[...]
    if has_ln:
        y = y_refs[0][...].astype(F32)
        for r in y_refs[1:]:
            y = y + r[...].astype(F32)
        v = DEEPNORM_ALPHA * x + gate_ref[...] * y
        mu = jnp.mean(v, axis=-1, keepdims=True)
        var = jnp.mean(jnp.square(v - mu), axis=-1, keepdims=True)
        x = (v - mu) * lax.rsqrt(var + LN_EPS) * lng_ref[...] + lnb_ref[...]
        so_ref[...] = x
    if has_h:
        h_ref[...] = (x * (1.0 + sc_ref[...]) + sh_ref[...]).astype(h_ref.dtype)


def ln_modulate(s, mods, *, n_ctx, ys=(), ln_layer=None, gate_idx=None, ln_g=None, ln_b=None,
                h_layer=None, sc_idx=None, sh_idx=None):
    R, D = s.shape
    tr = 128
    has_ln, has_h = len(ys) > 0, h_layer is not None
    n_lat_blocks = (R - n_ctx) // tr
    row_spec = pl.BlockSpec((tr, D), lambda i: (i, 0))

    def mod_spec(layer, idx):
        return pl.BlockSpec((None, None, None, 1, D),
                            lambda i: (layer, jnp.where(i >= n_lat_blocks, 0, 1), idx, 0, 0))

    vec_spec = pl.BlockSpec((None, 1, D), lambda i: (ln_layer, 0, 0))
    in_specs, args = [row_spec], [s]
    if has_ln:
        in_specs += [row_spec] * len(ys) + [mod_spec(ln_layer, gate_idx), vec_spec, vec_spec]
        args += list(ys) + [mods, ln_g.reshape(DEPTH, 1, D), ln_b.reshape(DEPTH, 1, D)]
    if has_h:
        in_specs += [mod_spec(h_layer, sc_idx), mod_spec(h_layer, sh_idx)]
        args += [mods, mods]
    out_specs, out_shape = [], []
    if has_ln:
        out_specs.append(row_spec)
        out_shape.append(jax.ShapeDtypeStruct((R, D), F32))
    if has_h:
        out_specs.append(row_spec)
        out_shape.append(jax.ShapeDtypeStruct((R, D), BF16))
    outs = pl.pallas_call(
        functools.partial(_lnmod_kernel, has_ln=has_ln, has_h=has_h, n_y=len(ys)),
        grid=(R // tr,),
        in_specs=in_specs,
        out_specs=out_specs,
        out_shape=out_shape,
        compiler_params=_params("parallel"),
    )(*args)
    return outs


def _log_sigmoid(x):
    return jnp.minimum(x, 0.0) - jnp.log(1.0 + jnp.exp(-jnp.abs(x)))


def _mlstm_kernel(bias_ref, q_ref, k_ref, v_ref, ic_ref, fc_ref, ir_ref, fr_ref, o_ref, ct_ref, n_ref, m_ref,
                  *, T, dqk):
    d, h, t = pl.program_id(0), pl.program_id(1), pl.program_id(2)

    @pl.when(t == 0)
    def _():
        ct_ref[...] = jnp.zeros_like(ct_ref)
        n_ref[...] = jnp.zeros_like(n_ref)
        m_ref[...] = jnp.zeros_like(m_ref)

    bi = bias_ref[d * 2 * ML_HEADS + h]
    bf = bias_ref[(d * 2 + 1) * ML_HEADS + h]
    i_c = ic_ref[...] + bi
    i_r = ir_ref[...] + bi
    f_c = _log_sigmoid(fc_ref[...] + bf)
    f_r = _log_sigmoid(fr_ref[...] + bf)
    row = lax.broadcasted_iota(jnp.int32, (T, T), 0)
    col = lax.broadcasted_iota(jnp.int32, (T, T), 1)
    sgn = 1 - 2 * d
    incl = (col - row) * sgn <= 0
    incl_t = (row - col) * sgn <= 0
    b_c = jnp.sum(jnp.where(incl, f_r, 0.0), axis=1, keepdims=True)
    b_r = jnp.sum(jnp.where(incl_t, f_c, 0.0), axis=0, keepdims=True)
    f_tot = jnp.sum(f_r, axis=1, keepdims=True)
    m_prev = m_ref[...]
    dmat = jnp.where(incl, b_c - b_r + i_r, -jnp.inf)
    inter = b_c + m_prev
    m_t = jnp.maximum(inter, jnp.max(dmat, axis=1, keepdims=True))
    w_intra = jnp.exp(dmat - m_t)
    w_inter = jnp.exp(inter - m_t)
    q = q_ref[...] * (dqk ** -0.5)
    k = k_ref[...]
    v = v_ref[...]
    qb, kb, vb = q.astype(BF16), k.astype(BF16), v.astype(BF16)
    s = _dot_nt(qb, kb) * w_intra
    ct = ct_ref[...]
    n = n_ref[...]
    num = w_inter * _dot(qb, ct.astype(BF16)) + _dot(s.astype(BF16), vb)
    den = w_inter * jnp.sum(q * n, axis=1, keepdims=True) + jnp.sum(s, axis=1, keepdims=True)
    o_ref[...] = num / jnp.maximum(jnp.abs(den), jnp.exp(-m_t))
    g_r = f_tot - b_r + i_r
    g_c = f_tot - b_c + i_c
    m_new = jnp.maximum(f_tot + m_prev, jnp.max(g_r, axis=1, keepdims=True))
    decay = jnp.exp(f_tot + m_prev - m_new)
    wk = jnp.exp(g_c - m_new)
    ct_ref[...] = decay * ct + _dot_tn(kb, (wk * v).astype(BF16))
    n_ref[...] = decay * n + jnp.sum(wk * k, axis=0, keepdims=True)
    m_ref[...] = m_new


def mlstm_scan(u, gates, b_if, *, n_ctx):
    NT = u.shape[0]
    H, T = ML_HEADS, ML_CHUNK
    D = u.shape[1] // 3
    dqk, dv = D // (2 * H), D // H
    nc, ncc = NT // T, n_ctx // T
    ncl = nc - ncc
    g4 = gates[:, :4 * H].T
    g_col = g4.reshape(4 * H, NT, 1)
    g_row = g4.reshape(4 * H, nc, 1, T)

    def chunk(d, t):
        fwd = jnp.where(t < ncc, ncl + t, t - ncc)
        return jnp.where(d == 0, fwd, nc - 1 - t)

    kq, kk, kv = 0, (H * dqk) // dqk, (2 * H * dqk) // dv
    col_spec = lambda off: pl.BlockSpec((None, T, 1), lambda d, h, t: ((2 * d + off) * H + h, chunk(d, t), 0))
    row_spec = lambda off: pl.BlockSpec((None, None, 1, T), lambda d, h, t: ((2 * d + off) * H + h, chunk(d, t), 0, 0))
    return pl.pallas_call(
        functools.partial(_mlstm_kernel, T=T, dqk=dqk),
        grid=(2, H, nc),
        in_specs=[
            pl.BlockSpec(memory_space=pltpu.SMEM),
            pl.BlockSpec((T, dqk), lambda d, h, t: (chunk(d, t), kq + h)),
            pl.BlockSpec((T, dqk), lambda d, h, t: (chunk(d, t), kk + h)),
            pl.BlockSpec((T, dv), lambda d, h, t: (chunk(d, t), kv + h)),
            col_spec(0), col_spec(1), row_spec(0), row_spec(1),
        ],
        out_specs=pl.BlockSpec((None, T, dv), lambda d, h, t: (d, chunk(d, t), h)),
        out_shape=jax.ShapeDtypeStruct((2, NT, H * dv), F32),
        scratch_shapes=[pltpu.VMEM((dqk, dv), F32), pltpu.VMEM((1, dqk), F32), pltpu.VMEM((1, 1), F32)],
        compiler_params=_params("parallel", "parallel", "arbitrary"),
    )(b_if, u, u, u, g_col, g_col, g_row, g_row)


def _mlgate_kernel(hs_ref, o_ref, g_ref, out_ref, *, dv):
    D = out_ref.shape[1]
    for h in range(D // dv):
        sl = slice(h * dv, (h + 1) * dv)
        x = hs_ref[0, :, sl] + hs_ref[1, :, sl]
        r = lax.rsqrt(jnp.mean(jnp.square(x), axis=-1, keepdims=True) + RMS_EPS)
        out_ref[:, sl] = (x * r * g_ref[:, sl] * jax.nn.sigmoid(o_ref[:, sl])).astype(out_ref.dtype)


def mlstm_gate(hs, u, norm_g, *, o_col_block):
    _, NT, D = hs.shape
    tr = 128
    return pl.pallas_call(
        functools.partial(_mlgate_kernel, dv=D // ML_HEADS),
        grid=(NT // tr,),
        in_specs=[
            pl.BlockSpec((2, tr, D), lambda i: (0, i, 0)),
            pl.BlockSpec((tr, D), lambda i: (i, o_col_block)),
            pl.BlockSpec((1, D), lambda i: (0, 0)),
        ],
        out_specs=pl.BlockSpec((tr, D), lambda i: (i, 0)),
        out_shape=jax.ShapeDtypeStruct((NT, D), BF16),
        compiler_params=_params("parallel"),
    )(hs, u, norm_g.reshape(1, D))


def _rope_kernel(x_ref, cos_ref, sin_ref, o_ref):
    cos, sin = cos_ref[...], sin_ref[...]
    lane = lax.broadcasted_iota(jnp.int32, cos.shape, 1)
    first_half = (lane % 64) < 32
    for g in range(x_ref.shape[1] // LANES):
        sl = slice(g * LANES, (g + 1) * LANES)
        x = x_ref[:, sl]
        rot = jnp.where(first_half, -pltpu.roll(x, LANES - 32, 1), pltpu.roll(x, 32, 1))
        o_ref[:, sl] = (x * cos + rot * sin).astype(o_ref.dtype)


def rope_cast(x, cos, sin, *, row0, nrows, col0, ncols):
    tr = _pick(nrows, (256, 128))
    tc = _pick(ncols, (1024, 512, 256, 128))
    rb, cb = row0 // tr, col0 // tc
    assert row0 % tr == 0 and col0 % tc == 0
    return pl.pallas_call(
        _rope_kernel,
        grid=(nrows // tr, ncols // tc),
        in_specs=[
            pl.BlockSpec((tr, tc), lambda i, j: (i + rb, j + cb)),
            pl.BlockSpec((tr, LANES), lambda i, j: (i, 0)),
            pl.BlockSpec((tr, LANES), lambda i, j: (i, 0)),
        ],
        out_specs=pl.BlockSpec((tr, tc), lambda i, j: (i, j)),
        out_shape=jax.ShapeDtypeStruct((nrows, ncols), BF16),
        compiler_params=_params("parallel", "parallel"),
    )(x, cos, sin)


SAFE_LOG2 = 64.0


def _dattn_kernel(lam_ref, q_ref, k_ref, v_ref, g_ref, o_ref, m_ref, l_ref, acc_ref, kn_ref,
                  *, tk, n_chunks, out_scale, lambda_init):
    dh = DA_HEAD_DIM
    nlt = tk // LANES
    l_ref[...] = jnp.zeros_like(l_ref)
    acc_ref[...] = jnp.zeros_like(acc_ref)

    @pl.when(pl.program_id(1) == 0)
    def _():
        for i in range(2):
            kk = k_ref[:, i * dh:(i + 1) * dh].astype(F32)
            k2 = jnp.max(jnp.sum(kk * kk, axis=1, keepdims=True), axis=0, keepdims=True)
            kn_ref[i] = jnp.broadcast_to(k2, kn_ref.shape[1:])

    bound2 = None
    for i in range(2):
        qq = q_ref[:, i * dh:(i + 1) * dh].astype(F32)
        b2 = jnp.max(jnp.sum(qq * qq, axis=1, keepdims=True), axis=0, keepdims=True) * kn_ref[i][0:1, 0:1]
        bound2 = b2 if bound2 is None else jnp.maximum(bound2, b2)
    scores_bounded = bound2[0, 0] <= SAFE_LOG2 * SAFE_LOG2

    def scores(c, i):
        kc = k_ref[pl.ds(pl.multiple_of(c * tk, tk), tk), i * dh:(i + 1) * dh]
        return _dot_nt(q_ref[:, i * dh:(i + 1) * dh], kc)

    @pl.when(scores_bounded)
    def _():
        m_ref[...] = jnp.zeros_like(m_ref)

    @pl.when(jnp.logical_not(scores_bounded))
    def _():
        m_ref[...] = jnp.full_like(m_ref, -jnp.inf)

        def max_pass(c, carry):
            for i in range(2):
                s = scores(c, i)
                m = m_ref[i]
                for t in range(nlt):
                    m = jnp.maximum(m, s[:, t * LANES:(t + 1) * LANES])
                m_ref[i] = m
            return carry

        lax.fori_loop(0, n_chunks, max_pass, 0)
        for i in range(2):
            m_ref[i] = jnp.broadcast_to(jnp.max(m_ref[i], axis=1, keepdims=True), m_ref.shape[1:])

    def sum_pass(c, carry):
        vc = v_ref[pl.ds(pl.multiple_of(c * tk, tk), tk), :]
        for i in range(2):
            s = scores(c, i)
            m = m_ref[i]
            p = [jnp.exp2(s[:, t * LANES:(t + 1) * LANES] - m) for t in range(nlt)]
            l_ref[i] += functools.reduce(jnp.add, p)
            acc_ref[i] += _dot(jnp.concatenate(p, axis=1).astype(BF16), vc)
        return carry

    lax.fori_loop(0, n_chunks, sum_pass, 0)
    lam = (jnp.exp(jnp.sum(lam_ref[0:1, :] * lam_ref[1:2, :], axis=1, keepdims=True))
           - jnp.exp(jnp.sum(lam_ref[2:3, :] * lam_ref[3:4, :], axis=1, keepdims=True)) + lambda_init)
    l = [jnp.sum(l_ref[i], axis=1, keepdims=True) for i in range(2)]
    o = acc_ref[0] / l[0] - lam * (acc_ref[1] / l[1])
    r = lax.rsqrt(jnp.mean(jnp.square(o), axis=-1, keepdims=True) + RMS_EPS)
    o_ref[...] = (o * r * g_ref[...] * out_scale).astype(o_ref.dtype)


def diff_attention(q, k, v, lam4, subln_g, *, kv_row0, kv_len, lambda_init):
    Nq, D = q.shape
    hd = 2 * DA_HEAD_DIM
    H = D // hd
    tq = _pick(Nq, (512, 256))
    tk = _pick(kv_len, (768, 512, 256))
    kvb = kv_row0 // kv_len
    assert kv_row0 % kv_len == 0
    return pl.pallas_call(
        functools.partial(_dattn_kernel, tk=tk, n_chunks=kv_len // tk, out_scale=1.0 - lambda_init,
                          lambda_init=lambda_init),
        grid=(H, Nq // tq),
        in_specs=[
            pl.BlockSpec((4, DA_HEAD_DIM), lambda h, i: (0, 0)),
            pl.BlockSpec((tq, hd), lambda h, i: (i, h)),
            pl.BlockSpec((kv_len, hd), lambda h, i: (kvb, h)),
            pl.BlockSpec((kv_len, hd), lambda h, i: (kvb, h)),
            pl.BlockSpec((1, hd), lambda h, i: (0, 0)),
        ],
        out_specs=pl.BlockSpec((tq, hd), lambda h, i: (i, h)),
        out_shape=jax.ShapeDtypeStruct((Nq, D), BF16),
        scratch_shapes=[pltpu.VMEM((2, tq, LANES), F32), pltpu.VMEM((2, tq, LANES), F32),
                        pltpu.VMEM((2, tq, hd), F32), pltpu.VMEM((2, 8, LANES), F32)],
        compiler_params=_params("parallel", "arbitrary"),
    )(lam4, q, k, v, subln_g.reshape(1, hd))


def _rope_tables(L, scale):
    a = DA_HEAD_DIM // 2
    t = jnp.arange(L)
    pos = jnp.stack([t // GRID_W, t % GRID_W], -1).astype(F32)
    inv = ROPE_BASE ** (-jnp.arange(0, a, 2, dtype=F32) / a)
    ang = pos[:, :, None] * inv
    ang = jnp.concatenate([ang, ang], -1).reshape(L, DA_HEAD_DIM)
    return jnp.cos(ang) * scale, jnp.sin(ang) * scale


def _shortconv_kernel(p_ref, prev_ref, next_ref, w_ref, b_ref, o_ref, *, n_seq_first, n_seq_last):
    i = pl.program_id(0)
    x = p_ref[...]
    tr = x.shape[0]
    rid = lax.broadcasted_iota(jnp.int32, x.shape, 0)
    is_first = functools.reduce(jnp.logical_or, [i == b for b in n_seq_first])
    is_last = functools.reduce(jnp.logical_or, [i == b for b in n_seq_last])
    prev_row = jnp.where(is_first, 0.0, prev_ref[7:8, :])
    next_row = jnp.where(is_last, 0.0, next_ref[0:1, :])
    xm = jnp.where(rid == 0, prev_row, pltpu.roll(x, 1, 0))
    xp = jnp.where(rid == tr - 1, next_row, pltpu.roll(x, tr - 1, 0))
    o_ref[...] = xm * w_ref[0:1, :] + x * w_ref[1:2, :] + xp * w_ref[2:3, :] + b_ref[...]


def short_conv(p, w, b, *, n_ctx):
    NT, N = p.shape
    tr = 256
    tc = _pick(N, (1024, 512, 256, 128))
    nb, n8 = NT // tr, NT // 8
    n_lat_blocks = (NT - n_ctx) // tr
    firsts = (0, n_lat_blocks)
    lasts = (n_lat_blocks - 1, nb - 1)
    return pl.pallas_call(
        functools.partial(_shortconv_kernel, n_seq_first=firsts, n_seq_last=lasts),
        grid=(nb, N // tc),
        in_specs=[
            pl.BlockSpec((tr, tc), lambda i, j: (i, j)),
            pl.BlockSpec((8, tc), lambda i, j: (jnp.maximum(i * (tr // 8) - 1, 0), j)),
            pl.BlockSpec((8, tc), lambda i, j: (jnp.minimum((i + 1) * (tr // 8), n8 - 1), j)),
            pl.BlockSpec((3, tc), lambda i, j: (0, j)),
            pl.BlockSpec((1, tc), lambda i, j: (0, j)),
        ],
        out_specs=pl.BlockSpec((tr, tc), lambda i, j: (i, j)),
        out_shape=jax.ShapeDtypeStruct((NT, N), F32),
        compiler_params=_params("parallel", "parallel"),
    )(p, p, p, w, b.reshape(1, N))


def _filter_kernel(zf_ref, t_ref, w1_ref, b1_ref, w2_ref, b2_ref, w3_ref, b3_ref, w4f_ref, w4b_ref, dl_ref,
                   o_ref, asum_ref, hdn_ref):
    i, j = pl.program_id(0), pl.program_id(1)

    @pl.when(j == 0)
    def _():
        hdn = jnp.sin(_dot(zf_ref[...].astype(BF16), w1_ref[...].astype(BF16)) + b1_ref[...])
        hdn = jnp.sin(_dot(hdn.astype(BF16), w2_ref[...].astype(BF16)) + b2_ref[...])
        hdn_ref[...] = jnp.sin(_dot(hdn.astype(BF16), w3_ref[...].astype(BF16)) + b3_ref[...]).astype(hdn_ref.dtype)

    hdn = hdn_ref[...]
    hf = _dot(hdn, w4f_ref[...].astype(BF16))
    hb = _dot(hdn, w4b_ref[...].astype(BF16))
    side = t_ref[:, 1:2]
    kern = jnp.where(side > 0.0, hf, jnp.where(side < 0.0, hb, 0.0)) * jnp.exp(-t_ref[:, 0:1] * dl_ref[...])
    o_ref[...] = kern

    @pl.when(i == 0)
    def _():
        asum_ref[j] = jnp.zeros(asum_ref.shape[1:], F32)

    asum_ref[j] += jnp.sum(jnp.abs(kern), axis=0, keepdims=True)


def hyena_filters(L, D, f_w1, f_b1, f_w2, f_b2, f_w3, f_b3, f_w4, *, t2_major_n2=None):
    t = jnp.linspace(0.0, 1.0, L, dtype=F32)[:, None]
    w = 2.0 * math.pi * jnp.arange(L, dtype=F32)[:, None] / L
    f = jnp.linspace(1e-4, HY_BANDS - 1, HY_BANDS, dtype=F32)[None]
    z = jnp.concatenate([t, jnp.cos(w * f), -jnp.sin(w * f)], -1)
    src = jnp.concatenate([jnp.arange(L), jnp.zeros((1,), jnp.int32), L - 1 - jnp.arange(L - 1)])
    r = jnp.arange(2 * L)
    side = jnp.where(r < L, 1.0, jnp.where(r > L, -1.0, 0.0)).astype(F32)[:, None]
    if t2_major_n2 is not None:
        perm = r.reshape(2 * L // t2_major_n2, t2_major_n2).T.reshape(-1)
        src, side = src[perm], side[perm]
    zf = jnp.pad(z[src], ((0, 0), (0, LANES - HY_EMB)))
    tcol = jnp.concatenate([t[src], side], axis=1)
    deltas = jnp.abs(jnp.linspace(math.log(HY_DECAY_TARGET) / HY_SLOW_DECAY, math.log(HY_DECAY_TARGET) / HY_FAST_DECAY,
                                  D, dtype=F32)).reshape(1, D)
    w1 = jnp.pad(f_w1, ((0, LANES - HY_EMB), (0, 0)))
    nf = f_w1.shape[1]
    R2 = 2 * L
    tr = _pick(R2, (512,))
    tc = _pick(D, (1024, 512, 256, 128))
    cpd = D // tc
    ncol = HY_ORDER * cpd
    full = lambda shape: pl.BlockSpec(shape, lambda i, j: (0, 0))
    kern, asum = pl.pallas_call(
        _filter_kernel,
        grid=(R2 // tr, ncol),
        in_specs=[
            pl.BlockSpec((tr, LANES), lambda i, j: (i, 0)),
            pl.BlockSpec((tr, 2), lambda i, j: (i, 0)),
            full((LANES, nf)), full((1, nf)), full((nf, nf)), full((1, nf)), full((nf, nf)), full((1, nf)),
            pl.BlockSpec((nf, tc), lambda i, j: (0, (j // cpd) * 2 * cpd + j % cpd)),
            pl.BlockSpec((nf, tc), lambda i, j: (0, ((j // cpd) * 2 + 1) * cpd + j % cpd)),
            pl.BlockSpec((1, tc), lambda i, j: (0, j % cpd)),
        ],
        out_specs=[pl.BlockSpec((tr, tc), lambda i, j: (i, j)), pl.BlockSpec((ncol, 1, tc), lambda i, j: (0, 0, 0))],
        out_shape=[jax.ShapeDtypeStruct((R2, HY_ORDER * D), F32), jax.ShapeDtypeStruct((ncol, 1, tc), F32)],
        scratch_shapes=[pltpu.VMEM((tr, nf), BF16)],
        compiler_params=_params("arbitrary", "arbitrary"),
    )(zf, tcol, w1, f_b1.reshape(1, nf), f_w2, f_b2.reshape(1, nf), f_w3, f_b3.reshape(1, nf), f_w4, f_w4, deltas)
    return kern, asum.reshape(1, HY_ORDER * D)


FFT_N2 = 128
FFT_TT2 = 8


def _split(x):
    hi = x.astype(BF16)
    return hi, (x - hi.astype(F32)).astype(BF16)


def _dot3(w_hi, w_lo, x):
    x_hi, x_lo = _split(x)
    return _dot(w_hi, x_hi) + (_dot(w_lo, x_hi) + _dot(w_hi, x_lo))


def _hilo(a):
    return _split(jnp.asarray(a, F32))


def _fft_first_kernel(whi_ref, wlo_ref, z_ref, o_ref, *, t2_major):
    n1 = o_ref.shape[1]
    for j in range(o_ref.shape[2]):
        a = _dot3(whi_ref[...], wlo_ref[...], z_ref[j] if t2_major else z_ref[:, j, :])
        o_ref[0, :, j, :] = a[:n1]
        o_ref[1, :, j, :] = a[n1:]


def fft_first(w_hl, z3, *, col0, C, t2_major=False):
    w_hi, w_lo = w_hl
    n1x2, l1 = w_hi.shape
    n1, n2 = n1x2 // 2, z3.shape[0 if t2_major else 1]
    tt2 = FFT_TT2
    tc = _pick(C, (512, 256, 128))
    cb = col0 // tc
    full = pl.BlockSpec((n1x2, l1), lambda t, j: (0, 0))
    if t2_major:
        z_spec = pl.BlockSpec((tt2, l1, tc), lambda t, j: (t, 0, j + cb))
    else:
        z_spec = pl.BlockSpec((l1, tt2, tc), lambda t, j: (0, t, j + cb))
    return pl.pallas_call(
        functools.partial(_fft_first_kernel, t2_major=t2_major),
        grid=(n2 // tt2, C // tc),
        in_specs=[full, full, z_spec],
        out_specs=pl.BlockSpec((2, n1, tt2, tc), lambda t, j: (0, 0, t, j)),
        out_shape=jax.ShapeDtypeStruct((2, n1, n2, C), F32),
        compiler_params=_params("parallel", "parallel"),
    )(w_hi, w_lo, z3)


def _fft_mid_kernel(*refs, g, conv):
    if conv:
        a_ref, kf_ref, twc_ref, tws_ref, w2h_ref, w2l_ref, w2ih_ref, w2il_ref, o_ref = refs
    else:
        a_ref, twc_ref, tws_ref, w2h_ref, w2l_ref, o_ref = refs
    n2 = a_ref.shape[2]
    for kk in range(g):
        ar, ai = a_ref[0, kk], a_ref[1, kk]
        c, s = twc_ref[kk], tws_ref[kk]
        x = _dot3(w2h_ref[...], w2l_ref[...], jnp.concatenate([ar * c + ai * s, ai * c - ar * s], axis=0))
        if conv:
            xr, xi = x[:n2], x[n2:]
            kr, ki = kf_ref[0, kk], kf_ref[1, kk]
            y = _dot3(w2ih_ref[...], w2il_ref[...], jnp.concatenate([xr * kr - xi * ki, xr * ki + xi * kr], axis=0))
            yr, yi = y[:n2], y[n2:]
            o_ref[0, kk] = yr * c - yi * s
            o_ref[1, kk] = yr * s + yi * c
        else:
            o_ref[0, kk] = x[:n2]
            o_ref[1, kk] = x[n2:]


def fft_mid(a, tw_c, tw_s, w2_hl, w2i_hl=None, kf=None, *, kf_col0=0):
    _, n1, n2, C = a.shape
    g = 4
    tc = _pick(C, (512, 256, 128))
    conv = kf is not None
    blk = pl.BlockSpec((2, g, n2, tc), lambda j, k: (0, k, 0, j))
    tw = pl.BlockSpec((g, n2, 1), lambda j, k: (k, 0, 0))
    mat = pl.BlockSpec((2 * n2, 2 * n2), lambda j, k: (0, 0))
    if conv:
        kb = kf_col0 // tc
        in_specs = [blk, pl.BlockSpec((2, g, n2, tc), lambda j, k: (0, k, 0, j + kb)), tw, tw, mat, mat, mat, mat]
        args = [a, kf, tw_c, tw_s, *w2_hl, *w2i_hl]
    else:
        in_specs = [blk, tw, tw, mat, mat]
        args = [a, tw_c, tw_s, *w2_hl]
    return pl.pallas_call(
        functools.partial(_fft_mid_kernel, g=g, conv=conv),
        grid=(C // tc, n1 // g),
        in_specs=in_specs,
        out_specs=blk,
        out_shape=jax.ShapeDtypeStruct(a.shape, F32),
        compiler_params=_params("parallel", "parallel"),
    )(*args)


def _fft_last_kernel(whi_ref, wlo_ref, b_ref, gate_ref, zp_ref, asum_ref, bias_ref, o_ref):
    inv_asum = 1.0 / asum_ref[...]
    for j in range(b_ref.shape[2]):
        bj = jnp.concatenate([b_ref[0, :, j, :], b_ref[1, :, j, :]], axis=0)
        x = _dot3(whi_ref[...], wlo_ref[...], bj)
        o_ref[:, j, :] = gate_ref[:, j, :] * (x * inv_asum + zp_ref[:, j, :] * bias_ref[...])


def fft_last(w_hl, b, gate3, gate_col0, zp3, zp_col0, asum, bias, *, vec_col0):
    w_hi, w_lo = w_hl
    l1 = w_hi.shape[0]
    _, n1, n2, C = b.shape
    tt2 = FFT_TT2
    tc = _pick(C, (512, 256, 128))
    gcb, zcb, vcb = gate_col0 // tc, zp_col0 // tc, vec_col0 // tc
    full = pl.BlockSpec((l1, 2 * n1), lambda t, j: (0, 0))
    return pl.pallas_call(
        _fft_last_kernel,
        grid=(n2 // tt2, C // tc),
        in_specs=[
            full, full,
            pl.BlockSpec((2, n1, tt2, tc), lambda t, j: (0, 0, t, j)),
            pl.BlockSpec((l1, tt2, tc), lambda t, j: (0, t, j + gcb)),
            pl.BlockSpec((l1, tt2, tc), lambda t, j: (0, t, j + zcb)),
            pl.BlockSpec((1, tc), lambda t, j: (0, j + vcb)),
            pl.BlockSpec((1, tc), lambda t, j: (0, j + vcb)),
        ],
        out_specs=pl.BlockSpec((l1, tt2, tc), lambda t, j: (0, t, j)),
        out_shape=jax.ShapeDtypeStruct((l1, n2, C), F32),
        compiler_params=_params("parallel", "parallel"),
    )(w_hi, w_lo, b, gate3, zp3, asum, bias)


def _hpmm_kernel(*refs, gated):
    if gated:
        ah_ref, al_ref, b_ref, gate_ref, z_ref, asum_ref, bias_ref, o_ref = refs
    else:
        ah_ref, al_ref, b_ref, o_ref = refs
    y = _dot3(ah_ref[...], al_ref[...], b_ref[...])
    if gated:
        y = gate_ref[...] * (y / asum_ref[...] + z_ref[...] * bias_ref[...])
    o_ref[...] = y


def hp_matmul(a_hl, b, *, gate=None, z=None, asum=None, bias=None, vec_col0=0):
    a_hi, a_lo = a_hl
    M, Kd = a_hi.shape
    N = b.shape[1]
    tn = _pick(N, (1024, 512, 256, 128))
    full = pl.BlockSpec((M, Kd), lambda j: (0, 0))
    in_specs = [full, full, pl.BlockSpec((Kd, tn), lambda j: (0, j))]
    args = [a_hi, a_lo, b]
    gated = gate is not None
    if gated:
        vb = vec_col0 // tn
        blk = pl.BlockSpec((M, tn), lambda j: (0, j))
        vec = pl.BlockSpec((1, tn), lambda j: (0, j + vb))
        in_specs += [blk, blk, vec, vec]
        args += [gate, z, asum, bias]
    return pl.pallas_call(
        functools.partial(_hpmm_kernel, gated=gated),
        grid=(N // tn,),
        in_specs=in_specs,
        out_specs=pl.BlockSpec((M, tn), lambda j: (0, j)),
        out_shape=jax.ShapeDtypeStruct((M, N), F32),
        compiler_params=_params("parallel"),
    )(*args)


def _cmul_kernel(x_ref, k_ref, o_ref):
    xr, xi, kr, ki = x_ref[0], x_ref[1], k_ref[0], k_ref[1]
    o_ref[0] = xr * kr - xi * ki
    o_ref[1] = xr * ki + xi * kr


def complex_mul(x, kf, *, kf_col0):
    _, R, C = x.shape
    tc = _pick(C, (512, 256, 128))
    kb = kf_col0 // tc
    return pl.pallas_call(
        _cmul_kernel,
        grid=(C // tc,),
        in_specs=[pl.BlockSpec((2, R, tc), lambda j: (0, 0, j)), pl.BlockSpec((2, R, tc), lambda j: (0, 0, j + kb))],
        out_specs=pl.BlockSpec((2, R, tc), lambda j: (0, 0, j)),
        out_shape=jax.ShapeDtypeStruct(x.shape, F32),
        compiler_params=_params("parallel"),
    )(x, kf)


def _dft_constants(n1, n2, l1):
    N = n1 * n2
    n1h = min(n1, -(-(n1 // 2 + 1) // 8) * 8)
    k1 = np.arange(n1h)[:, None]
    herm = np.where((k1 == 0) | (k1 == n1 // 2), 1.0, np.where(k1 < n1 // 2, 2.0, 0.0))
    ang1 = 2 * np.pi * k1 * np.arange(n1)[None, :] / n1
    w1 = np.concatenate([np.cos(ang1), -np.sin(ang1)], 0)
    w1_inv = np.concatenate([(herm * np.cos(ang1[:, :l1])).T, (-herm * np.sin(ang1[:, :l1])).T], 1) / N
    ang_t = 2 * np.pi * k1 * np.arange(n2)[None, :] / N
    k2 = np.arange(n2)[:, None]
    ang2 = 2 * np.pi * k2 * np.arange(n2)[None, :] / n2
    c2, s2 = np.cos(ang2), np.sin(ang2)
    f = lambda x: jnp.asarray(x, F32)
    return dict(w1_data=_hilo(w1[:, :l1]), w1_full=_hilo(w1), w1_inv=_hilo(w1_inv),
                tw_c=f(np.cos(ang_t))[:, :, None], tw_s=f(np.sin(ang_t))[:, :, None],
                w2=_hilo(np.block([[c2, s2], [-s2, c2]])), w2_inv=_hilo(np.block([[c2, -s2], [s2, c2]])))


def hyena_conv_lat(u3, kern, asum, bias, *, L, D):
    n2 = FFT_N2
    n1, l1 = 2 * L // n2, L // n2
    cst = _dft_constants(n1, n2, l1)
    ka = fft_first(cst["w1_full"], kern.reshape(n2, n1, HY_ORDER * D), col0=0, C=HY_ORDER * D, t2_major=True)
    kf = fft_mid(ka, cst["tw_c"], cst["tw_s"], cst["w2"])
    z3, zc0 = u3, 0
    for o in range(HY_ORDER):
        a = fft_first(cst["w1_data"], z3, col0=zc0, C=D)
        bmid = fft_mid(a, cst["tw_c"], cst["tw_s"], cst["w2"], cst["w2_inv"], kf, kf_col0=o * D)
        z3 = fft_last(cst["w1_inv"], bmid, u3, (o + 1) * D, z3, zc0, asum, bias, vec_col0=o * D)
        zc0 = 0
    return z3.reshape(L, D)


def _dense_dft_constants(L):
    N = 2 * L
    ang = 2 * np.pi * np.arange(N)[:, None] * np.arange(N)[None, :] / N
    wf = np.concatenate([np.cos(ang), -np.sin(ang)], 0)
    wi = np.concatenate([np.cos(ang[:L]), -np.sin(ang[:L])], 1) / N
    return _hilo(wf), _hilo(wf[:, :L]), _hilo(wi)


def hyena_conv_ctx(v, x1, x2, kern, asum, bias):
    L, D = v.shape
    wf, wf_data, wi = _dense_dft_constants(L)
    kf = hp_matmul(wf, kern).reshape(2, 2 * L, HY_ORDER * D)
    z = v
    for o, gate in enumerate((x1, x2)):
        xf = hp_matmul(wf_data, z).reshape(2, 2 * L, D)
        y = complex_mul(xf, kf, kf_col0=o * D).reshape(4 * L, D)
        z = hp_matmul(wi, y, gate=gate, z=z, asum=asum, bias=bias, vec_col0=o * D)
    return z


def _na_kernel(q_ref, k_ref, v_ref, *rest, n_ctx, rows, heads_per_step, rows_per_step):
    bias_refs, o_ref = rest[:rows_per_step], rest[rows_per_step]
    kr = NA_ROWS
    n_lat = rows * GRID_W
    dh = NA_HEAD_DIM
    scale = dh ** -0.5
    for rr in range(rows_per_step):
        r = pl.program_id(1) * rows_per_step + rr
        r0 = jnp.clip(r - kr // 2, 0, rows - kr)
        start = pl.multiple_of(r0 * GRID_W, GRID_W)
        qs = slice(rr * GRID_W, (rr + 1) * GRID_W)
        for hh in range(heads_per_step):
            sl = slice(hh * dh, (hh + 1) * dh)
            q = q_ref[qs, sl]
            s_ctx = _dot_nt(q, k_ref[n_lat:n_lat + n_ctx, sl]) * scale
            s_lat = _dot_nt(q, k_ref[pl.ds(start, kr * GRID_W), sl]) * scale + bias_refs[rr][hh]
            m = jnp.maximum(jnp.max(s_ctx, axis=1, keepdims=True), jnp.max(s_lat, axis=1, keepdims=True))
            p_ctx = jnp.exp(s_ctx - m)
            p_lat = jnp.exp(s_lat - m)
            l = jnp.sum(p_ctx, axis=1, keepdims=True) + jnp.sum(p_lat, axis=1, keepdims=True)
            o = (_dot(p_ctx.astype(BF16), v_ref[n_lat:n_lat + n_ctx, sl])
                 + _dot(p_lat.astype(BF16), v_ref[pl.ds(start, kr * GRID_W), sl]))
            o_ref[qs, sl] = (o / l).astype(o_ref.dtype)


def _na_bias_table(rpb, rows):
    H = rpb.shape[0]
    cols = jnp.arange(GRID_W)
    c0 = jnp.clip(cols - NA_COLS // 2, 0, GRID_W - NA_COLS)
    kc = jnp.arange(GRID_W)[None, :]
    inside = (kc >= c0[:, None]) & (kc < c0[:, None] + NA_COLS)
    rel = jnp.clip(kc - cols[:, None] + NA_COLS - 1, 0, 2 * NA_COLS - 2)
    rr = jnp.arange(NA_ROWS)[:, None] + jnp.arange(NA_ROWS)[None, :]
    t = rpb[:, rr]
    t = t[:, :, :, rel]
    t = jnp.where(inside[None, None, None], t, -1e30)
    t = jnp.transpose(t, (1, 0, 3, 2, 4))
    return t.reshape(NA_ROWS, H, GRID_W, NA_ROWS * GRID_W)


def na_attention(qkv, rpb, *, n_ctx):
    NT, D3 = qkv.shape
    D = D3 // 3
    L = NT - n_ctx
    rows = L // GRID_W
    hps = 4
    hw = hps * NA_HEAD_DIM
    nhb = D // hw
    table = _na_bias_table(rpb, rows)

    def pattern(r):
        r0 = jnp.clip(r - NA_ROWS // 2, 0, rows - NA_ROWS)
        return r0 - r + NA_ROWS - 1

    rps = 2
    bias_specs = [pl.BlockSpec((None, hps, GRID_W, NA_ROWS * GRID_W),
                               lambda h, r, rr=rr: (pattern(r * rps + rr), h, 0, 0)) for rr in range(rps)]
    return pl.pallas_call(
        functools.partial(_na_kernel, n_ctx=n_ctx, rows=rows, heads_per_step=hps, rows_per_step=rps),
        grid=(nhb, rows // rps),
        in_specs=[
            pl.BlockSpec((rps * GRID_W, hw), lambda h, r: (r, h)),
            pl.BlockSpec((NT, hw), lambda h, r: (0, nhb + h)),
            pl.BlockSpec((NT, hw), lambda h, r: (0, 2 * nhb + h)),
            *bias_specs,
        ],
        out_specs=pl.BlockSpec((rps * GRID_W, hw), lambda h, r: (r, h)),
        out_shape=jax.ShapeDtypeStruct((L, D), BF16),
        compiler_params=_params("parallel", "parallel"),
    )(qkv, qkv, qkv, *([table] * rps))


def _router_kernel(x_ref, w_ref, b_ref, o_ref):
    E, G = N_EXPERTS, N_GROUPS
    per = E // G
    logits = _dot_nt(w_ref[...].astype(BF16), x_ref[...])
    scores = jax.nn.sigmoid(logits)
    sel = scores + b_ref[...]
    tm = sel.shape[1]
    grp = sel.reshape(G, per, tm)
    eidx = lax.broadcasted_iota(jnp.int32, grp.shape, 1)
    m1 = jnp.max(grp, axis=1, keepdims=True)
    first = jnp.min(jnp.where(grp == m1, eidx, per), axis=1, keepdims=True)
    m2 = jnp.max(jnp.where(eidx == first, -jnp.inf, grp), axis=1, keepdims=True)
    gs = (m1 + m2).reshape(G, tm)
    gidx = lax.broadcasted_iota(jnp.int32, gs.shape, 0)
    gmask = jnp.zeros(gs.shape, jnp.bool_)
    for _ in range(TOPK_GROUPS):
        mx = jnp.max(gs, axis=0, keepdims=True)
        pick = gidx == jnp.min(jnp.where(gs == mx, gidx, G), axis=0, keepdims=True)
        gmask = jnp.logical_or(gmask, pick)
        gs = jnp.where(pick, -jnp.inf, gs)
    emask = jnp.broadcast_to(gmask.reshape(G, 1, tm), (G, per, tm)).reshape(E, tm)
    cand = jnp.where(emask, sel, -jnp.inf)
    xidx = lax.broadcasted_iota(jnp.int32, cand.shape, 0)
    chosen = jnp.zeros(cand.shape, jnp.bool_)
    for _ in range(TOP_K):
        mx = jnp.max(cand, axis=0, keepdims=True)
        pick = xidx == jnp.min(jnp.where(cand == mx, xidx, E), axis=0, keepdims=True)
        chosen = jnp.logical_or(chosen, pick)
        cand = jnp.where(pick, -jnp.inf, cand)
    wsel = jnp.where(chosen, scores, 0.0)
    gates = wsel / jnp.sum(wsel, axis=0, keepdims=True) * ROUTED_SCALE
    o_ref[...] = gates.T


def moe_router(h, router_w_t, router_b, layer):
    T, D = h.shape
    E = router_w_t.shape[1]
    tm = _pick(T, (256, 128))
    return pl.pallas_call(
        _router_kernel,
        grid=(T // tm,),
        in_specs=[
            pl.BlockSpec((tm, D), lambda i: (i, 0)),
            pl.BlockSpec((None, E, D), lambda i: (layer, 0, 0)),
            pl.BlockSpec((None, E, 1), lambda i: (layer, 0, 0)),
        ],
        out_specs=pl.BlockSpec((tm, E), lambda i: (i, 0)),
        out_shape=jax.ShapeDtypeStruct((T, E), F32),
        compiler_params=_params("parallel"),
    )(h, router_w_t, router_b.reshape(-1, E, 1))


def _moe_up_kernel(*refs, nk, ne, gated, F):
    if gated:
        x_ref, wg_ref, wu_ref, g_ref, o_ref, accg_ref, accu_ref = refs
    else:
        x_ref, wg_ref, wu_ref, o_ref, accg_ref, accu_ref = refs
    e0 = pl.program_id(1) * ne
    k = pl.program_id(2)

    @pl.when(k == 0)
    def _():
        accg_ref[...] = jnp.zeros_like(accg_ref)
        accu_ref[...] = jnp.zeros_like(accu_ref)

    def stacked_experts(ref):
        w = ref[0] if ne == 1 else jnp.concatenate([ref[ee] for ee in range(ne)], axis=0)
        return w.astype(BF16)

    x = x_ref[...]
    accg_ref[...] += _dot_nt(x, stacked_experts(wg_ref))
    accu_ref[...] += _dot_nt(x, stacked_experts(wu_ref))

    @pl.when(k == nk - 1)
    def _():
        a = accg_ref[...]
        hid = a * jax.nn.sigmoid(a) * accu_ref[...]
        if gated:
            g = g_ref[...]
            lane = lax.broadcasted_iota(jnp.int32, g.shape, 1)
            col = lax.broadcasted_iota(jnp.int32, (1, ne * F), 1) // F
            gexp = jnp.zeros_like(hid)
            for ee in range(ne):
                ge = jnp.sum(jnp.where(lane == e0 + ee, g, 0.0), axis=1, keepdims=True)
                gexp = jnp.where(col == ee, ge, gexp)
            hid = hid * gexp
        o_ref[...] = hid.astype(o_ref.dtype)


def moe_up(x, w_gate, w_up, layer, gates=None):
    T, D = x.shape
    if w_gate.ndim == 3:
        w_gate, w_up = w_gate[:, None], w_up[:, None]
    w_gate, w_up = jnp.swapaxes(w_gate, 2, 3), jnp.swapaxes(w_up, 2, 3)
    E, F = w_gate.shape[1], w_gate.shape[2]
    ne = 4 if E % 4 == 0 else 1
    tm = _pick(T, (1056, 1024, 512, 256, 128))
    tk = _pick(D, (2048, 1024, 512, 256, 128))
    nk = D // tk
    gated = gates is not None
    w_spec = pl.BlockSpec((None, ne, F, tk), lambda i, e, k: (layer, e, 0, k))
    in_specs = [pl.BlockSpec((tm, tk), lambda i, e, k: (i, k)), w_spec, w_spec]
    args = [x, w_gate, w_up]
    if gated:
        in_specs.append(pl.BlockSpec((tm, E), lambda i, e, k: (i, 0)))
        args.append(gates)
    return pl.pallas_call(
        functools.partial(_moe_up_kernel, nk=nk, ne=ne, gated=gated, F=F),
        grid=(T // tm, E // ne, nk),
        in_specs=in_specs,
        out_specs=pl.BlockSpec((tm, ne * F), lambda i, e, k: (i, e)),
        out_shape=jax.ShapeDtypeStruct((T, E * F), BF16),
        scratch_shapes=[pltpu.VMEM((tm, ne * F), F32), pltpu.VMEM((tm, ne * F), F32)],
        compiler_params=_params("parallel", "parallel", "arbitrary"),
    )(*args)


def moe(h, layer, router_w_t, router_b, w_gate, w_up, w_down, s_gate, s_up, s_down):
    gates = moe_router(h, router_w_t, router_b, layer)
    hid = moe_up(h, w_gate, w_up, layer, gates)
    hid_sh = moe_up(h, s_gate, s_up, layer)
    E, F, D = w_down.shape[1:]
    return matmul(hid, w_down.reshape(DEPTH, E * F, D), out_dtype=F32, layer=layer, extra=(hid_sh, s_down, layer))


def kernel(x, c, ctx, c_ctx, ada_a, ada_b, ada_bias, ln1_g, ln1_b, ln2_g, ln2_b, router_w, router_b, moe_w_gate, moe_w_up, moe_w_down, sh_w_gate, sh_w_up, sh_w_down, ml_w_in, ml_b_if, ml_norm_g, ml_w_out, da_w_qkv, da_lam_q1, da_lam_k1, da_lam_q2, da_lam_k2, da_subln_g, da_w_out, hy_w_in, hy_conv_w, hy_conv_b, hy_f_w1, hy_f_b1, hy_f_w2, hy_f_b2, hy_f_w3, hy_f_b3, hy_f_w4, hy_bias, hy_w_out, na_w_qkv, na_rpb, na_w_out):
    _, L, D = x.shape
    C = ctx.shape[1]
    NT = C + L
    s = jnp.concatenate([x[0], ctx[0]], axis=0)

    cc = jnp.zeros((8, D), F32).at[0].set(c_ctx).at[1].set(c[0])
    mods = ada_modulation(cc, ada_a, ada_b, ada_bias)[:, :2].reshape(DEPTH, 2, 6, 1, D)
    SH1, SC1, G1, SH2, SC2, G2 = range(6)
    router_w_t = jnp.swapaxes(router_w, 1, 2)
    moe_args = (router_w_t, router_b, moe_w_gate, moe_w_up, moe_w_down, sh_w_gate, sh_w_up, sh_w_down)

    (h,) = ln_modulate(s, mods, n_ctx=C, h_layer=0, sc_idx=SC1, sh_idx=SH1)
    for i in range(DEPTH):
        last = i == DEPTH - 1
        if i == 0:
            y = mlstm_mixer(h, ml_w_in, ml_b_if[0], ml_norm_g[0], ml_w_out, n_ctx=C)
        elif i == 1:
            lam4 = jnp.stack([da_lam_q1[0], da_lam_k1[0], da_lam_q2[0], da_lam_k2[0]])
            y = diff_attn_mixer(h, da_w_qkv, lam4, da_subln_g[0], da_w_out, n_ctx=C,
                                lambda_init=0.8 - 0.6 * math.exp(-0.3 * i))
        elif i == 2:
            y = hyena_mixer(h, hy_w_in, hy_conv_w[0], hy_conv_b[0], hy_f_w1[0], hy_f_b1[0], hy_f_w2[0], hy_f_b2[0],
                            hy_f_w3[0], hy_f_b3[0], hy_f_w4[0], hy_bias[0], hy_w_out, n_ctx=C)
        else:
            y = na_mixer(h, na_w_qkv, na_rpb[0], na_w_out, n_ctx=C)
        n_ctx = C
        if last:
            s, n_ctx = s[:L], 0
        s, h = ln_modulate(s, mods, n_ctx=n_ctx, ys=(y,), ln_layer=i, gate_idx=G1, ln_g=ln1_g, ln_b=ln1_b,
                           h_layer=i, sc_idx=SC2, sh_idx=SH2)
        y = moe(h, i, *moe_args)
        if last:
            (s,) = ln_modulate(s, mods, n_ctx=n_ctx, ys=(y,), ln_layer=i, gate_idx=G2, ln_g=ln2_g, ln_b=ln2_b)
        else:
            s, h = ln_modulate(s, mods, n_ctx=n_ctx, ys=(y,), ln_layer=i, gate_idx=G2, ln_g=ln2_g, ln_b=ln2_b,
                               h_layer=i + 1, sc_idx=SC1, sh_idx=SH1)
    return s[None]


def mlstm_mixer(h, w_in, b_if, norm_g, w_out, *, n_ctx):
    D = h.shape[1]
    w_in_t = jnp.swapaxes(w_in, 1, 2)
    u = matmul(h, w_in_t, out_dtype=F32, layer=0, ncols=3 * D, w_transposed=True)
    w_if_t = jnp.pad(w_in_t[0, 3 * D:], ((0, LANES - 4 * ML_HEADS), (0, 0)))
    g_if = matmul(h, w_if_t, out_dtype=F32, w_transposed=True)
    hs = mlstm_scan(u, g_if, b_if, n_ctx=n_ctx)
    return matmul(mlstm_gate(hs, u, norm_g, o_col_block=2), w_out, out_dtype=F32, layer=0)


def diff_attn_mixer(h, w_qkv, lam4, subln_g, w_out, *, n_ctx, lambda_init):
    NT, D = h.shape
    L = NT - n_ctx
    scale = DA_HEAD_DIM ** -0.5 * math.log2(math.e)
    qk = matmul(h, w_qkv, out_dtype=F32, layer=0, ncols=2 * D)
    v = matmul(h, w_qkv, out_dtype=BF16, layer=0, col0=2 * D, ncols=D)
    cos, sin = _rope_tables(L, 1.0)
    ones, zeros = jnp.ones((n_ctx, LANES), F32), jnp.zeros((n_ctx, LANES), F32)
    q_lat = rope_cast(qk, cos * scale, sin * scale, row0=0, nrows=L, col0=0, ncols=D)
    q_ctx = rope_cast(qk, ones * scale, zeros, row0=L, nrows=n_ctx, col0=0, ncols=D)
    k = rope_cast(qk, jnp.concatenate([cos, ones]), jnp.concatenate([sin, zeros]), row0=0, nrows=NT, col0=D, ncols=D)
    o_lat = diff_attention(q_lat, k, v, lam4, subln_g, kv_row0=0, kv_len=NT, lambda_init=lambda_init)
    o_ctx = diff_attention(q_ctx, k, v, lam4, subln_g, kv_row0=L, kv_len=n_ctx, lambda_init=lambda_init)
    return matmul(jnp.concatenate([o_lat, o_ctx]), w_out, out_dtype=F32, layer=0)


def hyena_mixer(h, w_in, conv_w, conv_b, f_w1, f_b1, f_w2, f_b2, f_w3, f_b3, f_w4, hy_bias, w_out, *, n_ctx):
    NT, D = h.shape
    L = NT - n_ctx
    u = short_conv(matmul(h, w_in, out_dtype=F32, layer=0), conv_w, conv_b, n_ctx=n_ctx)
    fargs = (f_w1, f_b1, f_w2, f_b2, f_w3, f_b3, f_w4)
    bias = hy_bias.reshape(1, HY_ORDER * D)
    kern_l, asum_l = hyena_filters(L, D, *fargs, t2_major_n2=FFT_N2)
    z_lat = hyena_conv_lat(u.reshape(NT // FFT_N2, FFT_N2, 3 * D), kern_l, asum_l, bias, L=L, D=D)
    kern_c, asum_c = hyena_filters(n_ctx, D, *fargs)
    v, x1, x2 = (u[L:, j * D:(j + 1) * D] for j in range(3))
    z_ctx = hyena_conv_ctx(v, x1, x2, kern_c, asum_c, bias)
    return matmul(jnp.concatenate([z_lat, z_ctx]), w_out, out_dtype=F32, layer=0)


def na_mixer(h, w_qkv, rpb, w_out, *, n_ctx):
    qkv = matmul(h, w_qkv, out_dtype=BF16, layer=0)
    return matmul(na_attention(qkv, rpb, n_ctx=n_ctx), w_out, out_dtype=F32, layer=0)
```

```python
import functools
import math

import numpy as np
import jax
import jax.numpy as jnp
from jax import lax
from jax.experimental import pallas as pl
from jax.experimental.pallas import tpu as pltpu

F32 = jnp.float32
BF16 = jnp.bfloat16

V7X_VMEM_LIMIT_BYTES = 56 * 1024 * 1024
LANES = 128

GRID_W = 64
DEPTH = 4
DEEPNORM_ALPHA = (2 * DEPTH) ** 0.25
LN_EPS = 1e-5
RMS_EPS = 1e-6
ROPE_BASE = 10000.0
ML_HEADS = 8
ML_CHUNK = 256
DA_HEAD_DIM = 128
HY_ORDER = 2
HY_EMB = 33
HY_BANDS = (HY_EMB - 1) // 2
HY_FAST_DECAY = 0.3
HY_SLOW_DECAY = 1.5
HY_DECAY_TARGET = 1e-2
NA_HEAD_DIM = 128
NA_ROWS = 8
NA_COLS = 16
N_EXPERTS = 64
TOP_K = 8
N_GROUPS = 8
TOPK_GROUPS = 4
ROUTED_SCALE = 2.5


def _params(*sem):
    return pltpu.CompilerParams(dimension_semantics=sem, vmem_limit_bytes=V7X_VMEM_LIMIT_BYTES)


def _dot(a, b):
    return jnp.dot(a, b, preferred_element_type=F32)


def _dot_nt(a, b):
    return lax.dot_general(a, b, (((1,), (1,)), ((), ())), preferred_element_type=F32)


def _dot_tn(a, b):
    return lax.dot_general(a, b, (((0,), (0,)), ((), ())), preferred_element_type=F32)


def _pick(n, prefs):
    for p in prefs:
        if n % p == 0:
            return p
    return n


def _mm_kernel(*refs, nk, has_extra):
    if has_extra:
        a_ref, w_ref, a2_ref, w2_ref, o_ref, acc_ref = refs
    else:
        a_ref, w_ref, o_ref, acc_ref = refs
    k = pl.program_id(2)

    @pl.when(k == 0)
    def _():
        acc_ref[...] = jnp.zeros_like(acc_ref)

    acc_ref[...] += _dot(a_ref[...].astype(BF16), w_ref[...].astype(BF16))

    @pl.when(k == nk - 1)
    def _():
        acc = acc_ref[...]
        if has_extra:
            acc = acc + _dot(a2_ref[...].astype(BF16), w2_ref[...].astype(BF16))
        o_ref[...] = acc.astype(o_ref.dtype)


def _rope(x, cos, sin):
    lane = lax.broadcasted_iota(jnp.int32, x.shape, 1)
    rot = jnp.where((lane % 64) < 32, -pltpu.roll(x, LANES - 32, 1), pltpu.roll(x, 32, 1))
    return x * cos + rot * sin


def _mm_resident_kernel(a_ref, w_ref, *rest, w_transposed, rope):
    o_ref = rest[-1]
    dot = _dot_nt if w_transposed else _dot
    acc = dot(a_ref[...].astype(BF16), w_ref[...].astype(BF16))
    if rope:
        cos, sin = rest[0][...], rest[1][...]
        for g in range(acc.shape[1] // LANES):
            sl = slice(g * LANES, (g + 1) * LANES)
            o_ref[:, sl] = _rope(acc[:, sl], cos, sin).astype(o_ref.dtype)
    else:
        o_ref[...] = acc.astype(o_ref.dtype)


def _matmul_resident(a, w, *, out_dtype, layer, col0, N, w_transposed, rope):
    M, K = a.shape
    tm = _pick(M, (2112, 2048, 1024, 512, 256, 128) if a.dtype == BF16 else (1056, 1024, 512, 256, 128))
    tn = _pick(N, (256, 128))
    cb = col0 // tn
    lead = (None,) if w.ndim == 3 else ()
    lidx = (layer,) if w.ndim == 3 else ()
    if w_transposed:
        w_spec = pl.BlockSpec(lead + (tn, K), lambda i, j: lidx + (j + cb, 0))
    else:
        w_spec = pl.BlockSpec(lead + (K, tn), lambda i, j: lidx + (0, j + cb))
    in_specs = [pl.BlockSpec((tm, K), lambda i, j: (i, 0), pipeline_mode=pl.Buffered(1)), w_spec]
    args = [a, w]
    if rope is not None:
        cos, sin, part_cols = rope
        bpp = part_cols // tn
        tab = pl.BlockSpec((None, tm, LANES), lambda i, j: (j // bpp, i, 0))
        in_specs += [tab, tab]
        args += [cos, sin]
    return pl.pallas_call(
        functools.partial(_mm_resident_kernel, w_transposed=w_transposed, rope=rope is not None),
        grid=(M // tm, N // tn),
        in_specs=in_specs,
        out_specs=pl.BlockSpec((tm, tn), lambda i, j: (i, j)),
        out_shape=jax.ShapeDtypeStruct((M, N), out_dtype),
        compiler_params=_params("parallel", "arbitrary"),
    )(*args)


def matmul(a, w, *, out_dtype, layer=None, col0=0, ncols=None, extra=None, row0=0, nrows=None, w_transposed=False,
           rope=None):
    M = a.shape[0] if nrows is None else nrows
    K = a.shape[1]
    N = (w.shape[-2] if w_transposed else w.shape[-1]) if ncols is None else ncols
    if K <= 4096 and extra is None and nrows is None and M >= 1024:
        return _matmul_resident(a, w, out_dtype=out_dtype, layer=layer, col0=col0, N=N, w_transposed=w_transposed,
                                rope=rope)
    assert not w_transposed and rope is None
    tm = _pick(M, (1056, 1024, 512, 256, 128, 64, 32, 16, 8))
    tn = _pick(N, (1024, 512, 256, 128))
    tk = _pick(K, (2048, 1024, 512, 256, 128))
    assert col0 % tn == 0 and row0 % tm == 0
    cb, rb = col0 // tn, row0 // tm
    nk = K // tk
    if w.ndim == 3:
        w_spec = pl.BlockSpec((None, tk, tn), lambda i, j, k: (layer, k, j + cb))
    else:
        w_spec = pl.BlockSpec((tk, tn), lambda i, j, k: (k, j + cb))
    in_specs = [pl.BlockSpec((tm, tk), lambda i, j, k: (i + rb, k)), w_spec]
    args = [a, w]
    if extra is not None:
        a2, w2, layer2 = extra
        k2 = a2.shape[1]
        in_specs.append(pl.BlockSpec((tm, k2), lambda i, j, k: (i + rb, 0)))
        in_specs.append(pl.BlockSpec((None, k2, tn), lambda i, j, k: (layer2, 0, j)))
        args += [a2, w2]
    return pl.pallas_call(
        functools.partial(_mm_kernel, nk=nk, has_extra=extra is not None),
        grid=(M // tm, N // tn, nk),
        in_specs=in_specs,
        out_specs=pl.BlockSpec((tm, tn), lambda i, j, k: (i, j)),
        out_shape=jax.ShapeDtypeStruct((M, N), out_dtype),
        scratch_shapes=[pltpu.VMEM((tm, tn), F32)],
        compiler_params=_params("parallel", "parallel", "arbitrary"),
    )(*args)


def _ada_kernel(cc_ref, a_ref, b_ref, bias_ref, o_ref):
    cc = cc_ref[...]
    act = cc * jax.nn.sigmoid(cc)
    t = _dot(act.astype(BF16), a_ref[...].astype(BF16))
    o_ref[...] = _dot(t.astype(BF16), b_ref[...].astype(BF16)) + bias_ref[...]


def ada_modulation(cc, ada_a, ada_b, ada_bias):
    depth, D, R = ada_a.shape
    n6 = ada_b.shape[-1]
    tn = D
    return pl.pallas_call(
        _ada_kernel,
        grid=(depth, n6 // tn),
        in_specs=[
            pl.BlockSpec((8, D), lambda l, n: (0, 0)),
            pl.BlockSpec((None, D, R), lambda l, n: (l, 0, 0)),
            pl.BlockSpec((None, R, tn), lambda l, n: (l, 0, n)),
            pl.BlockSpec((None, 1, tn), lambda l, n: (l, 0, n)),
        ],
        out_specs=pl.BlockSpec((None, 8, tn), lambda l, n: (l, 0, n)),
        out_shape=jax.ShapeDtypeStruct((depth, 8, n6), F32),
        compiler_params=_params("parallel", "parallel"),
    )(cc, ada_a, ada_b, ada_bias.reshape(depth, 1, n6))


def _lnmod_kernel(*refs, has_ln, has_h, n_y):
    it = iter(refs)
    s_ref = next(it)
    if has_ln:
        y_refs = [next(it) for _ in range(n_y)]
        gate_ref, lng_ref, lnb_ref = next(it), next(it), next(it)
    if has_h:
        sc_ref, sh_ref = next(it), next(it)
    if has_ln:
        so_ref = next(it)
    if has_h:
        h_ref = next(it)
    x = s_ref[...]
    if has_ln:
        y = y_refs[0][...].astype(F32)
        for r in y_refs[1:]:
            y = y + r[...].astype(F32)
        v = DEEPNORM_ALPHA * x + gate_ref[...] * y
        mu = jnp.mean(v, axis=-1, keepdims=True)
        var = jnp.mean(jnp.square(v - mu), axis=-1, keepdims=True)
        x = (v - mu) * lax.rsqrt(var + LN_EPS) * lng_ref[...] + lnb_ref[...]
        so_ref[...] = x
    if has_h:
        h_ref[...] = (x * (1.0 + sc_ref[...]) + sh_ref[...]).astype(h_ref.dtype)


def ln_modulate(s, mods, *, n_ctx, ys=(), ln_layer=None, gate_idx=None, ln_g=None, ln_b=None,
                h_layer=None, sc_idx=None, sh_idx=None):
    R, D = s.shape
    tr = 128
    has_ln, has_h = len(ys) > 0, h_layer is not None
    n_lat_blocks = (R - n_ctx) // tr
    row_spec = pl.BlockSpec((tr, D), lambda i: (i, 0))

    def mod_spec(layer, idx):
        return pl.BlockSpec((None, None, None, 1, D),
                            lambda i: (layer, jnp.where(i >= n_lat_blocks, 0, 1), idx, 0, 0))

    vec_spec = pl.BlockSpec((None, 1, D), lambda i: (ln_layer, 0, 0))
    in_specs, args = [row_spec], [s]
    if has_ln:
        in_specs += [row_spec] * len(ys) + [mod_spec(ln_layer, gate_idx), vec_spec, vec_spec]
        args += list(ys) + [mods, ln_g.reshape(DEPTH, 1, D), ln_b.reshape(DEPTH, 1, D)]
    if has_h:
        in_specs += [mod_spec(h_layer, sc_idx), mod_spec(h_layer, sh_idx)]
        args += [mods, mods]
    out_specs, out_shape = [], []
    if has_ln:
        out_specs.append(row_spec)
        out_shape.append(jax.ShapeDtypeStruct((R, D), F32))
    if has_h:
        out_specs.append(row_spec)
        out_shape.append(jax.ShapeDtypeStruct((R, D), BF16))
    outs = pl.pallas_call(
        functools.partial(_lnmod_kernel, has_ln=has_ln, has_h=has_h, n_y=len(ys)),
        grid=(R // tr,),
        in_specs=in_specs,
        out_specs=out_specs,
        out_shape=out_shape,
        compiler_params=_params("parallel"),
    )(*args)
    return outs


def _log_sigmoid(x):
    return jnp.minimum(x, 0.0) - jnp.log(1.0 + jnp.exp(-jnp.abs(x)))


def _mlstm_kernel(bias_ref, q_ref, k_ref, v_ref, ic_ref, fc_ref, ir_ref, fr_ref, o_ref, ct_ref, n_ref, m_ref,
                  *, T, dqk):
    d, h, t = pl.program_id(0), pl.program_id(1), pl.program_id(2)

    @pl.when(t == 0)
    def _():
        ct_ref[...] = jnp.zeros_like(ct_ref)
        n_ref[...] = jnp.zeros_like(n_ref)
        m_ref[...] = jnp.zeros_like(m_ref)

    bi = bias_ref[d * 2 * ML_HEADS + h]
    bf = bias_ref[(d * 2 + 1) * ML_HEADS + h]
    i_c = ic_ref[...] + bi
    i_r = ir_ref[...] + bi
    f_c = _log_sigmoid(fc_ref[...] + bf)
    f_r = _log_sigmoid(fr_ref[...] + bf)
    row = lax.broadcasted_iota(jnp.int32, (T, T), 0)
    col = lax.broadcasted_iota(jnp.int32, (T, T), 1)
    sgn = 1 - 2 * d
    incl = (col - row) * sgn <= 0
    incl_t = (row - col) * sgn <= 0
    b_c = jnp.sum(jnp.where(incl, f_r, 0.0), axis=1, keepdims=True)
    b_r = jnp.sum(jnp.where(incl_t, f_c, 0.0), axis=0, keepdims=True)
    f_tot = jnp.sum(f_r, axis=1, keepdims=True)
    m_prev = m_ref[...]
    dmat = jnp.where(incl, b_c - b_r + i_r, -jnp.inf)
    inter = b_c + m_prev
    m_t = jnp.maximum(inter, jnp.max(dmat, axis=1, keepdims=True))
    w_intra = jnp.exp(dmat - m_t)
    w_inter = jnp.exp(inter - m_t)
    q = q_ref[...] * (dqk ** -0.5)
    k = k_ref[...]
    v = v_ref[...]
    qb, kb, vb = q.astype(BF16), k.astype(BF16), v.astype(BF16)
    s = _dot_nt(qb, kb) * w_intra
    ct = ct_ref[...]
    n = n_ref[...]
    num = w_inter * _dot(qb, ct.astype(BF16)) + _dot(s.astype(BF16), vb)
    den = w_inter * jnp.sum(q * n, axis=1, keepdims=True) + jnp.sum(s, axis=1, keepdims=True)
    o_ref[...] = num / jnp.maximum(jnp.abs(den), jnp.exp(-m_t))
    g_r = f_tot - b_r + i_r
    g_c = f_tot - b_c + i_c
    m_new = jnp.maximum(f_tot + m_prev, jnp.max(g_r, axis=1, keepdims=True))
    decay = jnp.exp(f_tot + m_prev - m_new)
    wk = jnp.exp(g_c - m_new)
    ct_ref[...] = decay * ct + _dot_tn(kb, (wk * v).astype(BF16))
    n_ref[...] = decay * n + jnp.sum(wk * k, axis=0, keepdims=True)
    m_ref[...] = m_new


def mlstm_scan(u, gates, b_if, *, n_ctx):
    NT = u.shape[0]
    H, T = ML_HEADS, ML_CHUNK
    D = u.shape[1] // 3
    dqk, dv = D // (2 * H), D // H
    nc, ncc = NT // T, n_ctx // T
    ncl = nc - ncc
    g4 = gates[:, :4 * H].T
    g_col = g4.reshape(4 * H, NT, 1)
    g_row = g4.reshape(4 * H, nc, 1, T)

    def chunk(d, t):
        fwd = jnp.where(t < ncc, ncl + t, t - ncc)
        return jnp.where(d == 0, fwd, nc - 1 - t)

    kq, kk, kv = 0, (H * dqk) // dqk, (2 * H * dqk) // dv
    col_spec = lambda off: pl.BlockSpec((None, T, 1), lambda d, h, t: ((2 * d + off) * H + h, chunk(d, t), 0))
    row_spec = lambda off: pl.BlockSpec((None, None, 1, T), lambda d, h, t: ((2 * d + off) * H + h, chunk(d, t), 0, 0))
    return pl.pallas_call(
        functools.partial(_mlstm_kernel, T=T, dqk=dqk),
        grid=(2, H, nc),
        in_specs=[
            pl.BlockSpec(memory_space=pltpu.SMEM),
            pl.BlockSpec((T, dqk), lambda d, h, t: (chunk(d, t), kq + h)),
            pl.BlockSpec((T, dqk), lambda d, h, t: (chunk(d, t), kk + h)),
            pl.BlockSpec((T, dv), lambda d, h, t: (chunk(d, t), kv + h)),
            col_spec(0), col_spec(1), row_spec(0), row_spec(1),
        ],
        out_specs=pl.BlockSpec((None, T, dv), lambda d, h, t: (d, chunk(d, t), h)),
        out_shape=jax.ShapeDtypeStruct((2, NT, H * dv), F32),
        scratch_shapes=[pltpu.VMEM((dqk, dv), F32), pltpu.VMEM((1, dqk), F32), pltpu.VMEM((1, 1), F32)],
        compiler_params=_params("parallel", "parallel", "arbitrary"),
    )(b_if, u, u, u, g_col, g_col, g_row, g_row)


def _mlgate_kernel(hs_ref, o_ref, g_ref, out_ref, *, dv):
    D = out_ref.shape[1]
    for h in range(D // dv):
        sl = slice(h * dv, (h + 1) * dv)
        x = hs_ref[0, :, sl] + hs_ref[1, :, sl]
        r = lax.rsqrt(jnp.mean(jnp.square(x), axis=-1, keepdims=True) + RMS_EPS)
        out_ref[:, sl] = (x * r * g_ref[:, sl] * jax.nn.sigmoid(o_ref[:, sl])).astype(out_ref.dtype)


def mlstm_gate(hs, u, norm_g, *, o_col_block):
    _, NT, D = hs.shape
    tr = 128
    return pl.pallas_call(
        functools.partial(_mlgate_kernel, dv=D // ML_HEADS),
        grid=(NT // tr,),
        in_specs=[
            pl.BlockSpec((2, tr, D), lambda i: (0, i, 0)),
            pl.BlockSpec((tr, D), lambda i: (i, o_col_block)),
            pl.BlockSpec((1, D), lambda i: (0, 0)),
        ],
        out_specs=pl.BlockSpec((tr, D), lambda i: (i, 0)),
        out_shape=jax.ShapeDtypeStruct((NT, D), BF16),
        compiler_params=_params("parallel"),
    )(hs, u, norm_g.reshape(1, D))


SAFE_LOG2 = 64.0


def _dattn_kernel(lam_ref, q_ref, k_ref, v_ref, g_ref, o_ref, m_ref, l_ref, acc_ref, kn_ref,
                  *, tk, n_chunks, out_scale, lambda_init):
    dh = DA_HEAD_DIM
    nlt = tk // LANES
    l_ref[...] = jnp.zeros_like(l_ref)
    acc_ref[...] = jnp.zeros_like(acc_ref)

    @pl.when(pl.program_id(1) == 0)
    def _():
        for i in range(2):
            kk = k_ref[:, i * dh:(i + 1) * dh].astype(F32)
            k2 = jnp.max(jnp.sum(kk * kk, axis=1, keepdims=True), axis=0, keepdims=True)
            kn_ref[i] = jnp.broadcast_to(k2, kn_ref.shape[1:])

    bound2 = None
    for i in range(2):
        qq = q_ref[:, i * dh:(i + 1) * dh].astype(F32)
        b2 = jnp.max(jnp.sum(qq * qq, axis=1, keepdims=True), axis=0, keepdims=True) * kn_ref[i][0:1, 0:1]
        bound2 = b2 if bound2 is None else jnp.maximum(bound2, b2)
    scores_bounded = bound2[0, 0] <= SAFE_LOG2 * SAFE_LOG2

    def scores(c, i):
        kc = k_ref[pl.ds(pl.multiple_of(c * tk, tk), tk), i * dh:(i + 1) * dh]
        return _dot_nt(q_ref[:, i * dh:(i + 1) * dh], kc)

    @pl.when(scores_bounded)
    def _():
        m_ref[...] = jnp.zeros_like(m_ref)

    @pl.when(jnp.logical_not(scores_bounded))
    def _():
        m_ref[...] = jnp.full_like(m_ref, -jnp.inf)

        def max_pass(c, carry):
            for i in range(2):
                s = scores(c, i)
                m = m_ref[i]
                for t in range(nlt):
                    m = jnp.maximum(m, s[:, t * LANES:(t + 1) * LANES])
                m_ref[i] = m
            return carry

        lax.fori_loop(0, n_chunks, max_pass, 0)
        for i in range(2):
            m_ref[i] = jnp.broadcast_to(jnp.max(m_ref[i], axis=1, keepdims=True), m_ref.shape[1:])

    tq = q_ref.shape[0]

    def sum_pass(c, carry):
        vc = v_ref[pl.ds(pl.multiple_of(c * tk, tk), tk), :]
        ps = []
        for i in range(2):
            s = scores(c, i)
            m = m_ref[i]
            p = [jnp.exp2(s[:, t * LANES:(t + 1) * LANES] - m) for t in range(nlt)]
            l_ref[i] += functools.reduce(jnp.add, p)
            ps.append(jnp.concatenate(p, axis=1).astype(BF16))
        acc_ref[...] += _dot(jnp.concatenate(ps, axis=0), vc)
        return carry

    lax.fori_loop(0, n_chunks, sum_pass, 0)
    lam = (jnp.exp(jnp.sum(lam_ref[0:1, :] * lam_ref[1:2, :], axis=1, keepdims=True))
           - jnp.exp(jnp.sum(lam_ref[2:3, :] * lam_ref[3:4, :], axis=1, keepdims=True)) + lambda_init)
    l = [jnp.sum(l_ref[i], axis=1, keepdims=True) for i in range(2)]
    o = acc_ref[0:tq] / l[0] - lam * (acc_ref[tq:2 * tq] / l[1])
    r = lax.rsqrt(jnp.mean(jnp.square(o), axis=-1, keepdims=True) + RMS_EPS)
    o_ref[...] = (o * r * g_ref[...] * out_scale).astype(o_ref.dtype)


def diff_attention(qkv, lam4, subln_g, *, q_row0, n_q, kv_row0, kv_len, lambda_init):
    D = qkv.shape[1] // 3
    hd = 2 * DA_HEAD_DIM
    H = D // hd
    tq = _pick(n_q, (512, 256))
    tk = _pick(kv_len, (768, 512, 256))
    qb, kvb = q_row0 // tq, kv_row0 // kv_len
    assert kv_row0 % kv_len == 0 and q_row0 % tq == 0
    return pl.pallas_call(
        functools.partial(_dattn_kernel, tk=tk, n_chunks=kv_len // tk, out_scale=1.0 - lambda_init,
                          lambda_init=lambda_init),
        grid=(H, n_q // tq),
        in_specs=[
            pl.BlockSpec((4, DA_HEAD_DIM), lambda h, i: (0, 0)),
            pl.BlockSpec((tq, hd), lambda h, i: (i + qb, h)),
            pl.BlockSpec((kv_len, hd), lambda h, i: (kvb, H + h)),
            pl.BlockSpec((kv_len, hd), lambda h, i: (kvb, 2 * H + h)),
            pl.BlockSpec((1, hd), lambda h, i: (0, 0)),
        ],
        out_specs=pl.BlockSpec((tq, hd), lambda h, i: (i, h)),
        out_shape=jax.ShapeDtypeStruct((n_q, D), BF16),
        scratch_shapes=[pltpu.VMEM((2, tq, LANES), F32), pltpu.VMEM((2, tq, LANES), F32),
                        pltpu.VMEM((2 * tq, hd), F32), pltpu.VMEM((2, 8, LANES), F32)],
        compiler_params=_params("parallel", "arbitrary"),
    )(lam4, qkv, qkv, qkv, subln_g.reshape(1, hd))


def _rope_tables(L, scale):
    a = DA_HEAD_DIM // 2
    t = jnp.arange(L)
    pos = jnp.stack([t // GRID_W, t % GRID_W], -1).astype(F32)
    inv = ROPE_BASE ** (-jnp.arange(0, a, 2, dtype=F32) / a)
    ang = pos[:, :, None] * inv
    ang = jnp.concatenate([ang, ang], -1).reshape(L, DA_HEAD_DIM)
    return jnp.cos(ang) * scale, jnp.sin(ang) * scale


def _shortconv_kernel(p_ref, prev_ref, next_ref, w_ref, b_ref, o_ref, *, n_seq_first, n_seq_last):
    i = pl.program_id(0)
    x = p_ref[...]
    tr = x.shape[0]
    rid = lax.broadcasted_iota(jnp.int32, x.shape, 0)
    is_first = functools.reduce(jnp.logical_or, [i == b for b in n_seq_first])
    is_last = functools.reduce(jnp.logical_or, [i == b for b in n_seq_last])
    prev_row = jnp.where(is_first, 0.0, prev_ref[7:8, :])
    next_row = jnp.where(is_last, 0.0, next_ref[0:1, :])
    xm = jnp.where(rid == 0, prev_row, pltpu.roll(x, 1, 0))
    xp = jnp.where(rid == tr - 1, next_row, pltpu.roll(x, tr - 1, 0))
    o_ref[...] = xm * w_ref[0:1, :] + x * w_ref[1:2, :] + xp * w_ref[2:3, :] + b_ref[...]


def short_conv(p, w, b, *, n_ctx):
    NT, N = p.shape
    tr = 256
    tc = _pick(N, (1024, 512, 256, 128))
    nb, n8 = NT // tr, NT // 8
    n_lat_blocks = (NT - n_ctx) // tr
    firsts = (0, n_lat_blocks)
    lasts = (n_lat_blocks - 1, nb - 1)
    return pl.pallas_call(
        functools.partial(_shortconv_kernel, n_seq_first=firsts, n_seq_last=lasts),
        grid=(nb, N // tc),
        in_specs=[
            pl.BlockSpec((tr, tc), lambda i, j: (i, j)),
            pl.BlockSpec((8, tc), lambda i, j: (jnp.maximum(i * (tr // 8) - 1, 0), j)),
            pl.BlockSpec((8, tc), lambda i, j: (jnp.minimum((i + 1) * (tr // 8), n8 - 1), j)),
            pl.BlockSpec((3, tc), lambda i, j: (0, j)),
            pl.BlockSpec((1, tc), lambda i, j: (0, j)),
        ],
        out_specs=pl.BlockSpec((tr, tc), lambda i, j: (i, j)),
        out_shape=jax.ShapeDtypeStruct((NT, N), F32),
        compiler_params=_params("parallel", "parallel"),
    )(p, p, p, w, b.reshape(1, N))


def _filter_kernel(zf_ref, t_ref, w1_ref, b1_ref, w2_ref, b2_ref, w3_ref, b3_ref, w4f_ref, w4b_ref, dl_ref,
                   o_ref, asum_ref, hdn_ref):
    i, j = pl.program_id(0), pl.program_id(1)

    @pl.when(j == 0)
    def _():
        hdn = jnp.sin(_dot(zf_ref[...].astype(BF16), w1_ref[...].astype(BF16)) + b1_ref[...])
        hdn = jnp.sin(_dot(hdn.astype(BF16), w2_ref[...].astype(BF16)) + b2_ref[...])
        hdn_ref[...] = jnp.sin(_dot(hdn.astype(BF16), w3_ref[...].astype(BF16)) + b3_ref[...]).astype(hdn_ref.dtype)

    hdn = hdn_ref[...]
    hf = _dot(hdn, w4f_ref[...].astype(BF16))
    hb = _dot(hdn, w4b_ref[...].astype(BF16))
    side = t_ref[:, 1:2]
    kern = jnp.where(side > 0.0, hf, jnp.where(side < 0.0, hb, 0.0)) * jnp.exp(-t_ref[:, 0:1] * dl_ref[...])
    o_ref[...] = kern

    @pl.when(i == 0)
    def _():
        asum_ref[j] = jnp.zeros(asum_ref.shape[1:], F32)

    asum_ref[j] += jnp.sum(jnp.abs(kern), axis=0, keepdims=True)


def hyena_filters(L, D, f_w1, f_b1, f_w2, f_b2, f_w3, f_b3, f_w4, *, t2_major_n2=None):
    t = jnp.linspace(0.0, 1.0, L, dtype=F32)[:, None]
    w = 2.0 * math.pi * jnp.arange(L, dtype=F32)[:, None] / L
    f = jnp.linspace(1e-4, HY_BANDS - 1, HY_BANDS, dtype=F32)[None]
    z = jnp.concatenate([t, jnp.cos(w * f), -jnp.sin(w * f)], -1)
    src = jnp.concatenate([jnp.arange(L), jnp.zeros((1,), jnp.int32), L - 1 - jnp.arange(L - 1)])
    r = jnp.arange(2 * L)
    side = jnp.where(r < L, 1.0, jnp.where(r > L, -1.0, 0.0)).astype(F32)[:, None]
    if t2_major_n2 is not None:
        perm = r.reshape(2 * L // t2_major_n2, t2_major_n2).T.reshape(-1)
        src, side = src[perm], side[perm]
    zf = jnp.pad(z[src], ((0, 0), (0, LANES - HY_EMB)))
    tcol = jnp.concatenate([t[src], side], axis=1)
    deltas = jnp.abs(jnp.linspace(math.log(HY_DECAY_TARGET) / HY_SLOW_DECAY, math.log(HY_DECAY_TARGET) / HY_FAST_DECAY,
                                  D, dtype=F32)).reshape(1, D)
    w1 = jnp.pad(f_w1, ((0, LANES - HY_EMB), (0, 0)))
    nf = f_w1.shape[1]
    R2 = 2 * L
    tr = _pick(R2, (512,))
    tc = _pick(D, (1024, 512, 256, 128))
    cpd = D // tc
    ncol = HY_ORDER * cpd
    full = lambda shape: pl.BlockSpec(shape, lambda i, j: (0, 0))
    kern, asum = pl.pallas_call(
        _filter_kernel,
        grid=(R2 // tr, ncol),
        in_specs=[
            pl.BlockSpec((tr, LANES), lambda i, j: (i, 0)),
            pl.BlockSpec((tr, 2), lambda i, j: (i, 0)),
            full((LANES, nf)), full((1, nf)), full((nf, nf)), full((1, nf)), full((nf, nf)), full((1, nf)),
            pl.BlockSpec((nf, tc), lambda i, j: (0, (j // cpd) * 2 * cpd + j % cpd)),
            pl.BlockSpec((nf, tc), lambda i, j: (0, ((j // cpd) * 2 + 1) * cpd + j % cpd)),
            pl.BlockSpec((1, tc), lambda i, j: (0, j % cpd)),
        ],
        out_specs=[pl.BlockSpec((tr, tc), lambda i, j: (i, j)), pl.BlockSpec((ncol, 1, tc), lambda i, j: (0, 0, 0))],
        out_shape=[jax.ShapeDtypeStruct((R2, HY_ORDER * D), F32), jax.ShapeDtypeStruct((ncol, 1, tc), F32)],
        scratch_shapes=[pltpu.VMEM((tr, nf), BF16)],
        compiler_params=_params("arbitrary", "arbitrary"),
    )(zf, tcol, w1, f_b1.reshape(1, nf), f_w2, f_b2.reshape(1, nf), f_w3, f_b3.reshape(1, nf), f_w4, f_w4, deltas)
    return kern, asum.reshape(1, HY_ORDER * D)


FFT_N2 = 128
FFT_TT2 = 8


def _split(x):
    hi = x.astype(BF16)
    return hi, (x - hi.astype(F32)).astype(BF16)


def _dot3(w_hi, w_lo, x):
    x_hi, x_lo = _split(x)
    return _dot(w_hi, x_hi) + (_dot(w_lo, x_hi) + _dot(w_hi, x_lo))


def _hilo(a):
    return _split(jnp.asarray(a, F32))


def _fft_first_kernel(whi_ref, wlo_ref, z_ref, o_ref, *, t2_major):
    n1 = o_ref.shape[1]
    for j in range(o_ref.shape[2]):
        a = _dot3(whi_ref[...], wlo_ref[...], z_ref[j] if t2_major else z_ref[:, j, :])
        o_ref[0, :, j, :] = a[:n1]
        o_ref[1, :, j, :] = a[n1:]


def fft_first(w_hl, z3, *, col0, C, t2_major=False):
    w_hi, w_lo = w_hl
    n1x2, l1 = w_hi.shape
    n1, n2 = n1x2 // 2, z3.shape[0 if t2_major else 1]
    tt2 = FFT_TT2
    tc = _pick(C, (512, 256, 128))
    cb = col0 // tc
    full = pl.BlockSpec((n1x2, l1), lambda t, j: (0, 0))
    if t2_major:
        z_spec = pl.BlockSpec((tt2, l1, tc), lambda t, j: (t, 0, j + cb))
    else:
        z_spec = pl.BlockSpec((l1, tt2, tc), lambda t, j: (0, t, j + cb))
    return pl.pallas_call(
        functools.partial(_fft_first_kernel, t2_major=t2_major),
        grid=(n2 // tt2, C // tc),
        in_specs=[full, full, z_spec],
        out_specs=pl.BlockSpec((2, n1, tt2, tc), lambda t, j: (0, 0, t, j)),
        out_shape=jax.ShapeDtypeStruct((2, n1, n2, C), F32),
        compiler_params=_params("parallel", "parallel"),
    )(w_hi, w_lo, z3)


def _fft_mid_kernel(*refs, g, conv):
    if conv:
        a_ref, kf_ref, twc_ref, tws_ref, w2h_ref, w2l_ref, w2ih_ref, w2il_ref, o_ref = refs
    else:
        a_ref, twc_ref, tws_ref, w2h_ref, w2l_ref, o_ref = refs
    n2 = a_ref.shape[2]
    for kk in range(g):
        ar, ai = a_ref[0, kk], a_ref[1, kk]
        c, s = twc_ref[kk], tws_ref[kk]
        x = _dot3(w2h_ref[...], w2l_ref[...], jnp.concatenate([ar * c + ai * s, ai * c - ar * s], axis=0))
        if conv:
            xr, xi = x[:n2], x[n2:]
            kr, ki = kf_ref[0, kk], kf_ref[1, kk]
            y = _dot3(w2ih_ref[...], w2il_ref[...], jnp.concatenate([xr * kr - xi * ki, xr * ki + xi * kr], axis=0))
            yr, yi = y[:n2], y[n2:]
            o_ref[0, kk] = yr * c - yi * s
            o_ref[1, kk] = yr * s + yi * c
        else:
            o_ref[0, kk] = x[:n2]
            o_ref[1, kk] = x[n2:]


def fft_mid(a, tw_c, tw_s, w2_hl, w2i_hl=None, kf=None, *, kf_col0=0):
    _, n1, n2, C = a.shape
    g = 4
    tc = _pick(C, (512, 256, 128))
    conv = kf is not None
    blk = pl.BlockSpec((2, g, n2, tc), lambda j, k: (0, k, 0, j))
    tw = pl.BlockSpec((g, n2, 1), lambda j, k: (k, 0, 0))
    mat = pl.BlockSpec((2 * n2, 2 * n2), lambda j, k: (0, 0))
    if conv:
        kb = kf_col0 // tc
        in_specs = [blk, pl.BlockSpec((2, g, n2, tc), lambda j, k: (0, k, 0, j + kb)), tw, tw, mat, mat, mat, mat]
        args = [a, kf, tw_c, tw_s, *w2_hl, *w2i_hl]
    else:
        in_specs = [blk, tw, tw, mat, mat]
        args = [a, tw_c, tw_s, *w2_hl]
    return pl.pallas_call(
        functools.partial(_fft_mid_kernel, g=g, conv=conv),
        grid=(C // tc, n1 // g),
        in_specs=in_specs,
        out_specs=blk,
        out_shape=jax.ShapeDtypeStruct(a.shape, F32),
        compiler_params=_params("parallel", "parallel"),
    )(*args)


def _fft_last_kernel(whi_ref, wlo_ref, b_ref, gate_ref, zp_ref, asum_ref, bias_ref, o_ref):
    inv_asum = 1.0 / asum_ref[...]
    for j in range(b_ref.shape[2]):
        bj = jnp.concatenate([b_ref[0, :, j, :], b_ref[1, :, j, :]], axis=0)
        x = _dot3(whi_ref[...], wlo_ref[...], bj)
        o_ref[:, j, :] = gate_ref[:, j, :] * (x * inv_asum + zp_ref[:, j, :] * bias_ref[...])


def fft_last(w_hl, b, gate3, gate_col0, zp3, zp_col0, asum, bias, *, vec_col0):
    w_hi, w_lo = w_hl
    l1 = w_hi.shape[0]
    _, n1, n2, C = b.shape
    tt2 = FFT_TT2
    tc = _pick(C, (512, 256, 128))
    gcb, zcb, vcb = gate_col0 // tc, zp_col0 // tc, vec_col0 // tc
    full = pl.BlockSpec((l1, 2 * n1), lambda t, j: (0, 0))
    return pl.pallas_call(
        _fft_last_kernel,
        grid=(n2 // tt2, C // tc),
        in_specs=[
            full, full,
            pl.BlockSpec((2, n1, tt2, tc), lambda t, j: (0, 0, t, j)),
            pl.BlockSpec((l1, tt2, tc), lambda t, j: (0, t, j + gcb)),
            pl.BlockSpec((l1, tt2, tc), lambda t, j: (0, t, j + zcb)),
            pl.BlockSpec((1, tc), lambda t, j: (0, j + vcb)),
            pl.BlockSpec((1, tc), lambda t, j: (0, j + vcb)),
        ],
        out_specs=pl.BlockSpec((l1, tt2, tc), lambda t, j: (0, t, j)),
        out_shape=jax.ShapeDtypeStruct((l1, n2, C), F32),
        compiler_params=_params("parallel", "parallel"),
    )(w_hi, w_lo, b, gate3, zp3, asum, bias)


def _hpmm_kernel(*refs, gated):
    if gated:
        ah_ref, al_ref, b_ref, gate_ref, z_ref, asum_ref, bias_ref, o_ref = refs
    else:
        ah_ref, al_ref, b_ref, o_ref = refs
    y = _dot3(ah_ref[...], al_ref[...], b_ref[...])
    if gated:
        y = gate_ref[...] * (y / asum_ref[...] + z_ref[...] * bias_ref[...])
    o_ref[...] = y


def hp_matmul(a_hl, b, *, gate=None, z=None, asum=None, bias=None, vec_col0=0):
    a_hi, a_lo = a_hl
    M, Kd = a_hi.shape
    N = b.shape[1]
    tn = _pick(N, (1024, 512, 256, 128))
    full = pl.BlockSpec((M, Kd), lambda j: (0, 0))
    in_specs = [full, full, pl.BlockSpec((Kd, tn), lambda j: (0, j))]
    args = [a_hi, a_lo, b]
    gated = gate is not None
    if gated:
        vb = vec_col0 // tn
        blk = pl.BlockSpec((M, tn), lambda j: (0, j))
        vec = pl.BlockSpec((1, tn), lambda j: (0, j + vb))
        in_specs += [blk, blk, vec, vec]
        args += [gate, z, asum, bias]
    return pl.pallas_call(
        functools.partial(_hpmm_kernel, gated=gated),
        grid=(N // tn,),
        in_specs=in_specs,
        out_specs=pl.BlockSpec((M, tn), lambda j: (0, j)),
        out_shape=jax.ShapeDtypeStruct((M, N), F32),
        compiler_params=_params("parallel"),
    )(*args)


def _cmul_kernel(x_ref, k_ref, o_ref):
    xr, xi, kr, ki = x_ref[0], x_ref[1], k_ref[0], k_ref[1]
    o_ref[0] = xr * kr - xi * ki
    o_ref[1] = xr * ki + xi * kr


def complex_mul(x, kf, *, kf_col0):
    _, R, C = x.shape
    tc = _pick(C, (512, 256, 128))
    kb = kf_col0 // tc
    return pl.pallas_call(
        _cmul_kernel,
        grid=(C // tc,),
        in_specs=[pl.BlockSpec((2, R, tc), lambda j: (0, 0, j)), pl.BlockSpec((2, R, tc), lambda j: (0, 0, j + kb))],
        out_specs=pl.BlockSpec((2, R, tc), lambda j: (0, 0, j)),
        out_shape=jax.ShapeDtypeStruct(x.shape, F32),
        compiler_params=_params("parallel"),
    )(x, kf)


def _dft_constants(n1, n2, l1):
    N = n1 * n2
    n1h = min(n1, -(-(n1 // 2 + 1) // 8) * 8)
    k1 = np.arange(n1h)[:, None]
    herm = np.where((k1 == 0) | (k1 == n1 // 2), 1.0, np.where(k1 < n1 // 2, 2.0, 0.0))
    ang1 = 2 * np.pi * k1 * np.arange(n1)[None, :] / n1
    w1 = np.concatenate([np.cos(ang1), -np.sin(ang1)], 0)
    w1_inv = np.concatenate([(herm * np.cos(ang1[:, :l1])).T, (-herm * np.sin(ang1[:, :l1])).T], 1) / N
    ang_t = 2 * np.pi * k1 * np.arange(n2)[None, :] / N
    k2 = np.arange(n2)[:, None]
    ang2 = 2 * np.pi * k2 * np.arange(n2)[None, :] / n2
    c2, s2 = np.cos(ang2), np.sin(ang2)
    f = lambda x: jnp.asarray(x, F32)
    return dict(w1_data=_hilo(w1[:, :l1]), w1_full=_hilo(w1), w1_inv=_hilo(w1_inv),
                tw_c=f(np.cos(ang_t))[:, :, None], tw_s=f(np.sin(ang_t))[:, :, None],
                w2=_hilo(np.block([[c2, s2], [-s2, c2]])), w2_inv=_hilo(np.block([[c2, -s2], [s2, c2]])))


def hyena_conv_lat(u3, kern, asum, bias, *, L, D):
    n2 = FFT_N2
    n1, l1 = 2 * L // n2, L // n2
    cst = _dft_constants(n1, n2, l1)
    ka = fft_first(cst["w1_full"], kern.reshape(n2, n1, HY_ORDER * D), col0=0, C=HY_ORDER * D, t2_major=True)
    kf = fft_mid(ka, cst["tw_c"], cst["tw_s"], cst["w2"])
    z3, zc0 = u3, 0
    for o in range(HY_ORDER):
        a = fft_first(cst["w1_data"], z3, col0=zc0, C=D)
        bmid = fft_mid(a, cst["tw_c"], cst["tw_s"], cst["w2"], cst["w2_inv"], kf, kf_col0=o * D)
        z3 = fft_last(cst["w1_inv"], bmid, u3, (o + 1) * D, z3, zc0, asum, bias, vec_col0=o * D)
        zc0 = 0
    return z3.reshape(L, D)


def _dense_dft_constants(L):
    N = 2 * L
    ang = 2 * np.pi * np.arange(N)[:, None] * np.arange(N)[None, :] / N
    wf = np.concatenate([np.cos(ang), -np.sin(ang)], 0)
    wi = np.concatenate([np.cos(ang[:L]), -np.sin(ang[:L])], 1) / N
    return _hilo(wf), _hilo(wf[:, :L]), _hilo(wi)


def hyena_conv_ctx(v, x1, x2, kern, asum, bias):
    L, D = v.shape
    wf, wf_data, wi = _dense_dft_constants(L)
    kf = hp_matmul(wf, kern).reshape(2, 2 * L, HY_ORDER * D)
    z = v
    for o, gate in enumerate((x1, x2)):
        xf = hp_matmul(wf_data, z).reshape(2, 2 * L, D)
        y = complex_mul(xf, kf, kf_col0=o * D).reshape(4 * L, D)
        z = hp_matmul(wi, y, gate=gate, z=z, asum=asum, bias=bias, vec_col0=o * D)
    return z


def _na_kernel(q_ref, k_ref, v_ref, *rest, n_ctx, rows, heads_per_step, rows_per_step):
    bias_refs, o_ref = rest[:rows_per_step], rest[rows_per_step]
    kr = NA_ROWS
    n_lat = rows * GRID_W
    dh = NA_HEAD_DIM
    scale = dh ** -0.5
    for rr in range(rows_per_step):
        r = pl.program_id(1) * rows_per_step + rr
        r0 = jnp.clip(r - kr // 2, 0, rows - kr)
        start = pl.multiple_of(r0 * GRID_W, GRID_W)
        qs = slice(rr * GRID_W, (rr + 1) * GRID_W)
        for hh in range(heads_per_step):
            sl = slice(hh * dh, (hh + 1) * dh)
            q = q_ref[qs, sl]
            s_ctx = _dot_nt(q, k_ref[n_lat:n_lat + n_ctx, sl]) * scale
            s_lat = _dot_nt(q, k_ref[pl.ds(start, kr * GRID_W), sl]) * scale + bias_refs[rr][hh]
            m = jnp.maximum(jnp.max(s_ctx, axis=1, keepdims=True), jnp.max(s_lat, axis=1, keepdims=True))
            p_ctx = jnp.exp(s_ctx - m)
            p_lat = jnp.exp(s_lat - m)
            l = jnp.sum(p_ctx, axis=1, keepdims=True) + jnp.sum(p_lat, axis=1, keepdims=True)
            o = (_dot(p_ctx.astype(BF16), v_ref[n_lat:n_lat + n_ctx, sl])
                 + _dot(p_lat.astype(BF16), v_ref[pl.ds(start, kr * GRID_W), sl]))
            o_ref[qs, sl] = (o / l).astype(o_ref.dtype)


def _na_bias_table(rpb, rows):
    H = rpb.shape[0]
    cols = jnp.arange(GRID_W)
    c0 = jnp.clip(cols - NA_COLS // 2, 0, GRID_W - NA_COLS)
    kc = jnp.arange(GRID_W)[None, :]
    inside = (kc >= c0[:, None]) & (kc < c0[:, None] + NA_COLS)
    rel = jnp.clip(kc - cols[:, None] + NA_COLS - 1, 0, 2 * NA_COLS - 2)
    rr = jnp.arange(NA_ROWS)[:, None] + jnp.arange(NA_ROWS)[None, :]
    t = rpb[:, rr]
    t = t[:, :, :, rel]
    t = jnp.where(inside[None, None, None], t, -1e30)
    t = jnp.transpose(t, (1, 0, 3, 2, 4))
    return t.reshape(NA_ROWS, H, GRID_W, NA_ROWS * GRID_W)


def na_attention(qkv, rpb, *, n_ctx):
    NT, D3 = qkv.shape
    D = D3 // 3
    L = NT - n_ctx
    rows = L // GRID_W
    hps = 4
    hw = hps * NA_HEAD_DIM
    nhb = D // hw
    table = _na_bias_table(rpb, rows)

    def pattern(r):
        r0 = jnp.clip(r - NA_ROWS // 2, 0, rows - NA_ROWS)
        return r0 - r + NA_ROWS - 1

    rps = 4
    bias_specs = [pl.BlockSpec((None, hps, GRID_W, NA_ROWS * GRID_W),
                               lambda h, r, rr=rr: (pattern(r * rps + rr), h, 0, 0)) for rr in range(rps)]
    return pl.pallas_call(
        functools.partial(_na_kernel, n_ctx=n_ctx, rows=rows, heads_per_step=hps, rows_per_step=rps),
        grid=(nhb, rows // rps),
        in_specs=[
            pl.BlockSpec((rps * GRID_W, hw), lambda h, r: (r, h)),
            pl.BlockSpec((NT, hw), lambda h, r: (0, nhb + h)),
            pl.BlockSpec((NT, hw), lambda h, r: (0, 2 * nhb + h)),
            *bias_specs,
        ],
        out_specs=pl.BlockSpec((rps * GRID_W, hw), lambda h, r: (r, h)),
        out_shape=jax.ShapeDtypeStruct((L, D), BF16),
        compiler_params=_params("parallel", "parallel"),
    )(qkv, qkv, qkv, *([table] * rps))


def _router_kernel(x_ref, w_ref, b_ref, o_ref):
    E, G = N_EXPERTS, N_GROUPS
    per = E // G
    logits = _dot_nt(w_ref[...].astype(BF16), x_ref[...])
    scores = jax.nn.sigmoid(logits)
    sel = scores + b_ref[...]
    tm = sel.shape[1]
    grp = sel.reshape(G, per, tm)
    eidx = lax.broadcasted_iota(jnp.int32, grp.shape, 1)
    m1 = jnp.max(grp, axis=1, keepdims=True)
    first = jnp.min(jnp.where(grp == m1, eidx, per), axis=1, keepdims=True)
    m2 = jnp.max(jnp.where(eidx == first, -jnp.inf, grp), axis=1, keepdims=True)
    gs = (m1 + m2).reshape(G, tm)
    gidx = lax.broadcasted_iota(jnp.int32, gs.shape, 0)
    gmask = jnp.zeros(gs.shape, jnp.bool_)
    for _ in range(TOPK_GROUPS):
        mx = jnp.max(gs, axis=0, keepdims=True)
        pick = gidx == jnp.min(jnp.where(gs == mx, gidx, G), axis=0, keepdims=True)
        gmask = jnp.logical_or(gmask, pick)
        gs = jnp.where(pick, -jnp.inf, gs)
    emask = jnp.broadcast_to(gmask.reshape(G, 1, tm), (G, per, tm)).reshape(E, tm)
    cand = jnp.where(emask, sel, -jnp.inf)
    xidx = lax.broadcasted_iota(jnp.int32, cand.shape, 0)
    chosen = jnp.zeros(cand.shape, jnp.bool_)
    for _ in range(TOP_K):
        mx = jnp.max(cand, axis=0, keepdims=True)
        pick = xidx == jnp.min(jnp.where(cand == mx, xidx, E), axis=0, keepdims=True)
        chosen = jnp.logical_or(chosen, pick)
        cand = jnp.where(pick, -jnp.inf, cand)
    wsel = jnp.where(chosen, scores, 0.0)
    gates = wsel / jnp.sum(wsel, axis=0, keepdims=True) * ROUTED_SCALE
    o_ref[...] = gates.T


def moe_router(h, router_w_t, router_b, layer):
    T, D = h.shape
    E = router_w_t.shape[1]
    tm = _pick(T, (256, 128))
    return pl.pallas_call(
        _router_kernel,
        grid=(T // tm,),
        in_specs=[
            pl.BlockSpec((tm, D), lambda i: (i, 0)),
            pl.BlockSpec((None, E, D), lambda i: (layer, 0, 0)),
            pl.BlockSpec((None, E, 1), lambda i: (layer, 0, 0)),
        ],
        out_specs=pl.BlockSpec((tm, E), lambda i: (i, 0)),
        out_shape=jax.ShapeDtypeStruct((T, E), F32),
        compiler_params=_params("parallel"),
    )(h, router_w_t, router_b.reshape(-1, E, 1))


def _moe_up_kernel(*refs, nk, ne, gated, F):
    if gated:
        x_ref, wg_ref, wu_ref, g_ref, o_ref, accg_ref, accu_ref = refs
    else:
        x_ref, wg_ref, wu_ref, o_ref, accg_ref, accu_ref = refs
    e0 = pl.program_id(1) * ne
    k = pl.program_id(2)

    @pl.when(k == 0)
    def _():
        accg_ref[...] = jnp.zeros_like(accg_ref)
        accu_ref[...] = jnp.zeros_like(accu_ref)

    def stacked_experts(ref):
        w = ref[0] if ne == 1 else jnp.concatenate([ref[ee] for ee in range(ne)], axis=0)
        return w.astype(BF16)

    x = x_ref[...]
    accg_ref[...] += _dot_nt(x, stacked_experts(wg_ref))
    accu_ref[...] += _dot_nt(x, stacked_experts(wu_ref))

    @pl.when(k == nk - 1)
    def _():
        a = accg_ref[...]
        hid = a * jax.nn.sigmoid(a) * accu_ref[...]
        if gated:
            g = g_ref[...]
            lane = lax.broadcasted_iota(jnp.int32, g.shape, 1)
            col = lax.broadcasted_iota(jnp.int32, (1, ne * F), 1) // F
            gexp = jnp.zeros_like(hid)
            for ee in range(ne):
                ge = jnp.sum(jnp.where(lane == e0 + ee, g, 0.0), axis=1, keepdims=True)
                gexp = jnp.where(col == ee, ge, gexp)
            hid = hid * gexp
        o_ref[...] = hid.astype(o_ref.dtype)


def moe_up(x, w_gate, w_up, layer, gates=None):
    T, D = x.shape
    if w_gate.ndim == 3:
        w_gate, w_up = w_gate[:, None], w_up[:, None]
    w_gate, w_up = jnp.swapaxes(w_gate, 2, 3), jnp.swapaxes(w_up, 2, 3)
    E, F = w_gate.shape[1], w_gate.shape[2]
    ne = 4 if E % 4 == 0 else 1
    tm = _pick(T, (1056, 1024, 512, 256, 128))
    tk = _pick(D, (2048, 1024, 512, 256, 128))
    nk = D // tk
    gated = gates is not None
    w_spec = pl.BlockSpec((None, ne, F, tk), lambda i, e, k: (layer, e, 0, k))
    in_specs = [pl.BlockSpec((tm, tk), lambda i, e, k: (i, k)), w_spec, w_spec]
    args = [x, w_gate, w_up]
    if gated:
        in_specs.append(pl.BlockSpec((tm, E), lambda i, e, k: (i, 0)))
        args.append(gates)
    return pl.pallas_call(
        functools.partial(_moe_up_kernel, nk=nk, ne=ne, gated=gated, F=F),
        grid=(T // tm, E // ne, nk),
        in_specs=in_specs,
        out_specs=pl.BlockSpec((tm, ne * F), lambda i, e, k: (i, e)),
        out_shape=jax.ShapeDtypeStruct((T, E * F), BF16),
        scratch_shapes=[pltpu.VMEM((tm, ne * F), F32), pltpu.VMEM((tm, ne * F), F32)],
        compiler_params=_params("parallel", "parallel", "arbitrary"),
    )(*args)


def moe(h, layer, router_w_t, router_b, w_gate, w_up, w_down, s_gate, s_up, s_down):
    gates = moe_router(h, router_w_t, router_b, layer)
    hid = moe_up(h, w_gate, w_up, layer, gates)
    hid_sh = moe_up(h, s_gate, s_up, layer)
    E, F, D = w_down.shape[1:]
    return matmul(hid, w_down.reshape(DEPTH, E * F, D), out_dtype=F32, layer=layer, extra=(hid_sh, s_down, layer))


def kernel(x, c, ctx, c_ctx, ada_a, ada_b, ada_bias, ln1_g, ln1_b, ln2_g, ln2_b, router_w, router_b, moe_w_gate, moe_w_up, moe_w_down, sh_w_gate, sh_w_up, sh_w_down, ml_w_in, ml_b_if, ml_norm_g, ml_w_out, da_w_qkv, da_lam_q1, da_lam_k1, da_lam_q2, da_lam_k2, da_subln_g, da_w_out, hy_w_in, hy_conv_w, hy_conv_b, hy_f_w1, hy_f_b1, hy_f_w2, hy_f_b2, hy_f_w3, hy_f_b3, hy_f_w4, hy_bias, hy_w_out, na_w_qkv, na_rpb, na_w_out):
    _, L, D = x.shape
    C = ctx.shape[1]
    NT = C + L
    s = jnp.concatenate([x[0], ctx[0]], axis=0)

    cc = jnp.zeros((8, D), F32).at[0].set(c_ctx).at[1].set(c[0])
    mods = ada_modulation(cc, ada_a, ada_b, ada_bias)[:, :2].reshape(DEPTH, 2, 6, 1, D)
    SH1, SC1, G1, SH2, SC2, G2 = range(6)
    router_w_t = jnp.swapaxes(router_w, 1, 2)
    moe_args = (router_w_t, router_b, moe_w_gate, moe_w_up, moe_w_down, sh_w_gate, sh_w_up, sh_w_down)

    (h,) = ln_modulate(s, mods, n_ctx=C, h_layer=0, sc_idx=SC1, sh_idx=SH1)
    for i in range(DEPTH):
        last = i == DEPTH - 1
        if i == 0:
            y = mlstm_mixer(h, ml_w_in, ml_b_if[0], ml_norm_g[0], ml_w_out, n_ctx=C)
        elif i == 1:
            lam4 = jnp.stack([da_lam_q1[0], da_lam_k1[0], da_lam_q2[0], da_lam_k2[0]])
            y = diff_attn_mixer(h, da_w_qkv, lam4, da_subln_g[0], da_w_out, n_ctx=C,
                                lambda_init=0.8 - 0.6 * math.exp(-0.3 * i))
        elif i == 2:
            y = hyena_mixer(h, hy_w_in, hy_conv_w[0], hy_conv_b[0], hy_f_w1[0], hy_f_b1[0], hy_f_w2[0], hy_f_b2[0],
                            hy_f_w3[0], hy_f_b3[0], hy_f_w4[0], hy_bias[0], hy_w_out, n_ctx=C)
        else:
            y = na_mixer(h, na_w_qkv, na_rpb[0], na_w_out, n_ctx=C)
        n_ctx = C
        if last:
            s, n_ctx = s[:L], 0
        s, h = ln_modulate(s, mods, n_ctx=n_ctx, ys=(y,), ln_layer=i, gate_idx=G1, ln_g=ln1_g, ln_b=ln1_b,
                           h_layer=i, sc_idx=SC2, sh_idx=SH2)
        y = moe(h, i, *moe_args)
        if last:
            (s,) = ln_modulate(s, mods, n_ctx=n_ctx, ys=(y,), ln_layer=i, gate_idx=G2, ln_g=ln2_g, ln_b=ln2_b)
        else:
            s, h = ln_modulate(s, mods, n_ctx=n_ctx, ys=(y,), ln_layer=i, gate_idx=G2, ln_g=ln2_g, ln_b=ln2_b,
                               h_layer=i + 1, sc_idx=SC1, sh_idx=SH1)
    return s[None]


def mlstm_mixer(h, w_in, b_if, norm_g, w_out, *, n_ctx):
    D = h.shape[1]
    w_in_t = jnp.swapaxes(w_in, 1, 2)
    u = matmul(h, w_in_t, out_dtype=F32, layer=0, ncols=3 * D, w_transposed=True)
    w_if_t = jnp.pad(w_in_t[0, 3 * D:], ((0, LANES - 4 * ML_HEADS), (0, 0)))
    g_if = matmul(h, w_if_t, out_dtype=F32, w_transposed=True)
    hs = mlstm_scan(u, g_if, b_if, n_ctx=n_ctx)
    return matmul(mlstm_gate(hs, u, norm_g, o_col_block=2), w_out, out_dtype=F32, layer=0)


def diff_attn_mixer(h, w_qkv, lam4, subln_g, w_out, *, n_ctx, lambda_init):
    NT, D = h.shape
    L = NT - n_ctx
    scale = DA_HEAD_DIM ** -0.5 * math.log2(math.e)
    cos, sin = _rope_tables(L, 1.0)
    ones, zeros = jnp.ones((n_ctx, LANES), F32), jnp.zeros((n_ctx, LANES), F32)
    cos_k, sin_k = jnp.concatenate([cos, ones]), jnp.concatenate([sin, zeros])
    cos3 = jnp.stack([cos_k * scale, cos_k, jnp.ones_like(cos_k)])
    sin3 = jnp.stack([sin_k * scale, sin_k, jnp.zeros_like(sin_k)])
    qkv = matmul(h, w_qkv, out_dtype=BF16, layer=0, rope=(cos3, sin3, D))
    o_lat = diff_attention(qkv, lam4, subln_g, q_row0=0, n_q=L, kv_row0=0, kv_len=NT, lambda_init=lambda_init)
    o_ctx = diff_attention(qkv, lam4, subln_g, q_row0=L, n_q=n_ctx, kv_row0=L, kv_len=n_ctx, lambda_init=lambda_init)
    return matmul(jnp.concatenate([o_lat, o_ctx]), w_out, out_dtype=F32, layer=0)


def hyena_mixer(h, w_in, conv_w, conv_b, f_w1, f_b1, f_w2, f_b2, f_w3, f_b3, f_w4, hy_bias, w_out, *, n_ctx):
    NT, D = h.shape
    L = NT - n_ctx
    u = short_conv(matmul(h, w_in, out_dtype=F32, layer=0), conv_w, conv_b, n_ctx=n_ctx)
    fargs = (f_w1, f_b1, f_w2, f_b2, f_w3, f_b3, f_w4)
    bias = hy_bias.reshape(1, HY_ORDER * D)
    kern_l, asum_l = hyena_filters(L, D, *fargs, t2_major_n2=FFT_N2)
    z_lat = hyena_conv_lat(u.reshape(NT // FFT_N2, FFT_N2, 3 * D), kern_l, asum_l, bias, L=L, D=D)
    kern_c, asum_c = hyena_filters(n_ctx, D, *fargs)
    v, x1, x2 = (u[L:, j * D:(j + 1) * D] for j in range(3))
    z_ctx = hyena_conv_ctx(v, x1, x2, kern_c, asum_c, bias)
    return matmul(jnp.concatenate([z_lat, z_ctx]), w_out, out_dtype=F32, layer=0)


def na_mixer(h, w_qkv, rpb, w_out, *, n_ctx):
    qkv = matmul(h, w_qkv, out_dtype=BF16, layer=0)
    return matmul(na_attention(qkv, rpb, n_ctx=n_ctx), w_out, out_dtype=F32, layer=0)
```

```python
import functools
import math

import numpy as np
import jax
import jax.numpy as jnp
from jax import lax
from jax.experimental import pallas as pl
from jax.experimental.pallas import tpu as pltpu

F32 = jnp.float32
BF16 = jnp.bfloat16

V7X_VMEM_LIMIT_BYTES = 56 * 1024 * 1024
LANES = 128

GRID_W = 64
DEPTH = 4
DEEPNORM_ALPHA = (2 * DEPTH) ** 0.25
LN_EPS = 1e-5
RMS_EPS = 1e-6
ROPE_BASE = 10000.0
ML_HEADS = 8
ML_CHUNK = 256
DA_HEAD_DIM = 128
HY_ORDER = 2
HY_EMB = 33
HY_BANDS = (HY_EMB - 1) // 2
HY_FAST_DECAY = 0.3
HY_SLOW_DECAY = 1.5
HY_DECAY_TARGET = 1e-2
NA_HEAD_DIM = 128
NA_ROWS = 8
NA_COLS = 16
N_EXPERTS = 64
TOP_K = 8
N_GROUPS = 8
TOPK_GROUPS = 4
ROUTED_SCALE = 2.5


def _params(*sem):
    return pltpu.CompilerParams(dimension_semantics=sem, vmem_limit_bytes=V7X_VMEM_LIMIT_BYTES)


def _dot(a, b):
    return jnp.dot(a, b, preferred_element_type=F32)


def _dot_nt(a, b):
    return lax.dot_general(a, b, (((1,), (1,)), ((), ())), preferred_element_type=F32)


def _dot_tn(a, b):
    return lax.dot_general(a, b, (((0,), (0,)), ((), ())), preferred_element_type=F32)


def _pick(n, prefs):
    for p in prefs:
        if n % p == 0:
            return p
    return n


def _mm_kernel(*refs, nk, has_extra):
    if has_extra:
        a_ref, w_ref, a2_ref, w2_ref, o_ref, acc_ref = refs
    else:
        a_ref, w_ref, o_ref, acc_ref = refs
    k = pl.program_id(2)

    @pl.when(k == 0)
    def _():
        acc_ref[...] = jnp.zeros_like(acc_ref)

    acc_ref[...] += _dot(a_ref[...].astype(BF16), w_ref[...].astype(BF16))

    @pl.when(k == nk - 1)
    def _():
        acc = acc_ref[...]
        if has_extra:
            acc = acc + _dot(a2_ref[...].astype(BF16), w2_ref[...].astype(BF16))
        o_ref[...] = acc.astype(o_ref.dtype)


def _rope(x, cos, sin):
    lane = lax.broadcasted_iota(jnp.int32, x.shape, 1)
    rot = jnp.where((lane % 64) < 32, -pltpu.roll(x, LANES - 32, 1), pltpu.roll(x, 32, 1))
    return x * cos + rot * sin


def _mm_resident_kernel(a_ref, w_ref, *rest, w_transposed, rope, conv_seq_starts):
    o_ref = rest[-1]
    dot = _dot_nt if w_transposed else _dot
    w = w_ref[...].astype(BF16)
    acc = dot(a_ref[...].astype(BF16), w)
    if rope:
        cos, sin = rest[0][...], rest[1][...]
        for g in range(acc.shape[1] // LANES):
            sl = slice(g * LANES, (g + 1) * LANES)
            o_ref[:, sl] = _rope(acc[:, sl], cos, sin).astype(o_ref.dtype)
    elif conv_seq_starts is not None:
        prev_ref, next_ref, cw_ref, cb_ref = rest[:4]
        tm = acc.shape[0]
        p_prev = dot(prev_ref[...].astype(BF16), w)[7:8]
        p_next = dot(next_ref[...].astype(BF16), w)[0:1]
        local = lax.broadcasted_iota(jnp.int32, (tm, 1), 0)
        row = local + pl.program_id(0) * tm
        is_start = functools.reduce(jnp.logical_or, [row == r for r in conv_seq_starts[:-1]])
        is_end = functools.reduce(jnp.logical_or, [row == r - 1 for r in conv_seq_starts[1:]])
        xm = jnp.where(local == 0, p_prev, pltpu.roll(acc, 1, 0))
        xp = jnp.where(local == tm - 1, p_next, pltpu.roll(acc, tm - 1, 0))
        xm = jnp.where(is_start, 0.0, xm)
        xp = jnp.where(is_end, 0.0, xp)
        o_ref[...] = (xm * cw_ref[0:1, :] + acc * cw_ref[1:2, :] + xp * cw_ref[2:3, :] + cb_ref[...]).astype(o_ref.dtype)
    else:
        o_ref[...] = acc.astype(o_ref.dtype)


def _matmul_resident(a, w, *, out_dtype, layer, col0, N, w_transposed, rope, conv):
    M, K = a.shape
    tm = _pick(M, (2112, 2048, 1024, 512, 256, 128) if a.dtype == BF16 else (1056, 1024, 512, 256, 128))
    tn = _pick(N, (256, 128))
    cb = col0 // tn
    lead = (None,) if w.ndim == 3 else ()
    lidx = (layer,) if w.ndim == 3 else ()
    if w_transposed:
        w_spec = pl.BlockSpec(lead + (tn, K), lambda i, j: lidx + (j + cb, 0))
    else:
        w_spec = pl.BlockSpec(lead + (K, tn), lambda i, j: lidx + (0, j + cb))
    in_specs = [pl.BlockSpec((tm, K), lambda i, j: (i, 0), pipeline_mode=pl.Buffered(1)), w_spec]
    args = [a, w]
    conv_seq_starts = None
    if rope is not None:
        cos, sin, part_cols = rope
        bpp = part_cols // tn
        tab = pl.BlockSpec((None, tm, LANES), lambda i, j: (j // bpp, i, 0))
        in_specs += [tab, tab]
        args += [cos, sin]
    elif conv is not None:
        conv_w, conv_b, conv_seq_starts = conv
        t8, n8 = tm // 8, M // 8
        in_specs += [pl.BlockSpec((8, K), lambda i, j: (jnp.maximum(i * t8 - 1, 0), 0)),
                     pl.BlockSpec((8, K), lambda i, j: (jnp.minimum((i + 1) * t8, n8 - 1), 0)),
                     pl.BlockSpec((3, tn), lambda i, j: (0, j)), pl.BlockSpec((1, tn), lambda i, j: (0, j))]
        args += [a, a, conv_w, conv_b.reshape(1, N)]
    return pl.pallas_call(
        functools.partial(_mm_resident_kernel, w_transposed=w_transposed, rope=rope is not None,
                          conv_seq_starts=conv_seq_starts),
        grid=(M // tm, N // tn),
        in_specs=in_specs,
        out_specs=pl.BlockSpec((tm, tn), lambda i, j: (i, j)),
        out_shape=jax.ShapeDtypeStruct((M, N), out_dtype),
        compiler_params=_params("parallel", "arbitrary"),
    )(*args)


def matmul(a, w, *, out_dtype, layer=None, col0=0, ncols=None, extra=None, row0=0, nrows=None, w_transposed=False,
           rope=None, conv=None):
    M = a.shape[0] if nrows is None else nrows
    K = a.shape[1]
    N = (w.shape[-2] if w_transposed else w.shape[-1]) if ncols is None else ncols
    if K <= 4096 and extra is None and nrows is None and M >= 1024:
        return _matmul_resident(a, w, out_dtype=out_dtype, layer=layer, col0=col0, N=N, w_transposed=w_transposed,
                                rope=rope, conv=conv)
    assert not w_transposed and rope is None and conv is None
    tm = _pick(M, (1056, 1024, 512, 256, 128, 64, 32, 16, 8))
    tn = _pick(N, (1024, 512, 256, 128))
    tk = _pick(K, (2048, 1024, 512, 256, 128))
    assert col0 % tn == 0 and row0 % tm == 0
    cb, rb = col0 // tn, row0 // tm
    nk = K // tk
    if w.ndim == 3:
        w_spec = pl.BlockSpec((None, tk, tn), lambda i, j, k: (layer, k, j + cb))
    else:
        w_spec = pl.BlockSpec((tk, tn), lambda i, j, k: (k, j + cb))
    in_specs = [pl.BlockSpec((tm, tk), lambda i, j, k: (i + rb, k)), w_spec]
    args = [a, w]
    if extra is not None:
        a2, w2, layer2 = extra
        k2 = a2.shape[1]
        in_specs.append(pl.BlockSpec((tm, k2), lambda i, j, k: (i + rb, 0)))
        in_specs.append(pl.BlockSpec((None, k2, tn), lambda i, j, k: (layer2, 0, j)))
        args += [a2, w2]
    return pl.pallas_call(
        functools.partial(_mm_kernel, nk=nk, has_extra=extra is not None),
        grid=(M // tm, N // tn, nk),
        in_specs=in_specs,
        out_specs=pl.BlockSpec((tm, tn), lambda i, j, k: (i, j)),
        out_shape=jax.ShapeDtypeStruct((M, N), out_dtype),
        scratch_shapes=[pltpu.VMEM((tm, tn), F32)],
        compiler_params=_params("parallel", "parallel", "arbitrary"),
    )(*args)


def _ada_kernel(cc_ref, a_ref, b_ref, bias_ref, o_ref):
    cc = cc_ref[...]
    act = cc * jax.nn.sigmoid(cc)
    t = _dot(act.astype(BF16), a_ref[...].astype(BF16))
    o_ref[...] = _dot(t.astype(BF16), b_ref[...].astype(BF16)) + bias_ref[...]


def ada_modulation(cc, ada_a, ada_b, ada_bias):
    depth, D, R = ada_a.shape
    n6 = ada_b.shape[-1]
    tn = D
    return pl.pallas_call(
        _ada_kernel,
        grid=(depth, n6 // tn),
        in_specs=[
            pl.BlockSpec((8, D), lambda l, n: (0, 0)),
            pl.BlockSpec((None, D, R), lambda l, n: (l, 0, 0)),
            pl.BlockSpec((None, R, tn), lambda l, n: (l, 0, n)),
            pl.BlockSpec((None, 1, tn), lambda l, n: (l, 0, n)),
        ],
        out_specs=pl.BlockSpec((None, 8, tn), lambda l, n: (l, 0, n)),
        out_shape=jax.ShapeDtypeStruct((depth, 8, n6), F32),
        compiler_params=_params("parallel", "parallel"),
    )(cc, ada_a, ada_b, ada_bias.reshape(depth, 1, n6))


def _lnmod_kernel(*refs, has_ln, has_h, n_y):
    it = iter(refs)
    s_ref = next(it)
    if has_ln:
        y_refs = [next(it) for _ in range(n_y)]
        gate_ref, lng_ref, lnb_ref = next(it), next(it), next(it)
    if has_h:
        sc_ref, sh_ref = next(it), next(it)
    if has_ln:
        so_ref = next(it)
    if has_h:
        h_ref = next(it)
    x = s_ref[...]
    if has_ln:
        y = y_refs[0][...].astype(F32)
        for r in y_refs[1:]:
            y = y + r[...].astype(F32)
        v = DEEPNORM_ALPHA * x + gate_ref[...] * y
        mu = jnp.mean(v, axis=-1, keepdims=True)
        var = jnp.mean(jnp.square(v - mu), axis=-1, keepdims=True)
        x = (v - mu) * lax.rsqrt(var + LN_EPS) * lng_ref[...] + lnb_ref[...]
        so_ref[...] = x
    if has_h:
        h_ref[...] = (x * (1.0 + sc_ref[...]) + sh_ref[...]).astype(h_ref.dtype)


def ln_modulate(s, mods, *, n_ctx, ys=(), ln_layer=None, gate_idx=None, ln_g=None, ln_b=None,
                h_layer=None, sc_idx=None, sh_idx=None):
    R, D = s.shape
    tr = 128
    has_ln, has_h = len(ys) > 0, h_layer is not None
    n_lat_blocks = (R - n_ctx) // tr
    row_spec = pl.BlockSpec((tr, D), lambda i: (i, 0))

    def mod_spec(layer, idx):
        return pl.BlockSpec((None, None, None, 1, D),
                            lambda i: (layer, jnp.where(i >= n_lat_blocks, 0, 1), idx, 0, 0))

    vec_spec = pl.BlockSpec((None, 1, D), lambda i: (ln_layer, 0, 0))
    in_specs, args = [row_spec], [s]
    if has_ln:
        in_specs += [row_spec] * len(ys) + [mod_spec(ln_layer, gate_idx), vec_spec, vec_spec]
        args += list(ys) + [mods, ln_g.reshape(DEPTH, 1, D), ln_b.reshape(DEPTH, 1, D)]
    if has_h:
        in_specs += [mod_spec(h_layer, sc_idx), mod_spec(h_layer, sh_idx)]
        args += [mods, mods]
    out_specs, out_shape = [], []
    if has_ln:
        out_specs.append(row_spec)
        out_shape.append(jax.ShapeDtypeStruct((R, D), F32))
    if has_h:
        out_specs.append(row_spec)
        out_shape.append(jax.ShapeDtypeStruct((R, D), BF16))
    outs = pl.pallas_call(
        functools.partial(_lnmod_kernel, has_ln=has_ln, has_h=has_h, n_y=len(ys)),
        grid=(R // tr,),
        in_specs=in_specs,
        out_specs=out_specs,
        out_shape=out_shape,
        compiler_params=_params("parallel"),
    )(*args)
    return outs


def _log_sigmoid(x):
    return jnp.minimum(x, 0.0) - jnp.log(1.0 + jnp.exp(-jnp.abs(x)))


def _mlstm_kernel(bias_ref, q_ref, k_ref, v_ref, ic_ref, fc_ref, ir_ref, fr_ref, o_ref, ct_ref, n_ref, m_ref,
                  *, T, dqk):
    d, h, t = pl.program_id(0), pl.program_id(1), pl.program_id(2)

    @pl.when(t == 0)
    def _():
        ct_ref[...] = jnp.zeros_like(ct_ref)
        n_ref[...] = jnp.zeros_like(n_ref)
        m_ref[...] = jnp.zeros_like(m_ref)

    bi = bias_ref[d * 2 * ML_HEADS + h]
    bf = bias_ref[(d * 2 + 1) * ML_HEADS + h]
    i_c = ic_ref[...] + bi
    i_r = ir_ref[...] + bi
    f_c = _log_sigmoid(fc_ref[...] + bf)
    f_r = _log_sigmoid(fr_ref[...] + bf)
    row = lax.broadcasted_iota(jnp.int32, (T, T), 0)
    col = lax.broadcasted_iota(jnp.int32, (T, T), 1)
    sgn = 1 - 2 * d
    incl = (col - row) * sgn <= 0
    incl_t = (row - col) * sgn <= 0
    b_c = jnp.sum(jnp.where(incl, f_r, 0.0), axis=1, keepdims=True)
    b_r = jnp.sum(jnp.where(incl_t, f_c, 0.0), axis=0, keepdims=True)
    f_tot = jnp.sum(f_r, axis=1, keepdims=True)
    m_prev = m_ref[...]
    dmat = jnp.where(incl, b_c - b_r + i_r, -jnp.inf)
    inter = b_c + m_prev
    m_t = jnp.maximum(inter, jnp.max(dmat, axis=1, keepdims=True))
    w_intra = jnp.exp(dmat - m_t)
    w_inter = jnp.exp(inter - m_t)
    q = q_ref[...] * (dqk ** -0.5)
    k = k_ref[...]
    v = v_ref[...]
    qb, kb, vb = q.astype(BF16), k.astype(BF16), v.astype(BF16)
    s = _dot_nt(qb, kb) * w_intra
    ct = ct_ref[...]
    n = n_ref[...]
    num = w_inter * _dot(qb, ct.astype(BF16)) + _dot(s.astype(BF16), vb)
    den = w_inter * jnp.sum(q * n, axis=1, keepdims=True) + jnp.sum(s, axis=1, keepdims=True)
    o_ref[...] = num / jnp.maximum(jnp.abs(den), jnp.exp(-m_t))
    g_r = f_tot - b_r + i_r
    g_c = f_tot - b_c + i_c
    m_new = jnp.maximum(f_tot + m_prev, jnp.max(g_r, axis=1, keepdims=True))
    decay = jnp.exp(f_tot + m_prev - m_new)
    wk = jnp.exp(g_c - m_new)
    ct_ref[...] = decay * ct + _dot_tn(kb, (wk * v).astype(BF16))
    n_ref[...] = decay * n + jnp.sum(wk * k, axis=0, keepdims=True)
    m_ref[...] = m_new


def mlstm_scan(u, gates, b_if, *, n_ctx):
    NT = u.shape[0]
    H, T = ML_HEADS, ML_CHUNK
    D = u.shape[1] // 3
    dqk, dv = D // (2 * H), D // H
    nc, ncc = NT // T, n_ctx // T
    ncl = nc - ncc
    g4 = gates[:, :4 * H].T
    g_col = g4.reshape(4 * H, NT, 1)
    g_row = g4.reshape(4 * H, nc, 1, T)

    def chunk(d, t):
        fwd = jnp.where(t < ncc, ncl + t, t - ncc)
        return jnp.where(d == 0, fwd, nc - 1 - t)

    kq, kk, kv = 0, (H * dqk) // dqk, (2 * H * dqk) // dv
    col_spec = lambda off: pl.BlockSpec((None, T, 1), lambda d, h, t: ((2 * d + off) * H + h, chunk(d, t), 0))
    row_spec = lambda off: pl.BlockSpec((None, None, 1, T), lambda d, h, t: ((2 * d + off) * H + h, chunk(d, t), 0, 0))
    return pl.pallas_call(
        functools.partial(_mlstm_kernel, T=T, dqk=dqk),
        grid=(2, H, nc),
        in_specs=[
            pl.BlockSpec(memory_space=pltpu.SMEM),
            pl.BlockSpec((T, dqk), lambda d, h, t: (chunk(d, t), kq + h)),
            pl.BlockSpec((T, dqk), lambda d, h, t: (chunk(d, t), kk + h)),
            pl.BlockSpec((T, dv), lambda d, h, t: (chunk(d, t), kv + h)),
            col_spec(0), col_spec(1), row_spec(0), row_spec(1),
        ],
        out_specs=pl.BlockSpec((None, T, dv), lambda d, h, t: (d, chunk(d, t), h)),
        out_shape=jax.ShapeDtypeStruct((2, NT, H * dv), F32),
        scratch_shapes=[pltpu.VMEM((dqk, dv), F32), pltpu.VMEM((1, dqk), F32), pltpu.VMEM((1, 1), F32)],
        compiler_params=_params("parallel", "parallel", "arbitrary"),
    )(b_if, u, u, u, g_col, g_col, g_row, g_row)


def _mlgate_kernel(hs_ref, o_ref, g_ref, out_ref, *, dv):
    D = out_ref.shape[1]
    for h in range(D // dv):
        sl = slice(h * dv, (h + 1) * dv)
        x = hs_ref[0, :, sl] + hs_ref[1, :, sl]
        r = lax.rsqrt(jnp.mean(jnp.square(x), axis=-1, keepdims=True) + RMS_EPS)
        out_ref[:, sl] = (x * r * g_ref[:, sl] * jax.nn.sigmoid(o_ref[:, sl])).astype(out_ref.dtype)


def mlstm_gate(hs, u, norm_g, *, o_col_block):
    _, NT, D = hs.shape
    tr = 128
    return pl.pallas_call(
        functools.partial(_mlgate_kernel, dv=D // ML_HEADS),
        grid=(NT // tr,),
        in_specs=[
            pl.BlockSpec((2, tr, D), lambda i: (0, i, 0)),
            pl.BlockSpec((tr, D), lambda i: (i, o_col_block)),
            pl.BlockSpec((1, D), lambda i: (0, 0)),
        ],
        out_specs=pl.BlockSpec((tr, D), lambda i: (i, 0)),
        out_shape=jax.ShapeDtypeStruct((NT, D), BF16),
        compiler_params=_params("parallel"),
    )(hs, u, norm_g.reshape(1, D))


SAFE_LOG2 = 64.0


def _dattn_kernel(lam_ref, q_ref, k_ref, v_ref, g_ref, o_ref, m_ref, l_ref, acc_ref, kn_ref,
                  *, tk, n_chunks, out_scale, lambda_init):
    dh = DA_HEAD_DIM
    nlt = tk // LANES
    l_ref[...] = jnp.zeros_like(l_ref)
    acc_ref[...] = jnp.zeros_like(acc_ref)

    @pl.when(pl.program_id(1) == 0)
    def _():
        for i in range(2):
            kk = k_ref[:, i * dh:(i + 1) * dh].astype(F32)
            k2 = jnp.max(jnp.sum(kk * kk, axis=1, keepdims=True), axis=0, keepdims=True)
            kn_ref[i] = jnp.broadcast_to(k2, kn_ref.shape[1:])

    bound2 = None
    for i in range(2):
        qq = q_ref[:, i * dh:(i + 1) * dh].astype(F32)
        b2 = jnp.max(jnp.sum(qq * qq, axis=1, keepdims=True), axis=0, keepdims=True) * kn_ref[i][0:1, 0:1]
        bound2 = b2 if bound2 is None else jnp.maximum(bound2, b2)
    scores_bounded = bound2[0, 0] <= SAFE_LOG2 * SAFE_LOG2

    def scores(c, i):
        kc = k_ref[pl.ds(pl.multiple_of(c * tk, tk), tk), i * dh:(i + 1) * dh]
        return _dot_nt(q_ref[:, i * dh:(i + 1) * dh], kc)

    @pl.when(scores_bounded)
    def _():
        m_ref[...] = jnp.zeros_like(m_ref)

    @pl.when(jnp.logical_not(scores_bounded))
    def _():
        m_ref[...] = jnp.full_like(m_ref, -jnp.inf)

        def max_pass(c, carry):
            for i in range(2):
                s = scores(c, i)
                m = m_ref[i]
                for t in range(nlt):
                    m = jnp.maximum(m, s[:, t * LANES:(t + 1) * LANES])
                m_ref[i] = m
            return carry

        lax.fori_loop(0, n_chunks, max_pass, 0)
        for i in range(2):
            m_ref[i] = jnp.broadcast_to(jnp.max(m_ref[i], axis=1, keepdims=True), m_ref.shape[1:])

    tq = q_ref.shape[0]

    def sum_pass(c, carry):
        vc = v_ref[pl.ds(pl.multiple_of(c * tk, tk), tk), :]
        ps = []
        for i in range(2):
            s = scores(c, i)
            m = m_ref[i]
            p = [jnp.exp2(s[:, t * LANES:(t + 1) * LANES] - m) for t in range(nlt)]
            l_ref[i] += functools.reduce(jnp.add, p)
            ps.append(jnp.concatenate(p, axis=1).astype(BF16))
        acc_ref[...] += _dot(jnp.concatenate(ps, axis=0), vc)
        return carry

    lax.fori_loop(0, n_chunks, sum_pass, 0)
    lam = (jnp.exp(jnp.sum(lam_ref[0:1, :] * lam_ref[1:2, :], axis=1, keepdims=True))
           - jnp.exp(jnp.sum(lam_ref[2:3, :] * lam_ref[3:4, :], axis=1, keepdims=True)) + lambda_init)
    l = [jnp.sum(l_ref[i], axis=1, keepdims=True) for i in range(2)]
    o = acc_ref[0:tq] / l[0] - lam * (acc_ref[tq:2 * tq] / l[1])
    r = lax.rsqrt(jnp.mean(jnp.square(o), axis=-1, keepdims=True) + RMS_EPS)
    o_ref[...] = (o * r * g_ref[...] * out_scale).astype(o_ref.dtype)


def diff_attention(qkv, lam4, subln_g, *, q_row0, n_q, kv_row0, kv_len, lambda_init):
    D = qkv.shape[1] // 3
    hd = 2 * DA_HEAD_DIM
    H = D // hd
    tq = _pick(n_q, (512, 256))
    tk = _pick(kv_len, (768, 512, 256))
    qb, kvb = q_row0 // tq, kv_row0 // kv_len
    assert kv_row0 % kv_len == 0 and q_row0 % tq == 0
    return pl.pallas_call(
        functools.partial(_dattn_kernel, tk=tk, n_chunks=kv_len // tk, out_scale=1.0 - lambda_init,
                          lambda_init=lambda_init),
        grid=(H, n_q // tq),
        in_specs=[
            pl.BlockSpec((4, DA_HEAD_DIM), lambda h, i: (0, 0)),
            pl.BlockSpec((tq, hd), lambda h, i: (i + qb, h)),
            pl.BlockSpec((kv_len, hd), lambda h, i: (kvb, H + h)),
            pl.BlockSpec((kv_len, hd), lambda h, i: (kvb, 2 * H + h)),
            pl.BlockSpec((1, hd), lambda h, i: (0, 0)),
        ],
        out_specs=pl.BlockSpec((tq, hd), lambda h, i: (i, h)),
        out_shape=jax.ShapeDtypeStruct((n_q, D), BF16),
        scratch_shapes=[pltpu.VMEM((2, tq, LANES), F32), pltpu.VMEM((2, tq, LANES), F32),
                        pltpu.VMEM((2 * tq, hd), F32), pltpu.VMEM((2, 8, LANES), F32)],
        compiler_params=_params("parallel", "arbitrary"),
    )(lam4, qkv, qkv, qkv, subln_g.reshape(1, hd))


def _rope_tables(L, scale):
    a = DA_HEAD_DIM // 2
    t = jnp.arange(L)
    pos = jnp.stack([t // GRID_W, t % GRID_W], -1).astype(F32)
    inv = ROPE_BASE ** (-jnp.arange(0, a, 2, dtype=F32) / a)
    ang = pos[:, :, None] * inv
    ang = jnp.concatenate([ang, ang], -1).reshape(L, DA_HEAD_DIM)
    return jnp.cos(ang) * scale, jnp.sin(ang) * scale


def _filter_kernel(zf_ref, t_ref, w1_ref, b1_ref, w2_ref, b2_ref, w3_ref, b3_ref, w4f_ref, w4b_ref, dl_ref,
                   o_ref, asum_ref, hdn_ref):
    i, j = pl.program_id(0), pl.program_id(1)

    @pl.when(j == 0)
    def _():
        hdn = jnp.sin(_dot(zf_ref[...].astype(BF16), w1_ref[...].astype(BF16)) + b1_ref[...])
        hdn = jnp.sin(_dot(hdn.astype(BF16), w2_ref[...].astype(BF16)) + b2_ref[...])
        hdn_ref[...] = jnp.sin(_dot(hdn.astype(BF16), w3_ref[...].astype(BF16)) + b3_ref[...]).astype(hdn_ref.dtype)

    hdn = hdn_ref[...]
    hf = _dot(hdn, w4f_ref[...].astype(BF16))
    hb = _dot(hdn, w4b_ref[...].astype(BF16))
    side = t_ref[:, 1:2]
    kern = jnp.where(side > 0.0, hf, jnp.where(side < 0.0, hb, 0.0)) * jnp.exp(-t_ref[:, 0:1] * dl_ref[...])
    o_ref[...] = kern

    @pl.when(i == 0)
    def _():
        asum_ref[j] = jnp.zeros(asum_ref.shape[1:], F32)

    asum_ref[j] += jnp.sum(jnp.abs(kern), axis=0, keepdims=True)


def hyena_filters(L, D, f_w1, f_b1, f_w2, f_b2, f_w3, f_b3, f_w4, *, t2_major_n2=None):
    t = jnp.linspace(0.0, 1.0, L, dtype=F32)[:, None]
    w = 2.0 * math.pi * jnp.arange(L, dtype=F32)[:, None] / L
    f = jnp.linspace(1e-4, HY_BANDS - 1, HY_BANDS, dtype=F32)[None]
    z = jnp.concatenate([t, jnp.cos(w * f), -jnp.sin(w * f)], -1)
    src = jnp.concatenate([jnp.arange(L), jnp.zeros((1,), jnp.int32), L - 1 - jnp.arange(L - 1)])
    r = jnp.arange(2 * L)
    side = jnp.where(r < L, 1.0, jnp.where(r > L, -1.0, 0.0)).astype(F32)[:, None]
    if t2_major_n2 is not None:
        perm = r.reshape(2 * L // t2_major_n2, t2_major_n2).T.reshape(-1)
        src, side = src[perm], side[perm]
    zf = jnp.pad(z[src], ((0, 0), (0, LANES - HY_EMB)))
    tcol = jnp.concatenate([t[src], side], axis=1)
    deltas = jnp.abs(jnp.linspace(math.log(HY_DECAY_TARGET) / HY_SLOW_DECAY, math.log(HY_DECAY_TARGET) / HY_FAST_DECAY,
                                  D, dtype=F32)).reshape(1, D)
    w1 = jnp.pad(f_w1, ((0, LANES - HY_EMB), (0, 0)))
    nf = f_w1.shape[1]
    R2 = 2 * L
    tr = _pick(R2, (512,))
    tc = _pick(D, (1024, 512, 256, 128))
    cpd = D // tc
    ncol = HY_ORDER * cpd
    full = lambda shape: pl.BlockSpec(shape, lambda i, j: (0, 0))
    kern, asum = pl.pallas_call(
        _filter_kernel,
        grid=(R2 // tr, ncol),
        in_specs=[
            pl.BlockSpec((tr, LANES), lambda i, j: (i, 0)),
            pl.BlockSpec((tr, 2), lambda i, j: (i, 0)),
            full((LANES, nf)), full((1, nf)), full((nf, nf)), full((1, nf)), full((nf, nf)), full((1, nf)),
            pl.BlockSpec((nf, tc), lambda i, j: (0, (j // cpd) * 2 * cpd + j % cpd)),
            pl.BlockSpec((nf, tc), lambda i, j: (0, ((j // cpd) * 2 + 1) * cpd + j % cpd)),
            pl.BlockSpec((1, tc), lambda i, j: (0, j % cpd)),
        ],
        out_specs=[pl.BlockSpec((tr, tc), lambda i, j: (i, j)), pl.BlockSpec((ncol, 1, tc), lambda i, j: (0, 0, 0))],
        out_shape=[jax.ShapeDtypeStruct((R2, HY_ORDER * D), F32), jax.ShapeDtypeStruct((ncol, 1, tc), F32)],
        scratch_shapes=[pltpu.VMEM((tr, nf), BF16)],
        compiler_params=_params("arbitrary", "arbitrary"),
    )(zf, tcol, w1, f_b1.reshape(1, nf), f_w2, f_b2.reshape(1, nf), f_w3, f_b3.reshape(1, nf), f_w4, f_w4, deltas)
    return kern, asum.reshape(1, HY_ORDER * D)


FFT_N2 = 128
FFT_TT2 = 8


def _split(x):
    hi = x.astype(BF16)
    return hi, (x - hi.astype(F32)).astype(BF16)


def _dot3(w_hi, w_lo, x):
    x_hi, x_lo = _split(x)
    return _dot(w_hi, x_hi) + (_dot(w_lo, x_hi) + _dot(w_hi, x_lo))


def _hilo(a):
    return _split(jnp.asarray(a, F32))


def _fft_first_kernel(whi_ref, wlo_ref, z_ref, o_ref, *, t2_major):
    if t2_major:
        n1 = o_ref.shape[1]
        for j in range(o_ref.shape[2]):
            a = _dot3(whi_ref[...], wlo_ref[...], z_ref[j])
            o_ref[0, :, j, :] = a[:n1]
            o_ref[1, :, j, :] = a[n1:]
    else:
        l1, tt2, tc = z_ref.shape
        a = _dot3(whi_ref[...], wlo_ref[...], z_ref[...].reshape(l1 * tt2, tc))
        o_ref[...] = a.reshape(o_ref.shape)


def fft_first(w_hl, z3, *, col0, C, t2_major=False):
    w_hi, w_lo = w_hl
    tt2 = FFT_TT2
    n1x2, l1 = w_hi.shape if t2_major else (w_hi.shape[0] // tt2, w_hi.shape[1] // tt2)
    n1, n2 = n1x2 // 2, z3.shape[0 if t2_major else 1]
    tc = _pick(C, (512, 256, 128))
    cb = col0 // tc
    full = pl.BlockSpec(w_hi.shape, lambda t, j: (0, 0))
    if t2_major:
        z_spec = pl.BlockSpec((tt2, l1, tc), lambda t, j: (t, 0, j + cb))
    else:
        z_spec = pl.BlockSpec((l1, tt2, tc), lambda t, j: (0, t, j + cb))
    return pl.pallas_call(
        functools.partial(_fft_first_kernel, t2_major=t2_major),
        grid=(n2 // tt2, C // tc),
        in_specs=[full, full, z_spec],
        out_specs=pl.BlockSpec((2, n1, tt2, tc), lambda t, j: (0, 0, t, j)),
        out_shape=jax.ShapeDtypeStruct((2, n1, n2, C), F32),
        compiler_params=_params("parallel", "parallel"),
    )(w_hi, w_lo, z3)


def _fft_mid_kernel(*refs, g, conv):
    if conv:
        a_ref, kf_ref, twc_ref, tws_ref, w2h_ref, w2l_ref, w2ih_ref, w2il_ref, o_ref = refs
    else:
        a_ref, twc_ref, tws_ref, w2h_ref, w2l_ref, o_ref = refs
    n2 = a_ref.shape[2]
    for kk in range(g):
        ar, ai = a_ref[0, kk], a_ref[1, kk]
        c, s = twc_ref[kk], tws_ref[kk]
        x = _dot3(w2h_ref[...], w2l_ref[...], jnp.concatenate([ar * c + ai * s, ai * c - ar * s], axis=0))
        if conv:
            xr, xi = x[:n2], x[n2:]
            kr, ki = kf_ref[0, kk], kf_ref[1, kk]
            y = _dot3(w2ih_ref[...], w2il_ref[...], jnp.concatenate([xr * kr - xi * ki, xr * ki + xi * kr], axis=0))
            yr, yi = y[:n2], y[n2:]
            o_ref[0, kk] = yr * c - yi * s
            o_ref[1, kk] = yr * s + yi * c
        else:
            o_ref[0, kk] = x[:n2]
            o_ref[1, kk] = x[n2:]


def fft_mid(a, tw_c, tw_s, w2_hl, w2i_hl=None, kf=None, *, kf_col0=0):
    _, n1, n2, C = a.shape
    g = 4
    tc = _pick(C, (512, 256, 128))
    conv = kf is not None
    blk = pl.BlockSpec((2, g, n2, tc), lambda j, k: (0, k, 0, j))
    tw = pl.BlockSpec((g, n2, 1), lambda j, k: (k, 0, 0))
    mat = pl.BlockSpec((2 * n2, 2 * n2), lambda j, k: (0, 0))
    if conv:
        kb = kf_col0 // tc
        in_specs = [blk, pl.BlockSpec((2, g, n2, tc), lambda j, k: (0, k, 0, j + kb)), tw, tw, mat, mat, mat, mat]
        args = [a, kf, tw_c, tw_s, *w2_hl, *w2i_hl]
    else:
        in_specs = [blk, tw, tw, mat, mat]
        args = [a, tw_c, tw_s, *w2_hl]
    return pl.pallas_call(
        functools.partial(_fft_mid_kernel, g=g, conv=conv),
        grid=(C // tc, n1 // g),
        in_specs=in_specs,
        out_specs=blk,
        out_shape=jax.ShapeDtypeStruct(a.shape, F32),
        compiler_params=_params("parallel", "parallel"),
    )(*args)


def _fft_last_kernel(whi_ref, wlo_ref, b_ref, gate_ref, zp_ref, asum_ref, bias_ref, o_ref):
    inv_asum = 1.0 / asum_ref[...]
    for j in range(b_ref.shape[2]):
        bj = jnp.concatenate([b_ref[0, :, j, :], b_ref[1, :, j, :]], axis=0)
        x = _dot3(whi_ref[...], wlo_ref[...], bj)
        o_ref[:, j, :] = gate_ref[:, j, :] * (x * inv_asum + zp_ref[:, j, :] * bias_ref[...])


def fft_last(w_hl, b, gate3, gate_col0, zp3, zp_col0, asum, bias, *, vec_col0):
    w_hi, w_lo = w_hl
    l1 = w_hi.shape[0]
    _, n1, n2, C = b.shape
    tt2 = FFT_TT2
    tc = _pick(C, (512, 256, 128))
    gcb, zcb, vcb = gate_col0 // tc, zp_col0 // tc, vec_col0 // tc
    full = pl.BlockSpec((l1, 2 * n1), lambda t, j: (0, 0))
    return pl.pallas_call(
        _fft_last_kernel,
        grid=(n2 // tt2, C // tc),
        in_specs=[
            full, full,
            pl.BlockSpec((2, n1, tt2, tc), lambda t, j: (0, 0, t, j)),
            pl.BlockSpec((l1, tt2, tc), lambda t, j: (0, t, j + gcb)),
            pl.BlockSpec((l1, tt2, tc), lambda t, j: (0, t, j + zcb)),
            pl.BlockSpec((1, tc), lambda t, j: (0, j + vcb)),
            pl.BlockSpec((1, tc), lambda t, j: (0, j + vcb)),
        ],
        out_specs=pl.BlockSpec((l1, tt2, tc), lambda t, j: (0, t, j)),
        out_shape=jax.ShapeDtypeStruct((l1, n2, C), F32),
        compiler_params=_params("parallel", "parallel"),
    )(w_hi, w_lo, b, gate3, zp3, asum, bias)


def _hpmm_kernel(*refs, gated):
    if gated:
        ah_ref, al_ref, b_ref, gate_ref, z_ref, asum_ref, bias_ref, o_ref = refs
    else:
        ah_ref, al_ref, b_ref, o_ref = refs
    y = _dot3(ah_ref[...], al_ref[...], b_ref[...])
    if gated:
        y = gate_ref[...] * (y / asum_ref[...] + z_ref[...] * bias_ref[...])
    o_ref[...] = y


def hp_matmul(a_hl, b, *, gate=None, z=None, asum=None, bias=None, vec_col0=0):
    a_hi, a_lo = a_hl
    M, Kd = a_hi.shape
    N = b.shape[1]
    tn = _pick(N, (1024, 512, 256, 128))
    full = pl.BlockSpec((M, Kd), lambda j: (0, 0))
    in_specs = [full, full, pl.BlockSpec((Kd, tn), lambda j: (0, j))]
    args = [a_hi, a_lo, b]
    gated = gate is not None
    if gated:
        vb = vec_col0 // tn
        blk = pl.BlockSpec((M, tn), lambda j: (0, j))
        vec = pl.BlockSpec((1, tn), lambda j: (0, j + vb))
        in_specs += [blk, blk, vec, vec]
        args += [gate, z, asum, bias]
    return pl.pallas_call(
        functools.partial(_hpmm_kernel, gated=gated),
        grid=(N // tn,),
        in_specs=in_specs,
        out_specs=pl.BlockSpec((M, tn), lambda j: (0, j)),
        out_shape=jax.ShapeDtypeStruct((M, N), F32),
        compiler_params=_params("parallel"),
    )(*args)


def _cmul_kernel(x_ref, k_ref, o_ref):
    xr, xi, kr, ki = x_ref[0], x_ref[1], k_ref[0], k_ref[1]
    o_ref[0] = xr * kr - xi * ki
    o_ref[1] = xr * ki + xi * kr


def complex_mul(x, kf, *, kf_col0):
    _, R, C = x.shape
    tc = _pick(C, (512, 256, 128))
    kb = kf_col0 // tc
    return pl.pallas_call(
        _cmul_kernel,
        grid=(C // tc,),
        in_specs=[pl.BlockSpec((2, R, tc), lambda j: (0, 0, j)), pl.BlockSpec((2, R, tc), lambda j: (0, 0, j + kb))],
        out_specs=pl.BlockSpec((2, R, tc), lambda j: (0, 0, j)),
        out_shape=jax.ShapeDtypeStruct(x.shape, F32),
        compiler_params=_params("parallel"),
    )(x, kf)


def _dft_constants(n1, n2, l1):
    N = n1 * n2
    n1h = min(n1, -(-(n1 // 2 + 1) // 8) * 8)
    k1 = np.arange(n1h)[:, None]
    herm = np.where((k1 == 0) | (k1 == n1 // 2), 1.0, np.where(k1 < n1 // 2, 2.0, 0.0))
    ang1 = 2 * np.pi * k1 * np.arange(n1)[None, :] / n1
    w1 = np.concatenate([np.cos(ang1), -np.sin(ang1)], 0)
    w1_inv = np.concatenate([(herm * np.cos(ang1[:, :l1])).T, (-herm * np.sin(ang1[:, :l1])).T], 1) / N
    ang_t = 2 * np.pi * k1 * np.arange(n2)[None, :] / N
    k2 = np.arange(n2)[:, None]
    ang2 = 2 * np.pi * k2 * np.arange(n2)[None, :] / n2
    c2, s2 = np.cos(ang2), np.sin(ang2)
    f = lambda x: jnp.asarray(x, F32)
    return dict(w1_data=_hilo(np.kron(w1[:, :l1], np.eye(FFT_TT2))), w1_full=_hilo(w1), w1_inv=_hilo(w1_inv),
                tw_c=f(np.cos(ang_t))[:, :, None], tw_s=f(np.sin(ang_t))[:, :, None],
                w2=_hilo(np.block([[c2, s2], [-s2, c2]])), w2_inv=_hilo(np.block([[c2, -s2], [s2, c2]])))


def hyena_conv_lat(u3, kern, asum, bias, *, L, D):
    n2 = FFT_N2
    n1, l1 = 2 * L // n2, L // n2
    cst = _dft_constants(n1, n2, l1)
    ka = fft_first(cst["w1_full"], kern.reshape(n2, n1, HY_ORDER * D), col0=0, C=HY_ORDER * D, t2_major=True)
    kf = fft_mid(ka, cst["tw_c"], cst["tw_s"], cst["w2"])
    z3, zc0 = u3, 0
    for o in range(HY_ORDER):
        a = fft_first(cst["w1_data"], z3, col0=zc0, C=D)
        bmid = fft_mid(a, cst["tw_c"], cst["tw_s"], cst["w2"], cst["w2_inv"], kf, kf_col0=o * D)
        z3 = fft_last(cst["w1_inv"], bmid, u3, (o + 1) * D, z3, zc0, asum, bias, vec_col0=o * D)
        zc0 = 0
    return z3.reshape(L, D)


def _dense_dft_constants(L):
    N = 2 * L
    ang = 2 * np.pi * np.arange(N)[:, None] * np.arange(N)[None, :] / N
    wf = np.concatenate([np.cos(ang), -np.sin(ang)], 0)
    wi = np.concatenate([np.cos(ang[:L]), -np.sin(ang[:L])], 1) / N
    return _hilo(wf), _hilo(wf[:, :L]), _hilo(wi)


def hyena_conv_ctx(v, x1, x2, kern, asum, bias):
    L, D = v.shape
    wf, wf_data, wi = _dense_dft_constants(L)
    kf = hp_matmul(wf, kern).reshape(2, 2 * L, HY_ORDER * D)
    z = v
    for o, gate in enumerate((x1, x2)):
        xf = hp_matmul(wf_data, z).reshape(2, 2 * L, D)
        y = complex_mul(xf, kf, kf_col0=o * D).reshape(4 * L, D)
        z = hp_matmul(wi, y, gate=gate, z=z, asum=asum, bias=bias, vec_col0=o * D)
    return z


def _na_kernel(q_ref, k_ref, v_ref, *rest, n_ctx, rows, heads_per_step, rows_per_step):
    bias_refs, o_ref = rest[:rows_per_step], rest[rows_per_step]
    kr = NA_ROWS
    n_lat = rows * GRID_W
    dh = NA_HEAD_DIM
    scale = dh ** -0.5
    for rr in range(rows_per_step):
        r = pl.program_id(1) * rows_per_step + rr
        r0 = jnp.clip(r - kr // 2, 0, rows - kr)
        start = pl.multiple_of(r0 * GRID_W, GRID_W)
        qs = slice(rr * GRID_W, (rr + 1) * GRID_W)
        for hh in range(heads_per_step):
            sl = slice(hh * dh, (hh + 1) * dh)
            q = q_ref[qs, sl]
            s_ctx = _dot_nt(q, k_ref[n_lat:n_lat + n_ctx, sl]) * scale
            s_lat = _dot_nt(q, k_ref[pl.ds(start, kr * GRID_W), sl]) * scale + bias_refs[rr][hh]
            m = jnp.maximum(jnp.max(s_ctx, axis=1, keepdims=True), jnp.max(s_lat, axis=1, keepdims=True))
            p_ctx = jnp.exp(s_ctx - m)
            p_lat = jnp.exp(s_lat - m)
            l = jnp.sum(p_ctx, axis=1, keepdims=True) + jnp.sum(p_lat, axis=1, keepdims=True)
            o = (_dot(p_ctx.astype(BF16), v_ref[n_lat:n_lat + n_ctx, sl])
                 + _dot(p_lat.astype(BF16), v_ref[pl.ds(start, kr * GRID_W), sl]))
            o_ref[qs, sl] = (o / l).astype(o_ref.dtype)


def _na_bias_table(rpb, rows):
    H = rpb.shape[0]
    cols = jnp.arange(GRID_W)
    c0 = jnp.clip(cols - NA_COLS // 2, 0, GRID_W - NA_COLS)
    kc = jnp.arange(GRID_W)[None, :]
    inside = (kc >= c0[:, None]) & (kc < c0[:, None] + NA_COLS)
    rel = jnp.clip(kc - cols[:, None] + NA_COLS - 1, 0, 2 * NA_COLS - 2)
    rr = jnp.arange(NA_ROWS)[:, None] + jnp.arange(NA_ROWS)[None, :]
    t = rpb[:, rr]
    t = t[:, :, :, rel]
    t = jnp.where(inside[None, None, None], t, -1e30)
    t = jnp.transpose(t, (1, 0, 3, 2, 4))
    return t.reshape(NA_ROWS, H, GRID_W, NA_ROWS * GRID_W)


def na_attention(qkv, rpb, *, n_ctx):
    NT, D3 = qkv.shape
    D = D3 // 3
    L = NT - n_ctx
    rows = L // GRID_W
    hps = 4
    hw = hps * NA_HEAD_DIM
    nhb = D // hw
    table = _na_bias_table(rpb, rows)

    def pattern(r):
        r0 = jnp.clip(r - NA_ROWS // 2, 0, rows - NA_ROWS)
        return r0 - r + NA_ROWS - 1

    rps = 4
    bias_specs = [pl.BlockSpec((None, hps, GRID_W, NA_ROWS * GRID_W),
                               lambda h, r, rr=rr: (pattern(r * rps + rr), h, 0, 0)) for rr in range(rps)]
    return pl.pallas_call(
        functools.partial(_na_kernel, n_ctx=n_ctx, rows=rows, heads_per_step=hps, rows_per_step=rps),
        grid=(nhb, rows // rps),
        in_specs=[
            pl.BlockSpec((rps * GRID_W, hw), lambda h, r: (r, h)),
            pl.BlockSpec((NT, hw), lambda h, r: (0, nhb + h)),
            pl.BlockSpec((NT, hw), lambda h, r: (0, 2 * nhb + h)),
            *bias_specs,
        ],
        out_specs=pl.BlockSpec((rps * GRID_W, hw), lambda h, r: (r, h)),
        out_shape=jax.ShapeDtypeStruct((L, D), BF16),
        compiler_params=_params("parallel", "parallel"),
    )(qkv, qkv, qkv, *([table] * rps))


def _router_kernel(x_ref, w_ref, b_ref, o_ref):
    E, G = N_EXPERTS, N_GROUPS
    per = E // G
    logits = _dot_nt(w_ref[...].astype(BF16), x_ref[...])
    scores = jax.nn.sigmoid(logits)
    sel = scores + b_ref[...]
    tm = sel.shape[1]
    grp = sel.reshape(G, per, tm)
    eidx = lax.broadcasted_iota(jnp.int32, grp.shape, 1)
    m1 = jnp.max(grp, axis=1, keepdims=True)
    first = jnp.min(jnp.where(grp == m1, eidx, per), axis=1, keepdims=True)
    m2 = jnp.max(jnp.where(eidx == first, -jnp.inf, grp), axis=1, keepdims=True)
    gs = (m1 + m2).reshape(G, tm)
    gidx = lax.broadcasted_iota(jnp.int32, gs.shape, 0)
    gmask = jnp.zeros(gs.shape, jnp.bool_)
    for _ in range(TOPK_GROUPS):
        mx = jnp.max(gs, axis=0, keepdims=True)
        pick = gidx == jnp.min(jnp.where(gs == mx, gidx, G), axis=0, keepdims=True)
        gmask = jnp.logical_or(gmask, pick)
        gs = jnp.where(pick, -jnp.inf, gs)
    emask = jnp.broadcast_to(gmask.reshape(G, 1, tm), (G, per, tm)).reshape(E, tm)
    cand = jnp.where(emask, sel, -jnp.inf)
    xidx = lax.broadcasted_iota(jnp.int32, cand.shape, 0)
    chosen = jnp.zeros(cand.shape, jnp.bool_)
    for _ in range(TOP_K):
        mx = jnp.max(cand, axis=0, keepdims=True)
        pick = xidx == jnp.min(jnp.where(cand == mx, xidx, E), axis=0, keepdims=True)
        chosen = jnp.logical_or(chosen, pick)
        cand = jnp.where(pick, -jnp.inf, cand)
    wsel = jnp.where(chosen, scores, 0.0)
    gates = wsel / jnp.sum(wsel, axis=0, keepdims=True) * ROUTED_SCALE
    o_ref[...] = gates.T


def moe_router(h, router_w_t, router_b, layer):
    T, D = h.shape
    E = router_w_t.shape[1]
    tm = _pick(T, (256, 128))
    return pl.pallas_call(
        _router_kernel,
        grid=(T // tm,),
        in_specs=[
            pl.BlockSpec((tm, D), lambda i: (i, 0)),
            pl.BlockSpec((None, E, D), lambda i: (layer, 0, 0)),
            pl.BlockSpec((None, E, 1), lambda i: (layer, 0, 0)),
        ],
        out_specs=pl.BlockSpec((tm, E), lambda i: (i, 0)),
        out_shape=jax.ShapeDtypeStruct((T, E), F32),
        compiler_params=_params("parallel"),
    )(h, router_w_t, router_b.reshape(-1, E, 1))


def _moe_up_kernel(*refs, nk, ne, gated, F):
    if gated:
        x_ref, wg_ref, wu_ref, g_ref, o_ref, accg_ref, accu_ref = refs
    else:
        x_ref, wg_ref, wu_ref, o_ref, accg_ref, accu_ref = refs
    e0 = pl.program_id(1) * ne
    k = pl.program_id(2)

    @pl.when(k == 0)
    def _():
        accg_ref[...] = jnp.zeros_like(accg_ref)
        accu_ref[...] = jnp.zeros_like(accu_ref)

    def stacked_experts(ref):
        w = ref[0] if ne == 1 else jnp.concatenate([ref[ee] for ee in range(ne)], axis=0)
        return w.astype(BF16)

    x = x_ref[...]
    accg_ref[...] += _dot_nt(x, stacked_experts(wg_ref))
    accu_ref[...] += _dot_nt(x, stacked_experts(wu_ref))

    @pl.when(k == nk - 1)
    def _():
        a = accg_ref[...]
        hid = a * jax.nn.sigmoid(a) * accu_ref[...]
        if gated:
            g = g_ref[...]
            lane = lax.broadcasted_iota(jnp.int32, g.shape, 1)
            col = lax.broadcasted_iota(jnp.int32, (1, ne * F), 1) // F
            gexp = jnp.zeros_like(hid)
            for ee in range(ne):
                ge = jnp.sum(jnp.where(lane == e0 + ee, g, 0.0), axis=1, keepdims=True)
                gexp = jnp.where(col == ee, ge, gexp)
            hid = hid * gexp
        o_ref[...] = hid.astype(o_ref.dtype)


def moe_up(x, w_gate, w_up, layer, gates=None):
    T, D = x.shape
    if w_gate.ndim == 3:
        w_gate, w_up = w_gate[:, None], w_up[:, None]
    w_gate, w_up = jnp.swapaxes(w_gate, 2, 3), jnp.swapaxes(w_up, 2, 3)
    E, F = w_gate.shape[1], w_gate.shape[2]
    ne = 4 if E % 4 == 0 else 1
    tm = _pick(T, (1056, 1024, 512, 256, 128))
    tk = _pick(D, (2048, 1024, 512, 256, 128))
    nk = D // tk
    gated = gates is not None
    w_spec = pl.BlockSpec((None, ne, F, tk), lambda i, e, k: (layer, e, 0, k))
    in_specs = [pl.BlockSpec((tm, tk), lambda i, e, k: (i, k)), w_spec, w_spec]
    args = [x, w_gate, w_up]
    if gated:
        in_specs.append(pl.BlockSpec((tm, E), lambda i, e, k: (i, 0)))
        args.append(gates)
    return pl.pallas_call(
        functools.partial(_moe_up_kernel, nk=nk, ne=ne, gated=gated, F=F),
        grid=(T // tm, E // ne, nk),
        in_specs=in_specs,
        out_specs=pl.BlockSpec((tm, ne * F), lambda i, e, k: (i, e)),
        out_shape=jax.ShapeDtypeStruct((T, E * F), BF16),
        scratch_shapes=[pltpu.VMEM((tm, ne * F), F32), pltpu.VMEM((tm, ne * F), F32)],
        compiler_params=_params("parallel", "parallel", "arbitrary"),
    )(*args)


def moe(h, layer, router_w_t, router_b, w_gate, w_up, w_down, s_gate, s_up, s_down):
    gates = moe_router(h, router_w_t, router_b, layer)
    hid = moe_up(h, w_gate, w_up, layer, gates)
    hid_sh = moe_up(h, s_gate, s_up, layer)
    E, F, D = w_down.shape[1:]
    return matmul(hid, w_down.reshape(DEPTH, E * F, D), out_dtype=F32, layer=layer, extra=(hid_sh, s_down, layer))


def kernel(x, c, ctx, c_ctx, ada_a, ada_b, ada_bias, ln1_g, ln1_b, ln2_g, ln2_b, router_w, router_b, moe_w_gate, moe_w_up, moe_w_down, sh_w_gate, sh_w_up, sh_w_down, ml_w_in, ml_b_if, ml_norm_g, ml_w_out, da_w_qkv, da_lam_q1, da_lam_k1, da_lam_q2, da_lam_k2, da_subln_g, da_w_out, hy_w_in, hy_conv_w, hy_conv_b, hy_f_w1, hy_f_b1, hy_f_w2, hy_f_b2, hy_f_w3, hy_f_b3, hy_f_w4, hy_bias, hy_w_out, na_w_qkv, na_rpb, na_w_out):
    _, L, D = x.shape
    C = ctx.shape[1]
    NT = C + L
    s = jnp.concatenate([x[0], ctx[0]], axis=0)

    cc = jnp.zeros((8, D), F32).at[0].set(c_ctx).at[1].set(c[0])
    mods = ada_modulation(cc, ada_a, ada_b, ada_bias)[:, :2].reshape(DEPTH, 2, 6, 1, D)
    SH1, SC1, G1, SH2, SC2, G2 = range(6)
    router_w_t = jnp.swapaxes(router_w, 1, 2)
    moe_args = (router_w_t, router_b, moe_w_gate, moe_w_up, moe_w_down, sh_w_gate, sh_w_up, sh_w_down)

    (h,) = ln_modulate(s, mods, n_ctx=C, h_layer=0, sc_idx=SC1, sh_idx=SH1)
    for i in range(DEPTH):
        last = i == DEPTH - 1
        if i == 0:
            y = mlstm_mixer(h, ml_w_in, ml_b_if[0], ml_norm_g[0], ml_w_out, n_ctx=C)
        elif i == 1:
            lam4 = jnp.stack([da_lam_q1[0], da_lam_k1[0], da_lam_q2[0], da_lam_k2[0]])
            y = diff_attn_mixer(h, da_w_qkv, lam4, da_subln_g[0], da_w_out, n_ctx=C,
                                lambda_init=0.8 - 0.6 * math.exp(-0.3 * i))
        elif i == 2:
            y = hyena_mixer(h, hy_w_in, hy_conv_w[0], hy_conv_b[0], hy_f_w1[0], hy_f_b1[0], hy_f_w2[0], hy_f_b2[0],
                            hy_f_w3[0], hy_f_b3[0], hy_f_w4[0], hy_bias[0], hy_w_out, n_ctx=C)
        else:
            y = na_mixer(h, na_w_qkv, na_rpb[0], na_w_out, n_ctx=C)
        n_ctx = C
        if last:
            s, n_ctx = s[:L], 0
        s, h = ln_modulate(s, mods, n_ctx=n_ctx, ys=(y,), ln_layer=i, gate_idx=G1, ln_g=ln1_g, ln_b=ln1_b,
                           h_layer=i, sc_idx=SC2, sh_idx=SH2)
        y = moe(h, i, *moe_args)
        if last:
            (s,) = ln_modulate(s, mods, n_ctx=n_ctx, ys=(y,), ln_layer=i, gate_idx=G2, ln_g=ln2_g, ln_b=ln2_b)
        else:
            s, h = ln_modulate(s, mods, n_ctx=n_ctx, ys=(y,), ln_layer=i, gate_idx=G2, ln_g=ln2_g, ln_b=ln2_b,
                               h_layer=i + 1, sc_idx=SC1, sh_idx=SH1)
    return s[None]


def mlstm_mixer(h, w_in, b_if, norm_g, w_out, *, n_ctx):
    D = h.shape[1]
    w_in_t = jnp.swapaxes(w_in, 1, 2)
    u = matmul(h, w_in_t, out_dtype=F32, layer=0, ncols=3 * D, w_transposed=True)
    w_if_t = jnp.pad(w_in_t[0, 3 * D:], ((0, LANES - 4 * ML_HEADS), (0, 0)))
    g_if = matmul(h, w_if_t, out_dtype=F32, w_transposed=True)
    hs = mlstm_scan(u, g_if, b_if, n_ctx=n_ctx)
    return matmul(mlstm_gate(hs, u, norm_g, o_col_block=2), w_out, out_dtype=F32, layer=0)


def diff_attn_mixer(h, w_qkv, lam4, subln_g, w_out, *, n_ctx, lambda_init):
    NT, D = h.shape
    L = NT - n_ctx
    scale = DA_HEAD_DIM ** -0.5 * math.log2(math.e)
    cos, sin = _rope_tables(L, 1.0)
    ones, zeros = jnp.ones((n_ctx, LANES), F32), jnp.zeros((n_ctx, LANES), F32)
    cos_k, sin_k = jnp.concatenate([cos, ones]), jnp.concatenate([sin, zeros])
    cos3 = jnp.stack([cos_k * scale, cos_k, jnp.ones_like(cos_k)])
    sin3 = jnp.stack([sin_k * scale, sin_k, jnp.zeros_like(sin_k)])
    qkv = matmul(h, w_qkv, out_dtype=BF16, layer=0, rope=(cos3, sin3, D))
    o_lat = diff_attention(qkv, lam4, subln_g, q_row0=0, n_q=L, kv_row0=0, kv_len=NT, lambda_init=lambda_init)
    o_ctx = diff_attention(qkv, lam4, subln_g, q_row0=L, n_q=n_ctx, kv_row0=L, kv_len=n_ctx, lambda_init=lambda_init)
    return matmul(jnp.concatenate([o_lat, o_ctx]), w_out, out_dtype=F32, layer=0)


def hyena_mixer(h, w_in, conv_w, conv_b, f_w1, f_b1, f_w2, f_b2, f_w3, f_b3, f_w4, hy_bias, w_out, *, n_ctx):
    NT, D = h.shape
    L = NT - n_ctx
    u = matmul(h, w_in, out_dtype=F32, layer=0, conv=(conv_w, conv_b, (0, L, NT)))
    fargs = (f_w1, f_b1, f_w2, f_b2, f_w3, f_b3, f_w4)
    bias = hy_bias.reshape(1, HY_ORDER * D)
    kern_l, asum_l = hyena_filters(L, D, *fargs, t2_major_n2=FFT_N2)
    z_lat = hyena_conv_lat(u.reshape(NT // FFT_N2, FFT_N2, 3 * D), kern_l, asum_l, bias, L=L, D=D)
    kern_c, asum_c = hyena_filters(n_ctx, D, *fargs)
    v, x1, x2 = (u[L:, j * D:(j + 1) * D] for j in range(3))
    z_ctx = hyena_conv_ctx(v, x1, x2, kern_c, asum_c, bias)
    return matmul(jnp.concatenate([z_lat, z_ctx]), w_out, out_dtype=F32, layer=0)


def na_mixer(h, w_qkv, rpb, w_out, *, n_ctx):
    qkv = matmul(h, w_qkv, out_dtype=BF16, layer=0)
    return matmul(na_attention(qkv, rpb, n_ctx=n_ctx), w_out, out_dtype=F32, layer=0)
```

```python
import functools
import math

import numpy as np
import jax
import jax.numpy as jnp
from jax import lax
from jax.experimental import pallas as pl
from jax.experimental.pallas import tpu as pltpu

F32 = jnp.float32
BF16 = jnp.bfloat16

V7X_VMEM_LIMIT_BYTES = 56 * 1024 * 1024
LANES = 128

GRID_W = 64
DEPTH = 4
DEEPNORM_ALPHA = (2 * DEPTH) ** 0.25
LN_EPS = 1e-5
RMS_EPS = 1e-6
ROPE_BASE = 10000.0
ML_HEADS = 8
ML_CHUNK = 256
DA_HEAD_DIM = 128
HY_ORDER = 2
HY_EMB = 33
HY_BANDS = (HY_EMB - 1) // 2
HY_FAST_DECAY = 0.3
HY_SLOW_DECAY = 1.5
HY_DECAY_TARGET = 1e-2
NA_HEAD_DIM = 128
NA_ROWS = 8
NA_COLS = 16
N_EXPERTS = 64
TOP_K = 8
N_GROUPS = 8
TOPK_GROUPS = 4
ROUTED_SCALE = 2.5


def _params(*sem):
    return pltpu.CompilerParams(dimension_semantics=sem, vmem_limit_bytes=V7X_VMEM_LIMIT_BYTES)


def _dot(a, b):
    return jnp.dot(a, b, preferred_element_type=F32)


def _dot_nt(a, b):
    return lax.dot_general(a, b, (((1,), (1,)), ((), ())), preferred_element_type=F32)


def _dot_tn(a, b):
    return lax.dot_general(a, b, (((0,), (0,)), ((), ())), preferred_element_type=F32)


def _pick(n, prefs):
    for p in prefs:
        if n % p == 0:
            return p
    return n


def _mm_kernel(*refs, nk, has_extra):
    if has_extra:
        a_ref, w_ref, a2_ref, w2_ref, o_ref, acc_ref = refs
    else:
        a_ref, w_ref, o_ref, acc_ref = refs
    k = pl.program_id(2)

    @pl.when(k == 0)
    def _():
        acc_ref[...] = jnp.zeros_like(acc_ref)

    acc_ref[...] += _dot(a_ref[...].astype(BF16), w_ref[...].astype(BF16))

    @pl.when(k == nk - 1)
    def _():
        acc = acc_ref[...]
        if has_extra:
            acc = acc + _dot(a2_ref[...].astype(BF16), w2_ref[...].astype(BF16))
        o_ref[...] = acc.astype(o_ref.dtype)


def _rope(x, cos, sin):
    lane = lax.broadcasted_iota(jnp.int32, x.shape, 1)
    rot = jnp.where((lane % 64) < 32, -pltpu.roll(x, LANES - 32, 1), pltpu.roll(x, 32, 1))
    return x * cos + rot * sin


def _mm_resident_kernel(a_ref, w_ref, *rest, w_transposed, rope, conv_seq_starts):
    o_ref = rest[-1]
    dot = _dot_nt if w_transposed else _dot
    w = w_ref[...].astype(BF16)
    acc = dot(a_ref[...].astype(BF16), w)
    if rope:
        cos, sin = rest[0][...], rest[1][...]
        for g in range(acc.shape[1] // LANES):
            sl = slice(g * LANES, (g + 1) * LANES)
            o_ref[:, sl] = _rope(acc[:, sl], cos, sin).astype(o_ref.dtype)
    elif conv_seq_starts is not None:
        prev_ref, next_ref, cw_ref, cb_ref = rest[:4]
        tm = acc.shape[0]
        p_prev = dot(prev_ref[...].astype(BF16), w)[7:8]
        p_next = dot(next_ref[...].astype(BF16), w)[0:1]
        local = lax.broadcasted_iota(jnp.int32, (tm, 1), 0)
        row = local + pl.program_id(0) * tm
        is_start = functools.reduce(jnp.logical_or, [row == r for r in conv_seq_starts[:-1]])
        is_end = functools.reduce(jnp.logical_or, [row == r - 1 for r in conv_seq_starts[1:]])
        xm = jnp.where(local == 0, p_prev, pltpu.roll(acc, 1, 0))
        xp = jnp.where(local == tm - 1, p_next, pltpu.roll(acc, tm - 1, 0))
        xm = jnp.where(is_start, 0.0, xm)
        xp = jnp.where(is_end, 0.0, xp)
        o_ref[...] = (xm * cw_ref[0:1, :] + acc * cw_ref[1:2, :] + xp * cw_ref[2:3, :] + cb_ref[...]).astype(o_ref.dtype)
    else:
        o_ref[...] = acc.astype(o_ref.dtype)


def _matmul_resident(a, w, *, out_dtype, layer, col0, N, w_transposed, rope, conv):
    M, K = a.shape
    tm = _pick(M, (2112, 2048, 1024, 512, 256, 128) if a.dtype == BF16 else (1056, 1024, 512, 256, 128))
    tn = _pick(N, (256, 128))
    cb = col0 // tn
    lead = (None,) if w.ndim == 3 else ()
    lidx = (layer,) if w.ndim == 3 else ()
    if w_transposed:
        w_spec = pl.BlockSpec(lead + (tn, K), lambda i, j: lidx + (j + cb, 0))
    else:
        w_spec = pl.BlockSpec(lead + (K, tn), lambda i, j: lidx + (0, j + cb))
    in_specs = [pl.BlockSpec((tm, K), lambda i, j: (i, 0), pipeline_mode=pl.Buffered(1)), w_spec]
    args = [a, w]
    conv_seq_starts = None
    if rope is not None:
        cos, sin, part_cols = rope
        bpp = part_cols // tn
        tab = pl.BlockSpec((None, tm, LANES), lambda i, j: (j // bpp, i, 0))
        in_specs += [tab, tab]
        args += [cos, sin]
    elif conv is not None:
        conv_w, conv_b, conv_seq_starts = conv
        t8, n8 = tm // 8, M // 8
        in_specs += [pl.BlockSpec((8, K), lambda i, j: (jnp.maximum(i * t8 - 1, 0), 0)),
                     pl.BlockSpec((8, K), lambda i, j: (jnp.minimum((i + 1) * t8, n8 - 1), 0)),
                     pl.BlockSpec((3, tn), lambda i, j: (0, j)), pl.BlockSpec((1, tn), lambda i, j: (0, j))]
        args += [a, a, conv_w, conv_b.reshape(1, N)]
    return pl.pallas_call(
        functools.partial(_mm_resident_kernel, w_transposed=w_transposed, rope=rope is not None,
                          conv_seq_starts=conv_seq_starts),
        grid=(M // tm, N // tn),
        in_specs=in_specs,
        out_specs=pl.BlockSpec((tm, tn), lambda i, j: (i, j)),
        out_shape=jax.ShapeDtypeStruct((M, N), out_dtype),
        compiler_params=_params("parallel", "arbitrary"),
    )(*args)


def matmul(a, w, *, out_dtype, layer=None, col0=0, ncols=None, extra=None, row0=0, nrows=None, w_transposed=False,
           rope=None, conv=None):
    M = a.shape[0] if nrows is None else nrows
    K = a.shape[1]
    N = (w.shape[-2] if w_transposed else w.shape[-1]) if ncols is None else ncols
    if K <= 4096 and extra is None and nrows is None and M >= 1024:
        return _matmul_resident(a, w, out_dtype=out_dtype, layer=layer, col0=col0, N=N, w_transposed=w_transposed,
                                rope=rope, conv=conv)
    assert not w_transposed and rope is None and conv is None
    tm = _pick(M, (1056, 1024, 512, 256, 128, 64, 32, 16, 8))
    tn = _pick(N, (1024, 512, 256, 128))
    tk = _pick(K, (2048, 1024, 512, 256, 128))
    assert col0 % tn == 0 and row0 % tm == 0
    cb, rb = col0 // tn, row0 // tm
    nk = K // tk
    if w.ndim == 3:
        w_spec = pl.BlockSpec((None, tk, tn), lambda i, j, k: (layer, k, j + cb))
    else:
        w_spec = pl.BlockSpec((tk, tn), lambda i, j, k: (k, j + cb))
    in_specs = [pl.BlockSpec((tm, tk), lambda i, j, k: (i + rb, k)), w_spec]
    args = [a, w]
    if extra is not None:
        a2, w2, layer2 = extra
        k2 = a2.shape[1]
        in_specs.append(pl.BlockSpec((tm, k2), lambda i, j, k: (i + rb, 0)))
        in_specs.append(pl.BlockSpec((None, k2, tn), lambda i, j, k: (layer2, 0, j)))
        args += [a2, w2]
    return pl.pallas_call(
        functools.partial(_mm_kernel, nk=nk, has_extra=extra is not None),
        grid=(M // tm, N // tn, nk),
        in_specs=in_specs,
        out_specs=pl.BlockSpec((tm, tn), lambda i, j, k: (i, j)),
        out_shape=jax.ShapeDtypeStruct((M, N), out_dtype),
        scratch_shapes=[pltpu.VMEM((tm, tn), F32)],
        compiler_params=_params("parallel", "parallel", "arbitrary"),
    )(*args)


def _ada_kernel(cc_ref, a_ref, b_ref, bias_ref, o_ref):
    cc = cc_ref[...]
    act = cc * jax.nn.sigmoid(cc)
    t = _dot(act.astype(BF16), a_ref[...].astype(BF16))
    o_ref[...] = _dot(t.astype(BF16), b_ref[...].astype(BF16)) + bias_ref[...]


def ada_modulation(cc, ada_a, ada_b, ada_bias):
    depth, D, R = ada_a.shape
    n6 = ada_b.shape[-1]
    tn = D
    return pl.pallas_call(
        _ada_kernel,
        grid=(depth, n6 // tn),
        in_specs=[
            pl.BlockSpec((8, D), lambda l, n: (0, 0)),
            pl.BlockSpec((None, D, R), lambda l, n: (l, 0, 0)),
            pl.BlockSpec((None, R, tn), lambda l, n: (l, 0, n)),
            pl.BlockSpec((None, 1, tn), lambda l, n: (l, 0, n)),
        ],
        out_specs=pl.BlockSpec((None, 8, tn), lambda l, n: (l, 0, n)),
        out_shape=jax.ShapeDtypeStruct((depth, 8, n6), F32),
        compiler_params=_params("parallel", "parallel"),
    )(cc, ada_a, ada_b, ada_bias.reshape(depth, 1, n6))


def _lnmod_kernel(*refs, has_ln, has_h, n_y):
    it = iter(refs)
    s_ref = next(it)
    if has_ln:
        y_refs = [next(it) for _ in range(n_y)]
        gate_ref, lng_ref, lnb_ref = next(it), next(it), next(it)
    if has_h:
        sc_ref, sh_ref = next(it), next(it)
    if has_ln:
        so_ref = next(it)
    if has_h:
        h_ref = next(it)
    x = s_ref[...]
    if has_ln:
        y = y_refs[0][...].astype(F32)
        for r in y_refs[1:]:
            y = y + r[...].astype(F32)
        v = DEEPNORM_ALPHA * x + gate_ref[...] * y
        mu = jnp.mean(v, axis=-1, keepdims=True)
        var = jnp.mean(jnp.square(v - mu), axis=-1, keepdims=True)
        x = (v - mu) * lax.rsqrt(var + LN_EPS) * lng_ref[...] + lnb_ref[...]
        so_ref[...] = x
    if has_h:
        h_ref[...] = (x * (1.0 + sc_ref[...]) + sh_ref[...]).astype(h_ref.dtype)


def ln_modulate(s, mods, *, n_ctx, ys=(), ln_layer=None, gate_idx=None, ln_g=None, ln_b=None,
                h_layer=None, sc_idx=None, sh_idx=None):
    R, D = s.shape
    tr = 128
    has_ln, has_h = len(ys) > 0, h_layer is not None
    n_lat_blocks = (R - n_ctx) // tr
    row_spec = pl.BlockSpec((tr, D), lambda i: (i, 0))

    def mod_spec(layer, idx):
        return pl.BlockSpec((None, None, None, 1, D),
                            lambda i: (layer, jnp.where(i >= n_lat_blocks, 0, 1), idx, 0, 0))

    vec_spec = pl.BlockSpec((None, 1, D), lambda i: (ln_layer, 0, 0))
    in_specs, args = [row_spec], [s]
    if has_ln:
        in_specs += [row_spec] * len(ys) + [mod_spec(ln_layer, gate_idx), vec_spec, vec_spec]
        args += list(ys) + [mods, ln_g.reshape(DEPTH, 1, D), ln_b.reshape(DEPTH, 1, D)]
    if has_h:
        in_specs += [mod_spec(h_layer, sc_idx), mod_spec(h_layer, sh_idx)]
        args += [mods, mods]
    out_specs, out_shape = [], []
    if has_ln:
        out_specs.append(row_spec)
        out_shape.append(jax.ShapeDtypeStruct((R, D), F32))
    if has_h:
        out_specs.append(row_spec)
        out_shape.append(jax.ShapeDtypeStruct((R, D), BF16))
    outs = pl.pallas_call(
        functools.partial(_lnmod_kernel, has_ln=has_ln, has_h=has_h, n_y=len(ys)),
        grid=(R // tr,),
        in_specs=in_specs,
        out_specs=out_specs,
        out_shape=out_shape,
        compiler_params=_params("parallel"),
    )(*args)
    return outs


def _log_sigmoid(x):
    return jnp.minimum(x, 0.0) - jnp.log(1.0 + jnp.exp(-jnp.abs(x)))


def _mlstm_kernel(bias_ref, q_ref, k_ref, v_ref, ic_ref, fc_ref, ir_ref, fr_ref, o_ref, ct_ref, n_ref, m_ref,
                  *, T, dqk, dv, hps):
    d, t = pl.program_id(0), pl.program_id(2)

    @pl.when(t == 0)
    def _():
        ct_ref[...] = jnp.zeros_like(ct_ref)
        n_ref[...] = jnp.zeros_like(n_ref)
        m_ref[...] = jnp.zeros_like(m_ref)

    for hh in range(hps):
        _mlstm_head(d, pl.program_id(1) * hps + hh, hh, bias_ref, q_ref, k_ref, v_ref, ic_ref, fc_ref, ir_ref, fr_ref,
                    o_ref, ct_ref, n_ref, m_ref, T=T, dqk=dqk, dv=dv)


def _mlstm_head(d, h, hh, bias_ref, q_ref, k_ref, v_ref, ic_ref, fc_ref, ir_ref, fr_ref, o_ref, ct_ref, n_ref, m_ref,
                *, T, dqk, dv):
    qs, vs = slice(hh * dqk, (hh + 1) * dqk), slice(hh * dv, (hh + 1) * dv)
    bi = bias_ref[d * 2 * ML_HEADS + h]
    bf = bias_ref[(d * 2 + 1) * ML_HEADS + h]
    i_c = ic_ref[hh] + bi
    i_r = ir_ref[hh] + bi
    f_c = _log_sigmoid(fc_ref[hh] + bf)
    f_r = _log_sigmoid(fr_ref[hh] + bf)
    row = lax.broadcasted_iota(jnp.int32, (T, T), 0)
    col = lax.broadcasted_iota(jnp.int32, (T, T), 1)
    sgn = 1 - 2 * d
    incl = (col - row) * sgn <= 0
    incl_t = (row - col) * sgn <= 0
    b_c = jnp.sum(jnp.where(incl, f_r, 0.0), axis=1, keepdims=True)
    b_r = jnp.sum(jnp.where(incl_t, f_c, 0.0), axis=0, keepdims=True)
    f_tot = jnp.sum(f_r, axis=1, keepdims=True)
    m_prev = m_ref[hh]
    dmat = jnp.where(incl, b_c - b_r + i_r, -jnp.inf)
    inter = b_c + m_prev
    m_t = jnp.maximum(inter, jnp.max(dmat, axis=1, keepdims=True))
    w_intra = jnp.exp(dmat - m_t)
    w_inter = jnp.exp(inter - m_t)
    q = q_ref[:, qs] * (dqk ** -0.5)
    k = k_ref[:, qs]
    v = v_ref[:, vs]
    qb, kb, vb = q.astype(BF16), k.astype(BF16), v.astype(BF16)
    s = _dot_nt(qb, kb) * w_intra
    ct = ct_ref[hh]
    n = n_ref[hh]
    num = w_inter * _dot(qb, ct.astype(BF16)) + _dot(s.astype(BF16), vb)
    den = w_inter * jnp.sum(q * n, axis=1, keepdims=True) + jnp.sum(s, axis=1, keepdims=True)
    o_ref[:, vs] = num / jnp.maximum(jnp.abs(den), jnp.exp(-m_t))
    g_r = f_tot - b_r + i_r
    g_c = f_tot - b_c + i_c
    m_new = jnp.maximum(f_tot + m_prev, jnp.max(g_r, axis=1, keepdims=True))
    decay = jnp.exp(f_tot + m_prev - m_new)
    wk = jnp.exp(g_c - m_new)
    ct_ref[hh] = decay * ct + _dot_tn(kb, (wk * v).astype(BF16))
    n_ref[hh] = decay * n + jnp.sum(wk * k, axis=0, keepdims=True)
    m_ref[hh] = m_new


def mlstm_scan(u, gates, b_if, *, n_ctx):
    NT = u.shape[0]
    H, T = ML_HEADS, ML_CHUNK
    D = u.shape[1] // 3
    dqk, dv = D // (2 * H), D // H
    nc, ncc = NT // T, n_ctx // T
    ncl = nc - ncc
    g4 = gates[:, :4 * H].T
    g_col = g4.reshape(4 * H, NT, 1)
    g_row = g4.reshape(4 * H, nc, 1, T)

    def chunk(d, t):
        fwd = jnp.where(t < ncc, ncl + t, t - ncc)
        return jnp.where(d == 0, fwd, nc - 1 - t)

    hps = 2
    hb = H // hps
    col_spec = lambda off: pl.BlockSpec((hps, T, 1), lambda d, h, t: ((2 * d + off) * hb + h, chunk(d, t), 0))
    row_spec = lambda off: pl.BlockSpec((hps, None, 1, T), lambda d, h, t: ((2 * d + off) * hb + h, chunk(d, t), 0, 0))
    return pl.pallas_call(
        functools.partial(_mlstm_kernel, T=T, dqk=dqk, dv=dv, hps=hps),
        grid=(2, hb, nc),
        in_specs=[
            pl.BlockSpec(memory_space=pltpu.SMEM),
            pl.BlockSpec((T, hps * dqk), lambda d, h, t: (chunk(d, t), h)),
            pl.BlockSpec((T, hps * dqk), lambda d, h, t: (chunk(d, t), hb + h)),
            pl.BlockSpec((T, hps * dv), lambda d, h, t: (chunk(d, t), hb + h)),
            col_spec(0), col_spec(1), row_spec(0), row_spec(1),
        ],
        out_specs=pl.BlockSpec((None, T, hps * dv), lambda d, h, t: (d, chunk(d, t), h)),
        out_shape=jax.ShapeDtypeStruct((2, NT, H * dv), F32),
        scratch_shapes=[pltpu.VMEM((hps, dqk, dv), F32), pltpu.VMEM((hps, 1, dqk), F32), pltpu.VMEM((hps, 1, 1), F32)],
        compiler_params=_params("parallel", "parallel", "arbitrary"),
    )(b_if, u, u, u, g_col, g_col, g_row, g_row)


def _mlgate_kernel(hs_ref, o_ref, g_ref, out_ref, *, dv):
    D = out_ref.shape[1]
    for h in range(D // dv):
        sl = slice(h * dv, (h + 1) * dv)
        x = hs_ref[0, :, sl] + hs_ref[1, :, sl]
        r = lax.rsqrt(jnp.mean(jnp.square(x), axis=-1, keepdims=True) + RMS_EPS)
        out_ref[:, sl] = (x * r * g_ref[:, sl] * jax.nn.sigmoid(o_ref[:, sl])).astype(out_ref.dtype)


def mlstm_gate(hs, u, norm_g, *, o_col_block):
    _, NT, D = hs.shape
    tr = 128
    return pl.pallas_call(
        functools.partial(_mlgate_kernel, dv=D // ML_HEADS),
        grid=(NT // tr,),
        in_specs=[
            pl.BlockSpec((2, tr, D), lambda i: (0, i, 0)),
            pl.BlockSpec((tr, D), lambda i: (i, o_col_block)),
            pl.BlockSpec((1, D), lambda i: (0, 0)),
        ],
        out_specs=pl.BlockSpec((tr, D), lambda i: (i, 0)),
        out_shape=jax.ShapeDtypeStruct((NT, D), BF16),
        compiler_params=_params("parallel"),
    )(hs, u, norm_g.reshape(1, D))


SAFE_LOG2 = 64.0


def _dattn_kernel(lam_ref, q_ref, k_ref, v_ref, g_ref, o_ref, m_ref, l_ref, acc_ref, kn_ref,
                  *, tk, n_chunks, out_scale, lambda_init):
    dh = DA_HEAD_DIM
    nlt = tk // LANES
    l_ref[...] = jnp.zeros_like(l_ref)
    acc_ref[...] = jnp.zeros_like(acc_ref)

    @pl.when(pl.program_id(1) == 0)
    def _():
        for i in range(2):
            kk = k_ref[:, i * dh:(i + 1) * dh].astype(F32)
            k2 = jnp.max(jnp.sum(kk * kk, axis=1, keepdims=True), axis=0, keepdims=True)
            kn_ref[i] = jnp.broadcast_to(k2, kn_ref.shape[1:])

    bound2 = None
    for i in range(2):
        qq = q_ref[:, i * dh:(i + 1) * dh].astype(F32)
        b2 = jnp.max(jnp.sum(qq * qq, axis=1, keepdims=True), axis=0, keepdims=True) * kn_ref[i][0:1, 0:1]
        bound2 = b2 if bound2 is None else jnp.maximum(bound2, b2)
    scores_bounded = bound2[0, 0] <= SAFE_LOG2 * SAFE_LOG2

    def scores(c, i):
        kc = k_ref[pl.ds(pl.multiple_of(c * tk, tk), tk), i * dh:(i + 1) * dh]
        return _dot_nt(q_ref[:, i * dh:(i + 1) * dh], kc)

    @pl.when(scores_bounded)
    def _():
        m_ref[...] = jnp.zeros_like(m_ref)

    @pl.when(jnp.logical_not(scores_bounded))
    def _():
        m_ref[...] = jnp.full_like(m_ref, -jnp.inf)

        def max_pass(c, carry):
            for i in range(2):
                s = scores(c, i)
                m = m_ref[i]
                for t in range(nlt):
                    m = jnp.maximum(m, s[:, t * LANES:(t + 1) * LANES])
                m_ref[i] = m
            return carry

        lax.fori_loop(0, n_chunks, max_pass, 0)
        for i in range(2):
            m_ref[i] = jnp.broadcast_to(jnp.max(m_ref[i], axis=1, keepdims=True), m_ref.shape[1:])

    tq = q_ref.shape[0]

    def sum_pass(c, carry):
        vc = v_ref[pl.ds(pl.multiple_of(c * tk, tk), tk), :]
        ps = []
        for i in range(2):
            s = scores(c, i)
            m = m_ref[i]
            p = [jnp.exp2(s[:, t * LANES:(t + 1) * LANES] - m) for t in range(nlt)]
            l_ref[i] += functools.reduce(jnp.add, p)
            ps.append(jnp.concatenate(p, axis=1).astype(BF16))
        acc_ref[...] += _dot(jnp.concatenate(ps, axis=0), vc)
        return carry

    lax.fori_loop(0, n_chunks, sum_pass, 0)
    lam = (jnp.exp(jnp.sum(lam_ref[0:1, :] * lam_ref[1:2, :], axis=1, keepdims=True))
           - jnp.exp(jnp.sum(lam_ref[2:3, :] * lam_ref[3:4, :], axis=1, keepdims=True)) + lambda_init)
    l = [jnp.sum(l_ref[i], axis=1, keepdims=True) for i in range(2)]
    o = acc_ref[0:tq] / l[0] - lam * (acc_ref[tq:2 * tq] / l[1])
    r = lax.rsqrt(jnp.mean(jnp.square(o), axis=-1, keepdims=True) + RMS_EPS)
    o_ref[...] = (o * r * g_ref[...] * out_scale).astype(o_ref.dtype)


def diff_attention(qkv, lam4, subln_g, *, q_row0, n_q, kv_row0, kv_len, lambda_init):
    D = qkv.shape[1] // 3
    hd = 2 * DA_HEAD_DIM
    H = D // hd
    tq = _pick(n_q, (512, 256))
    tk = _pick(kv_len, (768, 512, 256))
    qb, kvb = q_row0 // tq, kv_row0 // kv_len
    assert kv_row0 % kv_len == 0 and q_row0 % tq == 0
    return pl.pallas_call(
        functools.partial(_dattn_kernel, tk=tk, n_chunks=kv_len // tk, out_scale=1.0 - lambda_init,
                          lambda_init=lambda_init),
        grid=(H, n_q // tq),
        in_specs=[
            pl.BlockSpec((4, DA_HEAD_DIM), lambda h, i: (0, 0)),
            pl.BlockSpec((tq, hd), lambda h, i: (i + qb, h)),
            pl.BlockSpec((kv_len, hd), lambda h, i: (kvb, H + h)),
            pl.BlockSpec((kv_len, hd), lambda h, i: (kvb, 2 * H + h)),
            pl.BlockSpec((1, hd), lambda h, i: (0, 0)),
        ],
        out_specs=pl.BlockSpec((tq, hd), lambda h, i: (i, h)),
        out_shape=jax.ShapeDtypeStruct((n_q, D), BF16),
        scratch_shapes=[pltpu.VMEM((2, tq, LANES), F32), pltpu.VMEM((2, tq, LANES), F32),
                        pltpu.VMEM((2 * tq, hd), F32), pltpu.VMEM((2, 8, LANES), F32)],
        compiler_params=_params("parallel", "arbitrary"),
    )(lam4, qkv, qkv, qkv, subln_g.reshape(1, hd))


def _rope_tables(L, scale):
    a = DA_HEAD_DIM // 2
    t = jnp.arange(L)
    pos = jnp.stack([t // GRID_W, t % GRID_W], -1).astype(F32)
    inv = ROPE_BASE ** (-jnp.arange(0, a, 2, dtype=F32) / a)
    ang = pos[:, :, None] * inv
    ang = jnp.concatenate([ang, ang], -1).reshape(L, DA_HEAD_DIM)
    return jnp.cos(ang) * scale, jnp.sin(ang) * scale


def _filter_kernel(zf_ref, t_ref, w1_ref, b1_ref, w2_ref, b2_ref, w3_ref, b3_ref, w4f_ref, w4b_ref, dl_ref,
                   o_ref, asum_ref, hdn_ref):
    i, j = pl.program_id(0), pl.program_id(1)

    @pl.when(j == 0)
    def _():
        hdn = jnp.sin(_dot(zf_ref[...].astype(BF16), w1_ref[...].astype(BF16)) + b1_ref[...])
        hdn = jnp.sin(_dot(hdn.astype(BF16), w2_ref[...].astype(BF16)) + b2_ref[...])
        hdn_ref[...] = jnp.sin(_dot(hdn.astype(BF16), w3_ref[...].astype(BF16)) + b3_ref[...]).astype(hdn_ref.dtype)

    hdn = hdn_ref[...]
    hf = _dot(hdn, w4f_ref[...].astype(BF16))
    hb = _dot(hdn, w4b_ref[...].astype(BF16))
    side = t_ref[:, 1:2]
    kern = jnp.where(side > 0.0, hf, jnp.where(side < 0.0, hb, 0.0)) * jnp.exp(-t_ref[:, 0:1] * dl_ref[...])
    o_ref[...] = kern

    @pl.when(i == 0)
    def _():
        asum_ref[j] = jnp.zeros(asum_ref.shape[1:], F32)

    asum_ref[j] += jnp.sum(jnp.abs(kern), axis=0, keepdims=True)


def hyena_filters(L, D, f_w1, f_b1, f_w2, f_b2, f_w3, f_b3, f_w4, *, t2_major_n2=None):
    t = jnp.linspace(0.0, 1.0, L, dtype=F32)[:, None]
    w = 2.0 * math.pi * jnp.arange(L, dtype=F32)[:, None] / L
    f = jnp.linspace(1e-4, HY_BANDS - 1, HY_BANDS, dtype=F32)[None]
    z = jnp.concatenate([t, jnp.cos(w * f), -jnp.sin(w * f)], -1)
    src = jnp.concatenate([jnp.arange(L), jnp.zeros((1,), jnp.int32), L - 1 - jnp.arange(L - 1)])
    r = jnp.arange(2 * L)
    side = jnp.where(r < L, 1.0, jnp.where(r > L, -1.0, 0.0)).astype(F32)[:, None]
    if t2_major_n2 is not None:
        perm = r.reshape(2 * L // t2_major_n2, t2_major_n2).T.reshape(-1)
        src, side = src[perm], side[perm]
    zf = jnp.pad(z[src], ((0, 0), (0, LANES - HY_EMB)))
    tcol = jnp.concatenate([t[src], side], axis=1)
    deltas = jnp.abs(jnp.linspace(math.log(HY_DECAY_TARGET) / HY_SLOW_DECAY, math.log(HY_DECAY_TARGET) / HY_FAST_DECAY,
                                  D, dtype=F32)).reshape(1, D)
    w1 = jnp.pad(f_w1, ((0, LANES - HY_EMB), (0, 0)))
    nf = f_w1.shape[1]
    R2 = 2 * L
    tr = _pick(R2, (512,))
    tc = _pick(D, (1024, 512, 256, 128))
    cpd = D // tc
    ncol = HY_ORDER * cpd
    full = lambda shape: pl.BlockSpec(shape, lambda i, j: (0, 0))
    kern, asum = pl.pallas_call(
        _filter_kernel,
        grid=(R2 // tr, ncol),
        in_specs=[
            pl.BlockSpec((tr, LANES), lambda i, j: (i, 0)),
            pl.BlockSpec((tr, 2), lambda i, j: (i, 0)),
            full((LANES, nf)), full((1, nf)), full((nf, nf)), full((1, nf)), full((nf, nf)), full((1, nf)),
            pl.BlockSpec((nf, tc), lambda i, j: (0, (j // cpd) * 2 * cpd + j % cpd)),
            pl.BlockSpec((nf, tc), lambda i, j: (0, ((j // cpd) * 2 + 1) * cpd + j % cpd)),
            pl.BlockSpec((1, tc), lambda i, j: (0, j % cpd)),
        ],
        out_specs=[pl.BlockSpec((tr, tc), lambda i, j: (i, j)), pl.BlockSpec((ncol, 1, tc), lambda i, j: (0, 0, 0))],
        out_shape=[jax.ShapeDtypeStruct((R2, HY_ORDER * D), F32), jax.ShapeDtypeStruct((ncol, 1, tc), F32)],
        scratch_shapes=[pltpu.VMEM((tr, nf), BF16)],
        compiler_params=_params("arbitrary", "arbitrary"),
    )(zf, tcol, w1, f_b1.reshape(1, nf), f_w2, f_b2.reshape(1, nf), f_w3, f_b3.reshape(1, nf), f_w4, f_w4, deltas)
    return kern, asum.reshape(1, HY_ORDER * D)


FFT_N2 = 128
FFT_TT2 = 8


def _split(x):
    hi = x.astype(BF16)
    return hi, (x - hi.astype(F32)).astype(BF16)


def _dot3(w_hi, w_lo, x):
    x_hi, x_lo = _split(x)
    return _dot(w_hi, x_hi) + (_dot(w_lo, x_hi) + _dot(w_hi, x_lo))


def _hilo(a):
    return _split(jnp.asarray(a, F32))


def _fft_first_kernel(whi_ref, wlo_ref, z_ref, o_ref, *, t2_major):
    if t2_major:
        n1 = o_ref.shape[1]
        for j in range(o_ref.shape[2]):
            a = _dot3(whi_ref[...], wlo_ref[...], z_ref[j])
            o_ref[0, :, j, :] = a[:n1]
            o_ref[1, :, j, :] = a[n1:]
    else:
        l1, tt2, tc = z_ref.shape
        a = _dot3(whi_ref[...], wlo_ref[...], z_ref[...].reshape(l1 * tt2, tc))
        o_ref[...] = a.reshape(o_ref.shape)


def fft_first(w_hl, z3, *, col0, C, t2_major=False):
    w_hi, w_lo = w_hl
    tt2 = FFT_TT2
    n1x2, l1 = w_hi.shape if t2_major else (w_hi.shape[0] // tt2, w_hi.shape[1] // tt2)
    n1, n2 = n1x2 // 2, z3.shape[0 if t2_major else 1]
    tc = _pick(C, (512, 256, 128))
    cb = col0 // tc
    full = pl.BlockSpec(w_hi.shape, lambda t, j: (0, 0))
    if t2_major:
        z_spec = pl.BlockSpec((tt2, l1, tc), lambda t, j: (t, 0, j + cb))
    else:
        z_spec = pl.BlockSpec((l1, tt2, tc), lambda t, j: (0, t, j + cb))
    return pl.pallas_call(
        functools.partial(_fft_first_kernel, t2_major=t2_major),
        grid=(n2 // tt2, C // tc),
        in_specs=[full, full, z_spec],
        out_specs=pl.BlockSpec((2, n1, tt2, tc), lambda t, j: (0, 0, t, j)),
        out_shape=jax.ShapeDtypeStruct((2, n1, n2, C), F32),
        compiler_params=_params("parallel", "parallel"),
    )(w_hi, w_lo, z3)


def _fft_mid_kernel(*refs, g, conv):
    if conv:
        a_ref, kf_ref, twc_ref, tws_ref, w2h_ref, w2l_ref, w2ih_ref, w2il_ref, o_ref = refs
    else:
        a_ref, twc_ref, tws_ref, w2h_ref, w2l_ref, o_ref = refs
    n2 = a_ref.shape[2]
    for kk in range(g):
        ar, ai = a_ref[0, kk], a_ref[1, kk]
        c, s = twc_ref[kk], tws_ref[kk]
        x = _dot3(w2h_ref[...], w2l_ref[...], jnp.concatenate([ar * c + ai * s, ai * c - ar * s], axis=0))
        if conv:
            xr, xi = x[:n2], x[n2:]
            kr, ki = kf_ref[0, kk], kf_ref[1, kk]
            y = _dot3(w2ih_ref[...], w2il_ref[...], jnp.concatenate([xr * kr - xi * ki, xr * ki + xi * kr], axis=0))
            yr, yi = y[:n2], y[n2:]
            o_ref[0, kk] = yr * c - yi * s
            o_ref[1, kk] = yr * s + yi * c
        else:
            o_ref[0, kk] = x[:n2]
            o_ref[1, kk] = x[n2:]


def fft_mid(a, tw_c, tw_s, w2_hl, w2i_hl=None, kf=None, *, kf_col0=0):
    _, n1, n2, C = a.shape
    g = 4
    tc = _pick(C, (512, 256, 128))
    conv = kf is not None
    blk = pl.BlockSpec((2, g, n2, tc), lambda j, k: (0, k, 0, j))
    tw = pl.BlockSpec((g, n2, 1), lambda j, k: (k, 0, 0))
    mat = pl.BlockSpec((2 * n2, 2 * n2), lambda j, k: (0, 0))
    if conv:
        kb = kf_col0 // tc
        in_specs = [blk, pl.BlockSpec((2, g, n2, tc), lambda j, k: (0, k, 0, j + kb)), tw, tw, mat, mat, mat, mat]
        args = [a, kf, tw_c, tw_s, *w2_hl, *w2i_hl]
    else:
        in_specs = [blk, tw, tw, mat, mat]
        args = [a, tw_c, tw_s, *w2_hl]
    return pl.pallas_call(
        functools.partial(_fft_mid_kernel, g=g, conv=conv),
        grid=(C // tc, n1 // g),
        in_specs=in_specs,
        out_specs=blk,
        out_shape=jax.ShapeDtypeStruct(a.shape, F32),
        compiler_params=_params("parallel", "parallel"),
    )(*args)


def _fft_last_kernel(whi_ref, wlo_ref, b_ref, gate_ref, zp_ref, asum_ref, bias_ref, o_ref):
    inv_asum = 1.0 / asum_ref[...]
    for j in range(b_ref.shape[2]):
        bj = jnp.concatenate([b_ref[0, :, j, :], b_ref[1, :, j, :]], axis=0)
        x = _dot3(whi_ref[...], wlo_ref[...], bj)
        o_ref[:, j, :] = gate_ref[:, j, :] * (x * inv_asum + zp_ref[:, j, :] * bias_ref[...])


def fft_last(w_hl, b, gate3, gate_col0, zp3, zp_col0, asum, bias, *, vec_col0):
    w_hi, w_lo = w_hl
    l1 = w_hi.shape[0]
    _, n1, n2, C = b.shape
    tt2 = FFT_TT2
    tc = _pick(C, (512, 256, 128))
    gcb, zcb, vcb = gate_col0 // tc, zp_col0 // tc, vec_col0 // tc
    full = pl.BlockSpec((l1, 2 * n1), lambda t, j: (0, 0))
    return pl.pallas_call(
        _fft_last_kernel,
        grid=(n2 // tt2, C // tc),
        in_specs=[
            full, full,
            pl.BlockSpec((2, n1, tt2, tc), lambda t, j: (0, 0, t, j)),
            pl.BlockSpec((l1, tt2, tc), lambda t, j: (0, t, j + gcb)),
            pl.BlockSpec((l1, tt2, tc), lambda t, j: (0, t, j + zcb)),
            pl.BlockSpec((1, tc), lambda t, j: (0, j + vcb)),
            pl.BlockSpec((1, tc), lambda t, j: (0, j + vcb)),
        ],
        out_specs=pl.BlockSpec((l1, tt2, tc), lambda t, j: (0, t, j)),
        out_shape=jax.ShapeDtypeStruct((l1, n2, C), F32),
        compiler_params=_params("parallel", "parallel"),
    )(w_hi, w_lo, b, gate3, zp3, asum, bias)


def _hpmm_kernel(*refs, gated):
    if gated:
        ah_ref, al_ref, b_ref, gate_ref, z_ref, asum_ref, bias_ref, o_ref = refs
    else:
        ah_ref, al_ref, b_ref, o_ref = refs
    y = _dot3(ah_ref[...], al_ref[...], b_ref[...])
    if gated:
        y = gate_ref[...] * (y / asum_ref[...] + z_ref[...] * bias_ref[...])
    o_ref[...] = y


def hp_matmul(a_hl, b, *, gate=None, z=None, asum=None, bias=None, vec_col0=0):
    a_hi, a_lo = a_hl
    M, Kd = a_hi.shape
    N = b.shape[1]
    tn = _pick(N, (1024, 512, 256, 128))
    full = pl.BlockSpec((M, Kd), lambda j: (0, 0))
    in_specs = [full, full, pl.BlockSpec((Kd, tn), lambda j: (0, j))]
    args = [a_hi, a_lo, b]
    gated = gate is not None
    if gated:
        vb = vec_col0 // tn
        blk = pl.BlockSpec((M, tn), lambda j: (0, j))
        vec = pl.BlockSpec((1, tn), lambda j: (0, j + vb))
        in_specs += [blk, blk, vec, vec]
        args += [gate, z, asum, bias]
    return pl.pallas_call(
        functools.partial(_hpmm_kernel, gated=gated),
        grid=(N // tn,),
        in_specs=in_specs,
        out_specs=pl.BlockSpec((M, tn), lambda j: (0, j)),
        out_shape=jax.ShapeDtypeStruct((M, N), F32),
        compiler_params=_params("parallel"),
    )(*args)


def _cmul_kernel(x_ref, k_ref, o_ref):
    xr, xi, kr, ki = x_ref[0], x_ref[1], k_ref[0], k_ref[1]
    o_ref[0] = xr * kr - xi * ki
    o_ref[1] = xr * ki + xi * kr


def complex_mul(x, kf, *, kf_col0):
    _, R, C = x.shape
    tc = _pick(C, (512, 256, 128))
    kb = kf_col0 // tc
    return pl.pallas_call(
        _cmul_kernel,
        grid=(C // tc,),
        in_specs=[pl.BlockSpec((2, R, tc), lambda j: (0, 0, j)), pl.BlockSpec((2, R, tc), lambda j: (0, 0, j + kb))],
        out_specs=pl.BlockSpec((2, R, tc), lambda j: (0, 0, j)),
        out_shape=jax.ShapeDtypeStruct(x.shape, F32),
        compiler_params=_params("parallel"),
    )(x, kf)


def _dft_constants(n1, n2, l1):
    N = n1 * n2
    n1h = min(n1, -(-(n1 // 2 + 1) // 8) * 8)
    k1 = np.arange(n1h)[:, None]
    herm = np.where((k1 == 0) | (k1 == n1 // 2), 1.0, np.where(k1 < n1 // 2, 2.0, 0.0))
    ang1 = 2 * np.pi * k1 * np.arange(n1)[None, :] / n1
    w1 = np.concatenate([np.cos(ang1), -np.sin(ang1)], 0)
    w1_inv = np.concatenate([(herm * np.cos(ang1[:, :l1])).T, (-herm * np.sin(ang1[:, :l1])).T], 1) / N
    ang_t = 2 * np.pi * k1 * np.arange(n2)[None, :] / N
    k2 = np.arange(n2)[:, None]
    ang2 = 2 * np.pi * k2 * np.arange(n2)[None, :] / n2
    c2, s2 = np.cos(ang2), np.sin(ang2)
    f = lambda x: jnp.asarray(x, F32)
    return dict(w1_data=_hilo(np.kron(w1[:, :l1], np.eye(FFT_TT2))), w1_full=_hilo(w1), w1_inv=_hilo(w1_inv),
                tw_c=f(np.cos(ang_t))[:, :, None], tw_s=f(np.sin(ang_t))[:, :, None],
                w2=_hilo(np.block([[c2, s2], [-s2, c2]])), w2_inv=_hilo(np.block([[c2, -s2], [s2, c2]])))


def hyena_conv_lat(u3, kern, asum, bias, *, L, D):
    n2 = FFT_N2
    n1, l1 = 2 * L // n2, L // n2
    cst = _dft_constants(n1, n2, l1)
    ka = fft_first(cst["w1_full"], kern.reshape(n2, n1, HY_ORDER * D), col0=0, C=HY_ORDER * D, t2_major=True)
    kf = fft_mid(ka, cst["tw_c"], cst["tw_s"], cst["w2"])
    z3, zc0 = u3, 0
    for o in range(HY_ORDER):
        a = fft_first(cst["w1_data"], z3, col0=zc0, C=D)
        bmid = fft_mid(a, cst["tw_c"], cst["tw_s"], cst["w2"], cst["w2_inv"], kf, kf_col0=o * D)
        z3 = fft_last(cst["w1_inv"], bmid, u3, (o + 1) * D, z3, zc0, asum, bias, vec_col0=o * D)
        zc0 = 0
    return z3.reshape(L, D)


def _dense_dft_constants(L):
    N = 2 * L
    ang = 2 * np.pi * np.arange(N)[:, None] * np.arange(N)[None, :] / N
    wf = np.concatenate([np.cos(ang), -np.sin(ang)], 0)
    wi = np.concatenate([np.cos(ang[:L]), -np.sin(ang[:L])], 1) / N
    return _hilo(wf), _hilo(wf[:, :L]), _hilo(wi)


def hyena_conv_ctx(v, x1, x2, kern, asum, bias):
    L, D = v.shape
    wf, wf_data, wi = _dense_dft_constants(L)
    kf = hp_matmul(wf, kern).reshape(2, 2 * L, HY_ORDER * D)
    z = v
    for o, gate in enumerate((x1, x2)):
        xf = hp_matmul(wf_data, z).reshape(2, 2 * L, D)
        y = complex_mul(xf, kf, kf_col0=o * D).reshape(4 * L, D)
        z = hp_matmul(wi, y, gate=gate, z=z, asum=asum, bias=bias, vec_col0=o * D)
    return z


def _na_kernel(q_ref, k_ref, v_ref, *rest, n_ctx, rows, heads_per_step, rows_per_step):
    bias_refs, o_ref = rest[:rows_per_step], rest[rows_per_step]
    kr = NA_ROWS
    n_lat = rows * GRID_W
    dh = NA_HEAD_DIM
    scale = dh ** -0.5
    for rr in range(rows_per_step):
        r = pl.program_id(1) * rows_per_step + rr
        r0 = jnp.clip(r - kr // 2, 0, rows - kr)
        start = pl.multiple_of(r0 * GRID_W, GRID_W)
        qs = slice(rr * GRID_W, (rr + 1) * GRID_W)
        for hh in range(heads_per_step):
            sl = slice(hh * dh, (hh + 1) * dh)
            q = q_ref[qs, sl]
            s_ctx = _dot_nt(q, k_ref[n_lat:n_lat + n_ctx, sl]) * scale
            s_lat = _dot_nt(q, k_ref[pl.ds(start, kr * GRID_W), sl]) * scale + bias_refs[rr][hh]
            m = jnp.maximum(jnp.max(s_ctx, axis=1, keepdims=True), jnp.max(s_lat, axis=1, keepdims=True))
            p_ctx = jnp.exp(s_ctx - m)
            p_lat = jnp.exp(s_lat - m)
            l = jnp.sum(p_ctx, axis=1, keepdims=True) + jnp.sum(p_lat, axis=1, keepdims=True)
            o = (_dot(p_ctx.astype(BF16), v_ref[n_lat:n_lat + n_ctx, sl])
                 + _dot(p_lat.astype(BF16), v_ref[pl.ds(start, kr * GRID_W), sl]))
            o_ref[qs, sl] = (o / l).astype(o_ref.dtype)


def _na_bias_table(rpb, rows):
    H = rpb.shape[0]
    cols = jnp.arange(GRID_W)
    c0 = jnp.clip(cols - NA_COLS // 2, 0, GRID_W - NA_COLS)
    kc = jnp.arange(GRID_W)[None, :]
    inside = (kc >= c0[:, None]) & (kc < c0[:, None] + NA_COLS)
    rel = jnp.clip(kc - cols[:, None] + NA_COLS - 1, 0, 2 * NA_COLS - 2)
    rr = jnp.arange(NA_ROWS)[:, None] + jnp.arange(NA_ROWS)[None, :]
    t = rpb[:, rr]
    t = t[:, :, :, rel]
    t = jnp.where(inside[None, None, None], t, -1e30)
    t = jnp.transpose(t, (1, 0, 3, 2, 4))
    return t.reshape(NA_ROWS, H, GRID_W, NA_ROWS * GRID_W)


def na_attention(qkv, rpb, *, n_ctx):
    NT, D3 = qkv.shape
    D = D3 // 3
    L = NT - n_ctx
    rows = L // GRID_W
    hps = 4
    hw = hps * NA_HEAD_DIM
    nhb = D // hw
    table = _na_bias_table(rpb, rows)

    def pattern(r):
        r0 = jnp.clip(r - NA_ROWS // 2, 0, rows - NA_ROWS)
        return r0 - r + NA_ROWS - 1

    rps = 4
    bias_specs = [pl.BlockSpec((None, hps, GRID_W, NA_ROWS * GRID_W),
                               lambda h, r, rr=rr: (pattern(r * rps + rr), h, 0, 0)) for rr in range(rps)]
    return pl.pallas_call(
        functools.partial(_na_kernel, n_ctx=n_ctx, rows=rows, heads_per_step=hps, rows_per_step=rps),
        grid=(nhb, rows // rps),
        in_specs=[
            pl.BlockSpec((rps * GRID_W, hw), lambda h, r: (r, h)),
            pl.BlockSpec((NT, hw), lambda h, r: (0, nhb + h)),
            pl.BlockSpec((NT, hw), lambda h, r: (0, 2 * nhb + h)),
            *bias_specs,
        ],
        out_specs=pl.BlockSpec((rps * GRID_W, hw), lambda h, r: (r, h)),
        out_shape=jax.ShapeDtypeStruct((L, D), BF16),
        compiler_params=_params("parallel", "parallel"),
    )(qkv, qkv, qkv, *([table] * rps))


def _router_kernel(x_ref, w_ref, b_ref, o_ref):
    E, G = N_EXPERTS, N_GROUPS
    per = E // G
    logits = _dot_nt(w_ref[...].astype(BF16), x_ref[...])
    scores = jax.nn.sigmoid(logits)
    sel = scores + b_ref[...]
    tm = sel.shape[1]
    grp = sel.reshape(G, per, tm)
    eidx = lax.broadcasted_iota(jnp.int32, grp.shape, 1)
    m1 = jnp.max(grp, axis=1, keepdims=True)
    first = jnp.min(jnp.where(grp == m1, eidx, per), axis=1, keepdims=True)
    m2 = jnp.max(jnp.where(eidx == first, -jnp.inf, grp), axis=1, keepdims=True)
    gs = (m1 + m2).reshape(G, tm)
    gidx = lax.broadcasted_iota(jnp.int32, gs.shape, 0)
    gmask = jnp.zeros(gs.shape, jnp.bool_)
    for _ in range(TOPK_GROUPS):
        mx = jnp.max(gs, axis=0, keepdims=True)
        pick = gidx == jnp.min(jnp.where(gs == mx, gidx, G), axis=0, keepdims=True)
        gmask = jnp.logical_or(gmask, pick)
        gs = jnp.where(pick, -jnp.inf, gs)
    emask = jnp.broadcast_to(gmask.reshape(G, 1, tm), (G, per, tm)).reshape(E, tm)
    cand = jnp.where(emask, sel, -jnp.inf)
    xidx = lax.broadcasted_iota(jnp.int32, cand.shape, 0)
    chosen = jnp.zeros(cand.shape, jnp.bool_)
    for _ in range(TOP_K):
        mx = jnp.max(cand, axis=0, keepdims=True)
        pick = xidx == jnp.min(jnp.where(cand == mx, xidx, E), axis=0, keepdims=True)
        chosen = jnp.logical_or(chosen, pick)
        cand = jnp.where(pick, -jnp.inf, cand)
    wsel = jnp.where(chosen, scores, 0.0)
    gates = wsel / jnp.sum(wsel, axis=0, keepdims=True) * ROUTED_SCALE
    o_ref[...] = gates.T


def moe_router(h, router_w_t, router_b, layer):
    T, D = h.shape
    E = router_w_t.shape[1]
    tm = _pick(T, (256, 128))
    return pl.pallas_call(
        _router_kernel,
        grid=(T // tm,),
        in_specs=[
            pl.BlockSpec((tm, D), lambda i: (i, 0)),
            pl.BlockSpec((None, E, D), lambda i: (layer, 0, 0)),
            pl.BlockSpec((None, E, 1), lambda i: (layer, 0, 0)),
        ],
        out_specs=pl.BlockSpec((tm, E), lambda i: (i, 0)),
        out_shape=jax.ShapeDtypeStruct((T, E), F32),
        compiler_params=_params("parallel"),
    )(h, router_w_t, router_b.reshape(-1, E, 1))


def _moe_up_kernel(*refs, nk, ne, gated, F):
    if gated:
        x_ref, wg_ref, wu_ref, g_ref, o_ref, accg_ref, accu_ref = refs
    else:
        x_ref, wg_ref, wu_ref, o_ref, accg_ref, accu_ref = refs
    e0 = pl.program_id(1) * ne
    k = pl.program_id(2)
    up_phase = pl.program_id(3)

    def accumulate(acc_ref, w_ref):
        w = w_ref[0] if ne == 1 else jnp.concatenate([w_ref[ee] for ee in range(ne)], axis=0)
        prod = _dot_nt(x_ref[...], w.astype(BF16))

        @pl.when(k == 0)
        def _():
            acc_ref[...] = prod

        @pl.when(k > 0)
        def _():
            acc_ref[...] += prod

    @pl.when(up_phase == 0)
    def _():
        accumulate(accg_ref, wg_ref)

    @pl.when(up_phase == 1)
    def _():
        accumulate(accu_ref, wu_ref)

    @pl.when(jnp.logical_and(k == nk - 1, up_phase == 1))
    def _():
        a = accg_ref[...]
        hid = a * jax.nn.sigmoid(a) * accu_ref[...]
        if gated:
            g = g_ref[...]
            lane = lax.broadcasted_iota(jnp.int32, g.shape, 1)
            col = lax.broadcasted_iota(jnp.int32, (1, ne * F), 1) // F
            gexp = jnp.zeros_like(hid)
            for ee in range(ne):
                ge = jnp.sum(jnp.where(lane == e0 + ee, g, 0.0), axis=1, keepdims=True)
                gexp = jnp.where(col == ee, ge, gexp)
            hid = hid * gexp
        o_ref[...] = hid.astype(o_ref.dtype)


def moe_up(x, w_gate, w_up, layer, gates=None):
    T, D = x.shape
    if w_gate.ndim == 3:
        w_gate, w_up = w_gate[:, None], w_up[:, None]
    w_gate, w_up = jnp.swapaxes(w_gate, 2, 3), jnp.swapaxes(w_up, 2, 3)
    E, F = w_gate.shape[1], w_gate.shape[2]
    ne = 4 if E % 4 == 0 else 1
    tm = _pick(T, (1056, 1024, 512, 256, 128))
    tk = _pick(D, (2048, 1024, 512, 256, 128))
    nk = D // tk
    gated = gates is not None
    w_spec = pl.BlockSpec((None, ne, F, tk), lambda i, e, k, p: (layer, e, 0, k))
    in_specs = [pl.BlockSpec((tm, tk), lambda i, e, k, p: (i, k)), w_spec, w_spec]
    args = [x, w_gate, w_up]
    if gated:
        in_specs.append(pl.BlockSpec((tm, E), lambda i, e, k, p: (i, 0)))
        args.append(gates)
    return pl.pallas_call(
        functools.partial(_moe_up_kernel, nk=nk, ne=ne, gated=gated, F=F),
        grid=(T // tm, E // ne, nk, 2),
        in_specs=in_specs,
        out_specs=pl.BlockSpec((tm, ne * F), lambda i, e, k, p: (i, e)),
        out_shape=jax.ShapeDtypeStruct((T, E * F), BF16),
        scratch_shapes=[pltpu.VMEM((tm, ne * F), F32), pltpu.VMEM((tm, ne * F), F32)],
        compiler_params=_params("parallel", "parallel", "arbitrary", "arbitrary"),
    )(*args)


def moe(h, layer, router_w_t, router_b, w_gate, w_up, w_down, s_gate, s_up, s_down):
    gates = moe_router(h, router_w_t, router_b, layer)
    hid = moe_up(h, w_gate, w_up, layer, gates)
    hid_sh = moe_up(h, s_gate, s_up, layer)
    E, F, D = w_down.shape[1:]
    return matmul(hid, w_down.reshape(DEPTH, E * F, D), out_dtype=F32, layer=layer, extra=(hid_sh, s_down, layer))


def kernel(x, c, ctx, c_ctx, ada_a, ada_b, ada_bias, ln1_g, ln1_b, ln2_g, ln2_b, router_w, router_b, moe_w_gate, moe_w_up, moe_w_down, sh_w_gate, sh_w_up, sh_w_down, ml_w_in, ml_b_if, ml_norm_g, ml_w_out, da_w_qkv, da_lam_q1, da_lam_k1, da_lam_q2, da_lam_k2, da_subln_g, da_w_out, hy_w_in, hy_conv_w, hy_conv_b, hy_f_w1, hy_f_b1, hy_f_w2, hy_f_b2, hy_f_w3, hy_f_b3, hy_f_w4, hy_bias, hy_w_out, na_w_qkv, na_rpb, na_w_out):
    _, L, D = x.shape
    C = ctx.shape[1]
    NT = C + L
    s = jnp.concatenate([x[0], ctx[0]], axis=0)

    cc = jnp.zeros((8, D), F32).at[0].set(c_ctx).at[1].set(c[0])
    mods = ada_modulation(cc, ada_a, ada_b, ada_bias)[:, :2].reshape(DEPTH, 2, 6, 1, D)
    SH1, SC1, G1, SH2, SC2, G2 = range(6)
    router_w_t = jnp.swapaxes(router_w, 1, 2)
    moe_args = (router_w_t, router_b, moe_w_gate, moe_w_up, moe_w_down, sh_w_gate, sh_w_up, sh_w_down)

    (h,) = ln_modulate(s, mods, n_ctx=C, h_layer=0, sc_idx=SC1, sh_idx=SH1)
    for i in range(DEPTH):
        last = i == DEPTH - 1
        if i == 0:
            y = mlstm_mixer(h, ml_w_in, ml_b_if[0], ml_norm_g[0], ml_w_out, n_ctx=C)
        elif i == 1:
            lam4 = jnp.stack([da_lam_q1[0], da_lam_k1[0], da_lam_q2[0], da_lam_k2[0]])
            y = diff_attn_mixer(h, da_w_qkv, lam4, da_subln_g[0], da_w_out, n_ctx=C,
                                lambda_init=0.8 - 0.6 * math.exp(-0.3 * i))
        elif i == 2:
            y = hyena_mixer(h, hy_w_in, hy_conv_w[0], hy_conv_b[0], hy_f_w1[0], hy_f_b1[0], hy_f_w2[0], hy_f_b2[0],
                            hy_f_w3[0], hy_f_b3[0], hy_f_w4[0], hy_bias[0], hy_w_out, n_ctx=C)
        else:
            y = na_mixer(h, na_w_qkv, na_rpb[0], na_w_out, n_ctx=C)
        n_ctx = C
        if last:
            s, n_ctx = s[:L], 0
        s, h = ln_modulate(s, mods, n_ctx=n_ctx, ys=(y,), ln_layer=i, gate_idx=G1, ln_g=ln1_g, ln_b=ln1_b,
                           h_layer=i, sc_idx=SC2, sh_idx=SH2)
        y = moe(h, i, *moe_args)
        if last:
            (s,) = ln_modulate(s, mods, n_ctx=n_ctx, ys=(y,), ln_layer=i, gate_idx=G2, ln_g=ln2_g, ln_b=ln2_b)
        else:
            s, h = ln_modulate(s, mods, n_ctx=n_ctx, ys=(y,), ln_layer=i, gate_idx=G2, ln_g=ln2_g, ln_b=ln2_b,
                               h_layer=i + 1, sc_idx=SC1, sh_idx=SH1)
    return s[None]


def mlstm_mixer(h, w_in, b_if, norm_g, w_out, *, n_ctx):
    D = h.shape[1]
    w_in_t = jnp.swapaxes(w_in, 1, 2)
    u = matmul(h, w_in_t, out_dtype=F32, layer=0, ncols=3 * D, w_transposed=True)
    w_if_t = jnp.pad(w_in_t[0, 3 * D:], ((0, LANES - 4 * ML_HEADS), (0, 0)))
    g_if = matmul(h, w_if_t, out_dtype=F32, w_transposed=True)
    hs = mlstm_scan(u, g_if, b_if, n_ctx=n_ctx)
    return matmul(mlstm_gate(hs, u, norm_g, o_col_block=2), w_out, out_dtype=F32, layer=0)


def diff_attn_mixer(h, w_qkv, lam4, subln_g, w_out, *, n_ctx, lambda_init):
    NT, D = h.shape
    L = NT - n_ctx
    scale = DA_HEAD_DIM ** -0.5 * math.log2(math.e)
    cos, sin = _rope_tables(L, 1.0)
    ones, zeros = jnp.ones((n_ctx, LANES), F32), jnp.zeros((n_ctx, LANES), F32)
    cos_k, sin_k = jnp.concatenate([cos, ones]), jnp.concatenate([sin, zeros])
    cos3 = jnp.stack([cos_k * scale, cos_k, jnp.ones_like(cos_k)])
    sin3 = jnp.stack([sin_k * scale, sin_k, jnp.zeros_like(sin_k)])
    qkv = matmul(h, w_qkv, out_dtype=BF16, layer=0, rope=(cos3, sin3, D))
    o_lat = diff_attention(qkv, lam4, subln_g, q_row0=0, n_q=L, kv_row0=0, kv_len=NT, lambda_init=lambda_init)
    o_ctx = diff_attention(qkv, lam4, subln_g, q_row0=L, n_q=n_ctx, kv_row0=L, kv_len=n_ctx, lambda_init=lambda_init)
    return matmul(jnp.concatenate([o_lat, o_ctx]), w_out, out_dtype=F32, layer=0)


def hyena_mixer(h, w_in, conv_w, conv_b, f_w1, f_b1, f_w2, f_b2, f_w3, f_b3, f_w4, hy_bias, w_out, *, n_ctx):
    NT, D = h.shape
    L = NT - n_ctx
    u = matmul(h, w_in, out_dtype=F32, layer=0, conv=(conv_w, conv_b, (0, L, NT)))
    fargs = (f_w1, f_b1, f_w2, f_b2, f_w3, f_b3, f_w4)
    bias = hy_bias.reshape(1, HY_ORDER * D)
    kern_l, asum_l = hyena_filters(L, D, *fargs, t2_major_n2=FFT_N2)
    z_lat = hyena_conv_lat(u.reshape(NT // FFT_N2, FFT_N2, 3 * D), kern_l, asum_l, bias, L=L, D=D)
    kern_c, asum_c = hyena_filters(n_ctx, D, *fargs)
    v, x1, x2 = (u[L:, j * D:(j + 1) * D] for j in range(3))
    z_ctx = hyena_conv_ctx(v, x1, x2, kern_c, asum_c, bias)
    return matmul(jnp.concatenate([z_lat, z_ctx]), w_out, out_dtype=F32, layer=0)


def na_mixer(h, w_qkv, rpb, w_out, *, n_ctx):
    qkv = matmul(h, w_qkv, out_dtype=BF16, layer=0)
    return matmul(na_attention(qkv, rpb, n_ctx=n_ctx), w_out, out_dtype=F32, layer=0)
```

```python
import functools
import math

import numpy as np
import jax
import jax.numpy as jnp
from jax import lax
from jax.experimental import pallas as pl
from jax.experimental.pallas import tpu as pltpu

F32 = jnp.float32
BF16 = jnp.bfloat16

V7X_VMEM_LIMIT_BYTES = 56 * 1024 * 1024
LANES = 128

GRID_W = 64
DEPTH = 4
DEEPNORM_ALPHA = (2 * DEPTH) ** 0.25
LN_EPS = 1e-5
RMS_EPS = 1e-6
ROPE_BASE = 10000.0
ML_HEADS = 8
ML_CHUNK = 256
DA_HEAD_DIM = 128
HY_ORDER = 2
HY_EMB = 33
HY_BANDS = (HY_EMB - 1) // 2
HY_FAST_DECAY = 0.3
HY_SLOW_DECAY = 1.5
HY_DECAY_TARGET = 1e-2
NA_HEAD_DIM = 128
NA_ROWS = 8
NA_COLS = 16
N_EXPERTS = 64
TOP_K = 8
N_GROUPS = 8
TOPK_GROUPS = 4
ROUTED_SCALE = 2.5


def _params(*sem):
    return pltpu.CompilerParams(dimension_semantics=sem, vmem_limit_bytes=V7X_VMEM_LIMIT_BYTES)


def _dot(a, b):
    return jnp.dot(a, b, preferred_element_type=F32)


def _dot_nt(a, b):
    return lax.dot_general(a, b, (((1,), (1,)), ((), ())), preferred_element_type=F32)


def _dot_tn(a, b):
    return lax.dot_general(a, b, (((0,), (0,)), ((), ())), preferred_element_type=F32)


def _pick(n, prefs):
    for p in prefs:
        if n % p == 0:
            return p
    return n


def _mm_kernel(*refs, nk, has_extra):
    if has_extra:
        a_ref, w_ref, a2_ref, w2_ref, o_ref, acc_ref = refs
    else:
        a_ref, w_ref, o_ref, acc_ref = refs
    k = pl.program_id(2)

    @pl.when(k == 0)
    def _():
        acc_ref[...] = jnp.zeros_like(acc_ref)

    acc_ref[...] += _dot(a_ref[...].astype(BF16), w_ref[...].astype(BF16))

    @pl.when(k == nk - 1)
    def _():
        acc = acc_ref[...]
        if has_extra:
            acc = acc + _dot(a2_ref[...].astype(BF16), w2_ref[...].astype(BF16))
        o_ref[...] = acc.astype(o_ref.dtype)


def _rope(x, cos, sin):
    lane = lax.broadcasted_iota(jnp.int32, x.shape, 1)
    rot = jnp.where((lane % 64) < 32, -pltpu.roll(x, LANES - 32, 1), pltpu.roll(x, 32, 1))
    return x * cos + rot * sin


def _mm_resident_kernel(a_ref, w_ref, *rest, w_transposed, rope, conv_seq_starts):
    o_ref = rest[-1]
    dot = _dot_nt if w_transposed else _dot
    w = w_ref[...].astype(BF16)
    acc = dot(a_ref[...].astype(BF16), w)
    if rope:
        cos, sin = rest[0][...], rest[1][...]
        for g in range(acc.shape[1] // LANES):
            sl = slice(g * LANES, (g + 1) * LANES)
            o_ref[:, sl] = _rope(acc[:, sl], cos, sin).astype(o_ref.dtype)
    elif conv_seq_starts is not None:
        prev_ref, next_ref, cw_ref, cb_ref = rest[:4]
        tm = acc.shape[0]
        p_prev = dot(prev_ref[...].astype(BF16), w)[7:8]
        p_next = dot(next_ref[...].astype(BF16), w)[0:1]
        local = lax.broadcasted_iota(jnp.int32, (tm, 1), 0)
        row = local + pl.program_id(0) * tm
        is_start = functools.reduce(jnp.logical_or, [row == r for r in conv_seq_starts[:-1]])
        is_end = functools.reduce(jnp.logical_or, [row == r - 1 for r in conv_seq_starts[1:]])
        xm = jnp.where(local == 0, p_prev, pltpu.roll(acc, 1, 0))
        xp = jnp.where(local == tm - 1, p_next, pltpu.roll(acc, tm - 1, 0))
        xm = jnp.where(is_start, 0.0, xm)
        xp = jnp.where(is_end, 0.0, xp)
        o_ref[...] = (xm * cw_ref[0:1, :] + acc * cw_ref[1:2, :] + xp * cw_ref[2:3, :] + cb_ref[...]).astype(o_ref.dtype)
    else:
        o_ref[...] = acc.astype(o_ref.dtype)


def _matmul_resident(a, w, *, out_dtype, layer, col0, N, w_transposed, rope, conv):
    M, K = a.shape
    tm = _pick(M, (2112, 2048, 1024, 512, 256, 128) if a.dtype == BF16 else (1056, 1024, 512, 256, 128))
    tn = _pick(N, (256, 128))
    cb = col0 // tn
    lead = (None,) if w.ndim == 3 else ()
    lidx = (layer,) if w.ndim == 3 else ()
    if w_transposed:
        w_spec = pl.BlockSpec(lead + (tn, K), lambda i, j: lidx + (j + cb, 0))
    else:
        w_spec = pl.BlockSpec(lead + (K, tn), lambda i, j: lidx + (0, j + cb))
    in_specs = [pl.BlockSpec((tm, K), lambda i, j: (i, 0), pipeline_mode=pl.Buffered(1)), w_spec]
    args = [a, w]
    conv_seq_starts = None
    if rope is not None:
        cos, sin, part_cols = rope
        bpp = part_cols // tn
        tab = pl.BlockSpec((None, tm, LANES), lambda i, j: (j // bpp, i, 0))
        in_specs += [tab, tab]
        args += [cos, sin]
    elif conv is not None:
        conv_w, conv_b, conv_seq_starts = conv
        t8, n8 = tm // 8, M // 8
        in_specs += [pl.BlockSpec((8, K), lambda i, j: (jnp.maximum(i * t8 - 1, 0), 0)),
                     pl.BlockSpec((8, K), lambda i, j: (jnp.minimum((i + 1) * t8, n8 - 1), 0)),
                     pl.BlockSpec((3, tn), lambda i, j: (0, j)), pl.BlockSpec((1, tn), lambda i, j: (0, j))]
        args += [a, a, conv_w, conv_b.reshape(1, N)]
    return pl.pallas_call(
        functools.partial(_mm_resident_kernel, w_transposed=w_transposed, rope=rope is not None,
                          conv_seq_starts=conv_seq_starts),
        grid=(M // tm, N // tn),
        in_specs=in_specs,
        out_specs=pl.BlockSpec((tm, tn), lambda i, j: (i, j)),
        out_shape=jax.ShapeDtypeStruct((M, N), out_dtype),
        compiler_params=_params("parallel", "arbitrary"),
    )(*args)


def matmul(a, w, *, out_dtype, layer=None, col0=0, ncols=None, extra=None, row0=0, nrows=None, w_transposed=False,
           rope=None, conv=None):
    M = a.shape[0] if nrows is None else nrows
    K = a.shape[1]
    N = (w.shape[-2] if w_transposed else w.shape[-1]) if ncols is None else ncols
    if K <= 4096 and extra is None and nrows is None and M >= 1024:
        return _matmul_resident(a, w, out_dtype=out_dtype, layer=layer, col0=col0, N=N, w_transposed=w_transposed,
                                rope=rope, conv=conv)
    assert not w_transposed and rope is None and conv is None
    tm = _pick(M, (1056, 1024, 512, 256, 128, 64, 32, 16, 8))
    tn = _pick(N, (1024, 512, 256, 128))
    tk = _pick(K, (2048, 1024, 512, 256, 128))
    assert col0 % tn == 0 and row0 % tm == 0
    cb, rb = col0 // tn, row0 // tm
    nk = K // tk
    if w.ndim == 3:
        w_spec = pl.BlockSpec((None, tk, tn), lambda i, j, k: (layer, k, j + cb))
    else:
        w_spec = pl.BlockSpec((tk, tn), lambda i, j, k: (k, j + cb))
    in_specs = [pl.BlockSpec((tm, tk), lambda i, j, k: (i + rb, k)), w_spec]
    args = [a, w]
    if extra is not None:
        a2, w2, layer2 = extra
        k2 = a2.shape[1]
        in_specs.append(pl.BlockSpec((tm, k2), lambda i, j, k: (i + rb, 0)))
        in_specs.append(pl.BlockSpec((None, k2, tn), lambda i, j, k: (layer2, 0, j)))
        args += [a2, w2]
    return pl.pallas_call(
        functools.partial(_mm_kernel, nk=nk, has_extra=extra is not None),
        grid=(M // tm, N // tn, nk),
        in_specs=in_specs,
        out_specs=pl.BlockSpec((tm, tn), lambda i, j, k: (i, j)),
        out_shape=jax.ShapeDtypeStruct((M, N), out_dtype),
        scratch_shapes=[pltpu.VMEM((tm, tn), F32)],
        compiler_params=_params("parallel", "parallel", "arbitrary"),
    )(*args)


def _ada_kernel(cc_ref, a_ref, b_ref, bias_ref, o_ref):
    cc = cc_ref[...]
    act = cc * jax.nn.sigmoid(cc)
    t = _dot(act.astype(BF16), a_ref[...].astype(BF16))
    o_ref[...] = _dot(t.astype(BF16), b_ref[...].astype(BF16)) + bias_ref[...]


def ada_modulation(cc, ada_a, ada_b, ada_bias):
    depth, D, R = ada_a.shape
    n6 = ada_b.shape[-1]
    tn = D
    return pl.pallas_call(
        _ada_kernel,
        grid=(depth, n6 // tn),
        in_specs=[
            pl.BlockSpec((8, D), lambda l, n: (0, 0)),
            pl.BlockSpec((None, D, R), lambda l, n: (l, 0, 0)),
            pl.BlockSpec((None, R, tn), lambda l, n: (l, 0, n)),
            pl.BlockSpec((None, 1, tn), lambda l, n: (l, 0, n)),
        ],
        out_specs=pl.BlockSpec((None, 8, tn), lambda l, n: (l, 0, n)),
        out_shape=jax.ShapeDtypeStruct((depth, 8, n6), F32),
        compiler_params=_params("parallel", "parallel"),
    )(cc, ada_a, ada_b, ada_bias.reshape(depth, 1, n6))


def _lnmod_kernel(*refs, has_ln, has_h, n_y):
    it = iter(refs)
    s_ref = next(it)
    if has_ln:
        y_refs = [next(it) for _ in range(n_y)]
        gate_ref, lng_ref, lnb_ref = next(it), next(it), next(it)
    if has_h:
        sc_ref, sh_ref = next(it), next(it)
    if has_ln:
        so_ref = next(it)
    if has_h:
        h_ref = next(it)
    x = s_ref[...]
    if has_ln:
        y = y_refs[0][...].astype(F32)
        for r in y_refs[1:]:
            y = y + r[...].astype(F32)
        v = DEEPNORM_ALPHA * x + gate_ref[...] * y
        mu = jnp.mean(v, axis=-1, keepdims=True)
        var = jnp.mean(jnp.square(v - mu), axis=-1, keepdims=True)
        x = (v - mu) * lax.rsqrt(var + LN_EPS) * lng_ref[...] + lnb_ref[...]
        so_ref[...] = x
    if has_h:
        h_ref[...] = (x * (1.0 + sc_ref[...]) + sh_ref[...]).astype(h_ref.dtype)


def ln_modulate(s, mods, *, n_ctx, ys=(), ln_layer=None, gate_idx=None, ln_g=None, ln_b=None,
                h_layer=None, sc_idx=None, sh_idx=None):
    R, D = s.shape
    tr = 128
    has_ln, has_h = len(ys) > 0, h_layer is not None
    n_lat_blocks = (R - n_ctx) // tr
    row_spec = pl.BlockSpec((tr, D), lambda i: (i, 0))

    def mod_spec(layer, idx):
        return pl.BlockSpec((None, None, None, 1, D),
                            lambda i: (layer, jnp.where(i >= n_lat_blocks, 0, 1), idx, 0, 0))

    vec_spec = pl.BlockSpec((None, 1, D), lambda i: (ln_layer, 0, 0))
    in_specs, args = [row_spec], [s]
    if has_ln:
        in_specs += [row_spec] * len(ys) + [mod_spec(ln_layer, gate_idx), vec_spec, vec_spec]
        args += list(ys) + [mods, ln_g.reshape(DEPTH, 1, D), ln_b.reshape(DEPTH, 1, D)]
    if has_h:
        in_specs += [mod_spec(h_layer, sc_idx), mod_spec(h_layer, sh_idx)]
        args += [mods, mods]
    out_specs, out_shape = [], []
    if has_ln:
        out_specs.append(row_spec)
        out_shape.append(jax.ShapeDtypeStruct((R, D), F32))
    if has_h:
        out_specs.append(row_spec)
        out_shape.append(jax.ShapeDtypeStruct((R, D), BF16))
    outs = pl.pallas_call(
        functools.partial(_lnmod_kernel, has_ln=has_ln, has_h=has_h, n_y=len(ys)),
        grid=(R // tr,),
        in_specs=in_specs,
        out_specs=out_specs,
        out_shape=out_shape,
        compiler_params=_params("parallel"),
    )(*args)
    return outs


def _log_sigmoid(x):
    return jnp.minimum(x, 0.0) - jnp.log(1.0 + jnp.exp(-jnp.abs(x)))


def _mlstm_kernel(bias_ref, q_ref, k_ref, v_ref, ic_ref, fc_ref, ir_ref, fr_ref, o_ref, ct_ref, n_ref, m_ref,
                  *, T, dqk, dv, hps):
    d, t = pl.program_id(0), pl.program_id(2)

    @pl.when(t == 0)
    def _():
        ct_ref[...] = jnp.zeros_like(ct_ref)
        n_ref[...] = jnp.zeros_like(n_ref)
        m_ref[...] = jnp.zeros_like(m_ref)

    for hh in range(hps):
        _mlstm_head(d, pl.program_id(1) * hps + hh, hh, bias_ref, q_ref, k_ref, v_ref, ic_ref, fc_ref, ir_ref, fr_ref,
                    o_ref, ct_ref, n_ref, m_ref, T=T, dqk=dqk, dv=dv)


def _mlstm_head(d, h, hh, bias_ref, q_ref, k_ref, v_ref, ic_ref, fc_ref, ir_ref, fr_ref, o_ref, ct_ref, n_ref, m_ref,
                *, T, dqk, dv):
    qs, vs = slice(hh * dqk, (hh + 1) * dqk), slice(hh * dv, (hh + 1) * dv)
    bi = bias_ref[d * 2 * ML_HEADS + h]
    bf = bias_ref[(d * 2 + 1) * ML_HEADS + h]
    i_c = ic_ref[hh] + bi
    i_r = ir_ref[hh] + bi
    f_c = _log_sigmoid(fc_ref[hh] + bf)
    f_r = _log_sigmoid(fr_ref[hh] + bf)
    row = lax.broadcasted_iota(jnp.int32, (T, T), 0)
    col = lax.broadcasted_iota(jnp.int32, (T, T), 1)
    sgn = 1 - 2 * d
    incl = (col - row) * sgn <= 0
    incl_t = (row - col) * sgn <= 0
    b_c = jnp.sum(jnp.where(incl, f_r, 0.0), axis=1, keepdims=True)
    b_r = jnp.sum(jnp.where(incl_t, f_c, 0.0), axis=0, keepdims=True)
    f_tot = jnp.sum(f_r, axis=1, keepdims=True)
    m_prev = m_ref[hh]
    dmat = jnp.where(incl, b_c - b_r + i_r, -jnp.inf)
    inter = b_c + m_prev
    m_t = jnp.maximum(inter, jnp.max(dmat, axis=1, keepdims=True))
    w_intra = jnp.exp(dmat - m_t)
    w_inter = jnp.exp(inter - m_t)
    q = q_ref[:, qs] * (dqk ** -0.5)
    k = k_ref[:, qs]
    v = v_ref[:, vs]
    qb, kb, vb = q.astype(BF16), k.astype(BF16), v.astype(BF16)
    s = _dot_nt(qb, kb) * w_intra
    ct = ct_ref[hh]
    n = n_ref[hh]
    num = w_inter * _dot(qb, ct.astype(BF16)) + _dot(s.astype(BF16), vb)
    den = w_inter * jnp.sum(q * n, axis=1, keepdims=True) + jnp.sum(s, axis=1, keepdims=True)
    o_ref[:, vs] = num / jnp.maximum(jnp.abs(den), jnp.exp(-m_t))
    g_r = f_tot - b_r + i_r
    g_c = f_tot - b_c + i_c
    m_new = jnp.maximum(f_tot + m_prev, jnp.max(g_r, axis=1, keepdims=True))
    decay = jnp.exp(f_tot + m_prev - m_new)
    wk = jnp.exp(g_c - m_new)
    ct_ref[hh] = decay * ct + _dot_tn(kb, (wk * v).astype(BF16))
    n_ref[hh] = decay * n + jnp.sum(wk * k, axis=0, keepdims=True)
    m_ref[hh] = m_new


def mlstm_scan(u, gates, b_if, *, n_ctx):
    NT = u.shape[0]
    H, T = ML_HEADS, ML_CHUNK
    D = u.shape[1] // 3
    dqk, dv = D // (2 * H), D // H
    nc, ncc = NT // T, n_ctx // T
    ncl = nc - ncc
    g4 = gates[:, :4 * H].T
    g_col = g4.reshape(4 * H, NT, 1)
    g_row = g4.reshape(4 * H, nc, 1, T)

    def chunk(d, t):
        fwd = jnp.where(t < ncc, ncl + t, t - ncc)
        return jnp.where(d == 0, fwd, nc - 1 - t)

    hps = 2
    hb = H // hps
    col_spec = lambda off: pl.BlockSpec((hps, T, 1), lambda d, h, t: ((2 * d + off) * hb + h, chunk(d, t), 0))
    row_spec = lambda off: pl.BlockSpec((hps, None, 1, T), lambda d, h, t: ((2 * d + off) * hb + h, chunk(d, t), 0, 0))
    return pl.pallas_call(
        functools.partial(_mlstm_kernel, T=T, dqk=dqk, dv=dv, hps=hps),
        grid=(2, hb, nc),
        in_specs=[
            pl.BlockSpec(memory_space=pltpu.SMEM),
            pl.BlockSpec((T, hps * dqk), lambda d, h, t: (chunk(d, t), h)),
            pl.BlockSpec((T, hps * dqk), lambda d, h, t: (chunk(d, t), hb + h)),
            pl.BlockSpec((T, hps * dv), lambda d, h, t: (chunk(d, t), hb + h)),
            col_spec(0), col_spec(1), row_spec(0), row_spec(1),
        ],
        out_specs=pl.BlockSpec((None, T, hps * dv), lambda d, h, t: (d, chunk(d, t), h)),
        out_shape=jax.ShapeDtypeStruct((2, NT, H * dv), F32),
        scratch_shapes=[pltpu.VMEM((hps, dqk, dv), F32), pltpu.VMEM((hps, 1, dqk), F32), pltpu.VMEM((hps, 1, 1), F32)],
        compiler_params=_params("parallel", "parallel", "arbitrary"),
    )(b_if, u, u, u, g_col, g_col, g_row, g_row)


def _mlgate_kernel(hs_ref, o_ref, g_ref, out_ref, *, dv):
    D = out_ref.shape[1]
    for h in range(D // dv):
        sl = slice(h * dv, (h + 1) * dv)
        x = hs_ref[0, :, sl] + hs_ref[1, :, sl]
        r = lax.rsqrt(jnp.mean(jnp.square(x), axis=-1, keepdims=True) + RMS_EPS)
        out_ref[:, sl] = (x * r * g_ref[:, sl] * jax.nn.sigmoid(o_ref[:, sl])).astype(out_ref.dtype)


def mlstm_gate(hs, u, norm_g, *, o_col_block):
    _, NT, D = hs.shape
    tr = 128
    return pl.pallas_call(
        functools.partial(_mlgate_kernel, dv=D // ML_HEADS),
        grid=(NT // tr,),
        in_specs=[
            pl.BlockSpec((2, tr, D), lambda i: (0, i, 0)),
            pl.BlockSpec((tr, D), lambda i: (i, o_col_block)),
            pl.BlockSpec((1, D), lambda i: (0, 0)),
        ],
        out_specs=pl.BlockSpec((tr, D), lambda i: (i, 0)),
        out_shape=jax.ShapeDtypeStruct((NT, D), BF16),
        compiler_params=_params("parallel"),
    )(hs, u, norm_g.reshape(1, D))


SAFE_LOG2 = 64.0


def _dattn_kernel(lam_ref, q_ref, k_ref, v_ref, g_ref, o_ref, m_ref, l_ref, acc_ref, kn_ref,
                  *, tk, n_chunks, out_scale, lambda_init):
    dh = DA_HEAD_DIM
    nlt = tk // LANES
    l_ref[...] = jnp.zeros_like(l_ref)
    acc_ref[...] = jnp.zeros_like(acc_ref)

    @pl.when(pl.program_id(1) == 0)
    def _():
        for i in range(2):
            kk = k_ref[:, i * dh:(i + 1) * dh].astype(F32)
            k2 = jnp.max(jnp.sum(kk * kk, axis=1, keepdims=True), axis=0, keepdims=True)
            kn_ref[i] = jnp.broadcast_to(k2, kn_ref.shape[1:])

    bound2 = None
    for i in range(2):
        qq = q_ref[:, i * dh:(i + 1) * dh].astype(F32)
        b2 = jnp.max(jnp.sum(qq * qq, axis=1, keepdims=True), axis=0, keepdims=True) * kn_ref[i][0:1, 0:1]
        bound2 = b2 if bound2 is None else jnp.maximum(bound2, b2)
    scores_bounded = bound2[0, 0] <= SAFE_LOG2 * SAFE_LOG2

    def scores(c, i):
        kc = k_ref[pl.ds(pl.multiple_of(c * tk, tk), tk), i * dh:(i + 1) * dh]
        return _dot_nt(q_ref[:, i * dh:(i + 1) * dh], kc)

    @pl.when(scores_bounded)
    def _():
        m_ref[...] = jnp.zeros_like(m_ref)

    @pl.when(jnp.logical_not(scores_bounded))
    def _():
        m_ref[...] = jnp.full_like(m_ref, -jnp.inf)

        def max_pass(c, carry):
            for i in range(2):
                s = scores(c, i)
                m = m_ref[i]
                for t in range(nlt):
                    m = jnp.maximum(m, s[:, t * LANES:(t + 1) * LANES])
                m_ref[i] = m
            return carry

        lax.fori_loop(0, n_chunks, max_pass, 0)
        for i in range(2):
            m_ref[i] = jnp.broadcast_to(jnp.max(m_ref[i], axis=1, keepdims=True), m_ref.shape[1:])

    tq = q_ref.shape[0]

    def sum_pass(c, carry):
        vc = v_ref[pl.ds(pl.multiple_of(c * tk, tk), tk), :]
        ps = []
        for i in range(2):
            s = scores(c, i)
            m = m_ref[i]
            p = [jnp.exp2(s[:, t * LANES:(t + 1) * LANES] - m) for t in range(nlt)]
            l_ref[i] += functools.reduce(jnp.add, p)
            ps.append(jnp.concatenate(p, axis=1).astype(BF16))
        acc_ref[...] += _dot(jnp.concatenate(ps, axis=0), vc)
        return carry

    lax.fori_loop(0, n_chunks, sum_pass, 0)
    lam = (jnp.exp(jnp.sum(lam_ref[0:1, :] * lam_ref[1:2, :], axis=1, keepdims=True))
           - jnp.exp(jnp.sum(lam_ref[2:3, :] * lam_ref[3:4, :], axis=1, keepdims=True)) + lambda_init)
    l = [jnp.sum(l_ref[i], axis=1, keepdims=True) for i in range(2)]
    o = acc_ref[0:tq] / l[0] - lam * (acc_ref[tq:2 * tq] / l[1])
    r = lax.rsqrt(jnp.mean(jnp.square(o), axis=-1, keepdims=True) + RMS_EPS)
    o_ref[...] = (o * r * g_ref[...] * out_scale).astype(o_ref.dtype)


def diff_attention(qkv, lam4, subln_g, *, q_row0, n_q, kv_row0, kv_len, lambda_init):
    D = qkv.shape[1] // 3
    hd = 2 * DA_HEAD_DIM
    H = D // hd
    tq = _pick(n_q, (512, 256))
    tk = _pick(kv_len, (768, 512, 256))
    qb, kvb = q_row0 // tq, kv_row0 // kv_len
    assert kv_row0 % kv_len == 0 and q_row0 % tq == 0
    return pl.pallas_call(
        functools.partial(_dattn_kernel, tk=tk, n_chunks=kv_len // tk, out_scale=1.0 - lambda_init,
                          lambda_init=lambda_init),
        grid=(H, n_q // tq),
        in_specs=[
            pl.BlockSpec((4, DA_HEAD_DIM), lambda h, i: (0, 0)),
            pl.BlockSpec((tq, hd), lambda h, i: (i + qb, h)),
            pl.BlockSpec((kv_len, hd), lambda h, i: (kvb, H + h)),
            pl.BlockSpec((kv_len, hd), lambda h, i: (kvb, 2 * H + h)),
            pl.BlockSpec((1, hd), lambda h, i: (0, 0)),
        ],
        out_specs=pl.BlockSpec((tq, hd), lambda h, i: (i, h)),
        out_shape=jax.ShapeDtypeStruct((n_q, D), BF16),
        scratch_shapes=[pltpu.VMEM((2, tq, LANES), F32), pltpu.VMEM((2, tq, LANES), F32),
                        pltpu.VMEM((2 * tq, hd), F32), pltpu.VMEM((2, 8, LANES), F32)],
        compiler_params=_params("parallel", "arbitrary"),
    )(lam4, qkv, qkv, qkv, subln_g.reshape(1, hd))


def _rope_tables(L, scale):
    a = DA_HEAD_DIM // 2
    t = jnp.arange(L)
    pos = jnp.stack([t // GRID_W, t % GRID_W], -1).astype(F32)
    inv = ROPE_BASE ** (-jnp.arange(0, a, 2, dtype=F32) / a)
    ang = pos[:, :, None] * inv
    ang = jnp.concatenate([ang, ang], -1).reshape(L, DA_HEAD_DIM)
    return jnp.cos(ang) * scale, jnp.sin(ang) * scale


def _filter_kernel(zf_ref, t_ref, w1_ref, b1_ref, w2_ref, b2_ref, w3_ref, b3_ref, w4f_ref, w4b_ref, dl_ref,
                   o_ref, asum_ref, hdn_ref):
    i, j = pl.program_id(0), pl.program_id(1)

    @pl.when(j == 0)
    def _():
        hdn = jnp.sin(_dot(zf_ref[...].astype(BF16), w1_ref[...].astype(BF16)) + b1_ref[...])
        hdn = jnp.sin(_dot(hdn.astype(BF16), w2_ref[...].astype(BF16)) + b2_ref[...])
        hdn_ref[...] = jnp.sin(_dot(hdn.astype(BF16), w3_ref[...].astype(BF16)) + b3_ref[...]).astype(hdn_ref.dtype)

    hdn = hdn_ref[...]
    hf = _dot(hdn, w4f_ref[...].astype(BF16))
    hb = _dot(hdn, w4b_ref[...].astype(BF16))
    side = t_ref[:, 1:2]
    kern = jnp.where(side > 0.0, hf, jnp.where(side < 0.0, hb, 0.0)) * jnp.exp(-t_ref[:, 0:1] * dl_ref[...])
    o_ref[...] = kern

    @pl.when(i == 0)
    def _():
        asum_ref[j] = jnp.zeros(asum_ref.shape[1:], F32)

    asum_ref[j] += jnp.sum(jnp.abs(kern), axis=0, keepdims=True)


def hyena_filters(L, D, f_w1, f_b1, f_w2, f_b2, f_w3, f_b3, f_w4, *, t2_major_n2=None):
    t = jnp.linspace(0.0, 1.0, L, dtype=F32)[:, None]
    w = 2.0 * math.pi * jnp.arange(L, dtype=F32)[:, None] / L
    f = jnp.linspace(1e-4, HY_BANDS - 1, HY_BANDS, dtype=F32)[None]
    z = jnp.concatenate([t, jnp.cos(w * f), -jnp.sin(w * f)], -1)
    src = jnp.concatenate([jnp.arange(L), jnp.zeros((1,), jnp.int32), L - 1 - jnp.arange(L - 1)])
    r = jnp.arange(2 * L)
    side = jnp.where(r < L, 1.0, jnp.where(r > L, -1.0, 0.0)).astype(F32)[:, None]
    if t2_major_n2 is not None:
        perm = r.reshape(2 * L // t2_major_n2, t2_major_n2).T.reshape(-1)
        src, side = src[perm], side[perm]
    zf = jnp.pad(z[src], ((0, 0), (0, LANES - HY_EMB)))
    tcol = jnp.concatenate([t[src], side], axis=1)
    deltas = jnp.abs(jnp.linspace(math.log(HY_DECAY_TARGET) / HY_SLOW_DECAY, math.log(HY_DECAY_TARGET) / HY_FAST_DECAY,
                                  D, dtype=F32)).reshape(1, D)
    w1 = jnp.pad(f_w1, ((0, LANES - HY_EMB), (0, 0)))
    nf = f_w1.shape[1]
    R2 = 2 * L
    tr = _pick(R2, (512,))
    tc = _pick(D, (1024, 512, 256, 128))
    cpd = D // tc
    ncol = HY_ORDER * cpd
    full = lambda shape: pl.BlockSpec(shape, lambda i, j: (0, 0))
    kern, asum = pl.pallas_call(
        _filter_kernel,
        grid=(R2 // tr, ncol),
        in_specs=[
            pl.BlockSpec((tr, LANES), lambda i, j: (i, 0)),
            pl.BlockSpec((tr, 2), lambda i, j: (i, 0)),
            full((LANES, nf)), full((1, nf)), full((nf, nf)), full((1, nf)), full((nf, nf)), full((1, nf)),
            pl.BlockSpec((nf, tc), lambda i, j: (0, (j // cpd) * 2 * cpd + j % cpd)),
            pl.BlockSpec((nf, tc), lambda i, j: (0, ((j // cpd) * 2 + 1) * cpd + j % cpd)),
            pl.BlockSpec((1, tc), lambda i, j: (0, j % cpd)),
        ],
        out_specs=[pl.BlockSpec((tr, tc), lambda i, j: (i, j)), pl.BlockSpec((ncol, 1, tc), lambda i, j: (0, 0, 0))],
        out_shape=[jax.ShapeDtypeStruct((R2, HY_ORDER * D), F32), jax.ShapeDtypeStruct((ncol, 1, tc), F32)],
        scratch_shapes=[pltpu.VMEM((tr, nf), BF16)],
        compiler_params=_params("arbitrary", "arbitrary"),
    )(zf, tcol, w1, f_b1.reshape(1, nf), f_w2, f_b2.reshape(1, nf), f_w3, f_b3.reshape(1, nf), f_w4, f_w4, deltas)
    return kern, asum.reshape(1, HY_ORDER * D)


FFT_N2 = 128
FFT_TT2 = 8


def _split(x):
    hi = x.astype(BF16)
    return hi, (x - hi.astype(F32)).astype(BF16)


def _dot3(w_hi, w_lo, x):
    x_hi, x_lo = _split(x)
    return _dot(w_hi, x_hi) + (_dot(w_lo, x_hi) + _dot(w_hi, x_lo))


def _hilo(a):
    return _split(jnp.asarray(a, F32))


def _fft_first_kernel(whi_ref, wlo_ref, z_ref, o_ref, *, t2_major):
    if t2_major:
        n1 = o_ref.shape[1]
        for j in range(o_ref.shape[2]):
            a = _dot3(whi_ref[...], wlo_ref[...], z_ref[j])
            o_ref[0, :, j, :] = a[:n1]
            o_ref[1, :, j, :] = a[n1:]
    else:
        l1, tt2, tc = z_ref.shape
        a = _dot3(whi_ref[...], wlo_ref[...], z_ref[...].reshape(l1 * tt2, tc))
        o_ref[...] = a.reshape(o_ref.shape)


def fft_first(w_hl, z3, *, col0, C, t2_major=False):
    w_hi, w_lo = w_hl
    tt2 = FFT_TT2
    n1x2, l1 = w_hi.shape if t2_major else (w_hi.shape[0] // tt2, w_hi.shape[1] // tt2)
    n1, n2 = n1x2 // 2, z3.shape[0 if t2_major else 1]
    tc = _pick(C, (512, 256, 128))
    cb = col0 // tc
    full = pl.BlockSpec(w_hi.shape, lambda t, j: (0, 0))
    if t2_major:
        z_spec = pl.BlockSpec((tt2, l1, tc), lambda t, j: (t, 0, j + cb))
    else:
        z_spec = pl.BlockSpec((l1, tt2, tc), lambda t, j: (0, t, j + cb))
    return pl.pallas_call(
        functools.partial(_fft_first_kernel, t2_major=t2_major),
        grid=(n2 // tt2, C // tc),
        in_specs=[full, full, z_spec],
        out_specs=pl.BlockSpec((2, n1, tt2, tc), lambda t, j: (0, 0, t, j)),
        out_shape=jax.ShapeDtypeStruct((2, n1, n2, C), F32),
        compiler_params=_params("parallel", "parallel"),
    )(w_hi, w_lo, z3)


def _fft_mid_kernel(*refs, g, conv):
    if conv:
        a_ref, kf_ref, twc_ref, tws_ref, w2h_ref, w2l_ref, w2ih_ref, w2il_ref, o_ref = refs
    else:
        a_ref, twc_ref, tws_ref, w2h_ref, w2l_ref, o_ref = refs
    n2 = a_ref.shape[2]
    for kk in range(g):
        ar, ai = a_ref[0, kk], a_ref[1, kk]
        c, s = twc_ref[kk], tws_ref[kk]
        x = _dot3(w2h_ref[...], w2l_ref[...], jnp.concatenate([ar * c + ai * s, ai * c - ar * s], axis=0))
        if conv:
            xr, xi = x[:n2], x[n2:]
            kr, ki = kf_ref[0, kk], kf_ref[1, kk]
            y = _dot3(w2ih_ref[...], w2il_ref[...], jnp.concatenate([xr * kr - xi * ki, xr * ki + xi * kr], axis=0))
            yr, yi = y[:n2], y[n2:]
            o_ref[0, kk] = yr * c - yi * s
            o_ref[1, kk] = yr * s + yi * c
        else:
            o_ref[0, kk] = x[:n2]
            o_ref[1, kk] = x[n2:]


def fft_mid(a, tw_c, tw_s, w2_hl, w2i_hl=None, kf=None, *, kf_col0=0):
    _, n1, n2, C = a.shape
    g = 4
    tc = _pick(C, (512, 256, 128))
    conv = kf is not None
    blk = pl.BlockSpec((2, g, n2, tc), lambda j, k: (0, k, 0, j))
    tw = pl.BlockSpec((g, n2, 1), lambda j, k: (k, 0, 0))
    mat = pl.BlockSpec((2 * n2, 2 * n2), lambda j, k: (0, 0))
    if conv:
        kb = kf_col0 // tc
        in_specs = [blk, pl.BlockSpec((2, g, n2, tc), lambda j, k: (0, k, 0, j + kb)), tw, tw, mat, mat, mat, mat]
        args = [a, kf, tw_c, tw_s, *w2_hl, *w2i_hl]
    else:
        in_specs = [blk, tw, tw, mat, mat]
        args = [a, tw_c, tw_s, *w2_hl]
    return pl.pallas_call(
        functools.partial(_fft_mid_kernel, g=g, conv=conv),
        grid=(C // tc, n1 // g),
        in_specs=in_specs,
        out_specs=blk,
        out_shape=jax.ShapeDtypeStruct(a.shape, F32),
        compiler_params=_params("parallel", "parallel"),
    )(*args)


def _fft_last_kernel(whi_ref, wlo_ref, b_ref, gate_ref, zp_ref, asum_ref, bias_ref, o_ref):
    inv_asum = 1.0 / asum_ref[...]
    for j in range(b_ref.shape[2]):
        bj = jnp.concatenate([b_ref[0, :, j, :], b_ref[1, :, j, :]], axis=0)
        x = _dot3(whi_ref[...], wlo_ref[...], bj)
        o_ref[:, j, :] = gate_ref[:, j, :] * (x * inv_asum + zp_ref[:, j, :] * bias_ref[...])


def fft_last(w_hl, b, gate3, gate_col0, zp3, zp_col0, asum, bias, *, vec_col0):
    w_hi, w_lo = w_hl
    l1 = w_hi.shape[0]
    _, n1, n2, C = b.shape
    tt2 = FFT_TT2
    tc = _pick(C, (512, 256, 128))
    gcb, zcb, vcb = gate_col0 // tc, zp_col0 // tc, vec_col0 // tc
    full = pl.BlockSpec((l1, 2 * n1), lambda t, j: (0, 0))
    return pl.pallas_call(
        _fft_last_kernel,
        grid=(n2 // tt2, C // tc),
        in_specs=[
            full, full,
            pl.BlockSpec((2, n1, tt2, tc), lambda t, j: (0, 0, t, j)),
            pl.BlockSpec((l1, tt2, tc), lambda t, j: (0, t, j + gcb)),
            pl.BlockSpec((l1, tt2, tc), lambda t, j: (0, t, j + zcb)),
            pl.BlockSpec((1, tc), lambda t, j: (0, j + vcb)),
            pl.BlockSpec((1, tc), lambda t, j: (0, j + vcb)),
        ],
        out_specs=pl.BlockSpec((l1, tt2, tc), lambda t, j: (0, t, j)),
        out_shape=jax.ShapeDtypeStruct((l1, n2, C), F32),
        compiler_params=_params("parallel", "parallel"),
    )(w_hi, w_lo, b, gate3, zp3, asum, bias)


def _hpmm_kernel(*refs, gated):
    if gated:
        ah_ref, al_ref, b_ref, gate_ref, z_ref, asum_ref, bias_ref, o_ref = refs
    else:
        ah_ref, al_ref, b_ref, o_ref = refs
    y = _dot3(ah_ref[...], al_ref[...], b_ref[...])
    if gated:
        y = gate_ref[...] * (y / asum_ref[...] + z_ref[...] * bias_ref[...])
    o_ref[...] = y


def hp_matmul(a_hl, b, *, gate=None, z=None, asum=None, bias=None, vec_col0=0):
    a_hi, a_lo = a_hl
    M, Kd = a_hi.shape
    N = b.shape[1]
    tn = _pick(N, (1024, 512, 256, 128))
    full = pl.BlockSpec((M, Kd), lambda j: (0, 0))
    in_specs = [full, full, pl.BlockSpec((Kd, tn), lambda j: (0, j))]
    args = [a_hi, a_lo, b]
    gated = gate is not None
    if gated:
        vb = vec_col0 // tn
        blk = pl.BlockSpec((M, tn), lambda j: (0, j))
        vec = pl.BlockSpec((1, tn), lambda j: (0, j + vb))
        in_specs += [blk, blk, vec, vec]
        args += [gate, z, asum, bias]
    return pl.pallas_call(
        functools.partial(_hpmm_kernel, gated=gated),
        grid=(N // tn,),
        in_specs=in_specs,
        out_specs=pl.BlockSpec((M, tn), lambda j: (0, j)),
        out_shape=jax.ShapeDtypeStruct((M, N), F32),
        compiler_params=_params("parallel"),
    )(*args)


def _cmul_kernel(x_ref, k_ref, o_ref):
    xr, xi, kr, ki = x_ref[0], x_ref[1], k_ref[0], k_ref[1]
    o_ref[0] = xr * kr - xi * ki
    o_ref[1] = xr * ki + xi * kr


def complex_mul(x, kf, *, kf_col0):
    _, R, C = x.shape
    tc = _pick(C, (512, 256, 128))
    kb = kf_col0 // tc
    return pl.pallas_call(
        _cmul_kernel,
        grid=(C // tc,),
        in_specs=[pl.BlockSpec((2, R, tc), lambda j: (0, 0, j)), pl.BlockSpec((2, R, tc), lambda j: (0, 0, j + kb))],
        out_specs=pl.BlockSpec((2, R, tc), lambda j: (0, 0, j)),
        out_shape=jax.ShapeDtypeStruct(x.shape, F32),
        compiler_params=_params("parallel"),
    )(x, kf)


def _dft_constants(n1, n2, l1):
    N = n1 * n2
    n1h = min(n1, -(-(n1 // 2 + 1) // 8) * 8)
    k1 = np.arange(n1h)[:, None]
    herm = np.where((k1 == 0) | (k1 == n1 // 2), 1.0, np.where(k1 < n1 // 2, 2.0, 0.0))
    ang1 = 2 * np.pi * k1 * np.arange(n1)[None, :] / n1
    w1 = np.concatenate([np.cos(ang1), -np.sin(ang1)], 0)
    w1_inv = np.concatenate([(herm * np.cos(ang1[:, :l1])).T, (-herm * np.sin(ang1[:, :l1])).T], 1) / N
    ang_t = 2 * np.pi * k1 * np.arange(n2)[None, :] / N
    k2 = np.arange(n2)[:, None]
    ang2 = 2 * np.pi * k2 * np.arange(n2)[None, :] / n2
    c2, s2 = np.cos(ang2), np.sin(ang2)
    f = lambda x: jnp.asarray(x, F32)
    return dict(w1_data=_hilo(np.kron(w1[:, :l1], np.eye(FFT_TT2))), w1_full=_hilo(w1), w1_inv=_hilo(w1_inv),
                tw_c=f(np.cos(ang_t))[:, :, None], tw_s=f(np.sin(ang_t))[:, :, None],
                w2=_hilo(np.block([[c2, s2], [-s2, c2]])), w2_inv=_hilo(np.block([[c2, -s2], [s2, c2]])))


def hyena_conv_lat(u3, kern, asum, bias, *, L, D):
    n2 = FFT_N2
    n1, l1 = 2 * L // n2, L // n2
    cst = _dft_constants(n1, n2, l1)
    ka = fft_first(cst["w1_full"], kern.reshape(n2, n1, HY_ORDER * D), col0=0, C=HY_ORDER * D, t2_major=True)
    kf = fft_mid(ka, cst["tw_c"], cst["tw_s"], cst["w2"])
    z3, zc0 = u3, 0
    for o in range(HY_ORDER):
        a = fft_first(cst["w1_data"], z3, col0=zc0, C=D)
        bmid = fft_mid(a, cst["tw_c"], cst["tw_s"], cst["w2"], cst["w2_inv"], kf, kf_col0=o * D)
        z3 = fft_last(cst["w1_inv"], bmid, u3, (o + 1) * D, z3, zc0, asum, bias, vec_col0=o * D)
        zc0 = 0
    return z3.reshape(L, D)


def _dense_dft_constants(L):
    N = 2 * L
    ang = 2 * np.pi * np.arange(N)[:, None] * np.arange(N)[None, :] / N
    wf = np.concatenate([np.cos(ang), -np.sin(ang)], 0)
    wi = np.concatenate([np.cos(ang[:L]), -np.sin(ang[:L])], 1) / N
    return _hilo(wf), _hilo(wf[:, :L]), _hilo(wi)


def hyena_conv_ctx(v, x1, x2, kern, asum, bias):
    L, D = v.shape
    wf, wf_data, wi = _dense_dft_constants(L)
    kf = hp_matmul(wf, kern).reshape(2, 2 * L, HY_ORDER * D)
    z = v
    for o, gate in enumerate((x1, x2)):
        xf = hp_matmul(wf_data, z).reshape(2, 2 * L, D)
        y = complex_mul(xf, kf, kf_col0=o * D).reshape(4 * L, D)
        z = hp_matmul(wi, y, gate=gate, z=z, asum=asum, bias=bias, vec_col0=o * D)
    return z


def _na_kernel(q_ref, k_ref, v_ref, *rest, n_ctx, rows, heads_per_step, rows_per_step):
    bias_refs, o_ref = rest[:rows_per_step], rest[rows_per_step]
    kr = NA_ROWS
    n_lat = rows * GRID_W
    dh = NA_HEAD_DIM
    scale = dh ** -0.5
    for rr in range(rows_per_step):
        r = pl.program_id(1) * rows_per_step + rr
        r0 = jnp.clip(r - kr // 2, 0, rows - kr)
        start = pl.multiple_of(r0 * GRID_W, GRID_W)
        qs = slice(rr * GRID_W, (rr + 1) * GRID_W)
        for hh in range(heads_per_step):
            sl = slice(hh * dh, (hh + 1) * dh)
            q = q_ref[qs, sl]
            s_ctx = _dot_nt(q, k_ref[n_lat:n_lat + n_ctx, sl]) * scale
            s_lat = _dot_nt(q, k_ref[pl.ds(start, kr * GRID_W), sl]) * scale + bias_refs[rr][hh]
            m = jnp.maximum(jnp.max(s_ctx, axis=1, keepdims=True), jnp.max(s_lat, axis=1, keepdims=True))
            p_ctx = jnp.exp(s_ctx - m)
            p_lat = jnp.exp(s_lat - m)
            l = jnp.sum(p_ctx, axis=1, keepdims=True) + jnp.sum(p_lat, axis=1, keepdims=True)
            o = (_dot(p_ctx.astype(BF16), v_ref[n_lat:n_lat + n_ctx, sl])
                 + _dot(p_lat.astype(BF16), v_ref[pl.ds(start, kr * GRID_W), sl]))
            o_ref[qs, sl] = (o / l).astype(o_ref.dtype)


def _na_bias_table(rpb, rows):
    H = rpb.shape[0]
    cols = jnp.arange(GRID_W)
    c0 = jnp.clip(cols - NA_COLS // 2, 0, GRID_W - NA_COLS)
    kc = jnp.arange(GRID_W)[None, :]
    inside = (kc >= c0[:, None]) & (kc < c0[:, None] + NA_COLS)
    rel = jnp.clip(kc - cols[:, None] + NA_COLS - 1, 0, 2 * NA_COLS - 2)
    rr = jnp.arange(NA_ROWS)[:, None] + jnp.arange(NA_ROWS)[None, :]
    t = rpb[:, rr]
    t = t[:, :, :, rel]
    t = jnp.where(inside[None, None, None], t, -1e30)
    t = jnp.transpose(t, (1, 0, 3, 2, 4))
    return t.reshape(NA_ROWS, H, GRID_W, NA_ROWS * GRID_W)


def na_attention(qkv, rpb, *, n_ctx):
    NT, D3 = qkv.shape
    D = D3 // 3
    L = NT - n_ctx
    rows = L // GRID_W
    hps = 4
    hw = hps * NA_HEAD_DIM
    nhb = D // hw
    table = _na_bias_table(rpb, rows)

    def pattern(r):
        r0 = jnp.clip(r - NA_ROWS // 2, 0, rows - NA_ROWS)
        return r0 - r + NA_ROWS - 1

    rps = 4
    bias_specs = [pl.BlockSpec((None, hps, GRID_W, NA_ROWS * GRID_W),
                               lambda h, r, rr=rr: (pattern(r * rps + rr), h, 0, 0)) for rr in range(rps)]
    return pl.pallas_call(
        functools.partial(_na_kernel, n_ctx=n_ctx, rows=rows, heads_per_step=hps, rows_per_step=rps),
        grid=(nhb, rows // rps),
        in_specs=[
            pl.BlockSpec((rps * GRID_W, hw), lambda h, r: (r, h)),
            pl.BlockSpec((NT, hw), lambda h, r: (0, nhb + h)),
            pl.BlockSpec((NT, hw), lambda h, r: (0, 2 * nhb + h)),
            *bias_specs,
        ],
        out_specs=pl.BlockSpec((rps * GRID_W, hw), lambda h, r: (r, h)),
        out_shape=jax.ShapeDtypeStruct((L, D), BF16),
        compiler_params=_params("parallel", "parallel"),
    )(qkv, qkv, qkv, *([table] * rps))


def _router_kernel(x_ref, w_ref, b_ref, o_ref):
    E, G = N_EXPERTS, N_GROUPS
    per = E // G
    logits = _dot_nt(w_ref[...].astype(BF16), x_ref[...])
    scores = jax.nn.sigmoid(logits)
    sel = scores + b_ref[...]
    tm = sel.shape[1]
    grp = sel.reshape(G, per, tm)
    eidx = lax.broadcasted_iota(jnp.int32, grp.shape, 1)
    m1 = jnp.max(grp, axis=1, keepdims=True)
    first = jnp.min(jnp.where(grp == m1, eidx, per), axis=1, keepdims=True)
    m2 = jnp.max(jnp.where(eidx == first, -jnp.inf, grp), axis=1, keepdims=True)
    gs = (m1 + m2).reshape(G, tm)
    gidx = lax.broadcasted_iota(jnp.int32, gs.shape, 0)
    gmask = jnp.zeros(gs.shape, jnp.bool_)
    for _ in range(TOPK_GROUPS):
        mx = jnp.max(gs, axis=0, keepdims=True)
        pick = gidx == jnp.min(jnp.where(gs == mx, gidx, G), axis=0, keepdims=True)
        gmask = jnp.logical_or(gmask, pick)
        gs = jnp.where(pick, -jnp.inf, gs)
    emask = jnp.broadcast_to(gmask.reshape(G, 1, tm), (G, per, tm)).reshape(E, tm)
    cand = jnp.where(emask, sel, -jnp.inf)
    xidx = lax.broadcasted_iota(jnp.int32, cand.shape, 0)
    chosen = jnp.zeros(cand.shape, jnp.bool_)
    for _ in range(TOP_K):
        mx = jnp.max(cand, axis=0, keepdims=True)
        pick = xidx == jnp.min(jnp.where(cand == mx, xidx, E), axis=0, keepdims=True)
        chosen = jnp.logical_or(chosen, pick)
        cand = jnp.where(pick, -jnp.inf, cand)
    wsel = jnp.where(chosen, scores, 0.0)
    gates = wsel / jnp.sum(wsel, axis=0, keepdims=True) * ROUTED_SCALE
    o_ref[...] = gates.T


def moe_router(h, router_w_t, router_b, layer):
    T, D = h.shape
    E = router_w_t.shape[1]
    tm = _pick(T, (256, 128))
    return pl.pallas_call(
        _router_kernel,
        grid=(T // tm,),
        in_specs=[
            pl.BlockSpec((tm, D), lambda i: (i, 0)),
            pl.BlockSpec((None, E, D), lambda i: (layer, 0, 0)),
            pl.BlockSpec((None, E, 1), lambda i: (layer, 0, 0)),
        ],
        out_specs=pl.BlockSpec((tm, E), lambda i: (i, 0)),
        out_shape=jax.ShapeDtypeStruct((T, E), F32),
        compiler_params=_params("parallel"),
    )(h, router_w_t, router_b.reshape(-1, E, 1))


def _moe_up_kernel(*refs, nk, ne, gated, F):
    if gated:
        x_ref, wg_ref, wu_ref, g_ref, o_ref, accg_ref, accu_ref = refs
    else:
        x_ref, wg_ref, wu_ref, o_ref, accg_ref, accu_ref = refs
    e0 = pl.program_id(1) * ne
    k = pl.program_id(2)

    @pl.when(k == 0)
    def _():
        accg_ref[...] = jnp.zeros_like(accg_ref)
        accu_ref[...] = jnp.zeros_like(accu_ref)

    def stacked_experts(ref):
        w = ref[0] if ne == 1 else jnp.concatenate([ref[ee] for ee in range(ne)], axis=0)
        return w.astype(BF16)

    x = x_ref[...]
    accg_ref[...] += _dot_nt(x, stacked_experts(wg_ref))
    accu_ref[...] += _dot_nt(x, stacked_experts(wu_ref))

    @pl.when(k == nk - 1)
    def _():
        a = accg_ref[...]
        hid = a * jax.nn.sigmoid(a) * accu_ref[...]
        if gated:
            g = g_ref[...]
            lane = lax.broadcasted_iota(jnp.int32, g.shape, 1)
            col = lax.broadcasted_iota(jnp.int32, (1, ne * F), 1) // F
            gexp = jnp.zeros_like(hid)
            for ee in range(ne):
                ge = jnp.sum(jnp.where(lane == e0 + ee, g, 0.0), axis=1, keepdims=True)
                gexp = jnp.where(col == ee, ge, gexp)
            hid = hid * gexp
        o_ref[...] = hid.astype(o_ref.dtype)


def moe_up(x, w_gate, w_up, layer, gates=None):
    T, D = x.shape
    if w_gate.ndim == 3:
        w_gate, w_up = w_gate[:, None], w_up[:, None]
    w_gate, w_up = jnp.swapaxes(w_gate, 2, 3), jnp.swapaxes(w_up, 2, 3)
    E, F = w_gate.shape[1], w_gate.shape[2]
    ne = 4 if E % 4 == 0 else 1
    tm = _pick(T, (1056, 1024, 512, 256, 128))
    tk = _pick(D, (2048, 1024, 512, 256, 128))
    nk = D // tk
    gated = gates is not None
    w_spec = pl.BlockSpec((None, ne, F, tk), lambda i, e, k: (layer, e, 0, k))
    in_specs = [pl.BlockSpec((tm, tk), lambda i, e, k: (i, k)), w_spec, w_spec]
    args = [x, w_gate, w_up]
    if gated:
        in_specs.append(pl.BlockSpec((tm, E), lambda i, e, k: (i, 0)))
        args.append(gates)
    return pl.pallas_call(
        functools.partial(_moe_up_kernel, nk=nk, ne=ne, gated=gated, F=F),
        grid=(T // tm, E // ne, nk),
        in_specs=in_specs,
        out_specs=pl.BlockSpec((tm, ne * F), lambda i, e, k: (i, e)),
        out_shape=jax.ShapeDtypeStruct((T, E * F), BF16),
        scratch_shapes=[pltpu.VMEM((tm, ne * F), F32), pltpu.VMEM((tm, ne * F), F32)],
        compiler_params=_params("parallel", "parallel", "arbitrary"),
    )(*args)


def moe(h, layer, router_w_t, router_b, w_gate, w_up, w_down, s_gate, s_up, s_down):
    gates = moe_router(h, router_w_t, router_b, layer)
    hid = moe_up(h, w_gate, w_up, layer, gates)
    hid_sh = moe_up(h, s_gate, s_up, layer)
    E, F, D = w_down.shape[1:]
    return matmul(hid, w_down.reshape(DEPTH, E * F, D), out_dtype=F32, layer=layer, extra=(hid_sh, s_down, layer))


def kernel(x, c, ctx, c_ctx, ada_a, ada_b, ada_bias, ln1_g, ln1_b, ln2_g, ln2_b, router_w, router_b, moe_w_gate, moe_w_up, moe_w_down, sh_w_gate, sh_w_up, sh_w_down, ml_w_in, ml_b_if, ml_norm_g, ml_w_out, da_w_qkv, da_lam_q1, da_lam_k1, da_lam_q2, da_lam_k2, da_subln_g, da_w_out, hy_w_in, hy_conv_w, hy_conv_b, hy_f_w1, hy_f_b1, hy_f_w2, hy_f_b2, hy_f_w3, hy_f_b3, hy_f_w4, hy_bias, hy_w_out, na_w_qkv, na_rpb, na_w_out):
    _, L, D = x.shape
    C = ctx.shape[1]
    NT = C + L
    s = jnp.concatenate([x[0], ctx[0]], axis=0)

    cc = jnp.zeros((8, D), F32).at[0].set(c_ctx).at[1].set(c[0])
    mods = ada_modulation(cc, ada_a, ada_b, ada_bias)[:, :2].reshape(DEPTH, 2, 6, 1, D)
    SH1, SC1, G1, SH2, SC2, G2 = range(6)
    router_w_t = jnp.swapaxes(router_w, 1, 2)
    moe_args = (router_w_t, router_b, moe_w_gate, moe_w_up, moe_w_down, sh_w_gate, sh_w_up, sh_w_down)

    (h,) = ln_modulate(s, mods, n_ctx=C, h_layer=0, sc_idx=SC1, sh_idx=SH1)
    for i in range(DEPTH):
        last = i == DEPTH - 1
        if i == 0:
            y = mlstm_mixer(h, ml_w_in, ml_b_if[0], ml_norm_g[0], ml_w_out, n_ctx=C)
        elif i == 1:
            lam4 = jnp.stack([da_lam_q1[0], da_lam_k1[0], da_lam_q2[0], da_lam_k2[0]])
            y = diff_attn_mixer(h, da_w_qkv, lam4, da_subln_g[0], da_w_out, n_ctx=C,
                                lambda_init=0.8 - 0.6 * math.exp(-0.3 * i))
        elif i == 2:
            y = hyena_mixer(h, hy_w_in, hy_conv_w[0], hy_conv_b[0], hy_f_w1[0], hy_f_b1[0], hy_f_w2[0], hy_f_b2[0],
                            hy_f_w3[0], hy_f_b3[0], hy_f_w4[0], hy_bias[0], hy_w_out, n_ctx=C)
        else:
            y = na_mixer(h, na_w_qkv, na_rpb[0], na_w_out, n_ctx=C)
        n_ctx = C
        if last:
            s, n_ctx = s[:L], 0
        s, h = ln_modulate(s, mods, n_ctx=n_ctx, ys=(y,), ln_layer=i, gate_idx=G1, ln_g=ln1_g, ln_b=ln1_b,
                           h_layer=i, sc_idx=SC2, sh_idx=SH2)
        y = moe(h, i, *moe_args)
        if last:
            (s,) = ln_modulate(s, mods, n_ctx=n_ctx, ys=(y,), ln_layer=i, gate_idx=G2, ln_g=ln2_g, ln_b=ln2_b)
        else:
            s, h = ln_modulate(s, mods, n_ctx=n_ctx, ys=(y,), ln_layer=i, gate_idx=G2, ln_g=ln2_g, ln_b=ln2_b,
                               h_layer=i + 1, sc_idx=SC1, sh_idx=SH1)
    return s[None]


def mlstm_mixer(h, w_in, b_if, norm_g, w_out, *, n_ctx):
    D = h.shape[1]
    w_in_t = jnp.swapaxes(w_in, 1, 2)
    u = matmul(h, w_in_t, out_dtype=F32, layer=0, ncols=3 * D, w_transposed=True)
    w_if_t = jnp.pad(w_in_t[0, 3 * D:], ((0, LANES - 4 * ML_HEADS), (0, 0)))
    g_if = matmul(h, w_if_t, out_dtype=F32, w_transposed=True)
    hs = mlstm_scan(u, g_if, b_if, n_ctx=n_ctx)
    return matmul(mlstm_gate(hs, u, norm_g, o_col_block=2), w_out, out_dtype=F32, layer=0)


def diff_attn_mixer(h, w_qkv, lam4, subln_g, w_out, *, n_ctx, lambda_init):
    NT, D = h.shape
    L = NT - n_ctx
    scale = DA_HEAD_DIM ** -0.5 * math.log2(math.e)
    cos, sin = _rope_tables(L, 1.0)
    ones, zeros = jnp.ones((n_ctx, LANES), F32), jnp.zeros((n_ctx, LANES), F32)
    cos_k, sin_k = jnp.concatenate([cos, ones]), jnp.concatenate([sin, zeros])
    cos3 = jnp.stack([cos_k * scale, cos_k, jnp.ones_like(cos_k)])
    sin3 = jnp.stack([sin_k * scale, sin_k, jnp.zeros_like(sin_k)])
    qkv = matmul(h, w_qkv, out_dtype=BF16, layer=0, rope=(cos3, sin3, D))
    o_lat = diff_attention(qkv, lam4, subln_g, q_row0=0, n_q=L, kv_row0=0, kv_len=NT, lambda_init=lambda_init)
    o_ctx = diff_attention(qkv, lam4, subln_g, q_row0=L, n_q=n_ctx, kv_row0=L, kv_len=n_ctx, lambda_init=lambda_init)
    return matmul(jnp.concatenate([o_lat, o_ctx]), w_out, out_dtype=F32, layer=0)


def hyena_mixer(h, w_in, conv_w, conv_b, f_w1, f_b1, f_w2, f_b2, f_w3, f_b3, f_w4, hy_bias, w_out, *, n_ctx):
    NT, D = h.shape
    L = NT - n_ctx
    u = matmul(h, w_in, out_dtype=F32, layer=0, conv=(conv_w, conv_b, (0, L, NT)))
    fargs = (f_w1, f_b1, f_w2, f_b2, f_w3, f_b3, f_w4)
    bias = hy_bias.reshape(1, HY_ORDER * D)
    kern_l, asum_l = hyena_filters(L, D, *fargs, t2_major_n2=FFT_N2)
    z_lat = hyena_conv_lat(u.reshape(NT // FFT_N2, FFT_N2, 3 * D), kern_l, asum_l, bias, L=L, D=D)
    kern_c, asum_c = hyena_filters(n_ctx, D, *fargs)
    v, x1, x2 = (u[L:, j * D:(j + 1) * D] for j in range(3))
    z_ctx = hyena_conv_ctx(v, x1, x2, kern_c, asum_c, bias)
    return matmul(jnp.concatenate([z_lat, z_ctx]), w_out, out_dtype=F32, layer=0)


def na_mixer(h, w_qkv, rpb, w_out, *, n_ctx):
    qkv = matmul(h, w_qkv, out_dtype=BF16, layer=0)
    return matmul(na_attention(qkv, rpb, n_ctx=n_ctx), w_out, out_dtype=F32, layer=0)
```

```python
import functools
import math

import numpy as np
import jax
import jax.numpy as jnp
from jax import lax
from jax.experimental import pallas as pl
from jax.experimental.pallas import tpu as pltpu

F32 = jnp.float32
BF16 = jnp.bfloat16

V7X_VMEM_LIMIT_BYTES = 56 * 1024 * 1024
LANES = 128

GRID_W = 64
DEPTH = 4
DEEPNORM_ALPHA = (2 * DEPTH) ** 0.25
LN_EPS = 1e-5
RMS_EPS = 1e-6
ROPE_BASE = 10000.0
ML_HEADS = 8
ML_CHUNK = 256
DA_HEAD_DIM = 128
HY_ORDER = 2
HY_EMB = 33
HY_BANDS = (HY_EMB - 1) // 2
HY_FAST_DECAY = 0.3
HY_SLOW_DECAY = 1.5
HY_DECAY_TARGET = 1e-2
NA_HEAD_DIM = 128
NA_ROWS = 8
NA_COLS = 16
N_EXPERTS = 64
TOP_K = 8
N_GROUPS = 8
TOPK_GROUPS = 4
ROUTED_SCALE = 2.5


def _params(*sem):
    return pltpu.CompilerParams(dimension_semantics=sem, vmem_limit_bytes=V7X_VMEM_LIMIT_BYTES)


def _dot(a, b):
    return jnp.dot(a, b, preferred_element_type=F32)


def _dot_nt(a, b):
    return lax.dot_general(a, b, (((1,), (1,)), ((), ())), preferred_element_type=F32)


def _dot_tn(a, b):
    return lax.dot_general(a, b, (((0,), (0,)), ((), ())), preferred_element_type=F32)


def _pick(n, prefs):
    for p in prefs:
        if n % p == 0:
            return p
    return n


def _mm_kernel(*refs, nk, has_extra):
    if has_extra:
        a_ref, w_ref, a2_ref, w2_ref, o_ref, acc_ref = refs
    else:
        a_ref, w_ref, o_ref, acc_ref = refs
    k = pl.program_id(2)

    @pl.when(k == 0)
    def _():
        acc_ref[...] = jnp.zeros_like(acc_ref)

    acc_ref[...] += _dot(a_ref[...].astype(BF16), w_ref[...].astype(BF16))

    @pl.when(k == nk - 1)
    def _():
        acc = acc_ref[...]
        if has_extra:
            acc = acc + _dot(a2_ref[...].astype(BF16), w2_ref[...].astype(BF16))
        o_ref[...] = acc.astype(o_ref.dtype)


def _rope(x, cos, sin):
    lane = lax.broadcasted_iota(jnp.int32, x.shape, 1)
    rot = jnp.where((lane % 64) < 32, -pltpu.roll(x, LANES - 32, 1), pltpu.roll(x, 32, 1))
    return x * cos + rot * sin


def _mm_resident_kernel(a_ref, w_ref, *rest, w_transposed, rope, conv_seq_starts):
    o_ref = rest[-1]
    dot = _dot_nt if w_transposed else _dot
    w = w_ref[...].astype(BF16)
    acc = dot(a_ref[...].astype(BF16), w)
    if rope:
        cos, sin = rest[0][...], rest[1][...]
        for g in range(acc.shape[1] // LANES):
            sl = slice(g * LANES, (g + 1) * LANES)
            o_ref[:, sl] = _rope(acc[:, sl], cos, sin).astype(o_ref.dtype)
    elif conv_seq_starts is not None:
        prev_ref, next_ref, cw_ref, cb_ref = rest[:4]
        tm = acc.shape[0]
        p_prev = dot(prev_ref[...].astype(BF16), w)[7:8]
        p_next = dot(next_ref[...].astype(BF16), w)[0:1]
        local = lax.broadcasted_iota(jnp.int32, (tm, 1), 0)
        row = local + pl.program_id(0) * tm
        is_start = functools.reduce(jnp.logical_or, [row == r for r in conv_seq_starts[:-1]])
        is_end = functools.reduce(jnp.logical_or, [row == r - 1 for r in conv_seq_starts[1:]])
        xm = jnp.where(local == 0, p_prev, pltpu.roll(acc, 1, 0))
        xp = jnp.where(local == tm - 1, p_next, pltpu.roll(acc, tm - 1, 0))
        xm = jnp.where(is_start, 0.0, xm)
        xp = jnp.where(is_end, 0.0, xp)
        o_ref[...] = (xm * cw_ref[0:1, :] + acc * cw_ref[1:2, :] + xp * cw_ref[2:3, :] + cb_ref[...]).astype(o_ref.dtype)
    else:
        o_ref[...] = acc.astype(o_ref.dtype)


def _matmul_resident(a, w, *, out_dtype, layer, col0, N, w_transposed, rope, conv):
    M, K = a.shape
    tm = _pick(M, (2112, 2048, 1024, 512, 256, 128) if a.dtype == BF16 else (1056, 1024, 512, 256, 128))
    tn = _pick(N, (256, 128))
    cb = col0 // tn
    lead = (None,) if w.ndim == 3 else ()
    lidx = (layer,) if w.ndim == 3 else ()
    if w_transposed:
        w_spec = pl.BlockSpec(lead + (tn, K), lambda i, j: lidx + (j + cb, 0))
    else:
        w_spec = pl.BlockSpec(lead + (K, tn), lambda i, j: lidx + (0, j + cb))
    in_specs = [pl.BlockSpec((tm, K), lambda i, j: (i, 0), pipeline_mode=pl.Buffered(1)), w_spec]
    args = [a, w]
    conv_seq_starts = None
    if rope is not None:
        cos, sin, part_cols = rope
        bpp = part_cols // tn
        tab = pl.BlockSpec((None, tm, LANES), lambda i, j: (j // bpp, i, 0))
        in_specs += [tab, tab]
        args += [cos, sin]
    elif conv is not None:
        conv_w, conv_b, conv_seq_starts = conv
        t8, n8 = tm // 8, M // 8
        in_specs += [pl.BlockSpec((8, K), lambda i, j: (jnp.maximum(i * t8 - 1, 0), 0)),
                     pl.BlockSpec((8, K), lambda i, j: (jnp.minimum((i + 1) * t8, n8 - 1), 0)),
                     pl.BlockSpec((3, tn), lambda i, j: (0, j)), pl.BlockSpec((1, tn), lambda i, j: (0, j))]
        args += [a, a, conv_w, conv_b.reshape(1, N)]
    return pl.pallas_call(
        functools.partial(_mm_resident_kernel, w_transposed=w_transposed, rope=rope is not None,
                          conv_seq_starts=conv_seq_starts),
        grid=(M // tm, N // tn),
        in_specs=in_specs,
        out_specs=pl.BlockSpec((tm, tn), lambda i, j: (i, j)),
        out_shape=jax.ShapeDtypeStruct((M, N), out_dtype),
        compiler_params=_params("parallel", "arbitrary"),
    )(*args)


def matmul(a, w, *, out_dtype, layer=None, col0=0, ncols=None, extra=None, row0=0, nrows=None, w_transposed=False,
           rope=None, conv=None):
    M = a.shape[0] if nrows is None else nrows
    K = a.shape[1]
    N = (w.shape[-2] if w_transposed else w.shape[-1]) if ncols is None else ncols
    if K <= 4096 and extra is None and nrows is None and M >= 1024:
        return _matmul_resident(a, w, out_dtype=out_dtype, layer=layer, col0=col0, N=N, w_transposed=w_transposed,
                                rope=rope, conv=conv)
    assert not w_transposed and rope is None and conv is None
    tm = _pick(M, (1056, 1024, 512, 256, 128, 64, 32, 16, 8))
    tn = _pick(N, (1024, 512, 256, 128))
    tk = _pick(K, (2048, 1024, 512, 256, 128))
    assert col0 % tn == 0 and row0 % tm == 0
    cb, rb = col0 // tn, row0 // tm
    nk = K // tk
    if w.ndim == 3:
        w_spec = pl.BlockSpec((None, tk, tn), lambda i, j, k: (layer, k, j + cb))
    else:
        w_spec = pl.BlockSpec((tk, tn), lambda i, j, k: (k, j + cb))
    in_specs = [pl.BlockSpec((tm, tk), lambda i, j, k: (i + rb, k)), w_spec]
    args = [a, w]
    if extra is not None:
        a2, w2, layer2 = extra
        k2 = a2.shape[1]
        in_specs.append(pl.BlockSpec((tm, k2), lambda i, j, k: (i + rb, 0)))
        in_specs.append(pl.BlockSpec((None, k2, tn), lambda i, j, k: (layer2, 0, j)))
        args += [a2, w2]
    return pl.pallas_call(
        functools.partial(_mm_kernel, nk=nk, has_extra=extra is not None),
        grid=(M // tm, N // tn, nk),
        in_specs=in_specs,
        out_specs=pl.BlockSpec((tm, tn), lambda i, j, k: (i, j)),
        out_shape=jax.ShapeDtypeStruct((M, N), out_dtype),
        scratch_shapes=[pltpu.VMEM((tm, tn), F32)],
        compiler_params=_params("parallel", "parallel", "arbitrary"),
    )(*args)


def _ada_kernel(cc_ref, a_ref, b_ref, bias_ref, o_ref):
    cc = cc_ref[...]
    act = cc * jax.nn.sigmoid(cc)
    t = _dot(act.astype(BF16), a_ref[...].astype(BF16))
    o_ref[...] = _dot(t.astype(BF16), b_ref[...].astype(BF16)) + bias_ref[...]


def ada_modulation(cc, ada_a, ada_b, ada_bias):
    depth, D, R = ada_a.shape
    n6 = ada_b.shape[-1]
    tn = D
    return pl.pallas_call(
        _ada_kernel,
        grid=(depth, n6 // tn),
        in_specs=[
            pl.BlockSpec((8, D), lambda l, n: (0, 0)),
            pl.BlockSpec((None, D, R), lambda l, n: (l, 0, 0)),
            pl.BlockSpec((None, R, tn), lambda l, n: (l, 0, n)),
            pl.BlockSpec((None, 1, tn), lambda l, n: (l, 0, n)),
        ],
        out_specs=pl.BlockSpec((None, 8, tn), lambda l, n: (l, 0, n)),
        out_shape=jax.ShapeDtypeStruct((depth, 8, n6), F32),
        compiler_params=_params("parallel", "parallel"),
    )(cc, ada_a, ada_b, ada_bias.reshape(depth, 1, n6))


def _lnmod_kernel(*refs, has_ln, has_h, n_y):
    it = iter(refs)
    s_ref = next(it)
    if has_ln:
        y_refs = [next(it) for _ in range(n_y)]
        gate_ref, lng_ref, lnb_ref = next(it), next(it), next(it)
    if has_h:
        sc_ref, sh_ref = next(it), next(it)
    if has_ln:
        so_ref = next(it)
    if has_h:
        h_ref = next(it)
    x = s_ref[...]
    if has_ln:
        y = y_refs[0][...].astype(F32)
        for r in y_refs[1:]:
            y = y + r[...].astype(F32)
        v = DEEPNORM_ALPHA * x + gate_ref[...] * y
        mu = jnp.mean(v, axis=-1, keepdims=True)
        var = jnp.mean(jnp.square(v - mu), axis=-1, keepdims=True)
        x = (v - mu) * lax.rsqrt(var + LN_EPS) * lng_ref[...] + lnb_ref[...]
        so_ref[...] = x
    if has_h:
        h_ref[...] = (x * (1.0 + sc_ref[...]) + sh_ref[...]).astype(h_ref.dtype)


def ln_modulate(s, mods, *, n_ctx, ys=(), ln_layer=None, gate_idx=None, ln_g=None, ln_b=None,
                h_layer=None, sc_idx=None, sh_idx=None):
    R, D = s.shape
    tr = 128
    has_ln, has_h = len(ys) > 0, h_layer is not None
    n_lat_blocks = (R - n_ctx) // tr
    row_spec = pl.BlockSpec((tr, D), lambda i: (i, 0))

    def mod_spec(layer, idx):
        return pl.BlockSpec((None, None, None, 1, D),
                            lambda i: (layer, jnp.where(i >= n_lat_blocks, 0, 1), idx, 0, 0))

    vec_spec = pl.BlockSpec((None, 1, D), lambda i: (ln_layer, 0, 0))
    in_specs, args = [row_spec], [s]
    if has_ln:
        in_specs += [row_spec] * len(ys) + [mod_spec(ln_layer, gate_idx), vec_spec, vec_spec]
        args += list(ys) + [mods, ln_g.reshape(DEPTH, 1, D), ln_b.reshape(DEPTH, 1, D)]
    if has_h:
        in_specs += [mod_spec(h_layer, sc_idx), mod_spec(h_layer, sh_idx)]
        args += [mods, mods]
    out_specs, out_shape = [], []
    if has_ln:
        out_specs.append(row_spec)
        out_shape.append(jax.ShapeDtypeStruct((R, D), F32))
    if has_h:
        out_specs.append(row_spec)
        out_shape.append(jax.ShapeDtypeStruct((R, D), BF16))
    outs = pl.pallas_call(
        functools.partial(_lnmod_kernel, has_ln=has_ln, has_h=has_h, n_y=len(ys)),
        grid=(R // tr,),
        in_specs=in_specs,
        out_specs=out_specs,
        out_shape=out_shape,
        compiler_params=_params("parallel"),
    )(*args)
    return outs


def _log_sigmoid(x):
    return jnp.minimum(x, 0.0) - jnp.log(1.0 + jnp.exp(-jnp.abs(x)))


def _mlstm_kernel(bias_ref, q_ref, k_ref, v_ref, ic_ref, fc_ref, ir_ref, fr_ref, o_ref, ct_ref, n_ref, m_ref,
                  *, T, dqk, dv, hps):
    d, t = pl.program_id(0), pl.program_id(2)

    @pl.when(t == 0)
    def _():
        ct_ref[...] = jnp.zeros_like(ct_ref)
        n_ref[...] = jnp.zeros_like(n_ref)
        m_ref[...] = jnp.zeros_like(m_ref)

    for hh in range(hps):
        _mlstm_head(d, pl.program_id(1) * hps + hh, hh, bias_ref, q_ref, k_ref, v_ref, ic_ref, fc_ref, ir_ref, fr_ref,
                    o_ref, ct_ref, n_ref, m_ref, T=T, dqk=dqk, dv=dv)


def _mlstm_head(d, h, hh, bias_ref, q_ref, k_ref, v_ref, ic_ref, fc_ref, ir_ref, fr_ref, o_ref, ct_ref, n_ref, m_ref,
                *, T, dqk, dv):
    qs, vs = slice(hh * dqk, (hh + 1) * dqk), slice(hh * dv, (hh + 1) * dv)
    bi = bias_ref[d * 2 * ML_HEADS + h]
    bf = bias_ref[(d * 2 + 1) * ML_HEADS + h]
    i_c = ic_ref[hh] + bi
    i_r = ir_ref[hh] + bi
    f_c = _log_sigmoid(fc_ref[hh] + bf)
    f_r = _log_sigmoid(fr_ref[hh] + bf)
    row = lax.broadcasted_iota(jnp.int32, (T, T), 0)
    col = lax.broadcasted_iota(jnp.int32, (T, T), 1)
    sgn = 1 - 2 * d
    incl = (col - row) * sgn <= 0
    incl_t = (row - col) * sgn <= 0
    b_c = jnp.sum(jnp.where(incl, f_r, 0.0), axis=1, keepdims=True)
    b_r = jnp.sum(jnp.where(incl_t, f_c, 0.0), axis=0, keepdims=True)
    f_tot = jnp.sum(f_r, axis=1, keepdims=True)
    m_prev = m_ref[hh]
    dmat = jnp.where(incl, b_c - b_r + i_r, -jnp.inf)
    inter = b_c + m_prev
    m_t = jnp.maximum(inter, jnp.max(dmat, axis=1, keepdims=True))
    w_intra = jnp.exp(dmat - m_t)
    w_inter = jnp.exp(inter - m_t)
    q = q_ref[:, qs] * (dqk ** -0.5)
    k = k_ref[:, qs]
    v = v_ref[:, vs]
    qb, kb, vb = q.astype(BF16), k.astype(BF16), v.astype(BF16)
    s = _dot_nt(qb, kb) * w_intra
    ct = ct_ref[hh]
    n = n_ref[hh]
    num = w_inter * _dot(qb, ct.astype(BF16)) + _dot(s.astype(BF16), vb)
    den = w_inter * jnp.sum(q * n, axis=1, keepdims=True) + jnp.sum(s, axis=1, keepdims=True)
    o_ref[:, vs] = num / jnp.maximum(jnp.abs(den), jnp.exp(-m_t))
    g_r = f_tot - b_r + i_r
    g_c = f_tot - b_c + i_c
    m_new = jnp.maximum(f_tot + m_prev, jnp.max(g_r, axis=1, keepdims=True))
    decay = jnp.exp(f_tot + m_prev - m_new)
    wk = jnp.exp(g_c - m_new)
    ct_ref[hh] = decay * ct + _dot_tn(kb, (wk * v).astype(BF16))
    n_ref[hh] = decay * n + jnp.sum(wk * k, axis=0, keepdims=True)
    m_ref[hh] = m_new


def mlstm_scan(u, gates, b_if, *, n_ctx):
    NT = u.shape[0]
    H, T = ML_HEADS, ML_CHUNK
    D = u.shape[1] // 3
    dqk, dv = D // (2 * H), D // H
    nc, ncc = NT // T, n_ctx // T
    ncl = nc - ncc
    g4 = gates[:, :4 * H].T
    g_col = g4.reshape(4 * H, NT, 1)
    g_row = g4.reshape(4 * H, nc, 1, T)

    def chunk(d, t):
        fwd = jnp.where(t < ncc, ncl + t, t - ncc)
        return jnp.where(d == 0, fwd, nc - 1 - t)

    hps = 4
    hb = H // hps
    col_spec = lambda off: pl.BlockSpec((hps, T, 1), lambda d, h, t: ((2 * d + off) * hb + h, chunk(d, t), 0))
    row_spec = lambda off: pl.BlockSpec((hps, None, 1, T), lambda d, h, t: ((2 * d + off) * hb + h, chunk(d, t), 0, 0))
    return pl.pallas_call(
        functools.partial(_mlstm_kernel, T=T, dqk=dqk, dv=dv, hps=hps),
        grid=(2, hb, nc),
        in_specs=[
            pl.BlockSpec(memory_space=pltpu.SMEM),
            pl.BlockSpec((T, hps * dqk), lambda d, h, t: (chunk(d, t), h)),
            pl.BlockSpec((T, hps * dqk), lambda d, h, t: (chunk(d, t), hb + h)),
            pl.BlockSpec((T, hps * dv), lambda d, h, t: (chunk(d, t), hb + h)),
            col_spec(0), col_spec(1), row_spec(0), row_spec(1),
        ],
        out_specs=pl.BlockSpec((None, T, hps * dv), lambda d, h, t: (d, chunk(d, t), h)),
        out_shape=jax.ShapeDtypeStruct((2, NT, H * dv), F32),
        scratch_shapes=[pltpu.VMEM((hps, dqk, dv), F32), pltpu.VMEM((hps, 1, dqk), F32), pltpu.VMEM((hps, 1, 1), F32)],
        compiler_params=_params("parallel", "parallel", "arbitrary"),
    )(b_if, u, u, u, g_col, g_col, g_row, g_row)


def _mlgate_kernel(hs_ref, o_ref, g_ref, out_ref, *, dv):
    D = out_ref.shape[1]
    for h in range(D // dv):
        sl = slice(h * dv, (h + 1) * dv)
        x = hs_ref[0, :, sl] + hs_ref[1, :, sl]
        r = lax.rsqrt(jnp.mean(jnp.square(x), axis=-1, keepdims=True) + RMS_EPS)
        out_ref[:, sl] = (x * r * g_ref[:, sl] * jax.nn.sigmoid(o_ref[:, sl])).astype(out_ref.dtype)


def mlstm_gate(hs, u, norm_g, *, o_col_block):
    _, NT, D = hs.shape
    tr = 128
    return pl.pallas_call(
        functools.partial(_mlgate_kernel, dv=D // ML_HEADS),
        grid=(NT // tr,),
        in_specs=[
            pl.BlockSpec((2, tr, D), lambda i: (0, i, 0)),
            pl.BlockSpec((tr, D), lambda i: (i, o_col_block)),
            pl.BlockSpec((1, D), lambda i: (0, 0)),
        ],
        out_specs=pl.BlockSpec((tr, D), lambda i: (i, 0)),
        out_shape=jax.ShapeDtypeStruct((NT, D), BF16),
        compiler_params=_params("parallel"),
    )(hs, u, norm_g.reshape(1, D))


SAFE_LOG2 = 64.0


def _dattn_kernel(lam_ref, q_ref, k_ref, v_ref, g_ref, o_ref, m_ref, l_ref, acc_ref, kn_ref,
                  *, tk, n_chunks, out_scale, lambda_init):
    dh = DA_HEAD_DIM
    nlt = tk // LANES
    l_ref[...] = jnp.zeros_like(l_ref)
    acc_ref[...] = jnp.zeros_like(acc_ref)

    @pl.when(pl.program_id(1) == 0)
    def _():
        for i in range(2):
            kk = k_ref[:, i * dh:(i + 1) * dh].astype(F32)
            k2 = jnp.max(jnp.sum(kk * kk, axis=1, keepdims=True), axis=0, keepdims=True)
            kn_ref[i] = jnp.broadcast_to(k2, kn_ref.shape[1:])

    bound2 = None
    for i in range(2):
        qq = q_ref[:, i * dh:(i + 1) * dh].astype(F32)
        b2 = jnp.max(jnp.sum(qq * qq, axis=1, keepdims=True), axis=0, keepdims=True) * kn_ref[i][0:1, 0:1]
        bound2 = b2 if bound2 is None else jnp.maximum(bound2, b2)
    scores_bounded = bound2[0, 0] <= SAFE_LOG2 * SAFE_LOG2

    def scores(c, i):
        kc = k_ref[pl.ds(pl.multiple_of(c * tk, tk), tk), i * dh:(i + 1) * dh]
        return _dot_nt(q_ref[:, i * dh:(i + 1) * dh], kc)

    @pl.when(scores_bounded)
    def _():
        m_ref[...] = jnp.zeros_like(m_ref)

    @pl.when(jnp.logical_not(scores_bounded))
    def _():
        m_ref[...] = jnp.full_like(m_ref, -jnp.inf)

        def max_pass(c, carry):
            for i in range(2):
                s = scores(c, i)
                m = m_ref[i]
                for t in range(nlt):
                    m = jnp.maximum(m, s[:, t * LANES:(t + 1) * LANES])
                m_ref[i] = m
            return carry

        lax.fori_loop(0, n_chunks, max_pass, 0)
        for i in range(2):
            m_ref[i] = jnp.broadcast_to(jnp.max(m_ref[i], axis=1, keepdims=True), m_ref.shape[1:])

    tq = q_ref.shape[0]

    def sum_pass(c, carry):
        vc = v_ref[pl.ds(pl.multiple_of(c * tk, tk), tk), :]
        ps = []
        for i in range(2):
            s = scores(c, i)
            m = m_ref[i]
            p = [jnp.exp2(s[:, t * LANES:(t + 1) * LANES] - m) for t in range(nlt)]
            l_ref[i] += functools.reduce(jnp.add, p)
            ps.append(jnp.concatenate(p, axis=1).astype(BF16))
        acc_ref[...] += _dot(jnp.concatenate(ps, axis=0), vc)
        return carry

    lax.fori_loop(0, n_chunks, sum_pass, 0)
    lam = (jnp.exp(jnp.sum(lam_ref[0:1, :] * lam_ref[1:2, :], axis=1, keepdims=True))
           - jnp.exp(jnp.sum(lam_ref[2:3, :] * lam_ref[3:4, :], axis=1, keepdims=True)) + lambda_init)
    l = [jnp.sum(l_ref[i], axis=1, keepdims=True) for i in range(2)]
    o = acc_ref[0:tq] / l[0] - lam * (acc_ref[tq:2 * tq] / l[1])
    r = lax.rsqrt(jnp.mean(jnp.square(o), axis=-1, keepdims=True) + RMS_EPS)
    o_ref[...] = (o * r * g_ref[...] * out_scale).astype(o_ref.dtype)


def diff_attention(qkv, lam4, subln_g, *, q_row0, n_q, kv_row0, kv_len, lambda_init):
    D = qkv.shape[1] // 3
    hd = 2 * DA_HEAD_DIM
    H = D // hd
    tq = _pick(n_q, (512, 256))
    tk = _pick(kv_len, (768, 512, 256))
    qb, kvb = q_row0 // tq, kv_row0 // kv_len
    assert kv_row0 % kv_len == 0 and q_row0 % tq == 0
    return pl.pallas_call(
        functools.partial(_dattn_kernel, tk=tk, n_chunks=kv_len // tk, out_scale=1.0 - lambda_init,
                          lambda_init=lambda_init),
        grid=(H, n_q // tq),
        in_specs=[
            pl.BlockSpec((4, DA_HEAD_DIM), lambda h, i: (0, 0)),
            pl.BlockSpec((tq, hd), lambda h, i: (i + qb, h)),
            pl.BlockSpec((kv_len, hd), lambda h, i: (kvb, H + h)),
            pl.BlockSpec((kv_len, hd), lambda h, i: (kvb, 2 * H + h)),
            pl.BlockSpec((1, hd), lambda h, i: (0, 0)),
        ],
        out_specs=pl.BlockSpec((tq, hd), lambda h, i: (i, h)),
        out_shape=jax.ShapeDtypeStruct((n_q, D), BF16),
        scratch_shapes=[pltpu.VMEM((2, tq, LANES), F32), pltpu.VMEM((2, tq, LANES), F32),
                        pltpu.VMEM((2 * tq, hd), F32), pltpu.VMEM((2, 8, LANES), F32)],
        compiler_params=_params("parallel", "arbitrary"),
    )(lam4, qkv, qkv, qkv, subln_g.reshape(1, hd))


def _rope_tables(L, scale):
    a = DA_HEAD_DIM // 2
    t = jnp.arange(L)
    pos = jnp.stack([t // GRID_W, t % GRID_W], -1).astype(F32)
    inv = ROPE_BASE ** (-jnp.arange(0, a, 2, dtype=F32) / a)
    ang = pos[:, :, None] * inv
    ang = jnp.concatenate([ang, ang], -1).reshape(L, DA_HEAD_DIM)
    return jnp.cos(ang) * scale, jnp.sin(ang) * scale


def _filter_kernel(zf_ref, t_ref, w1_ref, b1_ref, w2_ref, b2_ref, w3_ref, b3_ref, w4f_ref, w4b_ref, dl_ref,
                   o_ref, asum_ref, hdn_ref):
    i, j = pl.program_id(0), pl.program_id(1)

    @pl.when(j == 0)
    def _():
        hdn = jnp.sin(_dot(zf_ref[...].astype(BF16), w1_ref[...].astype(BF16)) + b1_ref[...])
        hdn = jnp.sin(_dot(hdn.astype(BF16), w2_ref[...].astype(BF16)) + b2_ref[...])
        hdn_ref[...] = jnp.sin(_dot(hdn.astype(BF16), w3_ref[...].astype(BF16)) + b3_ref[...]).astype(hdn_ref.dtype)

    hdn = hdn_ref[...]
    hf = _dot(hdn, w4f_ref[...].astype(BF16))
    hb = _dot(hdn, w4b_ref[...].astype(BF16))
    side = t_ref[:, 1:2]
    kern = jnp.where(side > 0.0, hf, jnp.where(side < 0.0, hb, 0.0)) * jnp.exp(-t_ref[:, 0:1] * dl_ref[...])
    o_ref[...] = kern

    @pl.when(i == 0)
    def _():
        asum_ref[j] = jnp.zeros(asum_ref.shape[1:], F32)

    asum_ref[j] += jnp.sum(jnp.abs(kern), axis=0, keepdims=True)


def hyena_filters(L, D, f_w1, f_b1, f_w2, f_b2, f_w3, f_b3, f_w4, *, t2_major_n2=None):
    t = jnp.linspace(0.0, 1.0, L, dtype=F32)[:, None]
    w = 2.0 * math.pi * jnp.arange(L, dtype=F32)[:, None] / L
    f = jnp.linspace(1e-4, HY_BANDS - 1, HY_BANDS, dtype=F32)[None]
    z = jnp.concatenate([t, jnp.cos(w * f), -jnp.sin(w * f)], -1)
    src = jnp.concatenate([jnp.arange(L), jnp.zeros((1,), jnp.int32), L - 1 - jnp.arange(L - 1)])
    r = jnp.arange(2 * L)
    side = jnp.where(r < L, 1.0, jnp.where(r > L, -1.0, 0.0)).astype(F32)[:, None]
    if t2_major_n2 is not None:
        perm = r.reshape(2 * L // t2_major_n2, t2_major_n2).T.reshape(-1)
        src, side = src[perm], side[perm]
    zf = jnp.pad(z[src], ((0, 0), (0, LANES - HY_EMB)))
    tcol = jnp.concatenate([t[src], side], axis=1)
    deltas = jnp.abs(jnp.linspace(math.log(HY_DECAY_TARGET) / HY_SLOW_DECAY, math.log(HY_DECAY_TARGET) / HY_FAST_DECAY,
                                  D, dtype=F32)).reshape(1, D)
    w1 = jnp.pad(f_w1, ((0, LANES - HY_EMB), (0, 0)))
    nf = f_w1.shape[1]
    R2 = 2 * L
    tr = _pick(R2, (512,))
    tc = _pick(D, (1024, 512, 256, 128))
    cpd = D // tc
    ncol = HY_ORDER * cpd
    full = lambda shape: pl.BlockSpec(shape, lambda i, j: (0, 0))
    kern, asum = pl.pallas_call(
        _filter_kernel,
        grid=(R2 // tr, ncol),
        in_specs=[
            pl.BlockSpec((tr, LANES), lambda i, j: (i, 0)),
            pl.BlockSpec((tr, 2), lambda i, j: (i, 0)),
            full((LANES, nf)), full((1, nf)), full((nf, nf)), full((1, nf)), full((nf, nf)), full((1, nf)),
            pl.BlockSpec((nf, tc), lambda i, j: (0, (j // cpd) * 2 * cpd + j % cpd)),
            pl.BlockSpec((nf, tc), lambda i, j: (0, ((j // cpd) * 2 + 1) * cpd + j % cpd)),
            pl.BlockSpec((1, tc), lambda i, j: (0, j % cpd)),
        ],
        out_specs=[pl.BlockSpec((tr, tc), lambda i, j: (i, j)), pl.BlockSpec((ncol, 1, tc), lambda i, j: (0, 0, 0))],
        out_shape=[jax.ShapeDtypeStruct((R2, HY_ORDER * D), F32), jax.ShapeDtypeStruct((ncol, 1, tc), F32)],
        scratch_shapes=[pltpu.VMEM((tr, nf), BF16)],
        compiler_params=_params("arbitrary", "arbitrary"),
    )(zf, tcol, w1, f_b1.reshape(1, nf), f_w2, f_b2.reshape(1, nf), f_w3, f_b3.reshape(1, nf), f_w4, f_w4, deltas)
    return kern, asum.reshape(1, HY_ORDER * D)


FFT_N2 = 128
FFT_TT2 = 8


def _split(x):
    hi = x.astype(BF16)
    return hi, (x - hi.astype(F32)).astype(BF16)


def _dot3(w_hi, w_lo, x):
    x_hi, x_lo = _split(x)
    return _dot(w_hi, x_hi) + (_dot(w_lo, x_hi) + _dot(w_hi, x_lo))


def _hilo(a):
    return _split(jnp.asarray(a, F32))


def _fft_first_kernel(whi_ref, wlo_ref, z_ref, o_ref, *, t2_major):
    if t2_major:
        n1 = o_ref.shape[1]
        for j in range(o_ref.shape[2]):
            a = _dot3(whi_ref[...], wlo_ref[...], z_ref[j])
            o_ref[0, :, j, :] = a[:n1]
            o_ref[1, :, j, :] = a[n1:]
    else:
        l1, tt2, tc = z_ref.shape
        a = _dot3(whi_ref[...], wlo_ref[...], z_ref[...].reshape(l1 * tt2, tc))
        o_ref[...] = a.reshape(o_ref.shape)


def fft_first(w_hl, z3, *, col0, C, t2_major=False):
    w_hi, w_lo = w_hl
    tt2 = FFT_TT2
    n1x2, l1 = w_hi.shape if t2_major else (w_hi.shape[0] // tt2, w_hi.shape[1] // tt2)
    n1, n2 = n1x2 // 2, z3.shape[0 if t2_major else 1]
    tc = _pick(C, (512, 256, 128))
    cb = col0 // tc
    full = pl.BlockSpec(w_hi.shape, lambda t, j: (0, 0))
    if t2_major:
        z_spec = pl.BlockSpec((tt2, l1, tc), lambda t, j: (t, 0, j + cb))
    else:
        z_spec = pl.BlockSpec((l1, tt2, tc), lambda t, j: (0, t, j + cb))
    return pl.pallas_call(
        functools.partial(_fft_first_kernel, t2_major=t2_major),
        grid=(n2 // tt2, C // tc),
        in_specs=[full, full, z_spec],
        out_specs=pl.BlockSpec((2, n1, tt2, tc), lambda t, j: (0, 0, t, j)),
        out_shape=jax.ShapeDtypeStruct((2, n1, n2, C), F32),
        compiler_params=_params("parallel", "parallel"),
    )(w_hi, w_lo, z3)


def _fft_mid_kernel(*refs, g, conv):
    if conv:
        a_ref, kf_ref, twc_ref, tws_ref, w2h_ref, w2l_ref, w2ih_ref, w2il_ref, o_ref = refs
    else:
        a_ref, twc_ref, tws_ref, w2h_ref, w2l_ref, o_ref = refs
    n2 = a_ref.shape[2]
    for kk in range(g):
        ar, ai = a_ref[0, kk], a_ref[1, kk]
        c, s = twc_ref[kk], tws_ref[kk]
        x = _dot3(w2h_ref[...], w2l_ref[...], jnp.concatenate([ar * c + ai * s, ai * c - ar * s], axis=0))
        if conv:
            xr, xi = x[:n2], x[n2:]
            kr, ki = kf_ref[0, kk], kf_ref[1, kk]
            y = _dot3(w2ih_ref[...], w2il_ref[...], jnp.concatenate([xr * kr - xi * ki, xr * ki + xi * kr], axis=0))
            yr, yi = y[:n2], y[n2:]
            o_ref[0, kk] = yr * c - yi * s
            o_ref[1, kk] = yr * s + yi * c
        else:
            o_ref[0, kk] = x[:n2]
            o_ref[1, kk] = x[n2:]


def fft_mid(a, tw_c, tw_s, w2_hl, w2i_hl=None, kf=None, *, kf_col0=0):
    _, n1, n2, C = a.shape
    g = 4
    tc = _pick(C, (512, 256, 128))
    conv = kf is not None
    blk = pl.BlockSpec((2, g, n2, tc), lambda j, k: (0, k, 0, j))
    tw = pl.BlockSpec((g, n2, 1), lambda j, k: (k, 0, 0))
    mat = pl.BlockSpec((2 * n2, 2 * n2), lambda j, k: (0, 0))
    if conv:
        kb = kf_col0 // tc
        in_specs = [blk, pl.BlockSpec((2, g, n2, tc), lambda j, k: (0, k, 0, j + kb)), tw, tw, mat, mat, mat, mat]
        args = [a, kf, tw_c, tw_s, *w2_hl, *w2i_hl]
    else:
        in_specs = [blk, tw, tw, mat, mat]
        args = [a, tw_c, tw_s, *w2_hl]
    return pl.pallas_call(
        functools.partial(_fft_mid_kernel, g=g, conv=conv),
        grid=(C // tc, n1 // g),
        in_specs=in_specs,
        out_specs=blk,
        out_shape=jax.ShapeDtypeStruct(a.shape, F32),
        compiler_params=_params("parallel", "parallel"),
    )(*args)


def _fft_last_kernel(whi_ref, wlo_ref, b_ref, gate_ref, zp_ref, asum_ref, bias_ref, o_ref):
    inv_asum = 1.0 / asum_ref[...]
    for j in range(b_ref.shape[2]):
        bj = jnp.concatenate([b_ref[0, :, j, :], b_ref[1, :, j, :]], axis=0)
        x = _dot3(whi_ref[...], wlo_ref[...], bj)
        o_ref[:, j, :] = gate_ref[:, j, :] * (x * inv_asum + zp_ref[:, j, :] * bias_ref[...])


def fft_last(w_hl, b, gate3, gate_col0, zp3, zp_col0, asum, bias, *, vec_col0):
    w_hi, w_lo = w_hl
    l1 = w_hi.shape[0]
    _, n1, n2, C = b.shape
    tt2 = FFT_TT2
    tc = _pick(C, (512, 256, 128))
    gcb, zcb, vcb = gate_col0 // tc, zp_col0 // tc, vec_col0 // tc
    full = pl.BlockSpec((l1, 2 * n1), lambda t, j: (0, 0))
    return pl.pallas_call(
        _fft_last_kernel,
        grid=(n2 // tt2, C // tc),
        in_specs=[
            full, full,
            pl.BlockSpec((2, n1, tt2, tc), lambda t, j: (0, 0, t, j)),
            pl.BlockSpec((l1, tt2, tc), lambda t, j: (0, t, j + gcb)),
            pl.BlockSpec((l1, tt2, tc), lambda t, j: (0, t, j + zcb)),
            pl.BlockSpec((1, tc), lambda t, j: (0, j + vcb)),
            pl.BlockSpec((1, tc), lambda t, j: (0, j + vcb)),
        ],
        out_specs=pl.BlockSpec((l1, tt2, tc), lambda t, j: (0, t, j)),
        out_shape=jax.ShapeDtypeStruct((l1, n2, C), F32),
        compiler_params=_params("parallel", "parallel"),
    )(w_hi, w_lo, b, gate3, zp3, asum, bias)


def _hpmm_kernel(*refs, gated):
    if gated:
        ah_ref, al_ref, b_ref, gate_ref, z_ref, asum_ref, bias_ref, o_ref = refs
    else:
        ah_ref, al_ref, b_ref, o_ref = refs
    y = _dot3(ah_ref[...], al_ref[...], b_ref[...])
    if gated:
        y = gate_ref[...] * (y / asum_ref[...] + z_ref[...] * bias_ref[...])
    o_ref[...] = y


def hp_matmul(a_hl, b, *, gate=None, z=None, asum=None, bias=None, vec_col0=0):
    a_hi, a_lo = a_hl
    M, Kd = a_hi.shape
    N = b.shape[1]
    tn = _pick(N, (1024, 512, 256, 128))
    full = pl.BlockSpec((M, Kd), lambda j: (0, 0))
    in_specs = [full, full, pl.BlockSpec((Kd, tn), lambda j: (0, j))]
    args = [a_hi, a_lo, b]
    gated = gate is not None
    if gated:
        vb = vec_col0 // tn
        blk = pl.BlockSpec((M, tn), lambda j: (0, j))
        vec = pl.BlockSpec((1, tn), lambda j: (0, j + vb))
        in_specs += [blk, blk, vec, vec]
        args += [gate, z, asum, bias]
    return pl.pallas_call(
        functools.partial(_hpmm_kernel, gated=gated),
        grid=(N // tn,),
        in_specs=in_specs,
        out_specs=pl.BlockSpec((M, tn), lambda j: (0, j)),
        out_shape=jax.ShapeDtypeStruct((M, N), F32),
        compiler_params=_params("parallel"),
    )(*args)


def _cmul_kernel(x_ref, k_ref, o_ref):
    xr, xi, kr, ki = x_ref[0], x_ref[1], k_ref[0], k_ref[1]
    o_ref[0] = xr * kr - xi * ki
    o_ref[1] = xr * ki + xi * kr


def complex_mul(x, kf, *, kf_col0):
    _, R, C = x.shape
    tc = _pick(C, (512, 256, 128))
    kb = kf_col0 // tc
    return pl.pallas_call(
        _cmul_kernel,
        grid=(C // tc,),
        in_specs=[pl.BlockSpec((2, R, tc), lambda j: (0, 0, j)), pl.BlockSpec((2, R, tc), lambda j: (0, 0, j + kb))],
        out_specs=pl.BlockSpec((2, R, tc), lambda j: (0, 0, j)),
        out_shape=jax.ShapeDtypeStruct(x.shape, F32),
        compiler_params=_params("parallel"),
    )(x, kf)


def _dft_constants(n1, n2, l1):
    N = n1 * n2
    n1h = min(n1, -(-(n1 // 2 + 1) // 8) * 8)
    k1 = np.arange(n1h)[:, None]
    herm = np.where((k1 == 0) | (k1 == n1 // 2), 1.0, np.where(k1 < n1 // 2, 2.0, 0.0))
    ang1 = 2 * np.pi * k1 * np.arange(n1)[None, :] / n1
    w1 = np.concatenate([np.cos(ang1), -np.sin(ang1)], 0)
    w1_inv = np.concatenate([(herm * np.cos(ang1[:, :l1])).T, (-herm * np.sin(ang1[:, :l1])).T], 1) / N
    ang_t = 2 * np.pi * k1 * np.arange(n2)[None, :] / N
    k2 = np.arange(n2)[:, None]
    ang2 = 2 * np.pi * k2 * np.arange(n2)[None, :] / n2
    c2, s2 = np.cos(ang2), np.sin(ang2)
    f = lambda x: jnp.asarray(x, F32)
    return dict(w1_data=_hilo(np.kron(w1[:, :l1], np.eye(FFT_TT2))), w1_full=_hilo(w1), w1_inv=_hilo(w1_inv),
                tw_c=f(np.cos(ang_t))[:, :, None], tw_s=f(np.sin(ang_t))[:, :, None],
                w2=_hilo(np.block([[c2, s2], [-s2, c2]])), w2_inv=_hilo(np.block([[c2, -s2], [s2, c2]])))


def hyena_conv_lat(u3, kern, asum, bias, *, L, D):
    n2 = FFT_N2
    n1, l1 = 2 * L // n2, L // n2
    cst = _dft_constants(n1, n2, l1)
    ka = fft_first(cst["w1_full"], kern.reshape(n2, n1, HY_ORDER * D), col0=0, C=HY_ORDER * D, t2_major=True)
    kf = fft_mid(ka, cst["tw_c"], cst["tw_s"], cst["w2"])
    z3, zc0 = u3, 0
    for o in range(HY_ORDER):
        a = fft_first(cst["w1_data"], z3, col0=zc0, C=D)
        bmid = fft_mid(a, cst["tw_c"], cst["tw_s"], cst["w2"], cst["w2_inv"], kf, kf_col0=o * D)
        z3 = fft_last(cst["w1_inv"], bmid, u3, (o + 1) * D, z3, zc0, asum, bias, vec_col0=o * D)
        zc0 = 0
    return z3.reshape(L, D)


def _dense_dft_constants(L):
    N = 2 * L
    ang = 2 * np.pi * np.arange(N)[:, None] * np.arange(N)[None, :] / N
    wf = np.concatenate([np.cos(ang), -np.sin(ang)], 0)
    wi = np.concatenate([np.cos(ang[:L]), -np.sin(ang[:L])], 1) / N
    return _hilo(wf), _hilo(wf[:, :L]), _hilo(wi)


def hyena_conv_ctx(v, x1, x2, kern, asum, bias):
    L, D = v.shape
    wf, wf_data, wi = _dense_dft_constants(L)
    kf = hp_matmul(wf, kern).reshape(2, 2 * L, HY_ORDER * D)
    z = v
    for o, gate in enumerate((x1, x2)):
        xf = hp_matmul(wf_data, z).reshape(2, 2 * L, D)
        y = complex_mul(xf, kf, kf_col0=o * D).reshape(4 * L, D)
        z = hp_matmul(wi, y, gate=gate, z=z, asum=asum, bias=bias, vec_col0=o * D)
    return z


def _na_kernel(q_ref, k_ref, v_ref, *rest, n_ctx, rows, heads_per_step, rows_per_step):
    bias_refs, o_ref = rest[:rows_per_step], rest[rows_per_step]
    kr = NA_ROWS
    n_lat = rows * GRID_W
    dh = NA_HEAD_DIM
    scale = dh ** -0.5
    for rr in range(rows_per_step):
        r = pl.program_id(1) * rows_per_step + rr
        r0 = jnp.clip(r - kr // 2, 0, rows - kr)
        start = pl.multiple_of(r0 * GRID_W, GRID_W)
        qs = slice(rr * GRID_W, (rr + 1) * GRID_W)
        for hh in range(heads_per_step):
            sl = slice(hh * dh, (hh + 1) * dh)
            q = q_ref[qs, sl]
            s_ctx = _dot_nt(q, k_ref[n_lat:n_lat + n_ctx, sl]) * scale
            s_lat = _dot_nt(q, k_ref[pl.ds(start, kr * GRID_W), sl]) * scale + bias_refs[rr][hh]
            m = jnp.maximum(jnp.max(s_ctx, axis=1, keepdims=True), jnp.max(s_lat, axis=1, keepdims=True))
            p_ctx = jnp.exp(s_ctx - m)
            p_lat = jnp.exp(s_lat - m)
            l = jnp.sum(p_ctx, axis=1, keepdims=True) + jnp.sum(p_lat, axis=1, keepdims=True)
            o = (_dot(p_ctx.astype(BF16), v_ref[n_lat:n_lat + n_ctx, sl])
                 + _dot(p_lat.astype(BF16), v_ref[pl.ds(start, kr * GRID_W), sl]))
            o_ref[qs, sl] = (o / l).astype(o_ref.dtype)


def _na_bias_table(rpb, rows):
    H = rpb.shape[0]
    cols = jnp.arange(GRID_W)
    c0 = jnp.clip(cols - NA_COLS // 2, 0, GRID_W - NA_COLS)
    kc = jnp.arange(GRID_W)[None, :]
    inside = (kc >= c0[:, None]) & (kc < c0[:, None] + NA_COLS)
    rel = jnp.clip(kc - cols[:, None] + NA_COLS - 1, 0, 2 * NA_COLS - 2)
    rr = jnp.arange(NA_ROWS)[:, None] + jnp.arange(NA_ROWS)[None, :]
    t = rpb[:, rr]
    t = t[:, :, :, rel]
    t = jnp.where(inside[None, None, None], t, -1e30)
    t = jnp.transpose(t, (1, 0, 3, 2, 4))
    return t.reshape(NA_ROWS, H, GRID_W, NA_ROWS * GRID_W)


def na_attention(qkv, rpb, *, n_ctx):
    NT, D3 = qkv.shape
    D = D3 // 3
    L = NT - n_ctx
    rows = L // GRID_W
    hps = 4
    hw = hps * NA_HEAD_DIM
    nhb = D // hw
    table = _na_bias_table(rpb, rows)

    def pattern(r):
        r0 = jnp.clip(r - NA_ROWS // 2, 0, rows - NA_ROWS)
        return r0 - r + NA_ROWS - 1

    rps = 8
    bias_specs = [pl.BlockSpec((None, hps, GRID_W, NA_ROWS * GRID_W),
                               lambda h, r, rr=rr: (pattern(r * rps + rr), h, 0, 0)) for rr in range(rps)]
    return pl.pallas_call(
        functools.partial(_na_kernel, n_ctx=n_ctx, rows=rows, heads_per_step=hps, rows_per_step=rps),
        grid=(nhb, rows // rps),
        in_specs=[
            pl.BlockSpec((rps * GRID_W, hw), lambda h, r: (r, h)),
            pl.BlockSpec((NT, hw), lambda h, r: (0, nhb + h)),
            pl.BlockSpec((NT, hw), lambda h, r: (0, 2 * nhb + h)),
            *bias_specs,
        ],
        out_specs=pl.BlockSpec((rps * GRID_W, hw), lambda h, r: (r, h)),
        out_shape=jax.ShapeDtypeStruct((L, D), BF16),
        compiler_params=_params("parallel", "parallel"),
    )(qkv, qkv, qkv, *([table] * rps))


def _router_kernel(x_ref, w_ref, b_ref, o_ref):
    E, G = N_EXPERTS, N_GROUPS
    per = E // G
    logits = _dot_nt(w_ref[...].astype(BF16), x_ref[...])
    scores = jax.nn.sigmoid(logits)
    sel = scores + b_ref[...]
    tm = sel.shape[1]
    grp = sel.reshape(G, per, tm)
    eidx = lax.broadcasted_iota(jnp.int32, grp.shape, 1)
    m1 = jnp.max(grp, axis=1, keepdims=True)
    first = jnp.min(jnp.where(grp == m1, eidx, per), axis=1, keepdims=True)
    m2 = jnp.max(jnp.where(eidx == first, -jnp.inf, grp), axis=1, keepdims=True)
    gs = (m1 + m2).reshape(G, tm)
    gidx = lax.broadcasted_iota(jnp.int32, gs.shape, 0)
    gmask = jnp.zeros(gs.shape, jnp.bool_)
    for _ in range(TOPK_GROUPS):
        mx = jnp.max(gs, axis=0, keepdims=True)
        pick = gidx == jnp.min(jnp.where(gs == mx, gidx, G), axis=0, keepdims=True)
        gmask = jnp.logical_or(gmask, pick)
        gs = jnp.where(pick, -jnp.inf, gs)
    emask = jnp.broadcast_to(gmask.reshape(G, 1, tm), (G, per, tm)).reshape(E, tm)
    cand = jnp.where(emask, sel, -jnp.inf)
    xidx = lax.broadcasted_iota(jnp.int32, cand.shape, 0)
    chosen = jnp.zeros(cand.shape, jnp.bool_)
    for _ in range(TOP_K):
        mx = jnp.max(cand, axis=0, keepdims=True)
        pick = xidx == jnp.min(jnp.where(cand == mx, xidx, E), axis=0, keepdims=True)
        chosen = jnp.logical_or(chosen, pick)
        cand = jnp.where(pick, -jnp.inf, cand)
    wsel = jnp.where(chosen, scores, 0.0)
    gates = wsel / jnp.sum(wsel, axis=0, keepdims=True) * ROUTED_SCALE
    o_ref[...] = gates.T


def moe_router(h, router_w_t, router_b, layer):
    T, D = h.shape
    E = router_w_t.shape[1]
    tm = _pick(T, (256, 128))
    return pl.pallas_call(
        _router_kernel,
        grid=(T // tm,),
        in_specs=[
            pl.BlockSpec((tm, D), lambda i: (i, 0)),
            pl.BlockSpec((None, E, D), lambda i: (layer, 0, 0)),
            pl.BlockSpec((None, E, 1), lambda i: (layer, 0, 0)),
        ],
        out_specs=pl.BlockSpec((tm, E), lambda i: (i, 0)),
        out_shape=jax.ShapeDtypeStruct((T, E), F32),
        compiler_params=_params("parallel"),
    )(h, router_w_t, router_b.reshape(-1, E, 1))


def _moe_up_kernel(*refs, nk, ne, gated, F):
    if gated:
        x_ref, wg_ref, wu_ref, g_ref, o_ref, accg_ref, accu_ref = refs
    else:
        x_ref, wg_ref, wu_ref, o_ref, accg_ref, accu_ref = refs
    e0 = pl.program_id(1) * ne
    k = pl.program_id(2)

    @pl.when(k == 0)
    def _():
        accg_ref[...] = jnp.zeros_like(accg_ref)
        accu_ref[...] = jnp.zeros_like(accu_ref)

    def stacked_experts(ref):
        w = ref[0] if ne == 1 else jnp.concatenate([ref[ee] for ee in range(ne)], axis=0)
        return w.astype(BF16)

    x = x_ref[...]
    accg_ref[...] += _dot_nt(x, stacked_experts(wg_ref))
    accu_ref[...] += _dot_nt(x, stacked_experts(wu_ref))

    @pl.when(k == nk - 1)
    def _():
        a = accg_ref[...]
        hid = a * jax.nn.sigmoid(a) * accu_ref[...]
        if gated:
            g = g_ref[...]
            lane = lax.broadcasted_iota(jnp.int32, g.shape, 1)
            col = lax.broadcasted_iota(jnp.int32, (1, ne * F), 1) // F
            gexp = jnp.zeros_like(hid)
            for ee in range(ne):
                ge = jnp.sum(jnp.where(lane == e0 + ee, g, 0.0), axis=1, keepdims=True)
                gexp = jnp.where(col == ee, ge, gexp)
            hid = hid * gexp
        o_ref[...] = hid.astype(o_ref.dtype)


def moe_up(x, w_gate, w_up, layer, gates=None):
    T, D = x.shape
    if w_gate.ndim == 3:
        w_gate, w_up = w_gate[:, None], w_up[:, None]
    w_gate, w_up = jnp.swapaxes(w_gate, 2, 3), jnp.swapaxes(w_up, 2, 3)
    E, F = w_gate.shape[1], w_gate.shape[2]
    ne = 4 if E % 4 == 0 else 1
    tm = _pick(T, (1056, 1024, 512, 256, 128))
    tk = _pick(D, (2048, 1024, 512, 256, 128))
    nk = D // tk
    gated = gates is not None
    w_spec = pl.BlockSpec((None, ne, F, tk), lambda i, e, k: (layer, e, 0, k))
    in_specs = [pl.BlockSpec((tm, tk), lambda i, e, k: (i, k)), w_spec, w_spec]
    args = [x, w_gate, w_up]
    if gated:
        in_specs.append(pl.BlockSpec((tm, E), lambda i, e, k: (i, 0)))
        args.append(gates)
    return pl.pallas_call(
        functools.partial(_moe_up_kernel, nk=nk, ne=ne, gated=gated, F=F),
        grid=(T // tm, E // ne, nk),
        in_specs=in_specs,
        out_specs=pl.BlockSpec((tm, ne * F), lambda i, e, k: (i, e)),
        out_shape=jax.ShapeDtypeStruct((T, E * F), BF16),
        scratch_shapes=[pltpu.VMEM((tm, ne * F), F32), pltpu.VMEM((tm, ne * F), F32)],
        compiler_params=_params("parallel", "parallel", "arbitrary"),
    )(*args)


def moe(h, layer, router_w_t, router_b, w_gate, w_up, w_down, s_gate, s_up, s_down):
    gates = moe_router(h, router_w_t, router_b, layer)
    hid = moe_up(h, w_gate, w_up, layer, gates)
    hid_sh = moe_up(h, s_gate, s_up, layer)
    E, F, D = w_down.shape[1:]
    return matmul(hid, w_down.reshape(DEPTH, E * F, D), out_dtype=F32, layer=layer, extra=(hid_sh, s_down, layer))


def kernel(x, c, ctx, c_ctx, ada_a, ada_b, ada_bias, ln1_g, ln1_b, ln2_g, ln2_b, router_w, router_b, moe_w_gate, moe_w_up, moe_w_down, sh_w_gate, sh_w_up, sh_w_down, ml_w_in, ml_b_if, ml_norm_g, ml_w_out, da_w_qkv, da_lam_q1, da_lam_k1, da_lam_q2, da_lam_k2, da_subln_g, da_w_out, hy_w_in, hy_conv_w, hy_conv_b, hy_f_w1, hy_f_b1, hy_f_w2, hy_f_b2, hy_f_w3, hy_f_b3, hy_f_w4, hy_bias, hy_w_out, na_w_qkv, na_rpb, na_w_out):
    _, L, D = x.shape
    C = ctx.shape[1]
    NT = C + L
    s = jnp.concatenate([x[0], ctx[0]], axis=0)

    cc = jnp.zeros((8, D), F32).at[0].set(c_ctx).at[1].set(c[0])
    mods = ada_modulation(cc, ada_a, ada_b, ada_bias)[:, :2].reshape(DEPTH, 2, 6, 1, D)
    SH1, SC1, G1, SH2, SC2, G2 = range(6)
    router_w_t = jnp.swapaxes(router_w, 1, 2)
    moe_args = (router_w_t, router_b, moe_w_gate, moe_w_up, moe_w_down, sh_w_gate, sh_w_up, sh_w_down)

    (h,) = ln_modulate(s, mods, n_ctx=C, h_layer=0, sc_idx=SC1, sh_idx=SH1)
    for i in range(DEPTH):
        last = i == DEPTH - 1
        if i == 0:
            y = mlstm_mixer(h, ml_w_in, ml_b_if[0], ml_norm_g[0], ml_w_out, n_ctx=C)
        elif i == 1:
            lam4 = jnp.stack([da_lam_q1[0], da_lam_k1[0], da_lam_q2[0], da_lam_k2[0]])
            y = diff_attn_mixer(h, da_w_qkv, lam4, da_subln_g[0], da_w_out, n_ctx=C,
                                lambda_init=0.8 - 0.6 * math.exp(-0.3 * i))
        elif i == 2:
            y = hyena_mixer(h, hy_w_in, hy_conv_w[0], hy_conv_b[0], hy_f_w1[0], hy_f_b1[0], hy_f_w2[0], hy_f_b2[0],
                            hy_f_w3[0], hy_f_b3[0], hy_f_w4[0], hy_bias[0], hy_w_out, n_ctx=C)
        else:
            y = na_mixer(h, na_w_qkv, na_rpb[0], na_w_out, n_ctx=C)
        n_ctx = C
        if last:
            s, n_ctx = s[:L], 0
        s, h = ln_modulate(s, mods, n_ctx=n_ctx, ys=(y,), ln_layer=i, gate_idx=G1, ln_g=ln1_g, ln_b=ln1_b,
                           h_layer=i, sc_idx=SC2, sh_idx=SH2)
        y = moe(h, i, *moe_args)
        if last:
            (s,) = ln_modulate(s, mods, n_ctx=n_ctx, ys=(y,), ln_layer=i, gate_idx=G2, ln_g=ln2_g, ln_b=ln2_b)
        else:
            s, h = ln_modulate(s, mods, n_ctx=n_ctx, ys=(y,), ln_layer=i, gate_idx=G2, ln_g=ln2_g, ln_b=ln2_b,
                               h_layer=i + 1, sc_idx=SC1, sh_idx=SH1)
    return s[None]


def mlstm_mixer(h, w_in, b_if, norm_g, w_out, *, n_ctx):
    D = h.shape[1]
    w_in_t = jnp.swapaxes(w_in, 1, 2)
    u = matmul(h, w_in_t, out_dtype=F32, layer=0, ncols=3 * D, w_transposed=True)
    w_if_t = jnp.pad(w_in_t[0, 3 * D:], ((0, LANES - 4 * ML_HEADS), (0, 0)))
    g_if = matmul(h, w_if_t, out_dtype=F32, w_transposed=True)
    hs = mlstm_scan(u, g_if, b_if, n_ctx=n_ctx)
    return matmul(mlstm_gate(hs, u, norm_g, o_col_block=2), w_out, out_dtype=F32, layer=0)


def diff_attn_mixer(h, w_qkv, lam4, subln_g, w_out, *, n_ctx, lambda_init):
    NT, D = h.shape
    L = NT - n_ctx
    scale = DA_HEAD_DIM ** -0.5 * math.log2(math.e)
    cos, sin = _rope_tables(L, 1.0)
    ones, zeros = jnp.ones((n_ctx, LANES), F32), jnp.zeros((n_ctx, LANES), F32)
    cos_k, sin_k = jnp.concatenate([cos, ones]), jnp.concatenate([sin, zeros])
    cos3 = jnp.stack([cos_k * scale, cos_k, jnp.ones_like(cos_k)])
    sin3 = jnp.stack([sin_k * scale, sin_k, jnp.zeros_like(sin_k)])
    qkv = matmul(h, w_qkv, out_dtype=BF16, layer=0, rope=(cos3, sin3, D))
    o_lat = diff_attention(qkv, lam4, subln_g, q_row0=0, n_q=L, kv_row0=0, kv_len=NT, lambda_init=lambda_init)
    o_ctx = diff_attention(qkv, lam4, subln_g, q_row0=L, n_q=n_ctx, kv_row0=L, kv_len=n_ctx, lambda_init=lambda_init)
    return matmul(jnp.concatenate([o_lat, o_ctx]), w_out, out_dtype=F32, layer=0)


def hyena_mixer(h, w_in, conv_w, conv_b, f_w1, f_b1, f_w2, f_b2, f_w3, f_b3, f_w4, hy_bias, w_out, *, n_ctx):
    NT, D = h.shape
    L = NT - n_ctx
    u = matmul(h, w_in, out_dtype=F32, layer=0, conv=(conv_w, conv_b, (0, L, NT)))
    fargs = (f_w1, f_b1, f_w2, f_b2, f_w3, f_b3, f_w4)
    bias = hy_bias.reshape(1, HY_ORDER * D)
    kern_l, asum_l = hyena_filters(L, D, *fargs, t2_major_n2=FFT_N2)
    z_lat = hyena_conv_lat(u.reshape(NT // FFT_N2, FFT_N2, 3 * D), kern_l, asum_l, bias, L=L, D=D)
    kern_c, asum_c = hyena_filters(n_ctx, D, *fargs)
    v, x1, x2 = (u[L:, j * D:(j + 1) * D] for j in range(3))
    z_ctx = hyena_conv_ctx(v, x1, x2, kern_c, asum_c, bias)
    return matmul(jnp.concatenate([z_lat, z_ctx]), w_out, out_dtype=F32, layer=0)


def na_mixer(h, w_qkv, rpb, w_out, *, n_ctx):
    qkv = matmul(h, w_qkv, out_dtype=BF16, layer=0)
    return matmul(na_attention(qkv, rpb, n_ctx=n_ctx), w_out, out_dtype=F32, layer=0)
```
